```python
import jax
import jax.numpy as jnp
from jax import lax
import numpy as np

D_MODEL = 2048
BATCH = 8
SEQ = 8192
DEPTH = 2

GRID_W = 64
CTX_LEN = 256
EPS = 1e-6
N_MOD = 6
MLA_HEADS = 8
QK_NOPE = 128
QK_ROPE = 64
V_HEAD = 128
Q_LORA = 512
KV_LORA = 256
ROPE_THETA = 10000.0
AXIS_ROPE = QK_ROPE // 2
Q_BLOCK = 128
CONV_CH = D_MODEL // 2
GLA_HEADS = 4
GLA_DK = D_MODEL // 2 // GLA_HEADS
GLA_DV = D_MODEL // GLA_HEADS
GLA_KEY = GLA_HEADS * GLA_DK
GLA_VAL = GLA_HEADS * GLA_DV
GATE_RANK = 16
GATE_NORMALIZER = 16.0
CHUNK = 64
D_FF = 5632
AB_CTX_COLS = KV_LORA + QK_ROPE
AB_SPLITS = (KV_LORA, AB_CTX_COLS, AB_CTX_COLS + Q_LORA, AB_CTX_COLS + Q_LORA + CONV_CH, AB_CTX_COLS + Q_LORA + 2 * CONV_CH)
AB_COLS = AB_CTX_COLS + Q_LORA + 3 * CONV_CH
AB_OUT = CONV_CH + MLA_HEADS * V_HEAD
GLA_CTX_COLS = GLA_KEY + GLA_VAL + 2 * GATE_RANK
GLA_SPLITS = (GLA_KEY, GLA_KEY + GLA_VAL, GLA_KEY + GLA_VAL + GATE_RANK, GLA_CTX_COLS, GLA_CTX_COLS + GLA_KEY)
GLA_COLS = GLA_CTX_COLS + GLA_KEY + GLA_VAL

kernel_name = 'hybrid_shortconv_mla_gla_convffn_dit'


def rmsnorm(x, w):
    xf = x.astype(jnp.float32)
    y = xf * lax.rsqrt(jnp.mean(xf * xf, axis=-1, keepdims=True) + EPS)
    return (y * w.astype(jnp.float32)).astype(x.dtype)


def modulate(x, w, shift, scale):
    return rmsnorm(x, w) * (1.0 + scale) + shift


def dwconv3(x, w, b=None):
    xp = jnp.pad(x, ((0, 0), (1, 1), (0, 0)))
    y = xp[:, :-2] * w[0] + xp[:, 1:-1] * w[1] + xp[:, 2:] * w[2]
    return y if b is None else y + b


def axial_rope_tables(n_tokens):
    rows = n_tokens // GRID_W
    row = jnp.repeat(jnp.arange(rows, dtype=jnp.float32), GRID_W)
    col = jnp.tile(jnp.arange(GRID_W, dtype=jnp.float32), rows)
    inv = ROPE_THETA ** (-jnp.arange(0, AXIS_ROPE, 2, dtype=jnp.float32) / AXIS_ROPE)
    ar = row[:, None] * inv
    ac = col[:, None] * inv
    ang = jnp.concatenate([ar, ar, ac, ac], axis=-1)
    return jnp.cos(ang), jnp.sin(ang)


def apply_rope(x, cos, sin):
    extra = x.ndim - 3
    cos = cos.reshape(cos.shape[0], *([1] * extra), cos.shape[1])
    sin = sin.reshape(sin.shape[0], *([1] * extra), sin.shape[1])
    xf = x.astype(jnp.float32)
    a1, a2, b1, b2 = jnp.split(xf, 4, axis=-1)
    rot = jnp.concatenate([-a2, a1, -b2, b1], axis=-1)
    return (xf * cos + rot * sin).astype(x.dtype)


def attention(q, k, v):
    s = jnp.einsum('bqhd,bkhd->bhqk', q, k, preferred_element_type=jnp.float32) * (q.shape[-1] ** -0.5)
    p = jax.nn.softmax(s, axis=-1).astype(v.dtype)
    return jnp.einsum('bhqk,bkhd->bqhd', p, v)


def blocked_attention(q, k, v):
    bsz, t, nh, dk = q.shape
    nb = t // Q_BLOCK
    qb = q.reshape(bsz, nb, Q_BLOCK, nh, dk).transpose(1, 0, 2, 3, 4)
    o = lax.map(lambda qi: attention(qi, k, v), qb)
    return o.transpose(1, 0, 2, 3, 4).reshape(bsz, t, nh, v.shape[-1])


def mla_q(q_lat, q_norm, w_qb, cos_sin):
    q = (rmsnorm(q_lat, q_norm) @ w_qb).reshape(q_lat.shape[0], q_lat.shape[1], MLA_HEADS, QK_NOPE + QK_ROPE)
    if cos_sin is None:
        return q
    return jnp.concatenate([q[..., :QK_NOPE], apply_rope(q[..., QK_NOPE:], *cos_sin)], axis=-1)


def mla_kv(kv_lat, k_rope, kv_norm, w_kvb, cos_sin):
    kv = (rmsnorm(kv_lat, kv_norm) @ w_kvb).reshape(kv_lat.shape[0], kv_lat.shape[1], MLA_HEADS, QK_NOPE + V_HEAD)
    if cos_sin is not None:
        k_rope = apply_rope(k_rope, *cos_sin)
    k_rope = jnp.broadcast_to(k_rope[:, :, None, :], kv.shape[:3] + (QK_ROPE,))
    k = jnp.concatenate([kv[..., :QK_NOPE], k_rope], axis=-1)
    return k, kv[..., QK_NOPE:]


def mixer_ab(h, hc, w_in, conv_a, q_norm, w_qb, kv_norm, w_kvb, w_out, ctx_out):
    bsz, t, _ = h.shape
    cos_sin = axial_rope_tables(t)
    z = h @ w_in
    zc = hc @ (w_in if ctx_out else w_in[:, :AB_CTX_COLS])
    kv_lat, k_rope, q_lat, a_x, a_b, a_c = jnp.split(z, list(AB_SPLITS), axis=-1)
    k_l, v_l = mla_kv(kv_lat, k_rope, kv_norm, w_kvb, cos_sin)
    k_c, v_c = mla_kv(zc[..., :KV_LORA], zc[..., KV_LORA:AB_CTX_COLS], kv_norm, w_kvb, None)
    q_l = mla_q(q_lat, q_norm, w_qb, cos_sin)
    o_l = blocked_attention(q_l, jnp.concatenate([k_l, k_c], axis=1), jnp.concatenate([v_l, v_c], axis=1))
    y_a = a_b * dwconv3(a_c * a_x, conv_a)
    y = jnp.concatenate([y_a, o_l.reshape(bsz, t, MLA_HEADS * V_HEAD)], axis=-1) @ w_out
    if not ctx_out:
        return y, None
    _, _, qc_lat, c_x, c_b, c_c = jnp.split(zc, list(AB_SPLITS), axis=-1)
    o_c = attention(mla_q(qc_lat, q_norm, w_qb, None), k_c, v_c)
    yc_a = c_b * dwconv3(c_c * c_x, conv_a)
    yc = jnp.concatenate([yc_a, o_c.reshape(o_c.shape[0], o_c.shape[1], MLA_HEADS * V_HEAD)], axis=-1) @ w_out
    return y, yc


def gla_chunked(q, k, v, g, h0):
    bsz, nh, t, dk = k.shape
    n = t // CHUNK
    with_q = q is not None

    def chunks(a):
        return a.reshape(bsz, nh, n, CHUNK, a.shape[-1]).transpose(2, 0, 1, 3, 4)

    mask = jnp.tril(jnp.ones((CHUNK, CHUNK), dtype=bool))

    def step(state, xs):
        if with_q:
            qc, kc, vc, gc = xs
        else:
            kc, vc, gc = xs
        b = jnp.cumsum(gc, axis=-2)
        b_last = b[..., -1:, :]
        new_state = state * jnp.swapaxes(jnp.exp(b_last), -1, -2) + jnp.einsum('bhcd,bhcv->bhdv', kc * jnp.exp(b_last - b), vc)
        if not with_q:
            return new_state, None
        qe = qc * jnp.exp(b)
        ke = kc * jnp.exp(-b)
        att = jnp.where(mask, jnp.einsum('bhid,bhjd->bhij', qe, ke), 0.0)
        out = jnp.einsum('bhij,bhjv->bhiv', att, vc) + jnp.einsum('bhid,bhdv->bhiv', qe, state)
        return new_state, out

    xs = (chunks(k), chunks(v), chunks(g))
    if with_q:
        xs = (chunks(q),) + xs
    final, out = lax.scan(step, h0, xs)
    if with_q:
        out = out.transpose(1, 2, 0, 3, 4).reshape(bsz, nh, t, v.shape[-1])
    return out, final


def to_heads(a, d):
    bsz, t, _ = a.shape
    return a.reshape(bsz, t, GLA_HEADS, d).transpose(0, 2, 1, 3).astype(jnp.float32)


def gla_gate(lr, w, b):
    return jax.nn.log_sigmoid((lr @ w + b).astype(jnp.float32)) / GATE_NORMALIZER


def rev(a):
    return a[:, :, ::-1]


def gla_prep(z, with_q, gfw_w, gfw_b, gbw_w, gbw_b):
    parts = jnp.split(z, list(GLA_SPLITS if with_q else GLA_SPLITS[:3]), axis=-1)
    k = to_heads(parts[0], GLA_DK)
    v = to_heads(parts[1], GLA_DV)
    g_fw = to_heads(gla_gate(parts[2], gfw_w, gfw_b), GLA_DK)
    g_bw = to_heads(gla_gate(parts[3], gbw_w, gbw_b), GLA_DK)
    if not with_q:
        return None, k, v, g_fw, g_bw, None
    q = to_heads(parts[4], GLA_DK) * (GLA_DK ** -0.5)
    return q, k, v, g_fw, g_bw, parts[5]


def gla_out(o, og, o_norm, w_out):
    bsz, nh, t, _ = o.shape
    o = rmsnorm(o, o_norm).transpose(0, 2, 1, 3).reshape(bsz, t, GLA_VAL).astype(og.dtype)
    return (o * jax.nn.silu(og)) @ w_out


def mixer_gla(h, hc, w_in, gfw_w, gfw_b, gbw_w, gbw_b, o_norm, w_out, ctx_out):
    bsz = h.shape[0]
    h0 = jnp.zeros((bsz, GLA_HEADS, GLA_DK, GLA_DV), jnp.float32)
    zc = hc @ (w_in if ctx_out else w_in[:, :GLA_CTX_COLS])
    qc, kc, vc, gfc, gbc, ogc = gla_prep(zc, ctx_out, gfw_w, gfw_b, gbw_w, gbw_b)
    oc_f, sc_f = gla_chunked(qc, kc, vc, gfc, h0)
    oc_b, sc_b = gla_chunked(rev(qc) if ctx_out else None, rev(kc), rev(vc), rev(gbc), h0)
    q, k, v, gf, gb, og = gla_prep(h @ w_in, True, gfw_w, gfw_b, gbw_w, gbw_b)
    o_f, _ = gla_chunked(q, k, v, gf, sc_f)
    o_b, _ = gla_chunked(rev(q), rev(k), rev(v), rev(gb), sc_b)
    y = gla_out(o_f + rev(o_b), og, o_norm, w_out)
    if not ctx_out:
        return y, None
    return y, gla_out(oc_f + rev(oc_b), ogc, o_norm, w_out)


def conv_ffn(h, w_up, conv_w, conv_b, w_down):
    u = dwconv3(h @ w_up, conv_w, conv_b)
    gate, val = jnp.split(u, 2, axis=-1)
    return (jax.nn.silu(gate) * val) @ w_down


def block(x, xc, c_act, cc_act, ada_w, ada_b, norm1, norm2, mixer_fn, mix, ffn, ctx_out):
    m = jnp.split((c_act @ ada_w + ada_b)[:, None, :], N_MOD, axis=-1)
    mc = jnp.split((cc_act @ ada_w + ada_b)[:, None, :], N_MOD, axis=-1)
    y, yc = mixer_fn(modulate(x, norm1, m[0], m[1]), modulate(xc, norm1, mc[0], mc[1]), *mix, ctx_out)
    x = x + m[2] * y
    x = x + m[5] * conv_ffn(modulate(x, norm2, m[3], m[4]), *ffn)
    if not ctx_out:
        return x, None
    xc = xc + mc[2] * yc
    xc = xc + mc[5] * conv_ffn(modulate(xc, norm2, mc[3], mc[4]), *ffn)
    return x, xc


def _fwd_setup_inputs(seed: int = 0) -> dict:
    key = jax.random.key(seed)
    keys = iter(jax.random.split(key, 16 + 24 * DEPTH))

    def rnd(shape, scale):
        return jax.random.normal(next(keys), shape, jnp.float32) * scale

    def gain(n):
        return 1.0 + rnd((n,), 0.02)

    inp = {}
    inp['x'] = rnd((BATCH, SEQ, D_MODEL), 1.0)
    inp['c'] = rnd((BATCH, D_MODEL), 1.0)
    inp['ctx'] = rnd((BATCH, CTX_LEN, D_MODEL), 1.0)
    inp['c_ctx'] = rnd((D_MODEL,), 1.0)
    for i in range(DEPTH):
        p = 'l%d_' % i
        inp[p + 'ada_w'] = rnd((D_MODEL, N_MOD * D_MODEL), 0.5 * D_MODEL ** -0.5)
        inp[p + 'ada_b'] = rnd((N_MOD * D_MODEL,), 0.02)
        inp[p + 'norm1'] = gain(D_MODEL)
        if i % 2 == 0:
            inp[p + 'w_in'] = rnd((D_MODEL, AB_COLS), D_MODEL ** -0.5)
            inp[p + 'conv_a'] = rnd((3, CONV_CH), 3 ** -0.5)
            inp[p + 'q_norm'] = gain(Q_LORA)
            inp[p + 'w_qb'] = rnd((Q_LORA, MLA_HEADS * (QK_NOPE + QK_ROPE)), Q_LORA ** -0.5)
            inp[p + 'kv_norm'] = gain(KV_LORA)
            inp[p + 'w_kvb'] = rnd((KV_LORA, MLA_HEADS * (QK_NOPE + V_HEAD)), KV_LORA ** -0.5)
            inp[p + 'w_out'] = rnd((AB_OUT, D_MODEL), AB_OUT ** -0.5)
        else:
            inp[p + 'w_in'] = rnd((D_MODEL, GLA_COLS), D_MODEL ** -0.5)
            inp[p + 'gate_fw_w'] = rnd((GATE_RANK, GLA_KEY), GATE_RANK ** -0.5)
            inp[p + 'gate_fw_b'] = rnd((GLA_KEY,), 0.02)
            inp[p + 'gate_bw_w'] = rnd((GATE_RANK, GLA_KEY), GATE_RANK ** -0.5)
            inp[p + 'gate_bw_b'] = rnd((GLA_KEY,), 0.02)
            inp[p + 'o_norm'] = gain(GLA_DV)
            inp[p + 'w_out'] = rnd((GLA_VAL, D_MODEL), GLA_VAL ** -0.5)
        inp[p + 'norm2'] = gain(D_MODEL)
        inp[p + 'ffn_up'] = rnd((D_MODEL, 2 * D_FF), D_MODEL ** -0.5)
        inp[p + 'ffn_conv_w'] = rnd((3, 2 * D_FF), 3 ** -0.5)
        inp[p + 'ffn_conv_b'] = rnd((2 * D_FF,), 0.02)
        inp[p + 'ffn_down'] = rnd((D_FF, D_MODEL), D_FF ** -0.5)
    inp['final_norm'] = gain(D_MODEL)
    return inp


def _fwd_reference(x, c, ctx, c_ctx,
              l0_ada_w, l0_ada_b, l0_norm1, l0_w_in, l0_conv_a, l0_q_norm, l0_w_qb, l0_kv_norm, l0_w_kvb, l0_w_out,
              l0_norm2, l0_ffn_up, l0_ffn_conv_w, l0_ffn_conv_b, l0_ffn_down,
              l1_ada_w, l1_ada_b, l1_norm1, l1_w_in, l1_gate_fw_w, l1_gate_fw_b, l1_gate_bw_w, l1_gate_bw_b, l1_o_norm, l1_w_out,
              l1_norm2, l1_ffn_up, l1_ffn_conv_w, l1_ffn_conv_b, l1_ffn_down,
              final_norm):
    layers = (
        (l0_ada_w, l0_ada_b, l0_norm1, l0_norm2,
         (l0_w_in, l0_conv_a, l0_q_norm, l0_w_qb, l0_kv_norm, l0_w_kvb, l0_w_out),
         (l0_ffn_up, l0_ffn_conv_w, l0_ffn_conv_b, l0_ffn_down)),
        (l1_ada_w, l1_ada_b, l1_norm1, l1_norm2,
         (l1_w_in, l1_gate_fw_w, l1_gate_fw_b, l1_gate_bw_w, l1_gate_bw_b, l1_o_norm, l1_w_out),
         (l1_ffn_up, l1_ffn_conv_w, l1_ffn_conv_b, l1_ffn_down)),
    )
    c_act = jax.nn.silu(c)
    cc_act = jax.nn.silu(c_ctx)[None, :]
    xc = ctx
    for i in range(DEPTH):
        ada_w, ada_b, n1, n2, mix, ffn = layers[i]
        mixer_fn = mixer_ab if i % 2 == 0 else mixer_gla
        x, xc = block(x, xc, c_act, cc_act, ada_w, ada_b, n1, n2, mixer_fn, mix, ffn, i < DEPTH - 1)
    return rmsnorm(x, final_norm)


import jax as _jax
import jax.numpy as _jnp

TWIN_FORMAT = 'train_step'
FWD_PARAMS = ['x', 'c', 'ctx', 'c_ctx', 'l0_ada_w', 'l0_ada_b', 'l0_norm1', 'l0_w_in', 'l0_conv_a', 'l0_q_norm', 'l0_w_qb', 'l0_kv_norm', 'l0_w_kvb', 'l0_w_out', 'l0_norm2', 'l0_ffn_up', 'l0_ffn_conv_w', 'l0_ffn_conv_b', 'l0_ffn_down', 'l1_ada_w', 'l1_ada_b', 'l1_norm1', 'l1_w_in', 'l1_gate_fw_w', 'l1_gate_fw_b', 'l1_gate_bw_w', 'l1_gate_bw_b', 'l1_o_norm', 'l1_w_out', 'l1_norm2', 'l1_ffn_up', 'l1_ffn_conv_w', 'l1_ffn_conv_b', 'l1_ffn_down', 'final_norm']
TWIN_WEIGHTS = ['c_ctx', 'l0_ada_w', 'l0_ada_b', 'l0_norm1', 'l0_w_in', 'l0_conv_a', 'l0_q_norm', 'l0_w_qb', 'l0_kv_norm', 'l0_w_kvb', 'l0_w_out', 'l0_norm2', 'l0_ffn_up', 'l0_ffn_conv_w', 'l0_ffn_conv_b', 'l0_ffn_down', 'l1_ada_w', 'l1_ada_b', 'l1_norm1', 'l1_w_in', 'l1_gate_fw_w', 'l1_gate_fw_b', 'l1_gate_bw_w', 'l1_gate_bw_b', 'l1_o_norm', 'l1_w_out', 'l1_norm2', 'l1_ffn_up', 'l1_ffn_conv_w', 'l1_ffn_conv_b', 'l1_ffn_down', 'final_norm']
TWIN_DIFF_INPUT = 'x'
TWIN_INPUTS = ['x', 'c', 'ctx', 'c_ctx', 'l0_ada_w', 'l0_ada_b', 'l0_norm1', 'l0_w_in', 'l0_conv_a', 'l0_q_norm', 'l0_w_qb', 'l0_kv_norm', 'l0_w_kvb', 'l0_w_out', 'l0_norm2', 'l0_ffn_up', 'l0_ffn_conv_w', 'l0_ffn_conv_b', 'l0_ffn_down', 'l1_ada_w', 'l1_ada_b', 'l1_norm1', 'l1_w_in', 'l1_gate_fw_w', 'l1_gate_fw_b', 'l1_gate_bw_w', 'l1_gate_bw_b', 'l1_o_norm', 'l1_w_out', 'l1_norm2', 'l1_ffn_up', 'l1_ffn_conv_w', 'l1_ffn_conv_b', 'l1_ffn_down', 'final_norm', 'loss_target', 'm_c_ctx', 'm_l0_ada_w', 'm_l0_ada_b', 'm_l0_norm1', 'm_l0_w_in', 'm_l0_conv_a', 'm_l0_q_norm', 'm_l0_w_qb', 'm_l0_kv_norm', 'm_l0_w_kvb', 'm_l0_w_out', 'm_l0_norm2', 'm_l0_ffn_up', 'm_l0_ffn_conv_w', 'm_l0_ffn_conv_b', 'm_l0_ffn_down', 'm_l1_ada_w', 'm_l1_ada_b', 'm_l1_norm1', 'm_l1_w_in', 'm_l1_gate_fw_w', 'm_l1_gate_fw_b', 'm_l1_gate_bw_w', 'm_l1_gate_bw_b', 'm_l1_o_norm', 'm_l1_w_out', 'm_l1_norm2', 'm_l1_ffn_up', 'm_l1_ffn_conv_w', 'm_l1_ffn_conv_b', 'm_l1_ffn_down', 'm_final_norm', 'v_c_ctx', 'v_l0_ada_w', 'v_l0_ada_b', 'v_l0_norm1', 'v_l0_w_in', 'v_l0_conv_a', 'v_l0_q_norm', 'v_l0_w_qb', 'v_l0_kv_norm', 'v_l0_w_kvb', 'v_l0_w_out', 'v_l0_norm2', 'v_l0_ffn_up', 'v_l0_ffn_conv_w', 'v_l0_ffn_conv_b', 'v_l0_ffn_down', 'v_l1_ada_w', 'v_l1_ada_b', 'v_l1_norm1', 'v_l1_w_in', 'v_l1_gate_fw_w', 'v_l1_gate_fw_b', 'v_l1_gate_bw_w', 'v_l1_gate_bw_b', 'v_l1_o_norm', 'v_l1_w_out', 'v_l1_norm2', 'v_l1_ffn_up', 'v_l1_ffn_conv_w', 'v_l1_ffn_conv_b', 'v_l1_ffn_down', 'v_final_norm']
TWIN_OUTPUTS = ['loss', 'grad_x', 'grad_c_ctx', 'grad_l0_ada_w', 'grad_l0_ada_b', 'grad_l0_norm1', 'grad_l0_w_in', 'grad_l0_conv_a', 'grad_l0_q_norm', 'grad_l0_w_qb', 'grad_l0_kv_norm', 'grad_l0_w_kvb', 'grad_l0_w_out', 'grad_l0_norm2', 'grad_l0_ffn_up', 'grad_l0_ffn_conv_w', 'grad_l0_ffn_conv_b', 'grad_l0_ffn_down', 'grad_l1_ada_w', 'grad_l1_ada_b', 'grad_l1_norm1', 'grad_l1_w_in', 'grad_l1_gate_fw_w', 'grad_l1_gate_fw_b', 'grad_l1_gate_bw_w', 'grad_l1_gate_bw_b', 'grad_l1_o_norm', 'grad_l1_w_out', 'grad_l1_norm2', 'grad_l1_ffn_up', 'grad_l1_ffn_conv_w', 'grad_l1_ffn_conv_b', 'grad_l1_ffn_down', 'grad_final_norm', 'delta_c_ctx', 'delta_l0_ada_w', 'delta_l0_ada_b', 'delta_l0_norm1', 'delta_l0_w_in', 'delta_l0_conv_a', 'delta_l0_q_norm', 'delta_l0_w_qb', 'delta_l0_kv_norm', 'delta_l0_w_kvb', 'delta_l0_w_out', 'delta_l0_norm2', 'delta_l0_ffn_up', 'delta_l0_ffn_conv_w', 'delta_l0_ffn_conv_b', 'delta_l0_ffn_down', 'delta_l1_ada_w', 'delta_l1_ada_b', 'delta_l1_norm1', 'delta_l1_w_in', 'delta_l1_gate_fw_w', 'delta_l1_gate_fw_b', 'delta_l1_gate_bw_w', 'delta_l1_gate_bw_b', 'delta_l1_o_norm', 'delta_l1_w_out', 'delta_l1_norm2', 'delta_l1_ffn_up', 'delta_l1_ffn_conv_w', 'delta_l1_ffn_conv_b', 'delta_l1_ffn_down', 'delta_final_norm', 'new_m_c_ctx', 'new_m_l0_ada_w', 'new_m_l0_ada_b', 'new_m_l0_norm1', 'new_m_l0_w_in', 'new_m_l0_conv_a', 'new_m_l0_q_norm', 'new_m_l0_w_qb', 'new_m_l0_kv_norm', 'new_m_l0_w_kvb', 'new_m_l0_w_out', 'new_m_l0_norm2', 'new_m_l0_ffn_up', 'new_m_l0_ffn_conv_w', 'new_m_l0_ffn_conv_b', 'new_m_l0_ffn_down', 'new_m_l1_ada_w', 'new_m_l1_ada_b', 'new_m_l1_norm1', 'new_m_l1_w_in', 'new_m_l1_gate_fw_w', 'new_m_l1_gate_fw_b', 'new_m_l1_gate_bw_w', 'new_m_l1_gate_bw_b', 'new_m_l1_o_norm', 'new_m_l1_w_out', 'new_m_l1_norm2', 'new_m_l1_ffn_up', 'new_m_l1_ffn_conv_w', 'new_m_l1_ffn_conv_b', 'new_m_l1_ffn_down', 'new_m_final_norm', 'new_v_c_ctx', 'new_v_l0_ada_w', 'new_v_l0_ada_b', 'new_v_l0_norm1', 'new_v_l0_w_in', 'new_v_l0_conv_a', 'new_v_l0_q_norm', 'new_v_l0_w_qb', 'new_v_l0_kv_norm', 'new_v_l0_w_kvb', 'new_v_l0_w_out', 'new_v_l0_norm2', 'new_v_l0_ffn_up', 'new_v_l0_ffn_conv_w', 'new_v_l0_ffn_conv_b', 'new_v_l0_ffn_down', 'new_v_l1_ada_w', 'new_v_l1_ada_b', 'new_v_l1_norm1', 'new_v_l1_w_in', 'new_v_l1_gate_fw_w', 'new_v_l1_gate_fw_b', 'new_v_l1_gate_bw_w', 'new_v_l1_gate_bw_b', 'new_v_l1_o_norm', 'new_v_l1_w_out', 'new_v_l1_norm2', 'new_v_l1_ffn_up', 'new_v_l1_ffn_conv_w', 'new_v_l1_ffn_conv_b', 'new_v_l1_ffn_down', 'new_v_final_norm']
TWIN_LEAF_KINDS = {'loss': 'loss', 'grad_x': 'grad_x', 'grad_c_ctx': 'grad_w', 'grad_l0_ada_w': 'grad_w', 'grad_l0_ada_b': 'grad_w', 'grad_l0_norm1': 'grad_w', 'grad_l0_w_in': 'grad_w', 'grad_l0_conv_a': 'grad_w', 'grad_l0_q_norm': 'grad_w', 'grad_l0_w_qb': 'grad_w', 'grad_l0_kv_norm': 'grad_w', 'grad_l0_w_kvb': 'grad_w', 'grad_l0_w_out': 'grad_w', 'grad_l0_norm2': 'grad_w', 'grad_l0_ffn_up': 'grad_w', 'grad_l0_ffn_conv_w': 'grad_w', 'grad_l0_ffn_conv_b': 'grad_w', 'grad_l0_ffn_down': 'grad_w', 'grad_l1_ada_w': 'grad_w', 'grad_l1_ada_b': 'grad_w', 'grad_l1_norm1': 'grad_w', 'grad_l1_w_in': 'grad_w', 'grad_l1_gate_fw_w': 'grad_w', 'grad_l1_gate_fw_b': 'grad_w', 'grad_l1_gate_bw_w': 'grad_w', 'grad_l1_gate_bw_b': 'grad_w', 'grad_l1_o_norm': 'grad_w', 'grad_l1_w_out': 'grad_w', 'grad_l1_norm2': 'grad_w', 'grad_l1_ffn_up': 'grad_w', 'grad_l1_ffn_conv_w': 'grad_w', 'grad_l1_ffn_conv_b': 'grad_w', 'grad_l1_ffn_down': 'grad_w', 'grad_final_norm': 'grad_w', 'delta_c_ctx': 'delta_w', 'delta_l0_ada_w': 'delta_w', 'delta_l0_ada_b': 'delta_w', 'delta_l0_norm1': 'delta_w', 'delta_l0_w_in': 'delta_w', 'delta_l0_conv_a': 'delta_w', 'delta_l0_q_norm': 'delta_w', 'delta_l0_w_qb': 'delta_w', 'delta_l0_kv_norm': 'delta_w', 'delta_l0_w_kvb': 'delta_w', 'delta_l0_w_out': 'delta_w', 'delta_l0_norm2': 'delta_w', 'delta_l0_ffn_up': 'delta_w', 'delta_l0_ffn_conv_w': 'delta_w', 'delta_l0_ffn_conv_b': 'delta_w', 'delta_l0_ffn_down': 'delta_w', 'delta_l1_ada_w': 'delta_w', 'delta_l1_ada_b': 'delta_w', 'delta_l1_norm1': 'delta_w', 'delta_l1_w_in': 'delta_w', 'delta_l1_gate_fw_w': 'delta_w', 'delta_l1_gate_fw_b': 'delta_w', 'delta_l1_gate_bw_w': 'delta_w', 'delta_l1_gate_bw_b': 'delta_w', 'delta_l1_o_norm': 'delta_w', 'delta_l1_w_out': 'delta_w', 'delta_l1_norm2': 'delta_w', 'delta_l1_ffn_up': 'delta_w', 'delta_l1_ffn_conv_w': 'delta_w', 'delta_l1_ffn_conv_b': 'delta_w', 'delta_l1_ffn_down': 'delta_w', 'delta_final_norm': 'delta_w', 'new_m_c_ctx': 'new_m', 'new_m_l0_ada_w': 'new_m', 'new_m_l0_ada_b': 'new_m', 'new_m_l0_norm1': 'new_m', 'new_m_l0_w_in': 'new_m', 'new_m_l0_conv_a': 'new_m', 'new_m_l0_q_norm': 'new_m', 'new_m_l0_w_qb': 'new_m', 'new_m_l0_kv_norm': 'new_m', 'new_m_l0_w_kvb': 'new_m', 'new_m_l0_w_out': 'new_m', 'new_m_l0_norm2': 'new_m', 'new_m_l0_ffn_up': 'new_m', 'new_m_l0_ffn_conv_w': 'new_m', 'new_m_l0_ffn_conv_b': 'new_m', 'new_m_l0_ffn_down': 'new_m', 'new_m_l1_ada_w': 'new_m', 'new_m_l1_ada_b': 'new_m', 'new_m_l1_norm1': 'new_m', 'new_m_l1_w_in': 'new_m', 'new_m_l1_gate_fw_w': 'new_m', 'new_m_l1_gate_fw_b': 'new_m', 'new_m_l1_gate_bw_w': 'new_m', 'new_m_l1_gate_bw_b': 'new_m', 'new_m_l1_o_norm': 'new_m', 'new_m_l1_w_out': 'new_m', 'new_m_l1_norm2': 'new_m', 'new_m_l1_ffn_up': 'new_m', 'new_m_l1_ffn_conv_w': 'new_m', 'new_m_l1_ffn_conv_b': 'new_m', 'new_m_l1_ffn_down': 'new_m', 'new_m_final_norm': 'new_m', 'new_v_c_ctx': 'new_v', 'new_v_l0_ada_w': 'new_v', 'new_v_l0_ada_b': 'new_v', 'new_v_l0_norm1': 'new_v', 'new_v_l0_w_in': 'new_v', 'new_v_l0_conv_a': 'new_v', 'new_v_l0_q_norm': 'new_v', 'new_v_l0_w_qb': 'new_v', 'new_v_l0_kv_norm': 'new_v', 'new_v_l0_w_kvb': 'new_v', 'new_v_l0_w_out': 'new_v', 'new_v_l0_norm2': 'new_v', 'new_v_l0_ffn_up': 'new_v', 'new_v_l0_ffn_conv_w': 'new_v', 'new_v_l0_ffn_conv_b': 'new_v', 'new_v_l0_ffn_down': 'new_v', 'new_v_l1_ada_w': 'new_v', 'new_v_l1_ada_b': 'new_v', 'new_v_l1_norm1': 'new_v', 'new_v_l1_w_in': 'new_v', 'new_v_l1_gate_fw_w': 'new_v', 'new_v_l1_gate_fw_b': 'new_v', 'new_v_l1_gate_bw_w': 'new_v', 'new_v_l1_gate_bw_b': 'new_v', 'new_v_l1_o_norm': 'new_v', 'new_v_l1_w_out': 'new_v', 'new_v_l1_norm2': 'new_v', 'new_v_l1_ffn_up': 'new_v', 'new_v_l1_ffn_conv_w': 'new_v', 'new_v_l1_ffn_conv_b': 'new_v', 'new_v_l1_ffn_down': 'new_v', 'new_v_final_norm': 'new_v'}


def _forward(args):
    return _fwd_reference(*[args[k] for k in FWD_PARAMS])


def _output_shape():
    def fwd():
        inp = _fwd_setup_inputs(0)
        return _fwd_reference(*[inp[k] for k in FWD_PARAMS])
    out = _jax.eval_shape(fwd)
    return out.shape, out.dtype

N_MICROBATCH = 1
ADAM_LR = 0.001
ADAM_B1 = 0.9
ADAM_B2 = 0.999
ADAM_EPS = 1e-08
ADAM_WD = 0.01
ADAM_STEP = 10
PER_EXAMPLE_BATCH_AXIS = {'x': 0, 'c': 0, 'ctx': 0, 'loss_target': 0}
SHARED_INPUTS = []
_WEIGHT_DTYPES = {'c_ctx': _jnp.float32, 'l0_ada_w': _jnp.float32, 'l0_ada_b': _jnp.float32, 'l0_norm1': _jnp.float32, 'l0_w_in': _jnp.float32, 'l0_conv_a': _jnp.float32, 'l0_q_norm': _jnp.float32, 'l0_w_qb': _jnp.float32, 'l0_kv_norm': _jnp.float32, 'l0_w_kvb': _jnp.float32, 'l0_w_out': _jnp.float32, 'l0_norm2': _jnp.float32, 'l0_ffn_up': _jnp.float32, 'l0_ffn_conv_w': _jnp.float32, 'l0_ffn_conv_b': _jnp.float32, 'l0_ffn_down': _jnp.float32, 'l1_ada_w': _jnp.float32, 'l1_ada_b': _jnp.float32, 'l1_norm1': _jnp.float32, 'l1_w_in': _jnp.float32, 'l1_gate_fw_w': _jnp.float32, 'l1_gate_fw_b': _jnp.float32, 'l1_gate_bw_w': _jnp.float32, 'l1_gate_bw_b': _jnp.float32, 'l1_o_norm': _jnp.float32, 'l1_w_out': _jnp.float32, 'l1_norm2': _jnp.float32, 'l1_ffn_up': _jnp.float32, 'l1_ffn_conv_w': _jnp.float32, 'l1_ffn_conv_b': _jnp.float32, 'l1_ffn_down': _jnp.float32, 'final_norm': _jnp.float32}
MOMENT_SCALE = {'c_ctx': 6.179930e-03, 'l0_ada_w': 4.579580e-02, 'l0_ada_b': 8.042018e-02, 'l0_norm1': 5.937676e-02, 'l0_w_in': 4.367796e-02, 'l0_conv_a': 4.965988e-02, 'l0_q_norm': 5.198359e-03, 'l0_w_qb': 2.988878e-03, 'l0_kv_norm': 2.042911e-02, 'l0_w_kvb': 7.733355e-03, 'l0_w_out': 3.556309e-02, 'l0_norm2': 3.898080e-02, 'l0_ffn_up': 1.692529e-02, 'l0_ffn_conv_w': 1.685931e-02, 'l0_ffn_conv_b': 1.549947e-02, 'l0_ffn_down': 2.759087e-02, 'l1_ada_w': 3.775317e-02, 'l1_ada_b': 6.475297e-02, 'l1_norm1': 4.674582e-02, 'l1_w_in': 2.805630e-02, 'l1_gate_fw_w': 4.271126e-03, 'l1_gate_fw_b': 1.102901e-02, 'l1_gate_bw_w': 4.081730e-03, 'l1_gate_bw_b': 1.153162e-02, 'l1_o_norm': 5.015716e-02, 'l1_w_out': 2.315907e-02, 'l1_norm2': 3.354948e-02, 'l1_ffn_up': 1.529397e-02, 'l1_ffn_conv_w': 1.540647e-02, 'l1_ffn_conv_b': 1.370329e-02, 'l1_ffn_down': 2.508120e-02, 'final_norm': 3.197716e+01}


def _to_microbatches(a, axis):
    t = _jnp.moveaxis(a, axis, 0)
    t = t.reshape((N_MICROBATCH, t.shape[0] // N_MICROBATCH) + t.shape[1:])
    return _jnp.moveaxis(t, 1, axis + 1)


def setup_inputs(seed: int = 0) -> dict:
    inp = _fwd_setup_inputs(seed)
    key = _jax.random.fold_in(_jax.random.key(seed), 7919)
    shape, _ = _output_shape()
    out = dict(inp)
    out["loss_target"] = _jax.random.normal(_jax.random.fold_in(key, 0), shape, _jnp.float32)
    for i, name in enumerate(TWIN_WEIGHTS):
        w = inp[name].astype(_jnp.float32)
        if MOMENT_SCALE is None:
            s = _jnp.sqrt(_jnp.mean(_jnp.square(w)) + 1e-30)
        else:
            s = MOMENT_SCALE[name]
        km, kv = _jax.random.split(_jax.random.fold_in(key, i + 1))
        out[name] = w
        out["m_" + name] = s * _jax.random.normal(km, w.shape, _jnp.float32)
        out["v_" + name] = (s * s) * _jax.random.uniform(kv, w.shape, _jnp.float32, 0.5, 1.5)
    if N_MICROBATCH > 1:
        for name, axis in PER_EXAMPLE_BATCH_AXIS.items():
            out[name] = _to_microbatches(out[name], axis)
    return {'x': out['x'], 'c': out['c'], 'ctx': out['ctx'], 'c_ctx': out['c_ctx'], 'l0_ada_w': out['l0_ada_w'], 'l0_ada_b': out['l0_ada_b'], 'l0_norm1': out['l0_norm1'], 'l0_w_in': out['l0_w_in'], 'l0_conv_a': out['l0_conv_a'], 'l0_q_norm': out['l0_q_norm'], 'l0_w_qb': out['l0_w_qb'], 'l0_kv_norm': out['l0_kv_norm'], 'l0_w_kvb': out['l0_w_kvb'], 'l0_w_out': out['l0_w_out'], 'l0_norm2': out['l0_norm2'], 'l0_ffn_up': out['l0_ffn_up'], 'l0_ffn_conv_w': out['l0_ffn_conv_w'], 'l0_ffn_conv_b': out['l0_ffn_conv_b'], 'l0_ffn_down': out['l0_ffn_down'], 'l1_ada_w': out['l1_ada_w'], 'l1_ada_b': out['l1_ada_b'], 'l1_norm1': out['l1_norm1'], 'l1_w_in': out['l1_w_in'], 'l1_gate_fw_w': out['l1_gate_fw_w'], 'l1_gate_fw_b': out['l1_gate_fw_b'], 'l1_gate_bw_w': out['l1_gate_bw_w'], 'l1_gate_bw_b': out['l1_gate_bw_b'], 'l1_o_norm': out['l1_o_norm'], 'l1_w_out': out['l1_w_out'], 'l1_norm2': out['l1_norm2'], 'l1_ffn_up': out['l1_ffn_up'], 'l1_ffn_conv_w': out['l1_ffn_conv_w'], 'l1_ffn_conv_b': out['l1_ffn_conv_b'], 'l1_ffn_down': out['l1_ffn_down'], 'final_norm': out['final_norm'], 'loss_target': out['loss_target'], 'm_c_ctx': out['m_c_ctx'], 'm_l0_ada_w': out['m_l0_ada_w'], 'm_l0_ada_b': out['m_l0_ada_b'], 'm_l0_norm1': out['m_l0_norm1'], 'm_l0_w_in': out['m_l0_w_in'], 'm_l0_conv_a': out['m_l0_conv_a'], 'm_l0_q_norm': out['m_l0_q_norm'], 'm_l0_w_qb': out['m_l0_w_qb'], 'm_l0_kv_norm': out['m_l0_kv_norm'], 'm_l0_w_kvb': out['m_l0_w_kvb'], 'm_l0_w_out': out['m_l0_w_out'], 'm_l0_norm2': out['m_l0_norm2'], 'm_l0_ffn_up': out['m_l0_ffn_up'], 'm_l0_ffn_conv_w': out['m_l0_ffn_conv_w'], 'm_l0_ffn_conv_b': out['m_l0_ffn_conv_b'], 'm_l0_ffn_down': out['m_l0_ffn_down'], 'm_l1_ada_w': out['m_l1_ada_w'], 'm_l1_ada_b': out['m_l1_ada_b'], 'm_l1_norm1': out['m_l1_norm1'], 'm_l1_w_in': out['m_l1_w_in'], 'm_l1_gate_fw_w': out['m_l1_gate_fw_w'], 'm_l1_gate_fw_b': out['m_l1_gate_fw_b'], 'm_l1_gate_bw_w': out['m_l1_gate_bw_w'], 'm_l1_gate_bw_b': out['m_l1_gate_bw_b'], 'm_l1_o_norm': out['m_l1_o_norm'], 'm_l1_w_out': out['m_l1_w_out'], 'm_l1_norm2': out['m_l1_norm2'], 'm_l1_ffn_up': out['m_l1_ffn_up'], 'm_l1_ffn_conv_w': out['m_l1_ffn_conv_w'], 'm_l1_ffn_conv_b': out['m_l1_ffn_conv_b'], 'm_l1_ffn_down': out['m_l1_ffn_down'], 'm_final_norm': out['m_final_norm'], 'v_c_ctx': out['v_c_ctx'], 'v_l0_ada_w': out['v_l0_ada_w'], 'v_l0_ada_b': out['v_l0_ada_b'], 'v_l0_norm1': out['v_l0_norm1'], 'v_l0_w_in': out['v_l0_w_in'], 'v_l0_conv_a': out['v_l0_conv_a'], 'v_l0_q_norm': out['v_l0_q_norm'], 'v_l0_w_qb': out['v_l0_w_qb'], 'v_l0_kv_norm': out['v_l0_kv_norm'], 'v_l0_w_kvb': out['v_l0_w_kvb'], 'v_l0_w_out': out['v_l0_w_out'], 'v_l0_norm2': out['v_l0_norm2'], 'v_l0_ffn_up': out['v_l0_ffn_up'], 'v_l0_ffn_conv_w': out['v_l0_ffn_conv_w'], 'v_l0_ffn_conv_b': out['v_l0_ffn_conv_b'], 'v_l0_ffn_down': out['v_l0_ffn_down'], 'v_l1_ada_w': out['v_l1_ada_w'], 'v_l1_ada_b': out['v_l1_ada_b'], 'v_l1_norm1': out['v_l1_norm1'], 'v_l1_w_in': out['v_l1_w_in'], 'v_l1_gate_fw_w': out['v_l1_gate_fw_w'], 'v_l1_gate_fw_b': out['v_l1_gate_fw_b'], 'v_l1_gate_bw_w': out['v_l1_gate_bw_w'], 'v_l1_gate_bw_b': out['v_l1_gate_bw_b'], 'v_l1_o_norm': out['v_l1_o_norm'], 'v_l1_w_out': out['v_l1_w_out'], 'v_l1_norm2': out['v_l1_norm2'], 'v_l1_ffn_up': out['v_l1_ffn_up'], 'v_l1_ffn_conv_w': out['v_l1_ffn_conv_w'], 'v_l1_ffn_conv_b': out['v_l1_ffn_conv_b'], 'v_l1_ffn_down': out['v_l1_ffn_down'], 'v_final_norm': out['v_final_norm']}


def _loss(weights, diff, rest, loss_target):
    with _jax.named_scope("forward"):
        args = {**rest, TWIN_DIFF_INPUT: diff, **{k: w.astype(_WEIGHT_DTYPES[k]) for k, w in weights.items()}}
        y = _forward(args)
    with _jax.named_scope("loss_head"):
        err = _jnp.square(y.astype(_jnp.float32) - loss_target)
        return 0.5 * _jnp.sum(_jnp.mean(err, axis=-1)) if err.ndim else 0.5 * err


def _adamw(w, g, m, v):
    m = ADAM_B1 * m + (1.0 - ADAM_B1) * g
    v = ADAM_B2 * v + (1.0 - ADAM_B2) * _jnp.square(g)
    m_hat = m / (1.0 - ADAM_B1 ** ADAM_STEP)
    v_hat = v / (1.0 - ADAM_B2 ** ADAM_STEP)
    delta = -ADAM_LR * (m_hat / (_jnp.sqrt(v_hat) + ADAM_EPS) + ADAM_WD * w)
    return delta, m, v


def reference(x, c, ctx, c_ctx, l0_ada_w, l0_ada_b, l0_norm1, l0_w_in, l0_conv_a, l0_q_norm, l0_w_qb, l0_kv_norm, l0_w_kvb, l0_w_out, l0_norm2, l0_ffn_up, l0_ffn_conv_w, l0_ffn_conv_b, l0_ffn_down, l1_ada_w, l1_ada_b, l1_norm1, l1_w_in, l1_gate_fw_w, l1_gate_fw_b, l1_gate_bw_w, l1_gate_bw_b, l1_o_norm, l1_w_out, l1_norm2, l1_ffn_up, l1_ffn_conv_w, l1_ffn_conv_b, l1_ffn_down, final_norm, loss_target, m_c_ctx, m_l0_ada_w, m_l0_ada_b, m_l0_norm1, m_l0_w_in, m_l0_conv_a, m_l0_q_norm, m_l0_w_qb, m_l0_kv_norm, m_l0_w_kvb, m_l0_w_out, m_l0_norm2, m_l0_ffn_up, m_l0_ffn_conv_w, m_l0_ffn_conv_b, m_l0_ffn_down, m_l1_ada_w, m_l1_ada_b, m_l1_norm1, m_l1_w_in, m_l1_gate_fw_w, m_l1_gate_fw_b, m_l1_gate_bw_w, m_l1_gate_bw_b, m_l1_o_norm, m_l1_w_out, m_l1_norm2, m_l1_ffn_up, m_l1_ffn_conv_w, m_l1_ffn_conv_b, m_l1_ffn_down, m_final_norm, v_c_ctx, v_l0_ada_w, v_l0_ada_b, v_l0_norm1, v_l0_w_in, v_l0_conv_a, v_l0_q_norm, v_l0_w_qb, v_l0_kv_norm, v_l0_w_kvb, v_l0_w_out, v_l0_norm2, v_l0_ffn_up, v_l0_ffn_conv_w, v_l0_ffn_conv_b, v_l0_ffn_down, v_l1_ada_w, v_l1_ada_b, v_l1_norm1, v_l1_w_in, v_l1_gate_fw_w, v_l1_gate_fw_b, v_l1_gate_bw_w, v_l1_gate_bw_b, v_l1_o_norm, v_l1_w_out, v_l1_norm2, v_l1_ffn_up, v_l1_ffn_conv_w, v_l1_ffn_conv_b, v_l1_ffn_down, v_final_norm):
    given = dict(x=x, c=c, ctx=ctx, c_ctx=c_ctx, l0_ada_w=l0_ada_w, l0_ada_b=l0_ada_b, l0_norm1=l0_norm1, l0_w_in=l0_w_in, l0_conv_a=l0_conv_a, l0_q_norm=l0_q_norm, l0_w_qb=l0_w_qb, l0_kv_norm=l0_kv_norm, l0_w_kvb=l0_w_kvb, l0_w_out=l0_w_out, l0_norm2=l0_norm2, l0_ffn_up=l0_ffn_up, l0_ffn_conv_w=l0_ffn_conv_w, l0_ffn_conv_b=l0_ffn_conv_b, l0_ffn_down=l0_ffn_down, l1_ada_w=l1_ada_w, l1_ada_b=l1_ada_b, l1_norm1=l1_norm1, l1_w_in=l1_w_in, l1_gate_fw_w=l1_gate_fw_w, l1_gate_fw_b=l1_gate_fw_b, l1_gate_bw_w=l1_gate_bw_w, l1_gate_bw_b=l1_gate_bw_b, l1_o_norm=l1_o_norm, l1_w_out=l1_w_out, l1_norm2=l1_norm2, l1_ffn_up=l1_ffn_up, l1_ffn_conv_w=l1_ffn_conv_w, l1_ffn_conv_b=l1_ffn_conv_b, l1_ffn_down=l1_ffn_down, final_norm=final_norm, loss_target=loss_target, m_c_ctx=m_c_ctx, m_l0_ada_w=m_l0_ada_w, m_l0_ada_b=m_l0_ada_b, m_l0_norm1=m_l0_norm1, m_l0_w_in=m_l0_w_in, m_l0_conv_a=m_l0_conv_a, m_l0_q_norm=m_l0_q_norm, m_l0_w_qb=m_l0_w_qb, m_l0_kv_norm=m_l0_kv_norm, m_l0_w_kvb=m_l0_w_kvb, m_l0_w_out=m_l0_w_out, m_l0_norm2=m_l0_norm2, m_l0_ffn_up=m_l0_ffn_up, m_l0_ffn_conv_w=m_l0_ffn_conv_w, m_l0_ffn_conv_b=m_l0_ffn_conv_b, m_l0_ffn_down=m_l0_ffn_down, m_l1_ada_w=m_l1_ada_w, m_l1_ada_b=m_l1_ada_b, m_l1_norm1=m_l1_norm1, m_l1_w_in=m_l1_w_in, m_l1_gate_fw_w=m_l1_gate_fw_w, m_l1_gate_fw_b=m_l1_gate_fw_b, m_l1_gate_bw_w=m_l1_gate_bw_w, m_l1_gate_bw_b=m_l1_gate_bw_b, m_l1_o_norm=m_l1_o_norm, m_l1_w_out=m_l1_w_out, m_l1_norm2=m_l1_norm2, m_l1_ffn_up=m_l1_ffn_up, m_l1_ffn_conv_w=m_l1_ffn_conv_w, m_l1_ffn_conv_b=m_l1_ffn_conv_b, m_l1_ffn_down=m_l1_ffn_down, m_final_norm=m_final_norm, v_c_ctx=v_c_ctx, v_l0_ada_w=v_l0_ada_w, v_l0_ada_b=v_l0_ada_b, v_l0_norm1=v_l0_norm1, v_l0_w_in=v_l0_w_in, v_l0_conv_a=v_l0_conv_a, v_l0_q_norm=v_l0_q_norm, v_l0_w_qb=v_l0_w_qb, v_l0_kv_norm=v_l0_kv_norm, v_l0_w_kvb=v_l0_w_kvb, v_l0_w_out=v_l0_w_out, v_l0_norm2=v_l0_norm2, v_l0_ffn_up=v_l0_ffn_up, v_l0_ffn_conv_w=v_l0_ffn_conv_w, v_l0_ffn_conv_b=v_l0_ffn_conv_b, v_l0_ffn_down=v_l0_ffn_down, v_l1_ada_w=v_l1_ada_w, v_l1_ada_b=v_l1_ada_b, v_l1_norm1=v_l1_norm1, v_l1_w_in=v_l1_w_in, v_l1_gate_fw_w=v_l1_gate_fw_w, v_l1_gate_fw_b=v_l1_gate_fw_b, v_l1_gate_bw_w=v_l1_gate_bw_w, v_l1_gate_bw_b=v_l1_gate_bw_b, v_l1_o_norm=v_l1_o_norm, v_l1_w_out=v_l1_w_out, v_l1_norm2=v_l1_norm2, v_l1_ffn_up=v_l1_ffn_up, v_l1_ffn_conv_w=v_l1_ffn_conv_w, v_l1_ffn_conv_b=v_l1_ffn_conv_b, v_l1_ffn_down=v_l1_ffn_down, v_final_norm=v_final_norm)
    weights = {n: given[n] for n in TWIN_WEIGHTS}
    shared = {n: given[n] for n in SHARED_INPUTS}
    per_example = {n: given[n] for n in ['x', 'c', 'ctx']}
    grad_fn = _jax.value_and_grad(_loss, argnums=(0, 1))

    def one_microbatch(ex, loss_target):
        ex = dict(ex)
        diff = ex.pop(TWIN_DIFF_INPUT)
        return grad_fn(weights, diff, {**shared, **ex}, loss_target)

    if N_MICROBATCH == 1:
        loss, (grad_w, grad_x) = one_microbatch(per_example, given["loss_target"])
    else:
        def body(carry, xs):
            loss_sum, grad_sum = carry
            l_k, (gw_k, gx_k) = one_microbatch(xs[0], xs[1])
            with _jax.named_scope("update"):
                return (loss_sum + l_k, _jax.tree.map(_jnp.add, grad_sum, gw_k)), gx_k

        init = (_jnp.zeros((), _jnp.float32), _jax.tree.map(_jnp.zeros_like, weights))
        (loss, grad_w), grad_x = _jax.lax.scan(body, init, (per_example, given["loss_target"]))
    with _jax.named_scope("update"):
        delta_w, new_m, new_v = {}, {}, {}
        for n in TWIN_WEIGHTS:
            delta_w[n], new_m[n], new_v[n] = _adamw(weights[n], grad_w[n], given["m_" + n], given["v_" + n])
    return (loss, grad_x, *[grad_w[n] for n in TWIN_WEIGHTS], *[delta_w[n] for n in TWIN_WEIGHTS],
            *[new_m[n] for n in TWIN_WEIGHTS], *[new_v[n] for n in TWIN_WEIGHTS])
```

```python
import functools
import types

import jax
import jax.numpy as jnp
from jax import lax
from jax.experimental import pallas as pl
from jax.experimental.pallas import tpu as pltpu

D_MODEL = 2048
SEQ = 8192
GRID_W = 64
CTX_LEN = 256
EPS = 1e-6
N_MOD = 6
MLA_HEADS = 8
QK_NOPE = 128
QK_ROPE = 64
V_HEAD = 128
Q_LORA = 512
KV_LORA = 256
ROPE_THETA = 10000.0
GLA_HEADS = 4
GATE_RANK = 16
GATE_NORMALIZER = 16.0
CHUNK = 64
D_FF = 5632
ADAM_LR = 0.001
ADAM_B1 = 0.9
ADAM_B2 = 0.999
ADAM_EPS = 1e-08
ADAM_WD = 0.01
ADAM_STEP = 10

N_DEV = 8
LANE = 128
VMEM_LIMIT = 56 * 1024 * 1024

F32 = jnp.float32
BF16 = jnp.bfloat16
MESH_ID = pl.DeviceIdType.MESH

_FWD = ['x', 'c', 'ctx', 'c_ctx', 'l0_ada_w', 'l0_ada_b', 'l0_norm1', 'l0_w_in', 'l0_conv_a', 'l0_q_norm', 'l0_w_qb',
        'l0_kv_norm', 'l0_w_kvb', 'l0_w_out', 'l0_norm2', 'l0_ffn_up', 'l0_ffn_conv_w', 'l0_ffn_conv_b', 'l0_ffn_down',
        'l1_ada_w', 'l1_ada_b', 'l1_norm1', 'l1_w_in', 'l1_gate_fw_w', 'l1_gate_fw_b', 'l1_gate_bw_w', 'l1_gate_bw_b',
        'l1_o_norm', 'l1_w_out', 'l1_norm2', 'l1_ffn_up', 'l1_ffn_conv_w', 'l1_ffn_conv_b', 'l1_ffn_down', 'final_norm']
_WEIGHTS = _FWD[3:]
_ARGS = _FWD + ['loss_target'] + ['m_' + n for n in _WEIGHTS] + ['v_' + n for n in _WEIGHTS]


def _dims():
    d = types.SimpleNamespace()
    d.D, d.T, d.TC = D_MODEL, SEQ, CTX_LEN
    d.NT = d.T + d.TC
    d.rt = 256 if d.TC % 256 == 0 else 128
    d.nlt = d.T // d.rt
    d.H = MLA_HEADS
    d.QL, d.KVL = Q_LORA, KV_LORA
    d.CC = D_MODEL // 2
    d.z0_kv = d.QL
    d.z0_kr = d.QL + d.KVL
    d.z0_pad = (-(d.QL + d.KVL + LANE)) % d.CC
    d.z0_ax = d.QL + d.KVL + LANE + d.z0_pad
    d.ZW0 = d.z0_ax + 3 * d.CC
    d.AB_COLS = d.KVL + QK_ROPE + d.QL + 3 * d.CC
    d.HQ = d.H * LANE
    d.GH = GLA_HEADS
    d.KEY = D_MODEL // 2
    d.VAL = D_MODEL
    d.dk = d.KEY // d.GH
    d.dv = d.VAL // d.GH
    d.z1_og = d.VAL
    d.z1_k = 2 * d.VAL
    d.z1_q = 2 * d.VAL + d.KEY
    d.z1_lr = 2 * d.VAL + 2 * d.KEY
    d.ZW1 = d.z1_lr + LANE
    d.GLA_COLS = 2 * d.KEY + 2 * d.VAL + 2 * GATE_RANK
    d.FF = D_FF
    d.tq = min(256, d.TC)
    return d


def _tile(n, pref, align=LANE):
    if n <= pref:
        return n
    t = (pref // align) * align
    while t >= align:
        if n % t == 0:
            return t
        t -= align
    raise ValueError(f"no tile for {n}")


def _cparams(sem):
    return pltpu.CompilerParams(dimension_semantics=sem, vmem_limit_bytes=VMEM_LIMIT)


def _dot(a, b, mode):
    dims = {'nn': (((1,), (0,)), ((), ())), 'nt': (((1,), (1,)), ((), ())), 'tn': (((0,), (0,)), ((), ()))}[mode]
    return lax.dot_general(a.astype(BF16), b.astype(BF16), dims, preferred_element_type=F32)


def _mm(pairs, mode, out_dtype, name, tm=768, tn=1024, tk=1024):
    a0, b0 = pairs[0]
    if mode == 'nn':
        (m, k), n = a0.shape, b0.shape[1]
    elif mode == 'nt':
        (m, k), n = a0.shape, b0.shape[0]
    else:
        (k, m), n = a0.shape, b0.shape[1]
    tm, tn, tk = _tile(m, tm), _tile(n, tn), _tile(k, tk)
    nk = k // tk
    if mode == 'nn':
        a_spec = pl.BlockSpec((tm, tk), lambda i, j, kk: (i, kk))
        b_spec = pl.BlockSpec((tk, tn), lambda i, j, kk: (kk, j))
    elif mode == 'nt':
        a_spec = pl.BlockSpec((tm, tk), lambda i, j, kk: (i, kk))
        b_spec = pl.BlockSpec((tn, tk), lambda i, j, kk: (j, kk))
    else:
        a_spec = pl.BlockSpec((tk, tm), lambda i, j, kk: (kk, i))
        b_spec = pl.BlockSpec((tk, tn), lambda i, j, kk: (kk, j))
    npairs = len(pairs)

    def body(*refs):
        o_ref, acc_ref = refs[-2], refs[-1]
        kk = pl.program_id(2)

        @pl.when(kk == 0)
        def _():
            acc_ref[...] = jnp.zeros_like(acc_ref)

        s = None
        for p in range(npairs):
            t = _dot(refs[2 * p][...], refs[2 * p + 1][...], mode)
            s = t if s is None else s + t
        acc_ref[...] += s

        @pl.when(kk == nk - 1)
        def _():
            o_ref[...] = acc_ref[...].astype(o_ref.dtype)

    flat = [t for ab in pairs for t in ab]
    return pl.pallas_call(
        body, name=name,
        out_shape=jax.ShapeDtypeStruct((m, n), out_dtype),
        grid=(m // tm, n // tn, nk),
        in_specs=[a_spec, b_spec] * npairs,
        out_specs=pl.BlockSpec((tm, tn), lambda i, j, kk: (i, j)),
        scratch_shapes=[pltpu.VMEM((tm, tn), F32)],
        compiler_params=_cparams(("parallel", "parallel", "arbitrary")),
    )(*flat)


def _rowwise(body, nrows, rt, nlt, ins, outs, name, ncol=1):
    ntiles = nrows // rt
    in_specs, args = [], []
    for spec in ins:
        kind, arr = spec[0], spec[1]
        if kind == 'row':
            in_specs.append(pl.BlockSpec((rt, spec[3]), functools.partial(lambda i, j, cb: (i, cb), cb=spec[2])))
        elif kind == 'rowc':
            in_specs.append(pl.BlockSpec((rt, spec[3]), functools.partial(lambda i, j, cb: (i, cb + j), cb=spec[2])))
        elif kind == 'rowm':
            in_specs.append(pl.BlockSpec(
                (rt, spec[3]), functools.partial(lambda i, j, cb, md: (i, cb + j % md), cb=spec[2], md=spec[4])))
        elif kind == 'rowclamp':
            in_specs.append(pl.BlockSpec(
                (rt, spec[3]), functools.partial(lambda i, j, mb: (jnp.minimum(i, mb), 0), mb=spec[2])))
        elif kind == 'row3':
            in_specs.append(pl.BlockSpec((arr.shape[0], rt, arr.shape[2]), lambda i, j: (0, i, 0)))
        elif kind == 'seg':
            in_specs.append(pl.BlockSpec((None, 1, arr.shape[2]), lambda i, j: (i // nlt, 0, 0)))
        else:
            in_specs.append(pl.BlockSpec(arr.shape, functools.partial(lambda i, j, nd: (0,) * nd, nd=arr.ndim)))
        args.append(arr)
    out_shapes, out_specs = [], []
    for spec in outs:
        kind = spec[0]
        if kind == 'row':
            out_shapes.append(jax.ShapeDtypeStruct((nrows, spec[1]), spec[2]))
            out_specs.append(pl.BlockSpec((rt, spec[1]), lambda i, j: (i, 0)))
        elif kind == 'rowc':
            out_shapes.append(jax.ShapeDtypeStruct((nrows, spec[1]), spec[3]))
            out_specs.append(pl.BlockSpec((rt, spec[2]), lambda i, j: (i, j)))
        elif kind == 'acc':
            out_shapes.append(jax.ShapeDtypeStruct(spec[1], F32))
            out_specs.append(pl.BlockSpec(spec[1], functools.partial(lambda i, j, nd: (0,) * nd, nd=len(spec[1]))))
        else:
            out_shapes.append(jax.ShapeDtypeStruct((2, 1, spec[1]), F32))
            out_specs.append(pl.BlockSpec((None, 1, spec[1]), lambda i, j: (i // nlt, 0, 0)))
    n_in = len(ins)
    has_acc = any(s[0] in ('acc', 'segacc') for s in outs)

    def kern(*refs):
        i = pl.program_id(0)
        j = pl.program_id(1)
        vals = [r[...] for r in refs[:n_in]]
        res = body(i, j, *vals)
        for spec, ref, val in zip(outs, refs[n_in:], res):
            if spec[0] in ('row', 'rowc'):
                ref[...] = val.astype(ref.dtype)
            else:
                first = (i == 0) if spec[0] == 'acc' else jnp.logical_or(i == 0, i == nlt)

                @pl.when(first)
                def _(ref=ref, val=val):
                    ref[...] = val

                @pl.when(jnp.logical_not(first))
                def _(ref=ref, val=val):
                    ref[...] += val

    return pl.pallas_call(
        kern, name=name, out_shape=tuple(out_shapes), grid=(ntiles, ncol),
        in_specs=in_specs, out_specs=tuple(out_specs),
        compiler_params=_cparams(("arbitrary", "arbitrary") if has_acc else ("parallel", "parallel")),
    )(*args)


def _small(body, args, out_shapes, name):
    n_in = len(args)

    def kern(*refs):
        res = body(*[r[...] for r in refs[:n_in]])
        for ref, val in zip(refs[n_in:], res):
            ref[...] = val.astype(ref.dtype)

    return pl.pallas_call(
        kern, name=name, out_shape=tuple(out_shapes),
        in_specs=[pl.BlockSpec(memory_space=pltpu.VMEM)] * n_in,
        out_specs=tuple(pl.BlockSpec(memory_space=pltpu.VMEM) for _ in out_shapes),
        compiler_params=pltpu.CompilerParams(vmem_limit_bytes=VMEM_LIMIT),
    )(*args)


def _exchange(x, stacked, name):
    shp = x.shape[1:] if stacked else x.shape

    def body(x_ref, o_ref, send_sems, recv_sems, local_sem):
        ix, iy, ic = lax.axis_index("x"), lax.axis_index("y"), lax.axis_index("c")
        me = 4 * ix + 2 * iy + ic

        def src(p):
            return x_ref.at[p] if stacked else x_ref

        mine = pltpu.make_async_copy(src(me), o_ref.at[me], local_sem)
        mine.start()
        sends, recvs = [], []
        for k in range(1, N_DEV):
            px, py, pc = (ix + ((k >> 2) & 1)) % 2, (iy + ((k >> 1) & 1)) % 2, (ic + (k & 1)) % 2
            peer = 4 * px + 2 * py + pc
            sends.append(pltpu.make_async_remote_copy(
                src_ref=src(peer), dst_ref=o_ref.at[me], send_sem=send_sems.at[k - 1], recv_sem=recv_sems.at[k - 1],
                device_id=(px, py, pc), device_id_type=MESH_ID))
            recvs.append(pltpu.make_async_remote_copy(
                src_ref=src(peer), dst_ref=o_ref.at[peer], send_sem=send_sems.at[k - 1], recv_sem=recv_sems.at[k - 1],
                device_id=(px, py, pc), device_id_type=MESH_ID))
        for cp in sends:
            cp.start()
        for cp in recvs:
            cp.wait_recv()
        for cp in sends:
            cp.wait_send()
        mine.wait()

    return pl.pallas_call(
        body, name=name,
        out_shape=jax.ShapeDtypeStruct((N_DEV,) + tuple(shp), x.dtype),
        in_specs=[pl.BlockSpec(memory_space=pl.ANY)],
        out_specs=pl.BlockSpec(memory_space=pl.ANY),
        scratch_shapes=[pltpu.SemaphoreType.DMA((N_DEV - 1,)), pltpu.SemaphoreType.DMA((N_DEV - 1,)),
                        pltpu.SemaphoreType.DMA(())],
    )(x)


def _cast_bf16(x, name):
    r, c = x.shape
    tr = _tile(r, 256, 8)
    return pl.pallas_call(
        lambda x_ref, o_ref: o_ref.__setitem__(Ellipsis, x_ref[...].astype(BF16)), name=name,
        out_shape=jax.ShapeDtypeStruct((r, c), BF16), grid=(r // tr,),
        in_specs=[pl.BlockSpec((tr, c), lambda i: (i, 0))], out_specs=pl.BlockSpec((tr, c), lambda i: (i, 0)),
        compiler_params=_cparams(("parallel",)),
    )(x)


def _adam(parts, w, m, v, name):
    p, r, c = parts.shape
    tr = _tile(r, 64, 8)

    def body(p_ref, w_ref, m_ref, v_ref, g_ref, d_ref, nm_ref, nv_ref):
        g = p_ref[0]
        for q in range(1, p):
            g = g + p_ref[q]
        nm = ADAM_B1 * m_ref[...] + (1.0 - ADAM_B1) * g
        nv = ADAM_B2 * v_ref[...] + (1.0 - ADAM_B2) * (g * g)
        m_hat = nm / (1.0 - ADAM_B1 ** ADAM_STEP)
        v_hat = nv / (1.0 - ADAM_B2 ** ADAM_STEP)
        g_ref[...] = g
        d_ref[...] = -ADAM_LR * (m_hat / (jnp.sqrt(v_hat) + ADAM_EPS) + ADAM_WD * w_ref[...])
        nm_ref[...] = nm
        nv_ref[...] = nv

    spec = pl.BlockSpec((tr, c), lambda i: (i, 0))
    return pl.pallas_call(
        body, name=name, out_shape=tuple(jax.ShapeDtypeStruct((r, c), F32) for _ in range(4)), grid=(r // tr,),
        in_specs=[pl.BlockSpec((p, tr, c), lambda i: (0, i, 0)), spec, spec, spec], out_specs=(spec,) * 4,
        compiler_params=_cparams(("parallel",)),
    )(parts, w, m, v)


def _sum_parts(parts, name):
    p, r, c = parts.shape
    tr = _tile(r, 256, 8)

    def body(p_ref, o_ref):
        g = p_ref[0]
        for q in range(1, p):
            g = g + p_ref[q]
        o_ref[...] = g

    return pl.pallas_call(
        body, name=name, out_shape=jax.ShapeDtypeStruct((r, c), F32), grid=(r // tr,),
        in_specs=[pl.BlockSpec((p, tr, c), lambda i: (0, i, 0))], out_specs=pl.BlockSpec((tr, c), lambda i: (i, 0)),
        compiler_params=_cparams(("parallel",)),
    )(parts)


def _shifted(cur, prev8, next8, i, rt, nlt, ntiles):
    first = jnp.logical_or(i == 0, i == nlt)
    last = jnp.logical_or(i == nlt - 1, i == ntiles - 1)
    prev_row = jnp.where(first, 0.0, prev8[7:8, :])
    next_row = jnp.where(last, 0.0, next8[0:1, :])
    rows = lax.broadcasted_iota(jnp.int32, (rt, 1), 0)
    x_m1 = jnp.where(rows == 0, prev_row, pltpu.roll(cur, 1, 0))
    x_p1 = jnp.where(rows == rt - 1, next_row, pltpu.roll(cur, rt - 1, 0))
    return x_m1, x_p1


def _halo_specs(rt, tc, nrows, col_axis_first):
    r8 = rt // 8
    last8 = nrows // 8 - 1
    if col_axis_first:
        return [pl.BlockSpec((8, tc), lambda j, i: (jnp.maximum(i * r8 - 1, 0), j)),
                pl.BlockSpec((rt, tc), lambda j, i: (i, j)),
                pl.BlockSpec((8, tc), lambda j, i: (jnp.minimum((i + 1) * r8, last8), j))]
    return [pl.BlockSpec((8, tc), lambda i, j: (jnp.maximum(i * r8 - 1, 0), j)),
            pl.BlockSpec((rt, tc), lambda i, j: (i, j)),
            pl.BlockSpec((8, tc), lambda i, j: (jnp.minimum((i + 1) * r8, last8), j))]


def _dwconv(x, w8, d, out_dtype, name):
    nrows, c = x.shape
    rt, nlt = d.rt, d.nlt
    tc = _tile(c, 512)
    ntiles = nrows // rt

    def body(p_ref, c_ref, n_ref, w_ref, o_ref):
        i = pl.program_id(0)
        cur = c_ref[...]
        x_m1, x_p1 = _shifted(cur, p_ref[...], n_ref[...], i, rt, nlt, ntiles)
        w = w_ref[...]
        o_ref[...] = (x_m1 * w[0:1] + cur * w[1:2] + x_p1 * w[2:3] + w[3:4]).astype(o_ref.dtype)

    return pl.pallas_call(
        body, name=name, out_shape=jax.ShapeDtypeStruct((nrows, c), out_dtype), grid=(ntiles, c // tc),
        in_specs=_halo_specs(rt, tc, nrows, False) + [pl.BlockSpec((8, tc), lambda i, j: (0, j))],
        out_specs=pl.BlockSpec((rt, tc), lambda i, j: (i, j)),
        compiler_params=_cparams(("parallel", "parallel")),
    )(x, x, x, w8)


def _dwconv_wgrad(x, dy, d, name):
    nrows, c = x.shape
    rt, nlt = d.rt, d.nlt
    tc = _tile(c, 512)
    ntiles = nrows // rt

    def body(p_ref, c_ref, n_ref, dy_ref, o_ref):
        i = pl.program_id(1)
        cur = c_ref[...]
        dy = dy_ref[...]
        x_m1, x_p1 = _shifted(cur, p_ref[...], n_ref[...], i, rt, nlt, ntiles)
        sums = [jnp.sum(t * dy, axis=0, keepdims=True) for t in (x_m1, cur, x_p1)] + [jnp.sum(dy, axis=0, keepdims=True)]
        row = lax.broadcasted_iota(jnp.int32, (8, 1), 0)
        part = jnp.zeros((8, tc), F32)
        for k, s in enumerate(sums):
            part = jnp.where(row == k, s, part)

        @pl.when(i == 0)
        def _():
            o_ref[...] = part

        @pl.when(i != 0)
        def _():
            o_ref[...] += part

    return pl.pallas_call(
        body, name=name, out_shape=jax.ShapeDtypeStruct((8, c), F32), grid=(c // tc, ntiles),
        in_specs=_halo_specs(rt, tc, nrows, True) + [pl.BlockSpec((rt, tc), lambda j, i: (i, j))],
        out_specs=pl.BlockSpec((8, tc), lambda j, i: (0, j)),
        compiler_params=_cparams(("parallel", "arbitrary")),
    )(x, x, x, dy)


def _w8(w3, b=None):
    c = w3.shape[1]
    brow = jnp.zeros((1, c), F32) if b is None else b.reshape(1, c)
    return jnp.concatenate([w3, brow, jnp.zeros((4, c), F32)], axis=0)


def _rms(x, w):
    r = lax.rsqrt(jnp.mean(x * x, axis=-1, keepdims=True) + EPS)
    xh = x * r
    return xh * w, xh, r


def _rms_bwd(dy, xh, r, w):
    dxh = dy * w
    dx = r * (dxh - xh * jnp.mean(dxh * xh, axis=-1, keepdims=True))
    return dx, jnp.sum(dy * xh, axis=0, keepdims=True)


def _mod_bwd(dh, x, w, shift, scale):
    n, xh, r = _rms(x, w)
    dshift = jnp.sum(dh, axis=0, keepdims=True)
    dscale = jnp.sum(dh * n, axis=0, keepdims=True)
    dx, dw = _rms_bwd(dh * (1.0 + scale), xh, r, w)
    return dx, dw, dshift, dscale


def _sigmoid(x):
    return 1.0 / (1.0 + jnp.exp(-x))


def _silu(x):
    return x * _sigmoid(x)


def _dsilu(x):
    s = _sigmoid(x)
    return s * (1.0 + x * (1.0 - s))


def _rope(x, cos, sin_s):
    return x * cos + pltpu.roll(x, LANE // 2, 1) * sin_s


def _rope_t(dy, cos, sin_s):
    return dy * cos + pltpu.roll(dy * sin_s, LANE // 2, 1)


def _attn_valid(d, q_tile, k_tile, tq, tk):
    return jnp.logical_or(q_tile < d.T // tq, k_tile >= d.T // tk)


def _flash_fwd(q, kv, kr, d):
    nt, h, tq = d.NT, d.H, d.tq
    tk = tq
    nk = nt // tk
    scale = float(QK_NOPE + QK_ROPE) ** -0.5

    def body(qn_ref, qr_ref, kn_ref, kr_ref, v_ref, o_ref, ob_ref, lse_ref):
        qi = pl.program_id(1)
        qn, qr = qn_ref[...], qr_ref[...]

        def step(kt, carry):
            m, l, acc = carry
            ks = pl.ds(pl.multiple_of(kt * tk, tk), tk)
            s = (_dot(qn, kn_ref[ks, :], 'nt') + _dot(qr, kr_ref[ks, :], 'nt')) * scale
            valid = _attn_valid(d, qi, kt, tq, tk)
            s = jnp.where(valid, s, -1e30)
            m_new = jnp.maximum(m, jnp.max(s, axis=1, keepdims=True))
            p = jnp.where(valid, jnp.exp(s - m_new), 0.0)
            alpha = jnp.exp(m - m_new)
            l = alpha * l + jnp.sum(p, axis=1, keepdims=True)
            acc = alpha * acc + _dot(p, v_ref[ks, :], 'nn')
            return m_new, l, acc

        init = (jnp.full((tq, 1), -1e30, F32), jnp.zeros((tq, 1), F32), jnp.zeros((tq, LANE), F32))
        m, l, acc = lax.fori_loop(0, nk, step, init)
        o = acc / l
        o_ref[...] = o
        ob_ref[...] = o.astype(BF16)
        lse_ref[...] = jnp.broadcast_to(m + jnp.log(l), (tq, LANE))

    qspec = lambda off: pl.BlockSpec((tq, LANE), functools.partial(lambda hh, i, off: (i, hh + off), off=off))
    res = lambda off: pl.BlockSpec((nt, LANE), functools.partial(lambda hh, i, off: (0, hh + off), off=off))
    out = pl.BlockSpec((tq, LANE), lambda hh, i: (i, hh))
    return pl.pallas_call(
        body, name="flash_fwd",
        out_shape=(jax.ShapeDtypeStruct((nt, d.HQ), F32), jax.ShapeDtypeStruct((nt, d.HQ), BF16),
                   jax.ShapeDtypeStruct((nt, d.HQ), F32)),
        grid=(h, nt // tq),
        in_specs=[qspec(0), qspec(h), res(0), pl.BlockSpec((nt, LANE), lambda hh, i: (0, 0)), res(h)],
        out_specs=(out, out, out),
        compiler_params=_cparams(("parallel", "parallel")),
    )(q, q, kv, kr, kv)


def _flash_dq(q, kv, kr, do, lse, delta, d):
    nt, h, tq = d.NT, d.H, d.tq
    tk = tq
    nk = nt // tk
    scale = float(QK_NOPE + QK_ROPE) ** -0.5

    def body(qn_ref, qr_ref, kn_ref, kr_ref, v_ref, do_ref, lse_ref, dl_ref, dqn_ref, dqr_ref):
        qi = pl.program_id(1)
        qn, qr, do_t = qn_ref[...], qr_ref[...], do_ref[...]
        lse_t, dl_t = lse_ref[:, 0:1], dl_ref[:, 0:1]

        def step(kt, carry):
            dqn, dqr = carry
            ks = pl.ds(pl.multiple_of(kt * tk, tk), tk)
            kn, krr, v = kn_ref[ks, :], kr_ref[ks, :], v_ref[ks, :]
            s = (_dot(qn, kn, 'nt') + _dot(qr, krr, 'nt')) * scale
            p = jnp.where(_attn_valid(d, qi, kt, tq, tk), jnp.exp(s - lse_t), 0.0)
            ds = p * (_dot(do_t, v, 'nt') - dl_t) * scale
            return dqn + _dot(ds, kn, 'nn'), dqr + _dot(ds, krr, 'nn')

        dqn, dqr = lax.fori_loop(0, nk, step, (jnp.zeros((tq, LANE), F32), jnp.zeros((tq, LANE), F32)))
        dqn_ref[...] = dqn
        dqr_ref[...] = dqr

    qspec = lambda off: pl.BlockSpec((tq, LANE), functools.partial(lambda hh, i, off: (i, hh + off), off=off))
    res = lambda off: pl.BlockSpec((nt, LANE), functools.partial(lambda hh, i, off: (0, hh + off), off=off))
    out = pl.BlockSpec((tq, LANE), lambda hh, i: (i, hh))
    return pl.pallas_call(
        body, name="flash_dq",
        out_shape=(jax.ShapeDtypeStruct((nt, d.HQ), F32), jax.ShapeDtypeStruct((nt, d.HQ), F32)),
        grid=(h, nt // tq),
        in_specs=[qspec(0), qspec(h), res(0), pl.BlockSpec((nt, LANE), lambda hh, i: (0, 0)), res(h),
                  qspec(0), qspec(0), qspec(0)],
        out_specs=(out, out),
        compiler_params=_cparams(("parallel", "parallel")),
    )(q, q, kv, kr, kv, do, lse, delta)


def _flash_dkv(q, kv, kr, do, lse, delta, d):
    nt, h, tq = d.NT, d.H, d.tq
    tk = tq
    nq = nt // tq
    scale = float(QK_NOPE + QK_ROPE) ** -0.5

    def body(qn_ref, qr_ref, kn_ref, kr_ref, v_ref, do_ref, lse_ref, dl_ref, dkn_ref, dv_ref, dkr_ref):
        kt = pl.program_id(1)
        kn, krr, v = kn_ref[...], kr_ref[...], v_ref[...]

        def step(qt, carry):
            dkn, dkr, dv = carry
            qs = pl.ds(pl.multiple_of(qt * tq, tq), tq)
            qn, qr, do_t = qn_ref[qs, :], qr_ref[qs, :], do_ref[qs, :]
            lse_t, dl_t = lse_ref[qs, :][:, 0:1], dl_ref[qs, :][:, 0:1]
            s = (_dot(qn, kn, 'nt') + _dot(qr, krr, 'nt')) * scale
            p = jnp.where(_attn_valid(d, qt, kt, tq, tk), jnp.exp(s - lse_t), 0.0)
            ds = p * (_dot(do_t, v, 'nt') - dl_t) * scale
            return dkn + _dot(ds, qn, 'tn'), dkr + _dot(ds, qr, 'tn'), dv + _dot(p, do_t, 'tn')

        z = jnp.zeros((tk, LANE), F32)
        dkn, dkr, dv = lax.fori_loop(0, nq, step, (z, z, z))
        dkn_ref[...] = dkn.astype(BF16)
        dv_ref[...] = dv.astype(BF16)
        dkr_ref[...] = dkr

    res = lambda off: pl.BlockSpec((nt, LANE), functools.partial(lambda hh, i, off: (0, hh + off), off=off))
    kspec = lambda off: pl.BlockSpec((tk, LANE), functools.partial(lambda hh, i, off: (i, hh + off), off=off))
    out = pl.BlockSpec((tk, LANE), lambda hh, i: (i, hh))
    return pl.pallas_call(
        body, name="flash_dkv",
        out_shape=(jax.ShapeDtypeStruct((nt, d.HQ), BF16), jax.ShapeDtypeStruct((nt, d.HQ), BF16),
                   jax.ShapeDtypeStruct((h, nt, LANE), F32)),
        grid=(h, nt // tk),
        in_specs=[res(0), res(h), kspec(0), pl.BlockSpec((tk, LANE), lambda hh, i: (i, 0)), kspec(h),
                  res(0), res(0), res(0)],
        out_specs=(out, out, pl.BlockSpec((None, tk, LANE), lambda hh, i: (hh, i, 0))),
        compiler_params=_cparams(("parallel", "parallel")),
    )(q, q, kv, kr, kv, do, lse, delta)


def _tri(dirn):
    r = lax.broadcasted_iota(jnp.int32, (CHUNK, CHUNK), 0)
    c = lax.broadcasted_iota(jnp.int32, (CHUNK, CHUNK), 1)
    return (c <= r) if dirn == 0 else (c >= r)


def _exact_mask_dot(mask_bf16, x):
    hi = x.astype(BF16)
    r1 = x - hi.astype(F32)
    mid = r1.astype(BF16)
    lo = (r1 - mid.astype(F32)).astype(BF16)
    dot = lambda t: lax.dot_general(mask_bf16, t, (((1,), (0,)), ((), ())), preferred_element_type=F32)
    return dot(hi) + dot(mid) + dot(lo)


def _gla_terms(q, k, g, dirn, dk):
    mb = _tri(dirn)
    b = _exact_mask_dot(mb.astype(BF16), g)
    tot = jnp.sum(g, axis=0, keepdims=True)
    qe = q * (float(dk) ** -0.5) * jnp.exp(b)
    ke = k * jnp.exp(-b)
    kd = k * jnp.exp(tot - b)
    att = jnp.where(mb, _dot(qe, ke, 'nt'), 0.0)
    return mb, b, tot, qe, ke, kd, att


def _gla_chunk_index(d, dirn):
    s_all = d.NT // CHUNK
    ncl = d.T // CHUNK
    if dirn == 0:
        return lambda s: (s + ncl) % s_all
    return lambda s: s_all - 1 - s


def _gla_fwd(z, g, dirn, d):
    nt, gh, dk, dv = d.NT, d.GH, d.dk, d.dv
    s_all = nt // CHUNK
    cidx = _gla_chunk_index(d, dirn)
    qb, kb, gb = d.z1_q // dk, d.z1_k // dk, dirn * gh

    def body(q_ref, k_ref, v_ref, g_ref, o_ref, st_ref, state):
        @pl.when(pl.program_id(1) == 0)
        def _():
            state[...] = jnp.zeros_like(state)

        st = state[...]
        st_ref[...] = st
        v = v_ref[...]
        _, _, tot, qe, _, kd, att = _gla_terms(q_ref[...], k_ref[...], g_ref[...], dirn, dk)
        o_ref[...] = _dot(att, v, 'nn') + _dot(qe, st, 'nt')
        state[...] = st * jnp.exp(tot) + _dot(v, kd, 'tn')

    col = lambda w, off: pl.BlockSpec((CHUNK, w), functools.partial(lambda hh, s, off: (cidx(s), off + hh), off=off))
    return pl.pallas_call(
        body, name=f"gla_fwd_{dirn}",
        out_shape=(jax.ShapeDtypeStruct((nt, d.VAL), F32), jax.ShapeDtypeStruct((gh, s_all, dv, dk), F32)),
        grid=(gh, s_all),
        in_specs=[col(dk, qb), col(dk, kb), col(dv, 0), col(dk, gb)],
        out_specs=(col(dv, 0), pl.BlockSpec((None, None, dv, dk), lambda hh, s: (hh, s, 0, 0))),
        scratch_shapes=[pltpu.VMEM((dv, dk), F32)],
        compiler_params=_cparams(("parallel", "arbitrary")),
    )(z, z, z, g)


def _gla_bwd(z, g, do, states, dirn, d):
    nt, gh, dk, dv = d.NT, d.GH, d.dk, d.dv
    s_all = nt // CHUNK
    cfwd = _gla_chunk_index(d, dirn)
    cidx = lambda s: cfwd(s_all - 1 - s)
    qb, kb, gb = d.z1_q // dk, d.z1_k // dk, dirn * gh
    qscale = float(dk) ** -0.5

    def body(q_ref, k_ref, v_ref, g_ref, do_ref, st_ref, dq_ref, dk_ref, dv_ref, dg_ref, dstate):
        @pl.when(pl.program_id(1) == 0)
        def _():
            dstate[...] = jnp.zeros_like(dstate)

        q, k, v, g_, dout, st, dst = q_ref[...], k_ref[...], v_ref[...], g_ref[...], do_ref[...], st_ref[...], dstate[...]
        mb, b, tot, qe, ke, kd, att = _gla_terms(q, k, g_, dirn, dk)
        etot = jnp.exp(tot)
        datt = jnp.where(mb, _dot(dout, v, 'nt'), 0.0)
        dv_ref[...] = _dot(att, dout, 'tn') + _dot(kd, dst, 'nt')
        dqe = _dot(datt, ke, 'nn') + _dot(dout, st, 'nn')
        dke = _dot(datt, qe, 'tn')
        dkd = _dot(v, dst, 'nn')
        dstate[...] = dst * etot + _dot(dout, qe, 'tn')
        dq_ref[...] = dqe * (qscale * jnp.exp(b))
        dk_ref[...] = dke * jnp.exp(-b) + dkd * jnp.exp(tot - b)
        dkd_kd = dkd * kd
        db = dqe * qe - dke * ke - dkd_kd
        dtot = jnp.sum(dkd_kd, axis=0, keepdims=True) + jnp.sum(dst * st, axis=0, keepdims=True) * etot
        dg_ref[...] = _exact_mask_dot(_tri(1 - dirn).astype(BF16), db) + dtot

    col = lambda w, off: pl.BlockSpec((CHUNK, w), functools.partial(lambda hh, s, off: (cidx(s), off + hh), off=off))
    return pl.pallas_call(
        body, name=f"gla_bwd_{dirn}",
        out_shape=(jax.ShapeDtypeStruct((nt, d.KEY), F32), jax.ShapeDtypeStruct((nt, d.KEY), F32),
                   jax.ShapeDtypeStruct((nt, d.VAL), F32), jax.ShapeDtypeStruct((nt, d.KEY), F32)),
        grid=(gh, s_all),
        in_specs=[col(dk, qb), col(dk, kb), col(dv, 0), col(dk, gb), col(dv, 0),
                  pl.BlockSpec((None, None, dv, dk), lambda hh, s: (hh, s_all - 1 - s, 0, 0))],
        out_specs=(col(dk, 0), col(dk, 0), col(dv, 0), col(dk, 0)),
        scratch_shapes=[pltpu.VMEM((dv, dk), F32)],
        compiler_params=_cparams(("parallel", "arbitrary")),
    )(z, z, z, g, do, states)


def _rope_pad(w):
    q = QK_ROPE // 4
    a1, a2, b1, b2 = (w[..., k * q:(k + 1) * q] for k in range(4))
    z = jnp.zeros(w.shape[:-1] + (LANE // 2 - 2 * q,), w.dtype)
    return jnp.concatenate([a1, b1, z, a2, b2, z], axis=-1)


def _rope_unpad(g):
    q = QK_ROPE // 4
    h = LANE // 2
    return jnp.concatenate([g[..., 0:q], g[..., h:h + q], g[..., q:2 * q], g[..., h + q:h + 2 * q]], axis=-1)


def _win0_to_kernel(w, d):
    kv_lat = w[:, :d.KVL]
    k_rope = w[:, d.KVL:d.KVL + QK_ROPE]
    q_lat = w[:, d.KVL + QK_ROPE:d.KVL + QK_ROPE + d.QL]
    rest = w[:, d.KVL + QK_ROPE + d.QL:]
    parts = [q_lat, kv_lat, _rope_pad(k_rope)]
    if d.z0_pad:
        parts.append(jnp.zeros((w.shape[0], d.z0_pad), w.dtype))
    return jnp.concatenate(parts + [rest], axis=1)


def _win0_from_kernel(g, d):
    return jnp.concatenate([g[:, d.z0_kv:d.z0_kv + d.KVL], _rope_unpad(g[:, d.z0_kr:d.z0_kr + LANE]), g[:, :d.QL],
                            g[:, d.z0_ax:]], axis=1)


def _wqb_to_kernel(w, d):
    wr = w.reshape(d.QL, d.H, QK_NOPE + QK_ROPE)
    return jnp.concatenate([wr[:, :, :QK_NOPE].reshape(d.QL, d.HQ), _rope_pad(wr[:, :, QK_NOPE:]).reshape(d.QL, d.HQ)], axis=1)


def _wqb_from_kernel(g, d):
    nope = g[:, :d.HQ].reshape(d.QL, d.H, QK_NOPE)
    rope = _rope_unpad(g[:, d.HQ:].reshape(d.QL, d.H, LANE))
    return jnp.concatenate([nope, rope], axis=2).reshape(d.QL, d.H * (QK_NOPE + QK_ROPE))


def _wkvb_to_kernel(w, d):
    return w.reshape(d.KVL, d.H, 2, LANE).transpose(0, 2, 1, 3).reshape(d.KVL, 2 * d.HQ)


def _wkvb_from_kernel(g, d):
    return g.reshape(d.KVL, 2, d.H, LANE).transpose(0, 2, 1, 3).reshape(d.KVL, 2 * d.HQ)


def _win1_to_kernel(w, d):
    k = w[:, :d.KEY]
    v = w[:, d.KEY:d.KEY + d.VAL]
    lr = w[:, d.KEY + d.VAL:d.KEY + d.VAL + 2 * GATE_RANK]
    q = w[:, d.KEY + d.VAL + 2 * GATE_RANK:2 * d.KEY + d.VAL + 2 * GATE_RANK]
    og = w[:, 2 * d.KEY + d.VAL + 2 * GATE_RANK:]
    return jnp.concatenate([v, og, k, q, lr, jnp.zeros((w.shape[0], LANE - 2 * GATE_RANK), w.dtype)], axis=1)


def _win1_from_kernel(g, d):
    return jnp.concatenate([g[:, d.z1_k:d.z1_k + d.KEY], g[:, :d.VAL], g[:, d.z1_lr:d.z1_lr + 2 * GATE_RANK],
                            g[:, d.z1_q:d.z1_q + d.KEY], g[:, d.z1_og:d.z1_og + d.VAL]], axis=1)


def _gate_weight(fw_w, bw_w, d):
    z = jnp.zeros((GATE_RANK, d.KEY), F32)
    return jnp.concatenate([jnp.concatenate([fw_w, z], axis=1), jnp.concatenate([z, bw_w], axis=1),
                            jnp.zeros((LANE - 2 * GATE_RANK, 2 * d.KEY), F32)], axis=0)


def _rope_tables(d):
    t = jnp.arange(d.T)
    inv = ROPE_THETA ** (-jnp.arange(0, QK_ROPE // 2, 2, dtype=F32) / (QK_ROPE // 2))
    ar = (t // GRID_W).astype(F32)[:, None] * inv
    ac = (t % GRID_W).astype(F32)[:, None] * inv
    z = jnp.zeros((d.T, LANE // 2 - 2 * inv.shape[0]), F32)
    ang = jnp.concatenate([ar, ac, z, ar, ac, z], axis=1)
    cos = jnp.concatenate([jnp.cos(ang), jnp.ones((d.TC, LANE), F32)], axis=0)
    sin = jnp.concatenate([jnp.sin(ang), jnp.zeros((d.TC, LANE), F32)], axis=0)
    sgn = jnp.where(jnp.arange(LANE) < LANE // 2, -1.0, 1.0).astype(F32)
    return cos, sin * sgn


def _gather_cols(shard, name):
    k, n = shard.shape
    return _exchange(shard, False, name).transpose(1, 0, 2).reshape(k, N_DEV * n)


def _gather_rows(shard, name):
    return _exchange(shard, False, name).reshape(N_DEV * shard.shape[0], shard.shape[1])


def _scatter_cols(dw, name):
    k, n = dw.shape[0], dw.shape[1] // N_DEV
    return _exchange(dw.reshape(k, N_DEV, n).transpose(1, 0, 2), True, name)


def _scatter_rows(dw, name):
    return _exchange(dw.reshape(N_DEV, dw.shape[0] // N_DEV, dw.shape[1]), True, name)


def _row(arr, cb=0, width=None):
    return ('row', arr, cb, arr.shape[1] if width is None else width)


def _mod_fwd(x, nw, shift, scale, d, name):
    def body(i, j, x, w, sh, sc):
        return (_rms(x, w)[0] * (1.0 + sc) + sh,)
    return _rowwise(body, d.NT, d.rt, d.nlt, [_row(x), ('full', nw), ('seg', shift), ('seg', scale)],
                    [('row', d.D, BF16)], name)[0]


def _mod_bwd_call(dres, dh, x, nw, shift, scale, d, name):
    def body(i, j, dres, dh, x, w, sh, sc):
        dx, dw, dsh, dsc = _mod_bwd(dh, x, w, sh, sc)
        return dres + dx, dw, dsh, dsc
    return _rowwise(body, d.NT, d.rt, d.nlt, [_row(dres), _row(dh), _row(x), ('full', nw), ('seg', shift), ('seg', scale)],
                    [('row', d.D, F32), ('acc', (1, d.D)), ('segacc', d.D), ('segacc', d.D)], name)


def _res_mod_fwd(x, y, gate, nw, shift, scale, d, name):
    def body(i, j, x, y, g, w, sh, sc):
        x1 = x + g * y
        return x1, _rms(x1, w)[0] * (1.0 + sc) + sh
    return _rowwise(body, d.NT, d.rt, d.nlt, [_row(x), _row(y), ('seg', gate), ('full', nw), ('seg', shift), ('seg', scale)],
                    [('row', d.D, F32), ('row', d.D, BF16)], name)


def _res_mod_bwd(dx2, dh2, x1, y, gate, nw, shift, scale, d, name):
    def body(i, j, dx2, dh2, x1, y, g, w, sh, sc):
        dx, dw, dsh, dsc = _mod_bwd(dh2, x1, w, sh, sc)
        dx1 = dx2 + dx
        return dx1, g * dx1, jnp.sum(dx1 * y, axis=0, keepdims=True), dw, dsh, dsc
    return _rowwise(body, d.NT, d.rt, d.nlt,
                    [_row(dx2), _row(dh2), _row(x1), _row(y), ('seg', gate), ('full', nw), ('seg', shift), ('seg', scale)],
                    [('row', d.D, F32), ('row', d.D, BF16), ('segacc', d.D), ('acc', (1, d.D)), ('segacc', d.D),
                     ('segacc', d.D)], name)


def _res_fwd(x1, f, gate, d, name):
    return _rowwise(lambda i, j, x1, f, g: (x1 + g * f,), d.NT, d.rt, d.nlt, [_row(x1), _row(f), ('seg', gate)],
                    [('row', d.D, F32)], name)[0]


def _res_bwd(dx2, f, gate, d, name):
    def body(i, j, dx2, f, g):
        return g * dx2, jnp.sum(dx2 * f, axis=0, keepdims=True)
    return _rowwise(body, d.NT, d.rt, d.nlt, [_row(dx2), _row(f), ('seg', gate)], [('row', d.D, BF16), ('segacc', d.D)], name)


def _ffn_fwd(h2, w_up, conv_w, conv_b, w_down, d, tag):
    p = _mm([(h2, w_up)], 'nn', F32, tag + '_up')
    u = _dwconv(p, _w8(conv_w, conv_b), d, F32, tag + '_conv')
    tc = _tile(d.FF, 512)
    nb = d.FF // tc
    a = _rowwise(lambda i, j, ug, uv: (_silu(ug) * uv,), d.NT, d.rt, d.nlt, [('rowc', u, 0, tc), ('rowc', u, nb, tc)],
                 [('rowc', d.FF, tc, BF16)], tag + '_glu', ncol=nb)[0]
    f = _mm([(a, w_down)], 'nn', F32, tag + '_down')
    return p, u, a, f


def _ffn_bwd(df, h2, p, u, a, w_up, conv_w, w_down, d, tag):
    da = _mm([(df, w_down)], 'nt', F32, tag + '_down_dx')
    dw_down = _mm([(a, df)], 'tn', F32, tag + '_down_dw')
    tc = _tile(d.FF, 512)
    nb = d.FF // tc

    def body(i, j, da, ug, uv):
        return (jnp.where(j < nb, da * uv * _dsilu(ug), da * _silu(ug)),)
    du = _rowwise(body, d.NT, d.rt, d.nlt, [('rowm', da, 0, tc, nb), ('rowm', u, 0, tc, nb), ('rowm', u, nb, tc, nb)],
                  [('rowc', 2 * d.FF, tc, F32)], tag + '_glu_bwd', ncol=2 * nb)[0]
    dp = _dwconv(du, _w8(conv_w[::-1]), d, BF16, tag + '_conv_dx')
    conv_g = _dwconv_wgrad(p, du, d, tag + '_conv_dw')
    dh2 = _mm([(dp, w_up)], 'nt', F32, tag + '_up_dx')
    dw_up = _mm([(h2, dp)], 'tn', F32, tag + '_up_dw')
    return dh2, dw_up, dw_down, conv_g[0:3], conv_g[3]


def _ab_fwd(z, w, cos, sin_s, d):
    qnw, kvnw = w['q_norm'], w['kv_norm']

    def prep(i, j, zq, zkv, zkr, qw, kw, cos, sin_s):
        return _rms(zq, qw)[0], _rms(zkv, kw)[0], _rope(zkr, cos, sin_s)
    qn, kvn, kr = _rowwise(prep, d.NT, d.rt, d.nlt,
                           [_row(z, 0, d.QL), _row(z, d.z0_kv // d.KVL, d.KVL), _row(z, d.z0_kr // LANE, LANE),
                            ('full', qnw), ('full', kvnw), _row(cos), _row(sin_s)],
                           [('row', d.QL, BF16), ('row', d.KVL, BF16), ('row', LANE, BF16)], 'ab_prep')
    qraw = _mm([(qn, w['w_qb'])], 'nn', F32, 'ab_qb')
    kv = _mm([(kvn, w['w_kvb'])], 'nn', BF16, 'ab_kvb')

    def qrope(i, j, qraw, cos, sin_s):
        parts = [qraw[:, :d.HQ]] + [_rope(qraw[:, d.HQ + h * LANE:d.HQ + (h + 1) * LANE], cos, sin_s) for h in range(d.H)]
        return (jnp.concatenate(parts, axis=1),)
    q = _rowwise(qrope, d.NT, d.rt, d.nlt, [_row(qraw), _row(cos), _row(sin_s)], [('row', 2 * d.HQ, BF16)], 'ab_qrope')[0]
    o, ob, lse = _flash_fwd(q, kv, kr, d)
    ab = d.z0_ax // d.CC
    s = _rowwise(lambda i, j, ax, ac: (ax * ac,), d.NT, d.rt, d.nlt, [_row(z, ab, d.CC), _row(z, ab + 2, d.CC)],
                 [('row', d.CC, F32)], 'ab_conv_in')[0]
    cv = _dwconv(s, _w8(w['conv_a']), d, F32, 'ab_conv')
    ymix = _rowwise(lambda i, j, a_b, cv, ob: (jnp.concatenate([(a_b * cv).astype(BF16), ob], axis=1),), d.NT, d.rt, d.nlt,
                    [_row(z, ab + 1, d.CC), _row(cv), _row(ob)], [('row', d.CC + d.HQ, BF16)], 'ab_mix')[0]
    return ymix, dict(qn=qn, kvn=kvn, kr=kr, q=q, kv=kv, o=o, lse=lse, s=s, cv=cv)


def _ab_bwd(dymix, z, sv, w, cos, sin_s, d):
    qnw, kvnw = w['q_norm'], w['kv_norm']
    assert d.CC % d.HQ == 0

    def dprep(i, j, dmo, o):
        cols = []
        for h in range(d.H):
            hs = slice(h * LANE, (h + 1) * LANE)
            cols.append(jnp.broadcast_to(jnp.sum(dmo[:, hs] * o[:, hs], axis=1, keepdims=True), (dmo.shape[0], LANE)))
        return dmo, jnp.concatenate(cols, axis=1)
    do, delta = _rowwise(dprep, d.NT, d.rt, d.nlt, [_row(dymix, d.CC // d.HQ, d.HQ), _row(sv['o'])],
                         [('row', d.HQ, BF16), ('row', d.HQ, F32)], 'ab_do')
    dqn_f, dqr_f = _flash_dq(sv['q'], sv['kv'], sv['kr'], do, sv['lse'], delta, d)
    dkn, dvv, dkr_h = _flash_dkv(sv['q'], sv['kv'], sv['kr'], do, sv['lse'], delta, d)

    def qrope_t(i, j, dqn, dqr, cos, sin_s):
        parts = [dqn] + [_rope_t(dqr[:, h * LANE:(h + 1) * LANE], cos, sin_s) for h in range(d.H)]
        return (jnp.concatenate(parts, axis=1),)
    dqraw = _rowwise(qrope_t, d.NT, d.rt, d.nlt, [_row(dqn_f), _row(dqr_f), _row(cos), _row(sin_s)],
                     [('row', 2 * d.HQ, BF16)], 'ab_qrope_bwd')[0]
    dkv = jnp.concatenate([dkn, dvv], axis=1)
    dqn = _mm([(dqraw, w['w_qb'])], 'nt', F32, 'ab_qb_dx')
    dw_qb = _mm([(sv['qn'], dqraw)], 'tn', F32, 'ab_qb_dw')
    dkvn = _mm([(dkv, w['w_kvb'])], 'nt', F32, 'ab_kvb_dx')
    dw_kvb = _mm([(sv['kvn'], dkv)], 'tn', F32, 'ab_kvb_dw')
    ab = d.z0_ax // d.CC
    dab, dcv = _rowwise(lambda i, j, dya, cv, a_b: (dya * cv, dya * a_b), d.NT, d.rt, d.nlt,
                        [_row(dymix, 0, d.CC), _row(sv['cv']), _row(z, ab + 1, d.CC)],
                        [('row', d.CC, BF16), ('row', d.CC, F32)], 'ab_mix_bwd')
    ds = _dwconv(dcv, _w8(w['conv_a'][::-1]), d, F32, 'ab_conv_dx')
    conv_g = _dwconv_wgrad(sv['s'], dcv, d, 'ab_conv_dw')

    def assemble(i, j, dqn, dkvn, dkr_h, zq, zkv, qw, kw, cos, sin_s, ds, ax, ac, dab):
        _, xq, rq = _rms(zq, qw)
        dzq, dqw = _rms_bwd(dqn, xq, rq, qw)
        _, xk, rk = _rms(zkv, kw)
        dzkv, dkw = _rms_bwd(dkvn, xk, rk, kw)
        dkr = dkr_h[0]
        for h in range(1, d.H):
            dkr = dkr + dkr_h[h]
        parts = [dzq, dzkv, _rope_t(dkr, cos, sin_s)]
        if d.z0_pad:
            parts.append(jnp.zeros((dzq.shape[0], d.z0_pad), F32))
        parts += [ds * ac, dab.astype(F32), ds * ax]
        return jnp.concatenate([t.astype(BF16) for t in parts], axis=1), dqw, dkw
    dz, dqw, dkw = _rowwise(assemble, d.NT, d.rt, d.nlt,
                            [_row(dqn), _row(dkvn), ('row3', dkr_h), _row(z, 0, d.QL), _row(z, d.z0_kv // d.KVL, d.KVL),
                             ('full', qnw), ('full', kvnw), _row(cos), _row(sin_s), _row(ds), _row(z, ab, d.CC),
                             _row(z, ab + 2, d.CC), _row(dab)],
                            [('row', d.ZW0, BF16), ('acc', (1, d.QL)), ('acc', (1, d.KVL))], 'ab_dz')
    return dz, dict(w_qb=dw_qb, w_kvb=dw_kvb, conv_a=conv_g[0:3], q_norm=dqw, kv_norm=dkw)


def _log_sigmoid(x):
    return jnp.minimum(x, 0.0) - jnp.log(1.0 + jnp.exp(-jnp.abs(x)))


def _gla_fwd_block(z, w, d):
    wg, bg, onw = w['gate_w'], w['gate_b'], w['o_norm']

    def gates(i, j, lr, wg, bg):
        return (_log_sigmoid(_dot(lr, wg, 'nn') + bg) / GATE_NORMALIZER,)
    g = _rowwise(gates, d.NT, d.rt, d.nlt, [_row(z, d.z1_lr // LANE, LANE), ('full', wg), ('full', bg)],
                 [('row', 2 * d.KEY, F32)], 'gla_gates')[0]
    of, stf = _gla_fwd(z, g, 0, d)
    ob, stb = _gla_fwd(z, g, 1, d)

    def outp(i, j, of, ob, og, ow):
        o = of + ob
        parts = [_rms(o[:, h * d.dv:(h + 1) * d.dv], ow)[0] for h in range(d.GH)]
        return (jnp.concatenate(parts, axis=1) * _silu(og),)
    ymix = _rowwise(outp, d.NT, d.rt, d.nlt, [_row(of), _row(ob), _row(z, d.z1_og // d.VAL, d.VAL), ('full', onw)],
                    [('row', d.VAL, BF16)], 'gla_out')[0]
    return ymix, dict(g=g, of=of, ob=ob, stf=stf, stb=stb)


def _gla_bwd_block(dymix, z, sv, w, d):
    wg, bg, onw = w['gate_w'], w['gate_b'], w['o_norm']

    def outp_bwd(i, j, dy, of, ob, og, ow):
        o = of + ob
        dn = dy * _silu(og)
        dos, ns = [], []
        dow = jnp.zeros((1, d.dv), F32)
        for h in range(d.GH):
            hs = slice(h * d.dv, (h + 1) * d.dv)
            n, xh, r = _rms(o[:, hs], ow)
            do_h, dw_h = _rms_bwd(dn[:, hs], xh, r, ow)
            dos.append(do_h)
            ns.append(n)
            dow = dow + dw_h
        return jnp.concatenate(dos, axis=1), dy * jnp.concatenate(ns, axis=1) * _dsilu(og), dow
    do, dog, dow = _rowwise(outp_bwd, d.NT, d.rt, d.nlt,
                            [_row(dymix), _row(sv['of']), _row(sv['ob']), _row(z, d.z1_og // d.VAL, d.VAL), ('full', onw)],
                            [('row', d.VAL, F32), ('row', d.VAL, BF16), ('acc', (1, d.dv))], 'gla_out_bwd')
    dq0, dk0, dv0, dg0 = _gla_bwd(z, sv['g'], do, sv['stf'], 0, d)
    dq1, dk1, dv1, dg1 = _gla_bwd(z, sv['g'], do, sv['stb'], 1, d)

    def assemble(i, j, dg0, dg1, lr, wg, bg, dq0, dq1, dk0, dk1, dv0, dv1, dog):
        pre = _dot(lr, wg, 'nn') + bg
        e = jnp.exp(-jnp.abs(pre))
        dpre = jnp.concatenate([dg0, dg1], axis=1) * jnp.where(pre >= 0, e, 1.0) / (1.0 + e) / GATE_NORMALIZER
        dlr = _dot(dpre, wg, 'nt')
        parts = [dv0 + dv1, dog.astype(F32), dk0 + dk1, dq0 + dq1, dlr]
        return (jnp.concatenate([t.astype(BF16) for t in parts], axis=1), _dot(lr, dpre, 'tn'),
                jnp.sum(dpre, axis=0, keepdims=True))
    dz, dwg, dbg = _rowwise(assemble, d.NT, d.rt, d.nlt,
                            [_row(dg0), _row(dg1), _row(z, d.z1_lr // LANE, LANE), ('full', wg), ('full', bg), _row(dq0),
                             _row(dq1), _row(dk0), _row(dk1), _row(dv0), _row(dv1), _row(dog)],
                            [('row', d.ZW1, BF16), ('acc', (LANE, 2 * d.KEY)), ('acc', (1, 2 * d.KEY))], 'gla_dz')
    return dz, dict(gate_fw_w=dwg[:GATE_RANK, :d.KEY], gate_bw_w=dwg[GATE_RANK:2 * GATE_RANK, d.KEY:],
                    gate_fw_b=dbg[:, :d.KEY], gate_bw_b=dbg[:, d.KEY:], o_norm=dow)


def _loss_bwd(x, fnw, target, d):
    def body(i, j, x, w, tgt):
        y, xh, r = _rms(x, w)
        e = y - tgt
        dx, dw = _rms_bwd(e * (1.0 / d.D), xh, r, w)
        lat = i < d.nlt
        part = jnp.sum(jnp.sum(e * e, axis=1, keepdims=True), axis=0, keepdims=True) * (0.5 / d.D)
        return (jnp.where(lat, dx, 0.0), jnp.where(lat, jnp.broadcast_to(part, (8, LANE)), 0.0), jnp.where(lat, dw, 0.0))
    return _rowwise(body, d.NT, d.rt, d.nlt, [_row(x), ('full', fnw), ('rowclamp', target, d.nlt - 1, d.D)],
                    [('row', d.D, F32), ('acc', (8, LANE)), ('acc', (1, d.D))], 'loss')


def _layer_fwd(x, mods, w, mixer_fwd, d, tag):
    sh1, sc1, g1, sh2, sc2, g2 = mods
    h = _mod_fwd(x, w['norm1'], sh1, sc1, d, tag + '_mod1')
    z = _mm([(h, w['w_in'])], 'nn', F32, tag + '_in')
    ymix, msv = mixer_fwd(z)
    y = _mm([(ymix, w['w_out'])], 'nn', F32, tag + '_out')
    x1, h2 = _res_mod_fwd(x, y, g1, w['norm2'], sh2, sc2, d, tag + '_mod2')
    p, u, a, f = _ffn_fwd(h2, w['ffn_up'], w['ffn_conv_w'], w['ffn_conv_b'], w['ffn_down'], d, tag + '_ffn')
    x2 = _res_fwd(x1, f, g2, d, tag + '_res')
    return x2, dict(x=x, h=h, z=z, ymix=ymix, msv=msv, y=y, x1=x1, h2=h2, p=p, u=u, a=a, f=f)


def _layer_bwd(dx2, sv, mods, w, mixer_bwd, d, tag):
    sh1, sc1, g1, sh2, sc2, g2 = mods
    df, dg2 = _res_bwd(dx2, sv['f'], g2, d, tag + '_res_bwd')
    dh2, dw_up, dw_down, dconv_w, dconv_b = _ffn_bwd(df, sv['h2'], sv['p'], sv['u'], sv['a'], w['ffn_up'], w['ffn_conv_w'],
                                                     w['ffn_down'], d, tag + '_ffn')
    dx1, dy, dg1, dn2, dsh2, dsc2 = _res_mod_bwd(dx2, dh2, sv['x1'], sv['y'], g1, w['norm2'], sh2, sc2, d, tag + '_mod2_bwd')
    dymix = _mm([(dy, w['w_out'])], 'nt', F32, tag + '_out_dx')
    dw_out = _mm([(sv['ymix'], dy)], 'tn', F32, tag + '_out_dw')
    dz, mg = mixer_bwd(dymix, sv['z'], sv['msv'])
    dh = _mm([(dz, w['w_in'])], 'nt', F32, tag + '_in_dx')
    dw_in = _mm([(sv['h'], dz)], 'tn', F32, tag + '_in_dw')
    dx, dn1, dsh1, dsc1 = _mod_bwd_call(dx1, dh, sv['x'], w['norm1'], sh1, sc1, d, tag + '_mod1_bwd')
    grads = dict(mg, w_in=dw_in, w_out=dw_out, ffn_up=dw_up, ffn_down=dw_down, ffn_conv_w=dconv_w, ffn_conv_b=dconv_b,
                 norm1=dn1, norm2=dn2)
    return dx, grads, [dsh1, dsc1, dg1, dsh2, dsc2, dg2]


def _pad_flat(v, mult=LANE):
    v = v.reshape(-1)
    return jnp.pad(v, (0, (-v.shape[0]) % mult))


def _pack(entries, row_mult):
    flat, offs, pos = [], [], 0
    for v in entries:
        f = _pad_flat(v)
        flat.append(f)
        offs.append(pos)
        pos += f.shape[0]
    tot = jnp.concatenate(flat)
    tot = jnp.pad(tot, (0, (-pos) % (row_mult * LANE)))
    return tot.reshape(-1, LANE), offs


def _unpack(packed, offs, shapes):
    flat = packed.reshape(-1)
    out = []
    for off, shp in zip(offs, shapes):
        n = 1
        for s in shp:
            n *= s
        out.append(flat[off:off + n].reshape(shp))
    return out


def _step(a):
    d = _dims()
    dm = d.D
    me = 4 * lax.axis_index("x") + 2 * lax.axis_index("y") + lax.axis_index("c")
    sds = jax.ShapeDtypeStruct

    def cols(name):
        return _gather_cols(_cast_bf16(a[name], 'cast_' + name), 'ag_' + name)

    def rows(name):
        return _gather_rows(_cast_bf16(a[name], 'cast_' + name), 'ag_' + name)

    small_names = ['l0_conv_a', 'l0_ffn_conv_w', 'l1_ffn_conv_w', 'l1_gate_fw_w', 'l1_gate_bw_w']
    spack, soffs = _pack([a[n] for n in small_names], 8)
    sg = _exchange(spack, False, 'ag_small')
    small_w = {}
    for n, off in zip(small_names, soffs):
        r, c = a[n].shape
        shards = sg.reshape(N_DEV, -1)[:, off:off + r * c].reshape(N_DEV, r, c)
        small_w[n] = shards.transpose(1, 0, 2).reshape(r, N_DEV * c)

    w0 = dict(norm1=a['l0_norm1'].reshape(1, dm), norm2=a['l0_norm2'].reshape(1, dm),
              w_in=_win0_to_kernel(cols('l0_w_in'), d), w_qb=_wqb_to_kernel(cols('l0_w_qb'), d),
              w_kvb=_wkvb_to_kernel(cols('l0_w_kvb'), d), w_out=rows('l0_w_out'),
              q_norm=a['l0_q_norm'].reshape(1, -1), kv_norm=a['l0_kv_norm'].reshape(1, -1), conv_a=small_w['l0_conv_a'],
              ffn_up=cols('l0_ffn_up'), ffn_conv_w=small_w['l0_ffn_conv_w'], ffn_conv_b=a['l0_ffn_conv_b'],
              ffn_down=rows('l0_ffn_down'))
    w1 = dict(norm1=a['l1_norm1'].reshape(1, dm), norm2=a['l1_norm2'].reshape(1, dm),
              w_in=_win1_to_kernel(cols('l1_w_in'), d), w_out=rows('l1_w_out'),
              gate_w=_gate_weight(small_w['l1_gate_fw_w'], small_w['l1_gate_bw_w'], d),
              gate_b=jnp.concatenate([a['l1_gate_fw_b'], a['l1_gate_bw_b']]).reshape(1, -1),
              o_norm=a['l1_o_norm'].reshape(1, -1),
              ffn_up=cols('l1_ffn_up'), ffn_conv_w=small_w['l1_ffn_conv_w'], ffn_conv_b=a['l1_ffn_conv_b'],
              ffn_down=rows('l1_ffn_down'))

    c8 = _exchange(a['c'], False, 'ag_c').reshape(N_DEV, dm)
    c16 = jnp.concatenate([c8, a['c_ctx'].reshape(1, dm), jnp.zeros((7, dm), F32)], axis=0)
    act16, dact16 = _small(lambda v: (_silu(v), _dsilu(v)), [c16], [sds((16, dm), BF16), sds((16, dm), F32)], 'cond_silu')
    n6 = N_MOD * dm // N_DEV
    mod_sh = [_mm([(act16, a[f'l{l}_ada_w'])], 'nn', F32, f'ada{l}') for l in (0, 1)]
    mod_all = _exchange(jnp.concatenate(mod_sh, axis=1), False, 'ag_mod')
    mods = []
    for l in (0, 1):
        full = mod_all[:, :, l * n6:(l + 1) * n6].transpose(1, 0, 2).reshape(16, N_MOD * dm)
        mine = jnp.concatenate([lax.dynamic_slice_in_dim(full, me, 1, 0), full[8:9]], axis=0)
        m2 = _small(lambda r, b: (r + b,), [mine, a[f'l{l}_ada_b'].reshape(1, -1)], [sds((2, N_MOD * dm), F32)], f'ada{l}_bias')[0]
        mods.append([m2[:, k * dm:(k + 1) * dm].reshape(2, 1, dm) for k in range(N_MOD)])

    cos, sin_s = _rope_tables(d)
    x0 = jnp.concatenate([a['x'][0], a['ctx'][0]], axis=0)
    x2, sv0 = _layer_fwd(x0, mods[0], w0, lambda z: _ab_fwd(z, w0, cos, sin_s, d), d, 'l0')
    x4, sv1 = _layer_fwd(x2, mods[1], w1, lambda z: _gla_fwd_block(z, w1, d), d, 'l1')
    dx4, loss_acc, dfn = _loss_bwd(x4, a['final_norm'].reshape(1, dm), a['loss_target'][0], d)

    dx2, g1, dmod1 = _layer_bwd(dx4, sv1, mods[1], w1, lambda dy, z, msv: _gla_bwd_block(dy, z, msv, w1, d), d, 'l1')
    dx0, g0, dmod0 = _layer_bwd(dx2, sv0, mods[0], w0, lambda dy, z, msv: _ab_bwd(dy, z, msv, w0, cos, sin_s, d), d, 'l0')

    dm_rows = jnp.concatenate([jnp.concatenate([t.reshape(2, dm) for t in dmod], axis=1) for dmod in (dmod0, dmod1)], axis=0)
    dm_all = _exchange(dm_rows, False, 'ag_dmod')
    lat = dm_all[:, 0::2].transpose(1, 0, 2)
    ctxs = dm_all[:, 1::2].transpose(1, 0, 2)

    def ada_prep(lat, ctxs):
        csum = jnp.sum(ctxs, axis=1, keepdims=True)
        row = lax.broadcasted_iota(jnp.int32, (1, 8, 1), 1)
        g16 = jnp.concatenate([lat, jnp.where(row == 0, csum, 0.0)], axis=1)
        return g16, jnp.sum(lat, axis=1, keepdims=True) + csum
    g16, gb = _small(ada_prep, [lat, ctxs], [sds((2, 16, N_MOD * dm), F32), sds((2, 1, N_MOD * dm), F32)], 'ada_bwd_prep')
    g16_sh = [lax.dynamic_slice_in_dim(g16[l], me * n6, n6, 1) for l in (0, 1)]
    grad_ada_w = [_mm([(act16, g16_sh[l])], 'tn', F32, f'ada{l}_dw') for l in (0, 1)]
    dact = _mm([(g16_sh[0], a['l0_ada_w']), (g16_sh[1], a['l1_ada_w'])], 'nt', F32, 'ada_dact')
    dcc = _small(lambda t, s: (t * s,), [dact[8:9], dact16[8:9]], [sds((1, dm), F32)], 'cctx_grad')[0]

    big = [('l0_w_in', 'c', _win0_from_kernel(g0['w_in'], d)), ('l0_w_qb', 'c', _wqb_from_kernel(g0['w_qb'], d)),
           ('l0_w_kvb', 'c', _wkvb_from_kernel(g0['w_kvb'], d)), ('l0_w_out', 'r', g0['w_out']),
           ('l0_ffn_up', 'c', g0['ffn_up']), ('l0_ffn_down', 'r', g0['ffn_down']),
           ('l1_w_in', 'c', _win1_from_kernel(g1['w_in'], d)), ('l1_w_out', 'r', g1['w_out']),
           ('l1_ffn_up', 'c', g1['ffn_up']), ('l1_ffn_down', 'r', g1['ffn_down'])]
    res = {}
    for name, kind, dw in big:
        parts = (_scatter_cols if kind == 'c' else _scatter_rows)(dw, 'rs_' + name)
        res[name] = _adam(parts, a[name], a['m_' + name], a['v_' + name], 'adam_' + name)
    for l in (0, 1):
        name = f'l{l}_ada_w'
        res[name] = _adam(grad_ada_w[l][None], a[name], a['m_' + name], a['v_' + name], 'adam_' + name)

    part = {'loss': loss_acc[0:1, 0:1], 'c_ctx': dcc, 'final_norm': dfn,
            'l0_norm1': g0['norm1'], 'l0_norm2': g0['norm2'], 'l0_q_norm': g0['q_norm'], 'l0_kv_norm': g0['kv_norm'],
            'l0_conv_a': g0['conv_a'], 'l0_ffn_conv_w': g0['ffn_conv_w'], 'l0_ffn_conv_b': g0['ffn_conv_b'],
            'l1_norm1': g1['norm1'], 'l1_norm2': g1['norm2'], 'l1_o_norm': g1['o_norm'],
            'l1_gate_fw_w': g1['gate_fw_w'], 'l1_gate_bw_w': g1['gate_bw_w'], 'l1_gate_fw_b': g1['gate_fw_b'],
            'l1_gate_bw_b': g1['gate_bw_b'], 'l1_ffn_conv_w': g1['ffn_conv_w'], 'l1_ffn_conv_b': g1['ffn_conv_b']}
    pkeys = list(part)
    ppack, poffs = _pack([part[k] for k in pkeys], 8)
    psum = _sum_parts(_exchange(ppack, False, 'ag_small_grads'), 'sum_small_grads')
    tot = dict(zip(pkeys, _unpack(psum, poffs, [part[k].shape for k in pkeys])))
    loss = tot['loss'].reshape(())
    sgrad = {}
    for n in _WEIGHTS:
        if n in res:
            continue
        if n.endswith('ada_b'):
            sgrad[n] = gb[int(n[1])].reshape(a[n].shape)
        elif n in small_names:
            c = a[n].shape[1]
            sgrad[n] = lax.dynamic_slice_in_dim(tot[n], me * c, c, 1)
        else:
            sgrad[n] = tot[n].reshape(a[n].shape)
    snames = list(sgrad)
    packs = [_pack([src[n] for n in snames], 8)[0] for src in
             (sgrad, {n: a[n] for n in snames}, {n: a['m_' + n] for n in snames}, {n: a['v_' + n] for n in snames})]
    offs = _pack([sgrad[n] for n in snames], 8)[1]
    outs = _adam(packs[0][None], packs[1], packs[2], packs[3], 'adam_small')
    for k in range(4):
        for n, val in zip(snames, _unpack(outs[k], offs, [a[n].shape for n in snames])):
            res.setdefault(n, [None] * 4)[k] = val

    grad_x = dx0[:d.T].reshape(1, d.T, dm)
    return (loss, grad_x, *[res[n][0] for n in _WEIGHTS], *[res[n][1] for n in _WEIGHTS], *[res[n][2] for n in _WEIGHTS],
            *[res[n][3] for n in _WEIGHTS])


def kernel(*args):
    return _step(dict(zip(_ARGS, args, strict=True)))
```

```python
import functools
import types

import jax
import jax.numpy as jnp
from jax import lax
from jax.experimental import pallas as pl
from jax.experimental.pallas import tpu as pltpu

D_MODEL = 2048
SEQ = 8192
GRID_W = 64
CTX_LEN = 256
EPS = 1e-6
N_MOD = 6
MLA_HEADS = 8
QK_NOPE = 128
QK_ROPE = 64
V_HEAD = 128
Q_LORA = 512
KV_LORA = 256
ROPE_THETA = 10000.0
GLA_HEADS = 4
GATE_RANK = 16
GATE_NORMALIZER = 16.0
CHUNK = 64
D_FF = 5632
ADAM_LR = 0.001
ADAM_B1 = 0.9
ADAM_B2 = 0.999
ADAM_EPS = 1e-08
ADAM_WD = 0.01
ADAM_STEP = 10

N_DEV = 8
LANE = 128
VMEM_LIMIT = 56 * 1024 * 1024
CONV_COLS = 2816
GLU_COLS = 1408

F32 = jnp.float32
BF16 = jnp.bfloat16
GRAD_WIRE = jnp.bfloat16
MESH_ID = pl.DeviceIdType.MESH

_FWD = ['x', 'c', 'ctx', 'c_ctx', 'l0_ada_w', 'l0_ada_b', 'l0_norm1', 'l0_w_in', 'l0_conv_a', 'l0_q_norm', 'l0_w_qb',
        'l0_kv_norm', 'l0_w_kvb', 'l0_w_out', 'l0_norm2', 'l0_ffn_up', 'l0_ffn_conv_w', 'l0_ffn_conv_b', 'l0_ffn_down',
        'l1_ada_w', 'l1_ada_b', 'l1_norm1', 'l1_w_in', 'l1_gate_fw_w', 'l1_gate_fw_b', 'l1_gate_bw_w', 'l1_gate_bw_b',
        'l1_o_norm', 'l1_w_out', 'l1_norm2', 'l1_ffn_up', 'l1_ffn_conv_w', 'l1_ffn_conv_b', 'l1_ffn_down', 'final_norm']
_WEIGHTS = _FWD[3:]
_ARGS = _FWD + ['loss_target'] + ['m_' + n for n in _WEIGHTS] + ['v_' + n for n in _WEIGHTS]


def _dims():
    d = types.SimpleNamespace()
    d.D, d.T, d.TC = D_MODEL, SEQ, CTX_LEN
    d.NT = d.T + d.TC
    d.rt = 256 if d.TC % 256 == 0 else 128
    d.nlt = d.T // d.rt
    d.H = MLA_HEADS
    d.QL, d.KVL = Q_LORA, KV_LORA
    d.CC = D_MODEL // 2
    d.z0_kv = d.QL
    d.z0_kr = d.QL + d.KVL
    d.z0_pad = (-(d.QL + d.KVL + LANE)) % d.CC
    d.z0_ax = d.QL + d.KVL + LANE + d.z0_pad
    d.ZW0 = d.z0_ax + 3 * d.CC
    d.AB_COLS = d.KVL + QK_ROPE + d.QL + 3 * d.CC
    d.HQ = d.H * LANE
    d.GH = GLA_HEADS
    d.KEY = D_MODEL // 2
    d.VAL = D_MODEL
    d.dk = d.KEY // d.GH
    d.dv = d.VAL // d.GH
    d.z1_og = d.VAL
    d.z1_k = 2 * d.VAL
    d.z1_q = 2 * d.VAL + d.KEY
    d.z1_lr = 2 * d.VAL + 2 * d.KEY
    d.ZW1 = d.z1_lr + LANE
    d.GLA_COLS = 2 * d.KEY + 2 * d.VAL + 2 * GATE_RANK
    d.FF = D_FF
    d.tq = min(256, d.TC)
    return d


def _tile(n, pref, align=LANE):
    if n <= pref:
        return n
    t = (pref // align) * align
    while t >= align:
        if n % t == 0:
            return t
        t -= align
    raise ValueError(f"no tile for {n}")


def _cparams(sem):
    return pltpu.CompilerParams(dimension_semantics=sem, vmem_limit_bytes=VMEM_LIMIT)


def _dot(a, b, mode):
    dims = {'nn': (((1,), (0,)), ((), ())), 'nt': (((1,), (1,)), ((), ())), 'tn': (((0,), (0,)), ((), ()))}[mode]
    return lax.dot_general(a.astype(BF16), b.astype(BF16), dims, preferred_element_type=F32)


def _mm(pairs, mode, out_dtype, name, tm=768, tn=1024, tk=1024):
    a0, b0 = pairs[0]
    if mode == 'nn':
        (m, k), n = a0.shape, b0.shape[1]
    elif mode == 'nt':
        (m, k), n = a0.shape, b0.shape[0]
    else:
        (k, m), n = a0.shape, b0.shape[1]
    tm, tn, tk = _tile(m, tm), _tile(n, tn), _tile(k, tk)
    nk = k // tk
    if mode == 'nn':
        a_spec = pl.BlockSpec((tm, tk), lambda i, j, kk: (i, kk))
        b_spec = pl.BlockSpec((tk, tn), lambda i, j, kk: (kk, j))
    elif mode == 'nt':
        a_spec = pl.BlockSpec((tm, tk), lambda i, j, kk: (i, kk))
        b_spec = pl.BlockSpec((tn, tk), lambda i, j, kk: (j, kk))
    else:
        a_spec = pl.BlockSpec((tk, tm), lambda i, j, kk: (kk, i))
        b_spec = pl.BlockSpec((tk, tn), lambda i, j, kk: (kk, j))
    npairs = len(pairs)

    def body(*refs):
        o_ref, acc_ref = refs[-2], refs[-1]
        kk = pl.program_id(2)

        @pl.when(kk == 0)
        def _():
            acc_ref[...] = jnp.zeros_like(acc_ref)

        s = None
        for p in range(npairs):
            t = _dot(refs[2 * p][...], refs[2 * p + 1][...], mode)
            s = t if s is None else s + t
        acc_ref[...] += s

        @pl.when(kk == nk - 1)
        def _():
            o_ref[...] = acc_ref[...].astype(o_ref.dtype)

    flat = [t for ab in pairs for t in ab]
    return pl.pallas_call(
        body, name=name,
        out_shape=jax.ShapeDtypeStruct((m, n), out_dtype),
        grid=(m // tm, n // tn, nk),
        in_specs=[a_spec, b_spec] * npairs,
        out_specs=pl.BlockSpec((tm, tn), lambda i, j, kk: (i, j)),
        scratch_shapes=[pltpu.VMEM((tm, tn), F32)],
        compiler_params=_cparams(("parallel", "parallel", "arbitrary")),
    )(*flat)


def _rowwise(body, nrows, rt, nlt, ins, outs, name, ncol=1):
    ntiles = nrows // rt
    in_specs, args = [], []
    for spec in ins:
        kind, arr = spec[0], spec[1]
        if kind == 'row':
            in_specs.append(pl.BlockSpec((rt, spec[3]), functools.partial(lambda i, j, cb: (i, cb), cb=spec[2])))
        elif kind == 'rowc':
            in_specs.append(pl.BlockSpec((rt, spec[3]), functools.partial(lambda i, j, cb: (i, cb + j), cb=spec[2])))
        elif kind == 'rowm':
            in_specs.append(pl.BlockSpec(
                (rt, spec[3]), functools.partial(lambda i, j, cb, md: (i, cb + j % md), cb=spec[2], md=spec[4])))
        elif kind == 'rowclamp':
            in_specs.append(pl.BlockSpec(
                (rt, spec[3]), functools.partial(lambda i, j, mb: (jnp.minimum(i, mb), 0), mb=spec[2])))
        elif kind == 'row3':
            in_specs.append(pl.BlockSpec((arr.shape[0], rt, arr.shape[2]), lambda i, j: (0, i, 0)))
        elif kind == 'seg':
            in_specs.append(pl.BlockSpec((None, 1, arr.shape[2]), lambda i, j: (i // nlt, 0, 0)))
        else:
            in_specs.append(pl.BlockSpec(arr.shape, functools.partial(lambda i, j, nd: (0,) * nd, nd=arr.ndim)))
        args.append(arr)
    out_shapes, out_specs = [], []
    for spec in outs:
        kind = spec[0]
        if kind == 'row':
            out_shapes.append(jax.ShapeDtypeStruct((nrows, spec[1]), spec[2]))
            out_specs.append(pl.BlockSpec((rt, spec[1]), lambda i, j: (i, 0)))
        elif kind == 'rowc':
            out_shapes.append(jax.ShapeDtypeStruct((nrows, spec[1]), spec[3]))
            out_specs.append(pl.BlockSpec((rt, spec[2]), lambda i, j: (i, j)))
        elif kind == 'acc':
            out_shapes.append(jax.ShapeDtypeStruct(spec[1], F32))
            out_specs.append(pl.BlockSpec(spec[1], functools.partial(lambda i, j, nd: (0,) * nd, nd=len(spec[1]))))
        else:
            out_shapes.append(jax.ShapeDtypeStruct((2, 1, spec[1]), F32))
            out_specs.append(pl.BlockSpec((None, 1, spec[1]), lambda i, j: (i // nlt, 0, 0)))
    n_in = len(ins)
    has_acc = any(s[0] in ('acc', 'segacc') for s in outs)

    def kern(*refs):
        i = pl.program_id(0)
        j = pl.program_id(1)
        vals = [r[...] for r in refs[:n_in]]
        res = body(i, j, *vals)
        for spec, ref, val in zip(outs, refs[n_in:], res):
            if spec[0] in ('row', 'rowc'):
                ref[...] = val.astype(ref.dtype)
            else:
                first = (i == 0) if spec[0] == 'acc' else jnp.logical_or(i == 0, i == nlt)

                @pl.when(first)
                def _(ref=ref, val=val):
                    ref[...] = val

                @pl.when(jnp.logical_not(first))
                def _(ref=ref, val=val):
                    ref[...] += val

    return pl.pallas_call(
        kern, name=name, out_shape=tuple(out_shapes), grid=(ntiles, ncol),
        in_specs=in_specs, out_specs=tuple(out_specs),
        compiler_params=_cparams(("arbitrary", "arbitrary") if has_acc else ("parallel", "parallel")),
    )(*args)


def _small(body, args, out_shapes, name):
    n_in = len(args)

    def kern(*refs):
        res = body(*[r[...] for r in refs[:n_in]])
        for ref, val in zip(refs[n_in:], res):
            ref[...] = val.astype(ref.dtype)

    return pl.pallas_call(
        kern, name=name, out_shape=tuple(out_shapes),
        in_specs=[pl.BlockSpec(memory_space=pltpu.VMEM)] * n_in,
        out_specs=tuple(pl.BlockSpec(memory_space=pltpu.VMEM) for _ in out_shapes),
        compiler_params=pltpu.CompilerParams(vmem_limit_bytes=VMEM_LIMIT),
    )(*args)


def _exchange(x, stacked, name):
    shp = x.shape[1:] if stacked else x.shape

    def body(x_ref, o_ref, send_sems, recv_sems, local_sem):
        ix, iy, ic = lax.axis_index("x"), lax.axis_index("y"), lax.axis_index("c")
        me = 4 * ix + 2 * iy + ic

        def src(p):
            return x_ref.at[p] if stacked else x_ref

        mine = pltpu.make_async_copy(src(me), o_ref.at[me], local_sem)
        mine.start()
        sends, recvs = [], []
        for k in range(1, N_DEV):
            px, py, pc = (ix + ((k >> 2) & 1)) % 2, (iy + ((k >> 1) & 1)) % 2, (ic + (k & 1)) % 2
            peer = 4 * px + 2 * py + pc
            sends.append(pltpu.make_async_remote_copy(
                src_ref=src(peer), dst_ref=o_ref.at[me], send_sem=send_sems.at[k - 1], recv_sem=recv_sems.at[k - 1],
                device_id=(px, py, pc), device_id_type=MESH_ID))
            recvs.append(pltpu.make_async_remote_copy(
                src_ref=src(peer), dst_ref=o_ref.at[peer], send_sem=send_sems.at[k - 1], recv_sem=recv_sems.at[k - 1],
                device_id=(px, py, pc), device_id_type=MESH_ID))
        for cp in sends:
            cp.start()
        for cp in recvs:
            cp.wait_recv()
        for cp in sends:
            cp.wait_send()
        mine.wait()

    return pl.pallas_call(
        body, name=name,
        out_shape=jax.ShapeDtypeStruct((N_DEV,) + tuple(shp), x.dtype),
        in_specs=[pl.BlockSpec(memory_space=pl.ANY)],
        out_specs=pl.BlockSpec(memory_space=pl.ANY),
        scratch_shapes=[pltpu.SemaphoreType.DMA((N_DEV - 1,)), pltpu.SemaphoreType.DMA((N_DEV - 1,)),
                        pltpu.SemaphoreType.DMA(())],
    )(x)


def _cast_bf16(x, name):
    r, c = x.shape
    tr = _tile(r, 256, 8)
    return pl.pallas_call(
        lambda x_ref, o_ref: o_ref.__setitem__(Ellipsis, x_ref[...].astype(BF16)), name=name,
        out_shape=jax.ShapeDtypeStruct((r, c), BF16), grid=(r // tr,),
        in_specs=[pl.BlockSpec((tr, c), lambda i: (i, 0))], out_specs=pl.BlockSpec((tr, c), lambda i: (i, 0)),
        compiler_params=_cparams(("parallel",)),
    )(x)


def _adam(parts, w, m, v, name):
    p, r, c = parts.shape
    tr = _tile(r, 64, 8)

    def body(p_ref, w_ref, m_ref, v_ref, g_ref, d_ref, nm_ref, nv_ref):
        g = p_ref[0].astype(F32)
        for q in range(1, p):
            g = g + p_ref[q].astype(F32)
        nm = ADAM_B1 * m_ref[...] + (1.0 - ADAM_B1) * g
        nv = ADAM_B2 * v_ref[...] + (1.0 - ADAM_B2) * (g * g)
        m_hat = nm / (1.0 - ADAM_B1 ** ADAM_STEP)
        v_hat = nv / (1.0 - ADAM_B2 ** ADAM_STEP)
        g_ref[...] = g
        d_ref[...] = -ADAM_LR * (m_hat / (jnp.sqrt(v_hat) + ADAM_EPS) + ADAM_WD * w_ref[...])
        nm_ref[...] = nm
        nv_ref[...] = nv

    spec = pl.BlockSpec((tr, c), lambda i: (i, 0))
    return pl.pallas_call(
        body, name=name, out_shape=tuple(jax.ShapeDtypeStruct((r, c), F32) for _ in range(4)), grid=(r // tr,),
        in_specs=[pl.BlockSpec((p, tr, c), lambda i: (0, i, 0)), spec, spec, spec], out_specs=(spec,) * 4,
        compiler_params=_cparams(("parallel",)),
    )(parts, w, m, v)


def _sum_parts(parts, name):
    p, r, c = parts.shape
    tr = _tile(r, 256, 8)

    def body(p_ref, o_ref):
        g = p_ref[0]
        for q in range(1, p):
            g = g + p_ref[q]
        o_ref[...] = g

    return pl.pallas_call(
        body, name=name, out_shape=jax.ShapeDtypeStruct((r, c), F32), grid=(r // tr,),
        in_specs=[pl.BlockSpec((p, tr, c), lambda i: (0, i, 0))], out_specs=pl.BlockSpec((tr, c), lambda i: (i, 0)),
        compiler_params=_cparams(("parallel",)),
    )(parts)


def _shifted(cur, prev8, next8, i, rt, nlt, ntiles):
    first = jnp.logical_or(i == 0, i == nlt)
    last = jnp.logical_or(i == nlt - 1, i == ntiles - 1)
    prev_row = jnp.where(first, 0.0, prev8[7:8, :])
    next_row = jnp.where(last, 0.0, next8[0:1, :])
    rows = lax.broadcasted_iota(jnp.int32, (rt, 1), 0)
    x_m1 = jnp.where(rows == 0, prev_row, pltpu.roll(cur, 1, 0))
    x_p1 = jnp.where(rows == rt - 1, next_row, pltpu.roll(cur, rt - 1, 0))
    return x_m1, x_p1


def _halo_specs(rt, tc, nrows, col_axis_first):
    r8 = rt // 8
    last8 = nrows // 8 - 1
    if col_axis_first:
        return [pl.BlockSpec((8, tc), lambda j, i: (jnp.maximum(i * r8 - 1, 0), j)),
                pl.BlockSpec((rt, tc), lambda j, i: (i, j)),
                pl.BlockSpec((8, tc), lambda j, i: (jnp.minimum((i + 1) * r8, last8), j))]
    return [pl.BlockSpec((8, tc), lambda i, j: (jnp.maximum(i * r8 - 1, 0), j)),
            pl.BlockSpec((rt, tc), lambda i, j: (i, j)),
            pl.BlockSpec((8, tc), lambda i, j: (jnp.minimum((i + 1) * r8, last8), j))]


def _dwconv(x, w8, d, out_dtype, name):
    nrows, c = x.shape
    rt, nlt = d.rt, d.nlt
    tc = _tile(c, CONV_COLS)
    ntiles = nrows // rt

    def body(p_ref, c_ref, n_ref, w_ref, o_ref):
        i = pl.program_id(0)
        cur = c_ref[...]
        x_m1, x_p1 = _shifted(cur, p_ref[...], n_ref[...], i, rt, nlt, ntiles)
        w = w_ref[...]
        o_ref[...] = (x_m1 * w[0:1] + cur * w[1:2] + x_p1 * w[2:3] + w[3:4]).astype(o_ref.dtype)

    return pl.pallas_call(
        body, name=name, out_shape=jax.ShapeDtypeStruct((nrows, c), out_dtype), grid=(ntiles, c // tc),
        in_specs=_halo_specs(rt, tc, nrows, False) + [pl.BlockSpec((8, tc), lambda i, j: (0, j))],
        out_specs=pl.BlockSpec((rt, tc), lambda i, j: (i, j)),
        compiler_params=_cparams(("parallel", "parallel")),
    )(x, x, x, w8)


def _dwconv_wgrad(x, dy, d, name):
    nrows, c = x.shape
    rt, nlt = d.rt, d.nlt
    tc = _tile(c, CONV_COLS)
    ntiles = nrows // rt

    def body(p_ref, c_ref, n_ref, dy_ref, o_ref):
        i = pl.program_id(1)
        cur = c_ref[...]
        dy = dy_ref[...]
        x_m1, x_p1 = _shifted(cur, p_ref[...], n_ref[...], i, rt, nlt, ntiles)
        sums = [jnp.sum(t * dy, axis=0, keepdims=True) for t in (x_m1, cur, x_p1)] + [jnp.sum(dy, axis=0, keepdims=True)]
        row = lax.broadcasted_iota(jnp.int32, (8, 1), 0)
        part = jnp.zeros((8, tc), F32)
        for k, s in enumerate(sums):
            part = jnp.where(row == k, s, part)

        @pl.when(i == 0)
        def _():
            o_ref[...] = part

        @pl.when(i != 0)
        def _():
            o_ref[...] += part

    return pl.pallas_call(
        body, name=name, out_shape=jax.ShapeDtypeStruct((8, c), F32), grid=(c // tc, ntiles),
        in_specs=_halo_specs(rt, tc, nrows, True) + [pl.BlockSpec((rt, tc), lambda j, i: (i, j))],
        out_specs=pl.BlockSpec((8, tc), lambda j, i: (0, j)),
        compiler_params=_cparams(("parallel", "arbitrary")),
    )(x, x, x, dy)


def _w8(w3, b=None):
    c = w3.shape[1]
    brow = jnp.zeros((1, c), F32) if b is None else b.reshape(1, c)
    return jnp.concatenate([w3, brow, jnp.zeros((4, c), F32)], axis=0)


def _rms(x, w):
    r = lax.rsqrt(jnp.mean(x * x, axis=-1, keepdims=True) + EPS)
    xh = x * r
    return xh * w, xh, r


def _rms_bwd(dy, xh, r, w):
    dxh = dy * w
    dx = r * (dxh - xh * jnp.mean(dxh * xh, axis=-1, keepdims=True))
    return dx, jnp.sum(dy * xh, axis=0, keepdims=True)


def _mod_bwd(dh, x, w, shift, scale):
    n, xh, r = _rms(x, w)
    dshift = jnp.sum(dh, axis=0, keepdims=True)
    dscale = jnp.sum(dh * n, axis=0, keepdims=True)
    dx, dw = _rms_bwd(dh * (1.0 + scale), xh, r, w)
    return dx, dw, dshift, dscale


def _sigmoid(x):
    return 1.0 / (1.0 + jnp.exp(-x))


def _silu(x):
    return x * _sigmoid(x)


def _dsilu(x):
    s = _sigmoid(x)
    return s * (1.0 + x * (1.0 - s))


def _rope(x, cos, sin_s):
    return x * cos + pltpu.roll(x, LANE // 2, 1) * sin_s


def _rope_t(dy, cos, sin_s):
    return dy * cos + pltpu.roll(dy * sin_s, LANE // 2, 1)


def _flash_fwd(q, kcat, kv, d):
    nt, h, tq = d.NT, d.H, d.tq
    tkb = _tile(d.T, 4096)
    n_big = d.T // tkb
    nq_lat = d.T // tq
    scale = float(QK_NOPE + QK_ROPE) ** -0.5

    def body(q_ref, k_ref, v_ref, o_ref, ob_ref, lse_ref):
        qi = pl.program_id(1)
        q_t = q_ref[...]

        def step(k0, tk, carry):
            m, l, acc = carry
            ks = pl.ds(k0, tk)
            s = _dot(q_t, k_ref[ks, :], 'nt') * scale
            m_new = jnp.maximum(m, jnp.max(s, axis=1, keepdims=True))
            p = jnp.exp(s - m_new)
            alpha = jnp.exp(m - m_new)
            return m_new, alpha * l + jnp.sum(p, axis=1, keepdims=True), alpha * acc + _dot(p, v_ref[ks, :], 'nn')

        init = (jnp.full((tq, 1), -1e30, F32), jnp.zeros((tq, 1), F32), jnp.zeros((tq, LANE), F32))
        trips = jnp.where(qi < nq_lat, n_big, 0)
        carry = lax.fori_loop(0, trips, lambda t, c: step(pl.multiple_of(t * tkb, tkb), tkb, c), init)
        m, l, acc = step(d.T, d.TC, carry)
        o = acc / l
        o_ref[...] = o
        ob_ref[...] = o.astype(BF16)
        lse_ref[...] = jnp.broadcast_to(m + jnp.log(l), (tq, LANE))

    out = pl.BlockSpec((tq, LANE), lambda hh, i: (i, hh))
    return pl.pallas_call(
        body, name="flash_fwd",
        out_shape=(jax.ShapeDtypeStruct((nt, d.HQ), F32), jax.ShapeDtypeStruct((nt, d.HQ), BF16),
                   jax.ShapeDtypeStruct((nt, d.HQ), F32)),
        grid=(h, nt // tq),
        in_specs=[pl.BlockSpec((tq, 2 * LANE), lambda hh, i: (i, hh)), pl.BlockSpec((nt, 2 * LANE), lambda hh, i: (0, hh)),
                  pl.BlockSpec((nt, LANE), lambda hh, i: (0, h + hh))],
        out_specs=(out, out, out),
        compiler_params=_cparams(("parallel", "parallel")),
    )(q, kcat, kv)


def _flash_bwd(q, kcat, kv, do, ld, d):
    nt, h, tk = d.NT, d.H, d.tq
    tqb = _tile(d.T, 2048)
    n_big = d.T // tqb
    nk_lat = d.T // tk
    scale = float(QK_NOPE + QK_ROPE) ** -0.5

    def body(q_ref, do_ref, ld_ref, k_ref, v_ref, dq_ref, dk_ref, dv_ref):
        kt = pl.program_id(1)
        k_t, v_t = k_ref[...], v_ref[...]

        @pl.when(kt == 0)
        def _():
            dq_ref[...] = jnp.zeros_like(dq_ref)

        def step(q0, tq, carry):
            dk, dv = carry
            qs = pl.ds(q0, tq)
            q_t, do_t, ld_t = q_ref[qs, :], do_ref[qs, :], ld_ref[qs, :]
            s = _dot(q_t, k_t, 'nt') * scale
            p = jnp.exp(s - ld_t[:, 0:1])
            ds = p * (_dot(do_t, v_t, 'nt') - ld_t[:, LANE // 2:LANE // 2 + 1]) * scale
            dq_ref[qs, :] += _dot(ds, k_t, 'nn')
            return dk + _dot(ds, q_t, 'tn'), dv + _dot(p, do_t, 'tn')

        init = (jnp.zeros((tk, 2 * LANE), F32), jnp.zeros((tk, LANE), F32))
        carry = lax.fori_loop(0, n_big, lambda t, c: step(pl.multiple_of(t * tqb, tqb), tqb, c), init)
        dk_ref[...] = carry[0]
        dv_ref[...] = carry[1].astype(BF16)

        @pl.when(kt >= nk_lat)
        def _():
            dk, dv = step(d.T, d.TC, carry)
            dk_ref[...] = dk
            dv_ref[...] = dv.astype(BF16)

    res = lambda w: pl.BlockSpec((nt, w), lambda hh, i: (0, hh))
    return pl.pallas_call(
        body, name="flash_bwd",
        out_shape=(jax.ShapeDtypeStruct((nt, 2 * d.HQ), F32), jax.ShapeDtypeStruct((nt, 2 * d.HQ), F32),
                   jax.ShapeDtypeStruct((nt, d.HQ), BF16)),
        grid=(h, nt // tk),
        in_specs=[res(2 * LANE), res(LANE), res(LANE), pl.BlockSpec((tk, 2 * LANE), lambda hh, i: (i, hh)),
                  pl.BlockSpec((tk, LANE), lambda hh, i: (i, h + hh))],
        out_specs=(res(2 * LANE), pl.BlockSpec((tk, 2 * LANE), lambda hh, i: (i, hh)),
                   pl.BlockSpec((tk, LANE), lambda hh, i: (i, hh))),
        compiler_params=_cparams(("parallel", "arbitrary")),
    )(q, do, ld, kcat, kv)


def _tri(dirn):
    r = lax.broadcasted_iota(jnp.int32, (CHUNK, CHUNK), 0)
    c = lax.broadcasted_iota(jnp.int32, (CHUNK, CHUNK), 1)
    return (c <= r) if dirn == 0 else (c >= r)


def _exact_mask_dot(mask_bf16, x):
    hi = x.astype(BF16)
    r1 = x - hi.astype(F32)
    mid = r1.astype(BF16)
    lo = (r1 - mid.astype(F32)).astype(BF16)
    dot = lambda t: lax.dot_general(mask_bf16, t, (((1,), (0,)), ((), ())), preferred_element_type=F32)
    return dot(hi) + dot(mid) + dot(lo)


def _gla_terms(q, k, g, dirn, dk):
    mb = _tri(dirn)
    b = _exact_mask_dot(mb.astype(BF16), g)
    tot = jnp.sum(g, axis=0, keepdims=True)
    qe = q * (float(dk) ** -0.5) * jnp.exp(b)
    ke = k * jnp.exp(-b)
    kd = k * jnp.exp(tot - b)
    att = jnp.where(mb, _dot(qe, ke, 'nt'), 0.0)
    return mb, b, tot, qe, ke, kd, att


def _gla_chunk_index(d, dirn):
    s_all = d.NT // CHUNK
    ncl = d.T // CHUNK
    if dirn == 0:
        return lambda s: (s + ncl) % s_all
    return lambda s: s_all - 1 - s


def _gla_fwd(z, g, dirn, d):
    nt, gh, dk, dv = d.NT, d.GH, d.dk, d.dv
    s_all = nt // CHUNK
    cidx = _gla_chunk_index(d, dirn)
    qb, kb, gb = d.z1_q // dk, d.z1_k // dk, dirn * gh

    def body(q_ref, k_ref, v_ref, g_ref, o_ref, st_ref, state):
        @pl.when(pl.program_id(1) == 0)
        def _():
            state[...] = jnp.zeros_like(state)

        st = state[...]
        st_ref[...] = st
        v = v_ref[...]
        _, _, tot, qe, _, kd, att = _gla_terms(q_ref[...], k_ref[...], g_ref[...], dirn, dk)
        o_ref[...] = _dot(att, v, 'nn') + _dot(qe, st, 'nt')
        state[...] = st * jnp.exp(tot) + _dot(v, kd, 'tn')

    col = lambda w, off: pl.BlockSpec((CHUNK, w), functools.partial(lambda hh, s, off: (cidx(s), off + hh), off=off))
    return pl.pallas_call(
        body, name=f"gla_fwd_{dirn}",
        out_shape=(jax.ShapeDtypeStruct((nt, d.VAL), F32), jax.ShapeDtypeStruct((gh, s_all, dv, dk), F32)),
        grid=(gh, s_all),
        in_specs=[col(dk, qb), col(dk, kb), col(dv, 0), col(dk, gb)],
        out_specs=(col(dv, 0), pl.BlockSpec((None, None, dv, dk), lambda hh, s: (hh, s, 0, 0))),
        scratch_shapes=[pltpu.VMEM((dv, dk), F32)],
        compiler_params=_cparams(("parallel", "arbitrary")),
    )(z, z, z, g)


def _gla_bwd(z, g, do, states, dirn, d):
    nt, gh, dk, dv = d.NT, d.GH, d.dk, d.dv
    s_all = nt // CHUNK
    cfwd = _gla_chunk_index(d, dirn)
    cidx = lambda s: cfwd(s_all - 1 - s)
    qb, kb, gb = d.z1_q // dk, d.z1_k // dk, dirn * gh
    qscale = float(dk) ** -0.5

    def body(q_ref, k_ref, v_ref, g_ref, do_ref, st_ref, dq_ref, dk_ref, dv_ref, dg_ref, dstate):
        @pl.when(pl.program_id(1) == 0)
        def _():
            dstate[...] = jnp.zeros_like(dstate)

        q, k, v, g_, dout, st, dst = q_ref[...], k_ref[...], v_ref[...], g_ref[...], do_ref[...], st_ref[...], dstate[...]
        mb, b, tot, qe, ke, kd, att = _gla_terms(q, k, g_, dirn, dk)
        etot = jnp.exp(tot)
        datt = jnp.where(mb, _dot(dout, v, 'nt'), 0.0)
        dv_ref[...] = _dot(att, dout, 'tn') + _dot(kd, dst, 'nt')
        dqe = _dot(datt, ke, 'nn') + _dot(dout, st, 'nn')
        dke = _dot(datt, qe, 'tn')
        dkd = _dot(v, dst, 'nn')
        dstate[...] = dst * etot + _dot(dout, qe, 'tn')
        dq_ref[...] = dqe * (qscale * jnp.exp(b))
        dk_ref[...] = dke * jnp.exp(-b) + dkd * jnp.exp(tot - b)
        dkd_kd = dkd * kd
        db = dqe * qe - dke * ke - dkd_kd
        dtot = jnp.sum(dkd_kd, axis=0, keepdims=True) + jnp.sum(dst * st, axis=0, keepdims=True) * etot
        dg_ref[...] = _exact_mask_dot(_tri(1 - dirn).astype(BF16), db) + dtot

    col = lambda w, off: pl.BlockSpec((CHUNK, w), functools.partial(lambda hh, s, off: (cidx(s), off + hh), off=off))
    return pl.pallas_call(
        body, name=f"gla_bwd_{dirn}",
        out_shape=(jax.ShapeDtypeStruct((nt, d.KEY), F32), jax.ShapeDtypeStruct((nt, d.KEY), F32),
                   jax.ShapeDtypeStruct((nt, d.VAL), F32), jax.ShapeDtypeStruct((nt, d.KEY), F32)),
        grid=(gh, s_all),
        in_specs=[col(dk, qb), col(dk, kb), col(dv, 0), col(dk, gb), col(dv, 0),
                  pl.BlockSpec((None, None, dv, dk), lambda hh, s: (hh, s_all - 1 - s, 0, 0))],
        out_specs=(col(dk, 0), col(dk, 0), col(dv, 0), col(dk, 0)),
        scratch_shapes=[pltpu.VMEM((dv, dk), F32)],
        compiler_params=_cparams(("parallel", "arbitrary")),
    )(z, z, z, g, do, states)


def _rope_pad(w):
    q = QK_ROPE // 4
    a1, a2, b1, b2 = (w[..., k * q:(k + 1) * q] for k in range(4))
    z = jnp.zeros(w.shape[:-1] + (LANE // 2 - 2 * q,), w.dtype)
    return jnp.concatenate([a1, b1, z, a2, b2, z], axis=-1)


def _rope_unpad(g):
    q = QK_ROPE // 4
    h = LANE // 2
    return jnp.concatenate([g[..., 0:q], g[..., h:h + q], g[..., q:2 * q], g[..., h + q:h + 2 * q]], axis=-1)


def _win0_to_kernel(w, d):
    kv_lat = w[:, :d.KVL]
    k_rope = w[:, d.KVL:d.KVL + QK_ROPE]
    q_lat = w[:, d.KVL + QK_ROPE:d.KVL + QK_ROPE + d.QL]
    rest = w[:, d.KVL + QK_ROPE + d.QL:]
    parts = [q_lat, kv_lat, _rope_pad(k_rope)]
    if d.z0_pad:
        parts.append(jnp.zeros((w.shape[0], d.z0_pad), w.dtype))
    return jnp.concatenate(parts + [rest], axis=1)


def _win0_from_kernel(g, d):
    return jnp.concatenate([g[:, d.z0_kv:d.z0_kv + d.KVL], _rope_unpad(g[:, d.z0_kr:d.z0_kr + LANE]), g[:, :d.QL],
                            g[:, d.z0_ax:]], axis=1)


def _wqb_to_kernel(w, d):
    wr = w.reshape(d.QL, d.H, QK_NOPE + QK_ROPE)
    return jnp.concatenate([wr[:, :, :QK_NOPE], _rope_pad(wr[:, :, QK_NOPE:])], axis=2).reshape(d.QL, 2 * d.HQ)


def _wqb_from_kernel(g, d):
    gr = g.reshape(d.QL, d.H, QK_NOPE + LANE)
    return jnp.concatenate([gr[:, :, :QK_NOPE], _rope_unpad(gr[:, :, QK_NOPE:])], axis=2).reshape(d.QL, d.H * (QK_NOPE + QK_ROPE))


def _wkvb_to_kernel(w, d):
    return w.reshape(d.KVL, d.H, 2, LANE).transpose(0, 2, 1, 3).reshape(d.KVL, 2 * d.HQ)


def _wkvb_from_kernel(g, d):
    return g.reshape(d.KVL, 2, d.H, LANE).transpose(0, 2, 1, 3).reshape(d.KVL, 2 * d.HQ)


def _win1_to_kernel(w, d):
    k = w[:, :d.KEY]
    v = w[:, d.KEY:d.KEY + d.VAL]
    lr = w[:, d.KEY + d.VAL:d.KEY + d.VAL + 2 * GATE_RANK]
    q = w[:, d.KEY + d.VAL + 2 * GATE_RANK:2 * d.KEY + d.VAL + 2 * GATE_RANK]
    og = w[:, 2 * d.KEY + d.VAL + 2 * GATE_RANK:]
    return jnp.concatenate([v, og, k, q, lr, jnp.zeros((w.shape[0], LANE - 2 * GATE_RANK), w.dtype)], axis=1)


def _win1_from_kernel(g, d):
    return jnp.concatenate([g[:, d.z1_k:d.z1_k + d.KEY], g[:, :d.VAL], g[:, d.z1_lr:d.z1_lr + 2 * GATE_RANK],
                            g[:, d.z1_q:d.z1_q + d.KEY], g[:, d.z1_og:d.z1_og + d.VAL]], axis=1)


def _gate_weight(fw_w, bw_w, d):
    z = jnp.zeros((GATE_RANK, d.KEY), F32)
    return jnp.concatenate([jnp.concatenate([fw_w, z], axis=1), jnp.concatenate([z, bw_w], axis=1),
                            jnp.zeros((LANE - 2 * GATE_RANK, 2 * d.KEY), F32)], axis=0)


def _rope_tables(d):
    t = jnp.arange(d.T)
    inv = ROPE_THETA ** (-jnp.arange(0, QK_ROPE // 2, 2, dtype=F32) / (QK_ROPE // 2))
    ar = (t // GRID_W).astype(F32)[:, None] * inv
    ac = (t % GRID_W).astype(F32)[:, None] * inv
    z = jnp.zeros((d.T, LANE // 2 - 2 * inv.shape[0]), F32)
    ang = jnp.concatenate([ar, ac, z, ar, ac, z], axis=1)
    cos = jnp.concatenate([jnp.cos(ang), jnp.ones((d.TC, LANE), F32)], axis=0)
    sin = jnp.concatenate([jnp.sin(ang), jnp.zeros((d.TC, LANE), F32)], axis=0)
    sgn = jnp.where(jnp.arange(LANE) < LANE // 2, -1.0, 1.0).astype(F32)
    return cos, sin * sgn


def _gather_cols(shard, name):
    k, n = shard.shape
    return _exchange(shard, False, name).transpose(1, 0, 2).reshape(k, N_DEV * n)


def _gather_rows(shard, name):
    return _exchange(shard, False, name).reshape(N_DEV * shard.shape[0], shard.shape[1])


def _scatter_cols(dw, name):
    k, n = dw.shape[0], dw.shape[1] // N_DEV
    return _exchange(dw.reshape(k, N_DEV, n).transpose(1, 0, 2), True, name)


def _scatter_rows(dw, name):
    return _exchange(dw.reshape(N_DEV, dw.shape[0] // N_DEV, dw.shape[1]), True, name)


def _row(arr, cb=0, width=None):
    return ('row', arr, cb, arr.shape[1] if width is None else width)


def _mod_fwd(x, nw, shift, scale, d, name):
    def body(i, j, x, w, sh, sc):
        return (_rms(x, w)[0] * (1.0 + sc) + sh,)
    return _rowwise(body, d.NT, d.rt, d.nlt, [_row(x), ('full', nw), ('seg', shift), ('seg', scale)],
                    [('row', d.D, BF16)], name)[0]


def _mod_bwd_call(dres, dh, x, nw, shift, scale, d, name):
    def body(i, j, dres, dh, x, w, sh, sc):
        dx, dw, dsh, dsc = _mod_bwd(dh, x, w, sh, sc)
        return dres + dx, dw, dsh, dsc
    return _rowwise(body, d.NT, d.rt, d.nlt, [_row(dres), _row(dh), _row(x), ('full', nw), ('seg', shift), ('seg', scale)],
                    [('row', d.D, F32), ('acc', (1, d.D)), ('segacc', d.D), ('segacc', d.D)], name)


def _res_mod_fwd(x, y, gate, nw, shift, scale, d, name):
    def body(i, j, x, y, g, w, sh, sc):
        x1 = x + g * y
        return x1, _rms(x1, w)[0] * (1.0 + sc) + sh
    return _rowwise(body, d.NT, d.rt, d.nlt, [_row(x), _row(y), ('seg', gate), ('full', nw), ('seg', shift), ('seg', scale)],
                    [('row', d.D, F32), ('row', d.D, BF16)], name)


def _res_mod_bwd(dx2, dh2, x1, y, gate, nw, shift, scale, d, name):
    def body(i, j, dx2, dh2, x1, y, g, w, sh, sc):
        dx, dw, dsh, dsc = _mod_bwd(dh2, x1, w, sh, sc)
        dx1 = dx2 + dx
        return dx1, g * dx1, jnp.sum(dx1 * y, axis=0, keepdims=True), dw, dsh, dsc
    return _rowwise(body, d.NT, d.rt, d.nlt,
                    [_row(dx2), _row(dh2), _row(x1), _row(y), ('seg', gate), ('full', nw), ('seg', shift), ('seg', scale)],
                    [('row', d.D, F32), ('row', d.D, BF16), ('segacc', d.D), ('acc', (1, d.D)), ('segacc', d.D),
                     ('segacc', d.D)], name)


def _res_fwd(x1, f, gate, d, name):
    return _rowwise(lambda i, j, x1, f, g: (x1 + g * f,), d.NT, d.rt, d.nlt, [_row(x1), _row(f), ('seg', gate)],
                    [('row', d.D, F32)], name)[0]


def _res_bwd(dx2, f, gate, d, name):
    def body(i, j, dx2, f, g):
        return g * dx2, jnp.sum(dx2 * f, axis=0, keepdims=True)
    return _rowwise(body, d.NT, d.rt, d.nlt, [_row(dx2), _row(f), ('seg', gate)], [('row', d.D, BF16), ('segacc', d.D)], name)


def _ffn_fwd(h2, w_up, conv_w, conv_b, w_down, d, tag):
    p = _mm([(h2, w_up)], 'nn', F32, tag + '_up')
    u = _dwconv(p, _w8(conv_w, conv_b), d, F32, tag + '_conv')
    tc = _tile(d.FF, GLU_COLS)
    nb = d.FF // tc
    a = _rowwise(lambda i, j, ug, uv: (_silu(ug) * uv,), d.NT, d.rt, d.nlt, [('rowc', u, 0, tc), ('rowc', u, nb, tc)],
                 [('rowc', d.FF, tc, BF16)], tag + '_glu', ncol=nb)[0]
    f = _mm([(a, w_down)], 'nn', F32, tag + '_down')
    return p, u, a, f


def _ffn_bwd(df, h2, p, u, a, w_up, conv_w, w_down, d, tag):
    da = _mm([(df, w_down)], 'nt', F32, tag + '_down_dx')
    dw_down = _mm([(a, df)], 'tn', GRAD_WIRE, tag + '_down_dw')
    tc = _tile(d.FF, GLU_COLS)
    nb = d.FF // tc

    def body(i, j, da, ug, uv):
        return (jnp.where(j < nb, da * uv * _dsilu(ug), da * _silu(ug)),)
    du = _rowwise(body, d.NT, d.rt, d.nlt, [('rowm', da, 0, tc, nb), ('rowm', u, 0, tc, nb), ('rowm', u, nb, tc, nb)],
                  [('rowc', 2 * d.FF, tc, F32)], tag + '_glu_bwd', ncol=2 * nb)[0]
    dp = _dwconv(du, _w8(conv_w[::-1]), d, BF16, tag + '_conv_dx')
    conv_g = _dwconv_wgrad(p, du, d, tag + '_conv_dw')
    dh2 = _mm([(dp, w_up)], 'nt', F32, tag + '_up_dx')
    dw_up = _mm([(h2, dp)], 'tn', GRAD_WIRE, tag + '_up_dw')
    return dh2, dw_up, dw_down, conv_g[0:3], conv_g[3]


def _ab_fwd(z, w, cos, sin_s, d):
    qnw, kvnw = w['q_norm'], w['kv_norm']

    def prep(i, j, zq, zkv, zkr, qw, kw, cos, sin_s):
        return _rms(zq, qw)[0], _rms(zkv, kw)[0], _rope(zkr, cos, sin_s)
    qn, kvn, kr = _rowwise(prep, d.NT, d.rt, d.nlt,
                           [_row(z, 0, d.QL), _row(z, d.z0_kv // d.KVL, d.KVL), _row(z, d.z0_kr // LANE, LANE),
                            ('full', qnw), ('full', kvnw), _row(cos), _row(sin_s)],
                           [('row', d.QL, BF16), ('row', d.KVL, BF16), ('row', LANE, BF16)], 'ab_prep')
    qraw = _mm([(qn, w['w_qb'])], 'nn', F32, 'ab_qb')
    kv = _mm([(kvn, w['w_kvb'])], 'nn', BF16, 'ab_kvb')

    def qrope(i, j, qraw, cos, sin_s):
        parts = []
        for h in range(d.H):
            parts += [qraw[:, 2 * h * LANE:(2 * h + 1) * LANE], _rope(qraw[:, (2 * h + 1) * LANE:(2 * h + 2) * LANE], cos, sin_s)]
        return (jnp.concatenate(parts, axis=1),)
    q = _rowwise(qrope, d.NT, d.rt, d.nlt, [_row(qraw), _row(cos), _row(sin_s)], [('row', 2 * d.HQ, BF16)], 'ab_qrope')[0]

    def kcat_body(i, j, kn, kr):
        parts = []
        for h in range(d.H):
            parts += [kn[:, h * LANE:(h + 1) * LANE], kr]
        return (jnp.concatenate(parts, axis=1),)
    kcat = _rowwise(kcat_body, d.NT, d.rt, d.nlt, [_row(kv, 0, d.HQ), _row(kr)], [('row', 2 * d.HQ, BF16)], 'ab_kcat')[0]
    o, ob, lse = _flash_fwd(q, kcat, kv, d)
    ab = d.z0_ax // d.CC
    s = _rowwise(lambda i, j, ax, ac: (ax * ac,), d.NT, d.rt, d.nlt, [_row(z, ab, d.CC), _row(z, ab + 2, d.CC)],
                 [('row', d.CC, F32)], 'ab_conv_in')[0]
    cv = _dwconv(s, _w8(w['conv_a']), d, F32, 'ab_conv')
    ymix = _rowwise(lambda i, j, a_b, cv, ob: (jnp.concatenate([(a_b * cv).astype(BF16), ob], axis=1),), d.NT, d.rt, d.nlt,
                    [_row(z, ab + 1, d.CC), _row(cv), _row(ob)], [('row', d.CC + d.HQ, BF16)], 'ab_mix')[0]
    return ymix, dict(qn=qn, kvn=kvn, q=q, kcat=kcat, kv=kv, o=o, lse=lse, s=s, cv=cv)


def _ab_bwd(dymix, z, sv, w, cos, sin_s, d):
    qnw, kvnw = w['q_norm'], w['kv_norm']
    assert d.CC % d.HQ == 0

    def dprep(i, j, dmo, o, lse):
        lane = lax.broadcasted_iota(jnp.int32, (1, LANE), 1)
        cols = []
        for h in range(d.H):
            hs = slice(h * LANE, (h + 1) * LANE)
            delta = jnp.sum(dmo[:, hs] * o[:, hs], axis=1, keepdims=True)
            cols.append(jnp.where(lane < LANE // 2, lse[:, hs], delta))
        return dmo, jnp.concatenate(cols, axis=1)
    do, ld = _rowwise(dprep, d.NT, d.rt, d.nlt, [_row(dymix, d.CC // d.HQ, d.HQ), _row(sv['o']), _row(sv['lse'])],
                      [('row', d.HQ, BF16), ('row', d.HQ, F32)], 'ab_do')
    dq, dkc, dvv = _flash_bwd(sv['q'], sv['kcat'], sv['kv'], do, ld, d)

    def qrope_t(i, j, dq, cos, sin_s):
        parts = []
        for h in range(d.H):
            parts += [dq[:, 2 * h * LANE:(2 * h + 1) * LANE], _rope_t(dq[:, (2 * h + 1) * LANE:(2 * h + 2) * LANE], cos, sin_s)]
        return (jnp.concatenate(parts, axis=1),)
    dqraw = _rowwise(qrope_t, d.NT, d.rt, d.nlt, [_row(dq), _row(cos), _row(sin_s)],
                     [('row', 2 * d.HQ, BF16)], 'ab_qrope_bwd')[0]

    def dkv_body(i, j, dkc, dv):
        dkr = dkc[:, LANE:2 * LANE]
        for h in range(1, d.H):
            dkr = dkr + dkc[:, (2 * h + 1) * LANE:(2 * h + 2) * LANE]
        parts = [dkc[:, 2 * h * LANE:(2 * h + 1) * LANE] for h in range(d.H)] + [dv.astype(F32)]
        return jnp.concatenate(parts, axis=1), dkr
    dkv, dkr = _rowwise(dkv_body, d.NT, d.rt, d.nlt, [_row(dkc), _row(dvv)], [('row', 2 * d.HQ, BF16), ('row', LANE, F32)],
                        'ab_dkv')
    dqn = _mm([(dqraw, w['w_qb'])], 'nt', F32, 'ab_qb_dx')
    dw_qb = _mm([(sv['qn'], dqraw)], 'tn', GRAD_WIRE, 'ab_qb_dw')
    dkvn = _mm([(dkv, w['w_kvb'])], 'nt', F32, 'ab_kvb_dx')
    dw_kvb = _mm([(sv['kvn'], dkv)], 'tn', GRAD_WIRE, 'ab_kvb_dw')
    ab = d.z0_ax // d.CC
    dab, dcv = _rowwise(lambda i, j, dya, cv, a_b: (dya * cv, dya * a_b), d.NT, d.rt, d.nlt,
                        [_row(dymix, 0, d.CC), _row(sv['cv']), _row(z, ab + 1, d.CC)],
                        [('row', d.CC, BF16), ('row', d.CC, F32)], 'ab_mix_bwd')
    ds = _dwconv(dcv, _w8(w['conv_a'][::-1]), d, F32, 'ab_conv_dx')
    conv_g = _dwconv_wgrad(sv['s'], dcv, d, 'ab_conv_dw')

    def assemble(i, j, dqn, dkvn, dkr, zq, zkv, qw, kw, cos, sin_s, ds, ax, ac, dab):
        _, xq, rq = _rms(zq, qw)
        dzq, dqw = _rms_bwd(dqn, xq, rq, qw)
        _, xk, rk = _rms(zkv, kw)
        dzkv, dkw = _rms_bwd(dkvn, xk, rk, kw)
        parts = [dzq, dzkv, _rope_t(dkr, cos, sin_s)]
        if d.z0_pad:
            parts.append(jnp.zeros((dzq.shape[0], d.z0_pad), F32))
        parts += [ds * ac, dab.astype(F32), ds * ax]
        return jnp.concatenate([t.astype(BF16) for t in parts], axis=1), dqw, dkw
    dz, dqw, dkw = _rowwise(assemble, d.NT, d.rt, d.nlt,
                            [_row(dqn), _row(dkvn), _row(dkr), _row(z, 0, d.QL), _row(z, d.z0_kv // d.KVL, d.KVL),
                             ('full', qnw), ('full', kvnw), _row(cos), _row(sin_s), _row(ds), _row(z, ab, d.CC),
                             _row(z, ab + 2, d.CC), _row(dab)],
                            [('row', d.ZW0, BF16), ('acc', (1, d.QL)), ('acc', (1, d.KVL))], 'ab_dz')
    return dz, dict(w_qb=dw_qb, w_kvb=dw_kvb, conv_a=conv_g[0:3], q_norm=dqw, kv_norm=dkw)


def _log_sigmoid(x):
    return jnp.minimum(x, 0.0) - jnp.log(1.0 + jnp.exp(-jnp.abs(x)))


def _gla_fwd_block(z, w, d):
    wg, bg, onw = w['gate_w'], w['gate_b'], w['o_norm']

    def gates(i, j, lr, wg, bg):
        return (_log_sigmoid(_dot(lr, wg, 'nn') + bg) / GATE_NORMALIZER,)
    g = _rowwise(gates, d.NT, d.rt, d.nlt, [_row(z, d.z1_lr // LANE, LANE), ('full', wg), ('full', bg)],
                 [('row', 2 * d.KEY, F32)], 'gla_gates')[0]
    of, stf = _gla_fwd(z, g, 0, d)
    ob, stb = _gla_fwd(z, g, 1, d)

    def outp(i, j, of, ob, og, ow):
        o = of + ob
        parts = [_rms(o[:, h * d.dv:(h + 1) * d.dv], ow)[0] for h in range(d.GH)]
        return (jnp.concatenate(parts, axis=1) * _silu(og),)
    ymix = _rowwise(outp, d.NT, d.rt, d.nlt, [_row(of), _row(ob), _row(z, d.z1_og // d.VAL, d.VAL), ('full', onw)],
                    [('row', d.VAL, BF16)], 'gla_out')[0]
    return ymix, dict(g=g, of=of, ob=ob, stf=stf, stb=stb)


def _gla_bwd_block(dymix, z, sv, w, d):
    wg, bg, onw = w['gate_w'], w['gate_b'], w['o_norm']

    def outp_bwd(i, j, dy, of, ob, og, ow):
        o = of + ob
        dn = dy * _silu(og)
        dos, ns = [], []
        dow = jnp.zeros((1, d.dv), F32)
        for h in range(d.GH):
            hs = slice(h * d.dv, (h + 1) * d.dv)
            n, xh, r = _rms(o[:, hs], ow)
            do_h, dw_h = _rms_bwd(dn[:, hs], xh, r, ow)
            dos.append(do_h)
            ns.append(n)
            dow = dow + dw_h
        return jnp.concatenate(dos, axis=1), dy * jnp.concatenate(ns, axis=1) * _dsilu(og), dow
    do, dog, dow = _rowwise(outp_bwd, d.NT, d.rt, d.nlt,
                            [_row(dymix), _row(sv['of']), _row(sv['ob']), _row(z, d.z1_og // d.VAL, d.VAL), ('full', onw)],
                            [('row', d.VAL, F32), ('row', d.VAL, BF16), ('acc', (1, d.dv))], 'gla_out_bwd')
    dq0, dk0, dv0, dg0 = _gla_bwd(z, sv['g'], do, sv['stf'], 0, d)
    dq1, dk1, dv1, dg1 = _gla_bwd(z, sv['g'], do, sv['stb'], 1, d)

    def assemble(i, j, dg0, dg1, lr, wg, bg, dq0, dq1, dk0, dk1, dv0, dv1, dog):
        pre = _dot(lr, wg, 'nn') + bg
        e = jnp.exp(-jnp.abs(pre))
        dpre = jnp.concatenate([dg0, dg1], axis=1) * jnp.where(pre >= 0, e, 1.0) / (1.0 + e) / GATE_NORMALIZER
        dlr = _dot(dpre, wg, 'nt')
        parts = [dv0 + dv1, dog.astype(F32), dk0 + dk1, dq0 + dq1, dlr]
        return (jnp.concatenate([t.astype(BF16) for t in parts], axis=1), _dot(lr, dpre, 'tn'),
                jnp.sum(dpre, axis=0, keepdims=True))
    dz, dwg, dbg = _rowwise(assemble, d.NT, d.rt, d.nlt,
                            [_row(dg0), _row(dg1), _row(z, d.z1_lr // LANE, LANE), ('full', wg), ('full', bg), _row(dq0),
                             _row(dq1), _row(dk0), _row(dk1), _row(dv0), _row(dv1), _row(dog)],
                            [('row', d.ZW1, BF16), ('acc', (LANE, 2 * d.KEY)), ('acc', (1, 2 * d.KEY))], 'gla_dz')
    return dz, dict(gate_fw_w=dwg[:GATE_RANK, :d.KEY], gate_bw_w=dwg[GATE_RANK:2 * GATE_RANK, d.KEY:],
                    gate_fw_b=dbg[:, :d.KEY], gate_bw_b=dbg[:, d.KEY:], o_norm=dow)


def _loss_bwd(x, fnw, target, d):
    def body(i, j, x, w, tgt):
        y, xh, r = _rms(x, w)
        e = y - tgt
        dx, dw = _rms_bwd(e * (1.0 / d.D), xh, r, w)
        lat = i < d.nlt
        part = jnp.sum(jnp.sum(e * e, axis=1, keepdims=True), axis=0, keepdims=True) * (0.5 / d.D)
        return (jnp.where(lat, dx, 0.0), jnp.where(lat, jnp.broadcast_to(part, (8, LANE)), 0.0), jnp.where(lat, dw, 0.0))
    return _rowwise(body, d.NT, d.rt, d.nlt, [_row(x), ('full', fnw), ('rowclamp', target, d.nlt - 1, d.D)],
                    [('row', d.D, F32), ('acc', (8, LANE)), ('acc', (1, d.D))], 'loss')


def _layer_fwd(x, mods, w, mixer_fwd, d, tag):
    sh1, sc1, g1, sh2, sc2, g2 = mods
    h = _mod_fwd(x, w['norm1'], sh1, sc1, d, tag + '_mod1')
    z = _mm([(h, w['w_in'])], 'nn', F32, tag + '_in')
    ymix, msv = mixer_fwd(z)
    y = _mm([(ymix, w['w_out'])], 'nn', F32, tag + '_out')
    x1, h2 = _res_mod_fwd(x, y, g1, w['norm2'], sh2, sc2, d, tag + '_mod2')
    p, u, a, f = _ffn_fwd(h2, w['ffn_up'], w['ffn_conv_w'], w['ffn_conv_b'], w['ffn_down'], d, tag + '_ffn')
    x2 = _res_fwd(x1, f, g2, d, tag + '_res')
    return x2, dict(x=x, h=h, z=z, ymix=ymix, msv=msv, y=y, x1=x1, h2=h2, p=p, u=u, a=a, f=f)


def _layer_bwd(dx2, sv, mods, w, mixer_bwd, d, tag):
    sh1, sc1, g1, sh2, sc2, g2 = mods
    df, dg2 = _res_bwd(dx2, sv['f'], g2, d, tag + '_res_bwd')
    dh2, dw_up, dw_down, dconv_w, dconv_b = _ffn_bwd(df, sv['h2'], sv['p'], sv['u'], sv['a'], w['ffn_up'], w['ffn_conv_w'],
                                                     w['ffn_down'], d, tag + '_ffn')
    dx1, dy, dg1, dn2, dsh2, dsc2 = _res_mod_bwd(dx2, dh2, sv['x1'], sv['y'], g1, w['norm2'], sh2, sc2, d, tag + '_mod2_bwd')
    dymix = _mm([(dy, w['w_out'])], 'nt', F32, tag + '_out_dx')
    dw_out = _mm([(sv['ymix'], dy)], 'tn', GRAD_WIRE, tag + '_out_dw')
    dz, mg = mixer_bwd(dymix, sv['z'], sv['msv'])
    dh = _mm([(dz, w['w_in'])], 'nt', F32, tag + '_in_dx')
    dw_in = _mm([(sv['h'], dz)], 'tn', GRAD_WIRE, tag + '_in_dw')
    dx, dn1, dsh1, dsc1 = _mod_bwd_call(dx1, dh, sv['x'], w['norm1'], sh1, sc1, d, tag + '_mod1_bwd')
    grads = dict(mg, w_in=dw_in, w_out=dw_out, ffn_up=dw_up, ffn_down=dw_down, ffn_conv_w=dconv_w, ffn_conv_b=dconv_b,
                 norm1=dn1, norm2=dn2)
    return dx, grads, [dsh1, dsc1, dg1, dsh2, dsc2, dg2]


def _pad_flat(v, mult=LANE):
    v = v.reshape(-1)
    return jnp.pad(v, (0, (-v.shape[0]) % mult))


def _pack(entries, row_mult):
    flat, offs, pos = [], [], 0
    for v in entries:
        f = _pad_flat(v)
        flat.append(f)
        offs.append(pos)
        pos += f.shape[0]
    tot = jnp.concatenate(flat)
    tot = jnp.pad(tot, (0, (-pos) % (row_mult * LANE)))
    return tot.reshape(-1, LANE), offs


def _unpack(packed, offs, shapes):
    flat = packed.reshape(-1)
    out = []
    for off, shp in zip(offs, shapes):
        n = 1
        for s in shp:
            n *= s
        out.append(flat[off:off + n].reshape(shp))
    return out


def _step(a):
    d = _dims()
    dm = d.D
    me = 4 * lax.axis_index("x") + 2 * lax.axis_index("y") + lax.axis_index("c")
    sds = jax.ShapeDtypeStruct

    def cols(name):
        return _gather_cols(_cast_bf16(a[name], 'cast_' + name), 'ag_' + name)

    def rows(name):
        return _gather_rows(_cast_bf16(a[name], 'cast_' + name), 'ag_' + name)

    small_names = ['l0_conv_a', 'l0_ffn_conv_w', 'l1_ffn_conv_w', 'l1_gate_fw_w', 'l1_gate_bw_w']
    spack, soffs = _pack([a[n] for n in small_names], 8)
    sg = _exchange(spack, False, 'ag_small')
    small_w = {}
    for n, off in zip(small_names, soffs):
        r, c = a[n].shape
        shards = sg.reshape(N_DEV, -1)[:, off:off + r * c].reshape(N_DEV, r, c)
        small_w[n] = shards.transpose(1, 0, 2).reshape(r, N_DEV * c)

    w0 = dict(norm1=a['l0_norm1'].reshape(1, dm), norm2=a['l0_norm2'].reshape(1, dm),
              w_in=_win0_to_kernel(cols('l0_w_in'), d), w_qb=_wqb_to_kernel(cols('l0_w_qb'), d),
              w_kvb=_wkvb_to_kernel(cols('l0_w_kvb'), d), w_out=rows('l0_w_out'),
              q_norm=a['l0_q_norm'].reshape(1, -1), kv_norm=a['l0_kv_norm'].reshape(1, -1), conv_a=small_w['l0_conv_a'],
              ffn_up=cols('l0_ffn_up'), ffn_conv_w=small_w['l0_ffn_conv_w'], ffn_conv_b=a['l0_ffn_conv_b'],
              ffn_down=rows('l0_ffn_down'))
    w1 = dict(norm1=a['l1_norm1'].reshape(1, dm), norm2=a['l1_norm2'].reshape(1, dm),
              w_in=_win1_to_kernel(cols('l1_w_in'), d), w_out=rows('l1_w_out'),
              gate_w=_gate_weight(small_w['l1_gate_fw_w'], small_w['l1_gate_bw_w'], d),
              gate_b=jnp.concatenate([a['l1_gate_fw_b'], a['l1_gate_bw_b']]).reshape(1, -1),
              o_norm=a['l1_o_norm'].reshape(1, -1),
              ffn_up=cols('l1_ffn_up'), ffn_conv_w=small_w['l1_ffn_conv_w'], ffn_conv_b=a['l1_ffn_conv_b'],
              ffn_down=rows('l1_ffn_down'))

    c8 = _exchange(a['c'], False, 'ag_c').reshape(N_DEV, dm)
    c16 = jnp.concatenate([c8, a['c_ctx'].reshape(1, dm), jnp.zeros((7, dm), F32)], axis=0)
    act16, dact16 = _small(lambda v: (_silu(v), _dsilu(v)), [c16], [sds((16, dm), BF16), sds((16, dm), F32)], 'cond_silu')
    n6 = N_MOD * dm // N_DEV
    mod_sh = [_mm([(act16, a[f'l{l}_ada_w'])], 'nn', F32, f'ada{l}') for l in (0, 1)]
    mod_all = _exchange(jnp.concatenate(mod_sh, axis=1), False, 'ag_mod')
    mods = []
    for l in (0, 1):
        full = mod_all[:, :, l * n6:(l + 1) * n6].transpose(1, 0, 2).reshape(16, N_MOD * dm)
        mine = jnp.concatenate([lax.dynamic_slice_in_dim(full, me, 1, 0), full[8:9]], axis=0)
        m2 = _small(lambda r, b: (r + b,), [mine, a[f'l{l}_ada_b'].reshape(1, -1)], [sds((2, N_MOD * dm), F32)], f'ada{l}_bias')[0]
        mods.append([m2[:, k * dm:(k + 1) * dm].reshape(2, 1, dm) for k in range(N_MOD)])

    cos, sin_s = _rope_tables(d)
    x0 = jnp.concatenate([a['x'][0], a['ctx'][0]], axis=0)
    x2, sv0 = _layer_fwd(x0, mods[0], w0, lambda z: _ab_fwd(z, w0, cos, sin_s, d), d, 'l0')
    x4, sv1 = _layer_fwd(x2, mods[1], w1, lambda z: _gla_fwd_block(z, w1, d), d, 'l1')
    dx4, loss_acc, dfn = _loss_bwd(x4, a['final_norm'].reshape(1, dm), a['loss_target'][0], d)

    dx2, g1, dmod1 = _layer_bwd(dx4, sv1, mods[1], w1, lambda dy, z, msv: _gla_bwd_block(dy, z, msv, w1, d), d, 'l1')
    dx0, g0, dmod0 = _layer_bwd(dx2, sv0, mods[0], w0, lambda dy, z, msv: _ab_bwd(dy, z, msv, w0, cos, sin_s, d), d, 'l0')

    dm_rows = jnp.concatenate([jnp.concatenate([t.reshape(2, dm) for t in dmod], axis=1) for dmod in (dmod0, dmod1)], axis=0)
    dm_all = _exchange(dm_rows, False, 'ag_dmod')
    lat = dm_all[:, 0::2].transpose(1, 0, 2)
    ctxs = dm_all[:, 1::2].transpose(1, 0, 2)

    def ada_prep(lat, ctxs):
        csum = jnp.sum(ctxs, axis=1, keepdims=True)
        row = lax.broadcasted_iota(jnp.int32, (1, 8, 1), 1)
        g16 = jnp.concatenate([lat, jnp.where(row == 0, csum, 0.0)], axis=1)
        return g16, jnp.sum(lat, axis=1, keepdims=True) + csum
    g16, gb = _small(ada_prep, [lat, ctxs], [sds((2, 16, N_MOD * dm), F32), sds((2, 1, N_MOD * dm), F32)], 'ada_bwd_prep')
    g16_sh = [lax.dynamic_slice_in_dim(g16[l], me * n6, n6, 1) for l in (0, 1)]
    grad_ada_w = [_mm([(act16, g16_sh[l])], 'tn', F32, f'ada{l}_dw') for l in (0, 1)]
    dact = _mm([(g16_sh[0], a['l0_ada_w']), (g16_sh[1], a['l1_ada_w'])], 'nt', F32, 'ada_dact')
    dcc = _small(lambda t, s: (t * s,), [dact[8:9], dact16[8:9]], [sds((1, dm), F32)], 'cctx_grad')[0]

    big = [('l0_w_in', 'c', _win0_from_kernel(g0['w_in'], d)), ('l0_w_qb', 'c', _wqb_from_kernel(g0['w_qb'], d)),
           ('l0_w_kvb', 'c', _wkvb_from_kernel(g0['w_kvb'], d)), ('l0_w_out', 'r', g0['w_out']),
           ('l0_ffn_up', 'c', g0['ffn_up']), ('l0_ffn_down', 'r', g0['ffn_down']),
           ('l1_w_in', 'c', _win1_from_kernel(g1['w_in'], d)), ('l1_w_out', 'r', g1['w_out']),
           ('l1_ffn_up', 'c', g1['ffn_up']), ('l1_ffn_down', 'r', g1['ffn_down'])]
    res = {}
    for name, kind, dw in big:
        parts = (_scatter_cols if kind == 'c' else _scatter_rows)(dw, 'rs_' + name)
        res[name] = _adam(parts, a[name], a['m_' + name], a['v_' + name], 'adam_' + name)
    for l in (0, 1):
        name = f'l{l}_ada_w'
        res[name] = _adam(grad_ada_w[l][None], a[name], a['m_' + name], a['v_' + name], 'adam_' + name)

    part = {'loss': loss_acc[0:1, 0:1], 'c_ctx': dcc, 'final_norm': dfn,
            'l0_norm1': g0['norm1'], 'l0_norm2': g0['norm2'], 'l0_q_norm': g0['q_norm'], 'l0_kv_norm': g0['kv_norm'],
            'l0_conv_a': g0['conv_a'], 'l0_ffn_conv_w': g0['ffn_conv_w'], 'l0_ffn_conv_b': g0['ffn_conv_b'],
            'l1_norm1': g1['norm1'], 'l1_norm2': g1['norm2'], 'l1_o_norm': g1['o_norm'],
            'l1_gate_fw_w': g1['gate_fw_w'], 'l1_gate_bw_w': g1['gate_bw_w'], 'l1_gate_fw_b': g1['gate_fw_b'],
            'l1_gate_bw_b': g1['gate_bw_b'], 'l1_ffn_conv_w': g1['ffn_conv_w'], 'l1_ffn_conv_b': g1['ffn_conv_b']}
    pkeys = list(part)
    ppack, poffs = _pack([part[k] for k in pkeys], 8)
    psum = _sum_parts(_exchange(ppack, False, 'ag_small_grads'), 'sum_small_grads')
    tot = dict(zip(pkeys, _unpack(psum, poffs, [part[k].shape for k in pkeys])))
    loss = tot['loss'].reshape(())
    sgrad = {}
    for n in _WEIGHTS:
        if n in res:
            continue
        if n.endswith('ada_b'):
            sgrad[n] = gb[int(n[1])].reshape(a[n].shape)
        elif n in small_names:
            c = a[n].shape[1]
            sgrad[n] = lax.dynamic_slice_in_dim(tot[n], me * c, c, 1)
        else:
            sgrad[n] = tot[n].reshape(a[n].shape)
    snames = list(sgrad)
    packs = [_pack([src[n] for n in snames], 8)[0] for src in
             (sgrad, {n: a[n] for n in snames}, {n: a['m_' + n] for n in snames}, {n: a['v_' + n] for n in snames})]
    offs = _pack([sgrad[n] for n in snames], 8)[1]
    outs = _adam(packs[0][None], packs[1], packs[2], packs[3], 'adam_small')
    for k in range(4):
        for n, val in zip(snames, _unpack(outs[k], offs, [a[n].shape for n in snames])):
            res.setdefault(n, [None] * 4)[k] = val

    grad_x = dx0[:d.T].reshape(1, d.T, dm)
    return (loss, grad_x, *[res[n][0] for n in _WEIGHTS], *[res[n][1] for n in _WEIGHTS], *[res[n][2] for n in _WEIGHTS],
            *[res[n][3] for n in _WEIGHTS])


def kernel(*args):
    return _step(dict(zip(_ARGS, args, strict=True)))
```

```python
import functools
import types

import jax
import jax.numpy as jnp
from jax import lax
from jax.experimental import pallas as pl
from jax.experimental.pallas import tpu as pltpu

D_MODEL = 2048
SEQ = 8192
GRID_W = 64
CTX_LEN = 256
EPS = 1e-6
N_MOD = 6
MLA_HEADS = 8
QK_NOPE = 128
QK_ROPE = 64
V_HEAD = 128
Q_LORA = 512
KV_LORA = 256
ROPE_THETA = 10000.0
GLA_HEADS = 4
GATE_RANK = 16
GATE_NORMALIZER = 16.0
CHUNK = 64
D_FF = 5632
ADAM_LR = 0.001
ADAM_B1 = 0.9
ADAM_B2 = 0.999
ADAM_EPS = 1e-08
ADAM_WD = 0.01
ADAM_STEP = 10

N_DEV = 8
LANE = 128
VMEM_LIMIT = 56 * 1024 * 1024
CONV_COLS = 2816
GLU_COLS = 1408

F32 = jnp.float32
BF16 = jnp.bfloat16
GRAD_WIRE = jnp.bfloat16
MESH_ID = pl.DeviceIdType.MESH

_FWD = ['x', 'c', 'ctx', 'c_ctx', 'l0_ada_w', 'l0_ada_b', 'l0_norm1', 'l0_w_in', 'l0_conv_a', 'l0_q_norm', 'l0_w_qb',
        'l0_kv_norm', 'l0_w_kvb', 'l0_w_out', 'l0_norm2', 'l0_ffn_up', 'l0_ffn_conv_w', 'l0_ffn_conv_b', 'l0_ffn_down',
        'l1_ada_w', 'l1_ada_b', 'l1_norm1', 'l1_w_in', 'l1_gate_fw_w', 'l1_gate_fw_b', 'l1_gate_bw_w', 'l1_gate_bw_b',
        'l1_o_norm', 'l1_w_out', 'l1_norm2', 'l1_ffn_up', 'l1_ffn_conv_w', 'l1_ffn_conv_b', 'l1_ffn_down', 'final_norm']
_WEIGHTS = _FWD[3:]
_ARGS = _FWD + ['loss_target'] + ['m_' + n for n in _WEIGHTS] + ['v_' + n for n in _WEIGHTS]


def _dims():
    d = types.SimpleNamespace()
    d.D, d.T, d.TC = D_MODEL, SEQ, CTX_LEN
    d.NT = d.T + d.TC
    d.rt = 256 if d.TC % 256 == 0 else 128
    d.nlt = d.T // d.rt
    d.H = MLA_HEADS
    d.QL, d.KVL = Q_LORA, KV_LORA
    d.CC = D_MODEL // 2
    d.z0_kv = d.QL
    d.z0_kr = d.QL + d.KVL
    d.z0_pad = (-(d.QL + d.KVL + LANE)) % d.CC
    d.z0_ax = d.QL + d.KVL + LANE + d.z0_pad
    d.ZW0 = d.z0_ax + 3 * d.CC
    d.AB_COLS = d.KVL + QK_ROPE + d.QL + 3 * d.CC
    d.HQ = d.H * LANE
    d.GH = GLA_HEADS
    d.KEY = D_MODEL // 2
    d.VAL = D_MODEL
    d.dk = d.KEY // d.GH
    d.dv = d.VAL // d.GH
    d.z1_og = d.VAL
    d.z1_k = 2 * d.VAL
    d.z1_q = 2 * d.VAL + d.KEY
    d.z1_lr = 2 * d.VAL + 2 * d.KEY
    d.ZW1 = d.z1_lr + LANE
    d.GLA_COLS = 2 * d.KEY + 2 * d.VAL + 2 * GATE_RANK
    d.FF = D_FF
    d.tq = min(256, d.TC)
    return d


def _tile(n, pref, align=LANE):
    if n <= pref:
        return n
    t = (pref // align) * align
    while t >= align:
        if n % t == 0:
            return t
        t -= align
    raise ValueError(f"no tile for {n}")


def _cparams(sem):
    return pltpu.CompilerParams(dimension_semantics=sem, vmem_limit_bytes=VMEM_LIMIT)


def _dot(a, b, mode):
    dims = {'nn': (((1,), (0,)), ((), ())), 'nt': (((1,), (1,)), ((), ())), 'tn': (((0,), (0,)), ((), ()))}[mode]
    return lax.dot_general(a.astype(BF16), b.astype(BF16), dims, preferred_element_type=F32)


class _Rides:
    def __init__(self):
        self.pending, self.done = {}, {}

    def add(self, host, key, x, stacked):
        self.pending.setdefault(host, []).append((key, x, stacked))

    def take(self, host):
        return self.pending.pop(host, [])


def _ride_plumbing(riders):
    n = len(riders)
    if not n:
        return [], [], [], []
    spec = pl.BlockSpec(memory_space=pl.ANY)
    shapes = [jax.ShapeDtypeStruct((N_DEV,) + tuple(x.shape[1:] if st else x.shape), x.dtype) for _, x, st in riders]
    sems = [pltpu.SemaphoreType.DMA((n * (N_DEV - 1),)), pltpu.SemaphoreType.DMA((n * (N_DEV - 1),)),
            pltpu.SemaphoreType.DMA((n,))]
    return [spec] * n, shapes, [spec] * n, sems


def _ride_copies(stacked_flags, x_refs, o_refs, send_sems, recv_sems, local_sems):
    ix, iy, ic = lax.axis_index("x"), lax.axis_index("y"), lax.axis_index("c")
    me = 4 * ix + 2 * iy + ic
    local, sends, recvs = [], [], []
    for r, (stacked, x_ref, o_ref) in enumerate(zip(stacked_flags, x_refs, o_refs)):
        def src(p, x_ref=x_ref, stacked=stacked):
            return x_ref.at[p] if stacked else x_ref

        local.append(pltpu.make_async_copy(src(me), o_ref.at[me], local_sems.at[r]))
        for k in range(1, N_DEV):
            px, py, pc = (ix + ((k >> 2) & 1)) % 2, (iy + ((k >> 1) & 1)) % 2, (ic + (k & 1)) % 2
            peer = 4 * px + 2 * py + pc
            s = r * (N_DEV - 1) + k - 1
            sends.append(pltpu.make_async_remote_copy(
                src_ref=src(peer), dst_ref=o_ref.at[me], send_sem=send_sems.at[s], recv_sem=recv_sems.at[s],
                device_id=(px, py, pc), device_id_type=MESH_ID))
            recvs.append(pltpu.make_async_remote_copy(
                src_ref=src(peer), dst_ref=o_ref.at[peer], send_sem=send_sems.at[s], recv_sem=recv_sems.at[s],
                device_id=(px, py, pc), device_id_type=MESH_ID))

    def start():
        for cp in local + sends:
            cp.start()

    def wait():
        for cp in recvs:
            cp.wait_recv()
        for cp in sends:
            cp.wait_send()
        for cp in local:
            cp.wait()

    return start, wait


def _mm(pairs, mode, out_dtype, name, tm=768, tn=1024, tk=2816, rides=None):
    riders = rides.take(name) if rides is not None else []
    nr = len(riders)
    r_in, r_shapes, r_out, r_sems = _ride_plumbing(riders)
    a0, b0 = pairs[0]
    if mode == 'nn':
        (m, k), n = a0.shape, b0.shape[1]
    elif mode == 'nt':
        (m, k), n = a0.shape, b0.shape[0]
    else:
        (k, m), n = a0.shape, b0.shape[1]
    tm, tn, tk = _tile(m, tm), _tile(n, tn), _tile(k, tk)
    nk = k // tk
    if mode == 'nn':
        a_spec = pl.BlockSpec((tm, tk), lambda i, j, kk: (i, kk))
        b_spec = pl.BlockSpec((tk, tn), lambda i, j, kk: (kk, j))
    elif mode == 'nt':
        a_spec = pl.BlockSpec((tm, tk), lambda i, j, kk: (i, kk))
        b_spec = pl.BlockSpec((tn, tk), lambda i, j, kk: (j, kk))
    else:
        a_spec = pl.BlockSpec((tk, tm), lambda i, j, kk: (kk, i))
        b_spec = pl.BlockSpec((tk, tn), lambda i, j, kk: (kk, j))
    npairs = len(pairs)

    nin = 2 * npairs
    gi, gj = m // tm, n // tn

    def body(*refs):
        o_ref, acc_ref = refs[nin + nr], refs[nin + 2 * nr + 1]
        i, j, kk = pl.program_id(0), pl.program_id(1), pl.program_id(2)
        if nr:
            start, wait = _ride_copies([st for _, _, st in riders], refs[nin:nin + nr], refs[nin + nr + 1:nin + 2 * nr + 1],
                                       *refs[nin + 2 * nr + 2:])
            pl.when(jnp.logical_and(jnp.logical_and(i == 0, j == 0), kk == 0))(start)

        def dots():
            s = None
            for p in range(npairs):
                t = _dot(refs[2 * p][...], refs[2 * p + 1][...], mode)
                s = t if s is None else s + t
            return s

        if nk == 1:
            o_ref[...] = dots().astype(o_ref.dtype)
        else:
            @pl.when(kk == 0)
            def _():
                acc_ref[...] = dots()

            @pl.when(jnp.logical_and(kk > 0, kk < nk - 1))
            def _():
                acc_ref[...] += dots()

            @pl.when(kk == nk - 1)
            def _():
                o_ref[...] = (acc_ref[...] + dots()).astype(o_ref.dtype)

        if nr:
            pl.when(jnp.logical_and(jnp.logical_and(i == gi - 1, j == gj - 1), kk == nk - 1))(wait)

    flat = [t for ab in pairs for t in ab]
    res = pl.pallas_call(
        body, name=name,
        out_shape=[jax.ShapeDtypeStruct((m, n), out_dtype)] + r_shapes,
        grid=(gi, gj, nk),
        in_specs=[a_spec, b_spec] * npairs + r_in,
        out_specs=[pl.BlockSpec((tm, tn), lambda i, j, kk: (i, j))] + r_out,
        scratch_shapes=[pltpu.VMEM((tm, tn), F32)] + r_sems,
        compiler_params=_cparams(("arbitrary",) * 3 if nr else ("parallel", "parallel", "arbitrary")),
    )(*flat, *[x for _, x, _ in riders])
    for (key, _, _), arr in zip(riders, res[1:]):
        rides.done[key] = arr
    return res[0]


def _rowwise(body, nrows, rt, nlt, ins, outs, name, ncol=1):
    ntiles = nrows // rt
    in_specs, args = [], []
    for spec in ins:
        kind, arr = spec[0], spec[1]
        if kind == 'row':
            in_specs.append(pl.BlockSpec((rt, spec[3]), functools.partial(lambda i, j, cb: (i, cb), cb=spec[2])))
        elif kind == 'rowc':
            in_specs.append(pl.BlockSpec((rt, spec[3]), functools.partial(lambda i, j, cb: (i, cb + j), cb=spec[2])))
        elif kind == 'rowm':
            in_specs.append(pl.BlockSpec(
                (rt, spec[3]), functools.partial(lambda i, j, cb, md: (i, cb + j % md), cb=spec[2], md=spec[4])))
        elif kind == 'rowclamp':
            in_specs.append(pl.BlockSpec(
                (rt, spec[3]), functools.partial(lambda i, j, mb: (jnp.minimum(i, mb), 0), mb=spec[2])))
        elif kind == 'row3':
            in_specs.append(pl.BlockSpec((arr.shape[0], rt, arr.shape[2]), lambda i, j: (0, i, 0)))
        elif kind == 'seg':
            in_specs.append(pl.BlockSpec((None, 1, arr.shape[2]), lambda i, j: (i // nlt, 0, 0)))
        else:
            in_specs.append(pl.BlockSpec(arr.shape, functools.partial(lambda i, j, nd: (0,) * nd, nd=arr.ndim)))
        args.append(arr)
    out_shapes, out_specs = [], []
    for spec in outs:
        kind = spec[0]
        if kind == 'row':
            out_shapes.append(jax.ShapeDtypeStruct((nrows, spec[1]), spec[2]))
            out_specs.append(pl.BlockSpec((rt, spec[1]), lambda i, j: (i, 0)))
        elif kind == 'rowc':
            out_shapes.append(jax.ShapeDtypeStruct((nrows, spec[1]), spec[3]))
            out_specs.append(pl.BlockSpec((rt, spec[2]), lambda i, j: (i, j)))
        elif kind == 'acc':
            out_shapes.append(jax.ShapeDtypeStruct(spec[1], F32))
            out_specs.append(pl.BlockSpec(spec[1], functools.partial(lambda i, j, nd: (0,) * nd, nd=len(spec[1]))))
        else:
            out_shapes.append(jax.ShapeDtypeStruct((2, 1, spec[1]), F32))
            out_specs.append(pl.BlockSpec((None, 1, spec[1]), lambda i, j: (i // nlt, 0, 0)))
    n_in = len(ins)
    has_acc = any(s[0] in ('acc', 'segacc') for s in outs)

    def kern(*refs):
        i = pl.program_id(0)
        j = pl.program_id(1)
        vals = [r[...] for r in refs[:n_in]]
        res = body(i, j, *vals)
        for spec, ref, val in zip(outs, refs[n_in:], res):
            if spec[0] in ('row', 'rowc'):
                ref[...] = val.astype(ref.dtype)
            else:
                first = (i == 0) if spec[0] == 'acc' else jnp.logical_or(i == 0, i == nlt)

                @pl.when(first)
                def _(ref=ref, val=val):
                    ref[...] = val

                @pl.when(jnp.logical_not(first))
                def _(ref=ref, val=val):
                    ref[...] += val

    return pl.pallas_call(
        kern, name=name, out_shape=tuple(out_shapes), grid=(ntiles, ncol),
        in_specs=in_specs, out_specs=tuple(out_specs),
        compiler_params=_cparams(("arbitrary", "arbitrary") if has_acc else ("parallel", "parallel")),
    )(*args)


def _small(body, args, out_shapes, name):
    n_in = len(args)

    def kern(*refs):
        res = body(*[r[...] for r in refs[:n_in]])
        for ref, val in zip(refs[n_in:], res):
            ref[...] = val.astype(ref.dtype)

    return pl.pallas_call(
        kern, name=name, out_shape=tuple(out_shapes),
        in_specs=[pl.BlockSpec(memory_space=pltpu.VMEM)] * n_in,
        out_specs=tuple(pl.BlockSpec(memory_space=pltpu.VMEM) for _ in out_shapes),
        compiler_params=pltpu.CompilerParams(vmem_limit_bytes=VMEM_LIMIT),
    )(*args)


def _exchange(x, stacked, name):
    r_in, r_shapes, r_out, r_sems = _ride_plumbing([(name, x, stacked)])

    def body(x_ref, o_ref, send_sems, recv_sems, local_sems):
        start, wait = _ride_copies([stacked], [x_ref], [o_ref], send_sems, recv_sems, local_sems)
        start()
        wait()

    return pl.pallas_call(body, name=name, out_shape=r_shapes[0], in_specs=r_in, out_specs=r_out[0], scratch_shapes=r_sems)(x)


def _cast_bf16(x, name):
    r, c = x.shape
    tr = _tile(r, 256, 8)
    return pl.pallas_call(
        lambda x_ref, o_ref: o_ref.__setitem__(Ellipsis, x_ref[...].astype(BF16)), name=name,
        out_shape=jax.ShapeDtypeStruct((r, c), BF16), grid=(r // tr,),
        in_specs=[pl.BlockSpec((tr, c), lambda i: (i, 0))], out_specs=pl.BlockSpec((tr, c), lambda i: (i, 0)),
        compiler_params=_cparams(("parallel",)),
    )(x)


def _adam(parts, w, m, v, name):
    p, r, c = parts.shape
    tr = _tile(r, 64, 8)

    def body(p_ref, w_ref, m_ref, v_ref, g_ref, d_ref, nm_ref, nv_ref):
        g = p_ref[0].astype(F32)
        for q in range(1, p):
            g = g + p_ref[q].astype(F32)
        nm = ADAM_B1 * m_ref[...] + (1.0 - ADAM_B1) * g
        nv = ADAM_B2 * v_ref[...] + (1.0 - ADAM_B2) * (g * g)
        m_hat = nm / (1.0 - ADAM_B1 ** ADAM_STEP)
        v_hat = nv / (1.0 - ADAM_B2 ** ADAM_STEP)
        g_ref[...] = g
        d_ref[...] = -ADAM_LR * (m_hat / (jnp.sqrt(v_hat) + ADAM_EPS) + ADAM_WD * w_ref[...])
        nm_ref[...] = nm
        nv_ref[...] = nv

    spec = pl.BlockSpec((tr, c), lambda i: (i, 0))
    return pl.pallas_call(
        body, name=name, out_shape=tuple(jax.ShapeDtypeStruct((r, c), F32) for _ in range(4)), grid=(r // tr,),
        in_specs=[pl.BlockSpec((p, tr, c), lambda i: (0, i, 0)), spec, spec, spec], out_specs=(spec,) * 4,
        compiler_params=_cparams(("parallel",)),
    )(parts, w, m, v)


def _sum_parts(parts, name):
    p, r, c = parts.shape
    tr = _tile(r, 256, 8)

    def body(p_ref, o_ref):
        g = p_ref[0]
        for q in range(1, p):
            g = g + p_ref[q]
        o_ref[...] = g

    return pl.pallas_call(
        body, name=name, out_shape=jax.ShapeDtypeStruct((r, c), F32), grid=(r // tr,),
        in_specs=[pl.BlockSpec((p, tr, c), lambda i: (0, i, 0))], out_specs=pl.BlockSpec((tr, c), lambda i: (i, 0)),
        compiler_params=_cparams(("parallel",)),
    )(parts)


def _shifted(cur, prev8, next8, i, rt, nlt, ntiles):
    first = jnp.logical_or(i == 0, i == nlt)
    last = jnp.logical_or(i == nlt - 1, i == ntiles - 1)
    prev_row = jnp.where(first, 0.0, prev8[7:8, :])
    next_row = jnp.where(last, 0.0, next8[0:1, :])
    rows = lax.broadcasted_iota(jnp.int32, (rt, 1), 0)
    x_m1 = jnp.where(rows == 0, prev_row, pltpu.roll(cur, 1, 0))
    x_p1 = jnp.where(rows == rt - 1, next_row, pltpu.roll(cur, rt - 1, 0))
    return x_m1, x_p1


def _halo_specs(rt, tc, nrows, col_axis_first):
    r8 = rt // 8
    last8 = nrows // 8 - 1
    if col_axis_first:
        return [pl.BlockSpec((8, tc), lambda j, i: (jnp.maximum(i * r8 - 1, 0), j)),
                pl.BlockSpec((rt, tc), lambda j, i: (i, j)),
                pl.BlockSpec((8, tc), lambda j, i: (jnp.minimum((i + 1) * r8, last8), j))]
    return [pl.BlockSpec((8, tc), lambda i, j: (jnp.maximum(i * r8 - 1, 0), j)),
            pl.BlockSpec((rt, tc), lambda i, j: (i, j)),
            pl.BlockSpec((8, tc), lambda i, j: (jnp.minimum((i + 1) * r8, last8), j))]


def _dwconv(x, w8, d, out_dtype, name):
    nrows, c = x.shape
    rt, nlt = d.rt, d.nlt
    tc = _tile(c, CONV_COLS)
    ntiles = nrows // rt

    def body(p_ref, c_ref, n_ref, w_ref, o_ref):
        i = pl.program_id(0)
        cur = c_ref[...]
        x_m1, x_p1 = _shifted(cur, p_ref[...], n_ref[...], i, rt, nlt, ntiles)
        w = w_ref[...]
        o_ref[...] = (x_m1 * w[0:1] + cur * w[1:2] + x_p1 * w[2:3] + w[3:4]).astype(o_ref.dtype)

    return pl.pallas_call(
        body, name=name, out_shape=jax.ShapeDtypeStruct((nrows, c), out_dtype), grid=(ntiles, c // tc),
        in_specs=_halo_specs(rt, tc, nrows, False) + [pl.BlockSpec((8, tc), lambda i, j: (0, j))],
        out_specs=pl.BlockSpec((rt, tc), lambda i, j: (i, j)),
        compiler_params=_cparams(("parallel", "parallel")),
    )(x, x, x, w8)


def _dwconv_wgrad(x, dy, d, name):
    nrows, c = x.shape
    rt, nlt = d.rt, d.nlt
    tc = _tile(c, CONV_COLS)
    ntiles = nrows // rt

    def body(p_ref, c_ref, n_ref, dy_ref, o_ref):
        i = pl.program_id(1)
        cur = c_ref[...]
        dy = dy_ref[...]
        x_m1, x_p1 = _shifted(cur, p_ref[...], n_ref[...], i, rt, nlt, ntiles)
        sums = [jnp.sum(t * dy, axis=0, keepdims=True) for t in (x_m1, cur, x_p1)] + [jnp.sum(dy, axis=0, keepdims=True)]
        row = lax.broadcasted_iota(jnp.int32, (8, 1), 0)
        part = jnp.zeros((8, tc), F32)
        for k, s in enumerate(sums):
            part = jnp.where(row == k, s, part)

        @pl.when(i == 0)
        def _():
            o_ref[...] = part

        @pl.when(i != 0)
        def _():
            o_ref[...] += part

    return pl.pallas_call(
        body, name=name, out_shape=jax.ShapeDtypeStruct((8, c), F32), grid=(c // tc, ntiles),
        in_specs=_halo_specs(rt, tc, nrows, True) + [pl.BlockSpec((rt, tc), lambda j, i: (i, j))],
        out_specs=pl.BlockSpec((8, tc), lambda j, i: (0, j)),
        compiler_params=_cparams(("parallel", "arbitrary")),
    )(x, x, x, dy)


def _w8(w3, b=None):
    c = w3.shape[1]
    brow = jnp.zeros((1, c), F32) if b is None else b.reshape(1, c)
    return jnp.concatenate([w3, brow, jnp.zeros((4, c), F32)], axis=0)


def _rms(x, w):
    r = lax.rsqrt(jnp.mean(x * x, axis=-1, keepdims=True) + EPS)
    xh = x * r
    return xh * w, xh, r


def _rms_bwd(dy, xh, r, w):
    dxh = dy * w
    dx = r * (dxh - xh * jnp.mean(dxh * xh, axis=-1, keepdims=True))
    return dx, jnp.sum(dy * xh, axis=0, keepdims=True)


def _mod_bwd(dh, x, w, shift, scale):
    n, xh, r = _rms(x, w)
    dshift = jnp.sum(dh, axis=0, keepdims=True)
    dscale = jnp.sum(dh * n, axis=0, keepdims=True)
    dx, dw = _rms_bwd(dh * (1.0 + scale), xh, r, w)
    return dx, dw, dshift, dscale


def _sigmoid(x):
    return 1.0 / (1.0 + jnp.exp(-x))


def _silu(x):
    return x * _sigmoid(x)


def _dsilu(x):
    s = _sigmoid(x)
    return s * (1.0 + x * (1.0 - s))


def _rope(x, cos, sin_s):
    return x * cos + pltpu.roll(x, LANE // 2, 1) * sin_s


def _rope_t(dy, cos, sin_s):
    return dy * cos + pltpu.roll(dy * sin_s, LANE // 2, 1)


def _flash_fwd(q, kcat, kv, d, rides=None):
    riders = rides.take("flash_fwd") if rides is not None else []
    nr = len(riders)
    r_in, r_shapes, r_out, r_sems = _ride_plumbing(riders)
    nt, h, tq = d.NT, d.H, d.tq
    tkb = _tile(d.T, 4096)
    n_big = d.T // tkb
    nq_lat = d.T // tq
    scale = float(QK_NOPE + QK_ROPE) ** -0.5

    def body(*refs):
        q_ref, k_ref, v_ref = refs[:3]
        o_ref, ob_ref, lse_ref = refs[3 + nr:6 + nr]
        qi = pl.program_id(1)
        if nr:
            start, wait = _ride_copies([st for _, _, st in riders], refs[3:3 + nr], refs[6 + nr:6 + 2 * nr], *refs[6 + 2 * nr:])
            pl.when(jnp.logical_and(pl.program_id(0) == 0, qi == 0))(start)
        q_t = q_ref[...]

        def step(k0, tk, carry):
            m, l, acc = carry
            ks = pl.ds(k0, tk)
            s = _dot(q_t, k_ref[ks, :], 'nt') * scale
            m_new = jnp.maximum(m, jnp.max(s, axis=1, keepdims=True))
            p = jnp.exp(s - m_new)
            alpha = jnp.exp(m - m_new)
            return m_new, alpha * l + jnp.sum(p, axis=1, keepdims=True), alpha * acc + _dot(p, v_ref[ks, :], 'nn')

        init = (jnp.full((tq, 1), -1e30, F32), jnp.zeros((tq, 1), F32), jnp.zeros((tq, LANE), F32))
        trips = jnp.where(qi < nq_lat, n_big, 0)
        carry = lax.fori_loop(0, trips, lambda t, c: step(pl.multiple_of(t * tkb, tkb), tkb, c), init)
        m, l, acc = step(d.T, d.TC, carry)
        o = acc / l
        o_ref[...] = o
        ob_ref[...] = o.astype(BF16)
        lse_ref[...] = jnp.broadcast_to(m + jnp.log(l), (tq, LANE))
        if nr:
            pl.when(jnp.logical_and(pl.program_id(0) == h - 1, qi == nt // tq - 1))(wait)

    out = pl.BlockSpec((tq, LANE), lambda hh, i: (i, hh))
    res = pl.pallas_call(
        body, name="flash_fwd",
        out_shape=[jax.ShapeDtypeStruct((nt, d.HQ), F32), jax.ShapeDtypeStruct((nt, d.HQ), BF16),
                   jax.ShapeDtypeStruct((nt, d.HQ), F32)] + r_shapes,
        grid=(h, nt // tq),
        in_specs=[pl.BlockSpec((tq, 2 * LANE), lambda hh, i: (i, hh)), pl.BlockSpec((nt, 2 * LANE), lambda hh, i: (0, hh)),
                  pl.BlockSpec((nt, LANE), lambda hh, i: (0, h + hh))] + r_in,
        out_specs=[out, out, out] + r_out,
        scratch_shapes=r_sems,
        compiler_params=_cparams(("arbitrary", "arbitrary") if nr else ("parallel", "parallel")),
    )(q, kcat, kv, *[x for _, x, _ in riders])
    for (key, _, _), arr in zip(riders, res[3:]):
        rides.done[key] = arr
    return res[0], res[1], res[2]


def _flash_bwd(q, kcat, kv, do, ld, d, rides=None):
    riders = rides.take("flash_bwd") if rides is not None else []
    nr = len(riders)
    r_in, r_shapes, r_out, r_sems = _ride_plumbing(riders)
    nt, h, tk = d.NT, d.H, d.tq
    tqb =_tile(d.T, 2048)
    n_big = d.T // tqb
    nk_lat = d.T // tk
    scale = float(QK_NOPE + QK_ROPE) ** -0.5

    def body(*refs):
        q_ref, do_ref, ld_ref, k_ref, v_ref = refs[:5]
        dq_ref, dk_ref, dv_ref = refs[5 + nr:8 + nr]
        kt = pl.program_id(1)
        if nr:
            start, wait = _ride_copies([st for _, _, st in riders], refs[5:5 + nr], refs[8 + nr:8 + 2 * nr], *refs[8 + 2 * nr:])
            pl.when(jnp.logical_and(pl.program_id(0) == 0, kt == 0))(start)
        k_t, v_t = k_ref[...], v_ref[...]

        @pl.when(kt == 0)
        def _():
            dq_ref[...] = jnp.zeros_like(dq_ref)

        def step(q0, tq, carry):
            dk, dv = carry
            qs = pl.ds(q0, tq)
            q_t, do_t, ld_t = q_ref[qs, :], do_ref[qs, :], ld_ref[qs, :]
            s = _dot(q_t, k_t, 'nt') * scale
            p = jnp.exp(s - ld_t[:, 0:1])
            ds = p * (_dot(do_t, v_t, 'nt') - ld_t[:, LANE // 2:LANE // 2 + 1]) * scale
            dq_ref[qs, :] += _dot(ds, k_t, 'nn')
            return dk + _dot(ds, q_t, 'tn'), dv + _dot(p, do_t, 'tn')

        init = (jnp.zeros((tk, 2 * LANE), F32), jnp.zeros((tk, LANE), F32))
        carry = lax.fori_loop(0, n_big, lambda t, c: step(pl.multiple_of(t * tqb, tqb), tqb, c), init)
        dk_ref[...] = carry[0]
        dv_ref[...] = carry[1].astype(BF16)

        @pl.when(kt >= nk_lat)
        def _():
            dk, dv = step(d.T, d.TC, carry)
            dk_ref[...] = dk
            dv_ref[...] = dv.astype(BF16)

        if nr:
            pl.when(jnp.logical_and(pl.program_id(0) == h - 1, kt == nt // tk - 1))(wait)

    res = lambda w: pl.BlockSpec((nt, w), lambda hh, i: (0, hh))
    outs = pl.pallas_call(
        body, name="flash_bwd",
        out_shape=[jax.ShapeDtypeStruct((nt, 2 * d.HQ), F32), jax.ShapeDtypeStruct((nt, 2 * d.HQ), F32),
                   jax.ShapeDtypeStruct((nt, d.HQ), BF16)] + r_shapes,
        grid=(h, nt // tk),
        in_specs=[res(2 * LANE), res(LANE), res(LANE), pl.BlockSpec((tk, 2 * LANE), lambda hh, i: (i, hh)),
                  pl.BlockSpec((tk, LANE), lambda hh, i: (i, h + hh))] + r_in,
        out_specs=[res(2 * LANE), pl.BlockSpec((tk, 2 * LANE), lambda hh, i: (i, hh)),
                   pl.BlockSpec((tk, LANE), lambda hh, i: (i, hh))] + r_out,
        scratch_shapes=r_sems,
        compiler_params=_cparams(("arbitrary", "arbitrary") if nr else ("parallel", "arbitrary")),
    )(q, do, ld, kcat, kv, *[x for _, x, _ in riders])
    for (key, _, _), arr in zip(riders, outs[3:]):
        rides.done[key] = arr
    return outs[0], outs[1], outs[2]


def _tri(dirn):
    r = lax.broadcasted_iota(jnp.int32, (CHUNK, CHUNK), 0)
    c = lax.broadcasted_iota(jnp.int32, (CHUNK, CHUNK), 1)
    return (c <= r) if dirn == 0 else (c >= r)


def _exact_mask_dot(mask_bf16, x):
    hi = x.astype(BF16)
    r1 = x - hi.astype(F32)
    mid = r1.astype(BF16)
    lo = (r1 - mid.astype(F32)).astype(BF16)
    dot = lambda t: lax.dot_general(mask_bf16, t, (((1,), (0,)), ((), ())), preferred_element_type=F32)
    return dot(hi) + dot(mid) + dot(lo)


def _gla_terms(q, k, g, dirn, dk):
    mb = _tri(dirn)
    b = _exact_mask_dot(mb.astype(BF16), g)
    tot = jnp.sum(g, axis=0, keepdims=True)
    qe = q * (float(dk) ** -0.5) * jnp.exp(b)
    ke = k * jnp.exp(-b)
    kd = k * jnp.exp(tot - b)
    att = jnp.where(mb, _dot(qe, ke, 'nt'), 0.0)
    return mb, b, tot, qe, ke, kd, att


def _gla_block_index(d, dirn):
    nb = d.NT // d.rt
    if dirn == 0:
        return lambda s: (s + d.nlt) % nb
    return lambda s: nb - 1 - s


def _gla_rows(j, nsub, dirn):
    r0 = (j if dirn == 0 else nsub - 1 - j) * CHUNK
    return slice(r0, r0 + CHUNK)


def _gla_fwd(z, g, dirn, d):
    nt, gh, dk, dv, rb = d.NT, d.GH, d.dk, d.dv, d.rt
    nb, nsub = nt // rb, rb // CHUNK
    bidx = _gla_block_index(d, dirn)
    qb, kb, gb = d.z1_q // dk, d.z1_k // dk, dirn * gh

    def body(q_ref, k_ref, v_ref, g_ref, o_ref, st_ref, state):
        @pl.when(pl.program_id(1) == 0)
        def _():
            state[...] = jnp.zeros_like(state)

        st = state[...]
        for j in range(nsub):
            rs = _gla_rows(j, nsub, dirn)
            st_ref[j] = st
            v = v_ref[rs, :]
            _, _, tot, qe, _, kd, att = _gla_terms(q_ref[rs, :], k_ref[rs, :], g_ref[rs, :], dirn, dk)
            o_ref[rs, :] = _dot(att, v, 'nn') + _dot(qe, st, 'nt')
            st = st * jnp.exp(tot) + _dot(v, kd, 'tn')
        state[...] = st

    col = lambda w, off: pl.BlockSpec((rb, w), functools.partial(lambda hh, s, off: (bidx(s), off + hh), off=off))
    return pl.pallas_call(
        body, name=f"gla_fwd_{dirn}",
        out_shape=(jax.ShapeDtypeStruct((nt, d.VAL), F32), jax.ShapeDtypeStruct((gh, nb * nsub, dv, dk), F32)),
        grid=(gh, nb),
        in_specs=[col(dk, qb), col(dk, kb), col(dv, 0), col(dk, gb)],
        out_specs=(col(dv, 0), pl.BlockSpec((None, nsub, dv, dk), lambda hh, s: (hh, s, 0, 0))),
        scratch_shapes=[pltpu.VMEM((dv, dk), F32)],
        compiler_params=_cparams(("parallel", "arbitrary")),
    )(z, z, z, g)


def _gla_bwd(z, g, do, states, dirn, d):
    nt, gh, dk, dv, rb = d.NT, d.GH, d.dk, d.dv, d.rt
    nb, nsub = nt // rb, rb // CHUNK
    bfwd = _gla_block_index(d, dirn)
    bidx = lambda s: bfwd(nb - 1 - s)
    qb, kb, gb = d.z1_q // dk, d.z1_k // dk, dirn * gh
    qscale = float(dk) ** -0.5

    def body(q_ref, k_ref, v_ref, g_ref, do_ref, st_ref, dq_ref, dk_ref, dv_ref, dg_ref, dstate):
        @pl.when(pl.program_id(1) == 0)
        def _():
            dstate[...] = jnp.zeros_like(dstate)

        dst = dstate[...]
        for j in reversed(range(nsub)):
            rs = _gla_rows(j, nsub, dirn)
            q, k, v, g_, dout, st = q_ref[rs, :], k_ref[rs, :], v_ref[rs, :], g_ref[rs, :], do_ref[rs, :], st_ref[j]
            mb, b, tot, qe, ke, kd, att = _gla_terms(q, k, g_, dirn, dk)
            etot = jnp.exp(tot)
            datt = jnp.where(mb, _dot(dout, v, 'nt'), 0.0)
            dv_ref[rs, :] = _dot(att, dout, 'tn') + _dot(kd, dst, 'nt')
            dqe = _dot(datt, ke, 'nn') + _dot(dout, st, 'nn')
            dke = _dot(datt, qe, 'tn')
            dkd = _dot(v, dst, 'nn')
            dq_ref[rs, :] = dqe * (qscale * jnp.exp(b))
            dk_ref[rs, :] = dke * jnp.exp(-b) + dkd * jnp.exp(tot - b)
            dkd_kd = dkd * kd
            db = dqe * qe - dke * ke - dkd_kd
            dtot = jnp.sum(dkd_kd, axis=0, keepdims=True) + jnp.sum(dst * st, axis=0, keepdims=True) * etot
            dg_ref[rs, :] = _exact_mask_dot(_tri(1 - dirn).astype(BF16), db) + dtot
            dst = dst * etot + _dot(dout, qe, 'tn')
        dstate[...] = dst

    col = lambda w, off: pl.BlockSpec((rb, w), functools.partial(lambda hh, s, off: (bidx(s), off + hh), off=off))
    return pl.pallas_call(
        body, name=f"gla_bwd_{dirn}",
        out_shape=(jax.ShapeDtypeStruct((nt, d.KEY), F32), jax.ShapeDtypeStruct((nt, d.KEY), F32),
                   jax.ShapeDtypeStruct((nt, d.VAL), F32), jax.ShapeDtypeStruct((nt, d.KEY), F32)),
        grid=(gh, nb),
        in_specs=[col(dk, qb), col(dk, kb), col(dv, 0), col(dk, gb), col(dv, 0),
                  pl.BlockSpec((None, nsub, dv, dk), lambda hh, s: (hh, nb - 1 - s, 0, 0))],
        out_specs=(col(dk, 0), col(dk, 0), col(dv, 0), col(dk, 0)),
        scratch_shapes=[pltpu.VMEM((dv, dk), F32)],
        compiler_params=_cparams(("parallel", "arbitrary")),
    )(z, z, z, g, do, states)


def _rope_pad(w):
    q = QK_ROPE // 4
    a1, a2, b1, b2 = (w[..., k * q:(k + 1) * q] for k in range(4))
    z = jnp.zeros(w.shape[:-1] + (LANE // 2 - 2 * q,), w.dtype)
    return jnp.concatenate([a1, b1, z, a2, b2, z], axis=-1)


def _rope_unpad(g):
    q = QK_ROPE // 4
    h = LANE // 2
    return jnp.concatenate([g[..., 0:q], g[..., h:h + q], g[..., q:2 * q], g[..., h + q:h + 2 * q]], axis=-1)


def _win0_to_kernel(w, d):
    kv_lat = w[:, :d.KVL]
    k_rope = w[:, d.KVL:d.KVL + QK_ROPE]
    q_lat = w[:, d.KVL + QK_ROPE:d.KVL + QK_ROPE + d.QL]
    rest = w[:, d.KVL + QK_ROPE + d.QL:]
    parts = [q_lat, kv_lat, _rope_pad(k_rope)]
    if d.z0_pad:
        parts.append(jnp.zeros((w.shape[0], d.z0_pad), w.dtype))
    return jnp.concatenate(parts + [rest], axis=1)


def _win0_from_kernel(g, d):
    return jnp.concatenate([g[:, d.z0_kv:d.z0_kv + d.KVL], _rope_unpad(g[:, d.z0_kr:d.z0_kr + LANE]), g[:, :d.QL],
                            g[:, d.z0_ax:]], axis=1)


def _wqb_to_kernel(w, d):
    wr = w.reshape(d.QL, d.H, QK_NOPE + QK_ROPE)
    return jnp.concatenate([wr[:, :, :QK_NOPE], _rope_pad(wr[:, :, QK_NOPE:])], axis=2).reshape(d.QL, 2 * d.HQ)


def _wqb_from_kernel(g, d):
    gr = g.reshape(d.QL, d.H, QK_NOPE + LANE)
    return jnp.concatenate([gr[:, :, :QK_NOPE], _rope_unpad(gr[:, :, QK_NOPE:])], axis=2).reshape(d.QL, d.H * (QK_NOPE + QK_ROPE))


def _wkvb_to_kernel(w, d):
    return w.reshape(d.KVL, d.H, 2, LANE).transpose(0, 2, 1, 3).reshape(d.KVL, 2 * d.HQ)


def _wkvb_from_kernel(g, d):
    return g.reshape(d.KVL, 2, d.H, LANE).transpose(0, 2, 1, 3).reshape(d.KVL, 2 * d.HQ)


def _win1_to_kernel(w, d):
    k = w[:, :d.KEY]
    v = w[:, d.KEY:d.KEY + d.VAL]
    lr = w[:, d.KEY + d.VAL:d.KEY + d.VAL + 2 * GATE_RANK]
    q = w[:, d.KEY + d.VAL + 2 * GATE_RANK:2 * d.KEY + d.VAL + 2 * GATE_RANK]
    og = w[:, 2 * d.KEY + d.VAL + 2 * GATE_RANK:]
    return jnp.concatenate([v, og, k, q, lr, jnp.zeros((w.shape[0], LANE - 2 * GATE_RANK), w.dtype)], axis=1)


def _win1_from_kernel(g, d):
    return jnp.concatenate([g[:, d.z1_k:d.z1_k + d.KEY], g[:, :d.VAL], g[:, d.z1_lr:d.z1_lr + 2 * GATE_RANK],
                            g[:, d.z1_q:d.z1_q + d.KEY], g[:, d.z1_og:d.z1_og + d.VAL]], axis=1)


def _gate_weight(fw_w, bw_w, d):
    z = jnp.zeros((GATE_RANK, d.KEY), F32)
    return jnp.concatenate([jnp.concatenate([fw_w, z], axis=1), jnp.concatenate([z, bw_w], axis=1),
                            jnp.zeros((LANE - 2 * GATE_RANK, 2 * d.KEY), F32)], axis=0)


def _rope_tables(d):
    t = jnp.arange(d.T)
    inv = ROPE_THETA ** (-jnp.arange(0, QK_ROPE // 2, 2, dtype=F32) / (QK_ROPE // 2))
    ar = (t // GRID_W).astype(F32)[:, None] * inv
    ac = (t % GRID_W).astype(F32)[:, None] * inv
    z = jnp.zeros((d.T, LANE // 2 - 2 * inv.shape[0]), F32)
    ang = jnp.concatenate([ar, ac, z, ar, ac, z], axis=1)
    cos = jnp.concatenate([jnp.cos(ang), jnp.ones((d.TC, LANE), F32)], axis=0)
    sin = jnp.concatenate([jnp.sin(ang), jnp.zeros((d.TC, LANE), F32)], axis=0)
    sgn = jnp.where(jnp.arange(LANE) < LANE // 2, -1.0, 1.0).astype(F32)
    return cos, sin * sgn


def _row(arr, cb=0, width=None):
    return ('row', arr, cb, arr.shape[1] if width is None else width)


def _mod_fwd(x, nw, shift, scale, d, name):
    def body(i, j, x, w, sh, sc):
        return (_rms(x, w)[0] * (1.0 + sc) + sh,)
    return _rowwise(body, d.NT, d.rt, d.nlt, [_row(x), ('full', nw), ('seg', shift), ('seg', scale)],
                    [('row', d.D, BF16)], name)[0]


def _mod_bwd_call(dres, dh, x, nw, shift, scale, d, name):
    def body(i, j, dres, dh, x, w, sh, sc):
        dx, dw, dsh, dsc = _mod_bwd(dh, x, w, sh, sc)
        return dres + dx, dw, dsh, dsc
    return _rowwise(body, d.NT, d.rt, d.nlt, [_row(dres), _row(dh), _row(x), ('full', nw), ('seg', shift), ('seg', scale)],
                    [('row', d.D, F32), ('acc', (1, d.D)), ('segacc', d.D), ('segacc', d.D)], name)


def _res_mod_fwd(x, y, gate, nw, shift, scale, d, name):
    def body(i, j, x, y, g, w, sh, sc):
        x1 = x + g * y
        return x1, _rms(x1, w)[0] * (1.0 + sc) + sh
    return _rowwise(body, d.NT, d.rt, d.nlt, [_row(x), _row(y), ('seg', gate), ('full', nw), ('seg', shift), ('seg', scale)],
                    [('row', d.D, F32), ('row', d.D, BF16)], name)


def _res_mod_bwd(dx2, dh2, x1, y, gate, nw, shift, scale, d, name):
    def body(i, j, dx2, dh2, x1, y, g, w, sh, sc):
        dx, dw, dsh, dsc = _mod_bwd(dh2, x1, w, sh, sc)
        dx1 = dx2 + dx
        return dx1, g * dx1, jnp.sum(dx1 * y, axis=0, keepdims=True), dw, dsh, dsc
    return _rowwise(body, d.NT, d.rt, d.nlt,
                    [_row(dx2), _row(dh2), _row(x1), _row(y), ('seg', gate), ('full', nw), ('seg', shift), ('seg', scale)],
                    [('row', d.D, F32), ('row', d.D, BF16), ('segacc', d.D), ('acc', (1, d.D)), ('segacc', d.D),
                     ('segacc', d.D)], name)


def _res_fwd(x1, f, gate, d, name):
    return _rowwise(lambda i, j, x1, f, g: (x1 + g * f,), d.NT, d.rt, d.nlt, [_row(x1), _row(f), ('seg', gate)],
                    [('row', d.D, F32)], name)[0]


def _res_bwd(dx2, f, gate, d, name):
    def body(i, j, dx2, f, g):
        return g * dx2, jnp.sum(dx2 * f, axis=0, keepdims=True)
    return _rowwise(body, d.NT, d.rt, d.nlt, [_row(dx2), _row(f), ('seg', gate)], [('row', d.D, BF16), ('segacc', d.D)], name)


def _w(w, key):
    if callable(w[key]):
        w[key] = w[key]()
    return w[key]


def _ffn_fwd(h2, w, d, tag, rides):
    p = _mm([(h2, _w(w, 'ffn_up'))], 'nn', F32, tag + '_up', rides=rides)
    u = _dwconv(p, _w8(w['ffn_conv_w'], w['ffn_conv_b']), d, F32, tag + '_conv')
    tc = _tile(d.FF, GLU_COLS)
    nb = d.FF // tc
    a = _rowwise(lambda i, j, ug, uv: (_silu(ug) * uv,), d.NT, d.rt, d.nlt, [('rowc', u, 0, tc), ('rowc', u, nb, tc)],
                 [('rowc', d.FF, tc, BF16)], tag + '_glu', ncol=nb)[0]
    f = _mm([(a, _w(w, 'ffn_down'))], 'nn', F32, tag + '_down', rides=rides)
    return p, u, a, f


def _ffn_bwd(df, h2, p, u, a, w_up, conv_w, w_down, d, tag, rides, emit):
    da = _mm([(df, w_down)], 'nt', F32, tag + '_down_dx', rides=rides)
    emit('ffn_down', _mm([(a, df)], 'tn', GRAD_WIRE, tag + '_down_dw'))
    tc = _tile(d.FF, GLU_COLS)
    nb = d.FF // tc

    def body(i, j, da, ug, uv):
        return (jnp.where(j < nb, da * uv * _dsilu(ug), da * _silu(ug)),)
    du = _rowwise(body, d.NT, d.rt, d.nlt, [('rowm', da, 0, tc, nb), ('rowm', u, 0, tc, nb), ('rowm', u, nb, tc, nb)],
                  [('rowc', 2 * d.FF, tc, F32)], tag + '_glu_bwd', ncol=2 * nb)[0]
    dp = _dwconv(du, _w8(conv_w[::-1]), d, BF16, tag + '_conv_dx')
    conv_g = _dwconv_wgrad(p, du, d, tag + '_conv_dw')
    dh2 = _mm([(dp, w_up)], 'nt', F32, tag + '_up_dx', rides=rides)
    emit('ffn_up', _mm([(h2, dp)], 'tn', GRAD_WIRE, tag + '_up_dw'))
    return dh2, conv_g[0:3], conv_g[3]


def _ab_fwd(z, w, cos, sin_s, d, rides):
    qnw, kvnw = w['q_norm'], w['kv_norm']

    def prep(i, j, zq, zkv, zkr, qw, kw, cos, sin_s):
        return _rms(zq, qw)[0], _rms(zkv, kw)[0], _rope(zkr, cos, sin_s)
    qn, kvn, kr = _rowwise(prep, d.NT, d.rt, d.nlt,
                           [_row(z, 0, d.QL), _row(z, d.z0_kv // d.KVL, d.KVL), _row(z, d.z0_kr // LANE, LANE),
                            ('full', qnw), ('full', kvnw), _row(cos), _row(sin_s)],
                           [('row', d.QL, BF16), ('row', d.KVL, BF16), ('row', LANE, BF16)], 'ab_prep')
    qraw = _mm([(qn, w['w_qb'])], 'nn', F32, 'ab_qb')
    kv = _mm([(kvn, w['w_kvb'])], 'nn', BF16, 'ab_kvb')

    def qrope(i, j, qraw, cos, sin_s):
        parts = []
        for h in range(d.H):
            parts += [qraw[:, 2 * h * LANE:(2 * h + 1) * LANE], _rope(qraw[:, (2 * h + 1) * LANE:(2 * h + 2) * LANE], cos, sin_s)]
        return (jnp.concatenate(parts, axis=1),)
    q = _rowwise(qrope, d.NT, d.rt, d.nlt, [_row(qraw), _row(cos), _row(sin_s)], [('row', 2 * d.HQ, BF16)], 'ab_qrope')[0]

    def kcat_body(i, j, kn, kr):
        parts = []
        for h in range(d.H):
            parts += [kn[:, h * LANE:(h + 1) * LANE], kr]
        return (jnp.concatenate(parts, axis=1),)
    kcat = _rowwise(kcat_body, d.NT, d.rt, d.nlt, [_row(kv, 0, d.HQ), _row(kr)], [('row', 2 * d.HQ, BF16)], 'ab_kcat')[0]
    o, ob, lse = _flash_fwd(q, kcat, kv, d, rides)
    ab = d.z0_ax // d.CC
    s = _rowwise(lambda i, j, ax, ac: (ax * ac,), d.NT, d.rt, d.nlt, [_row(z, ab, d.CC), _row(z, ab + 2, d.CC)],
                 [('row', d.CC, F32)], 'ab_conv_in')[0]
    cv = _dwconv(s, _w8(w['conv_a']), d, F32, 'ab_conv')
    ymix = _rowwise(lambda i, j, a_b, cv, ob: (jnp.concatenate([(a_b * cv).astype(BF16), ob], axis=1),), d.NT, d.rt, d.nlt,
                    [_row(z, ab + 1, d.CC), _row(cv), _row(ob)], [('row', d.CC + d.HQ, BF16)], 'ab_mix')[0]
    return ymix, dict(qn=qn, kvn=kvn, q=q, kcat=kcat, kv=kv, o=o, lse=lse, s=s, cv=cv)


def _ab_bwd(dymix, z, sv, w, cos, sin_s, d, rides, emit):
    qnw, kvnw = w['q_norm'], w['kv_norm']
    assert d.CC % d.HQ == 0

    def dprep(i, j, dmo, o, lse):
        lane = lax.broadcasted_iota(jnp.int32, (1, LANE), 1)
        cols = []
        for h in range(d.H):
            hs = slice(h * LANE, (h + 1) * LANE)
            delta = jnp.sum(dmo[:, hs] * o[:, hs], axis=1, keepdims=True)
            cols.append(jnp.where(lane < LANE // 2, lse[:, hs], delta))
        return dmo, jnp.concatenate(cols, axis=1)
    do, ld = _rowwise(dprep, d.NT, d.rt, d.nlt, [_row(dymix, d.CC // d.HQ, d.HQ), _row(sv['o']), _row(sv['lse'])],
                      [('row', d.HQ, BF16), ('row', d.HQ, F32)], 'ab_do')
    dq, dkc, dvv = _flash_bwd(sv['q'], sv['kcat'], sv['kv'], do, ld, d, rides)

    def qrope_t(i, j, dq, cos, sin_s):
        parts = []
        for h in range(d.H):
            parts += [dq[:, 2 * h * LANE:(2 * h + 1) * LANE], _rope_t(dq[:, (2 * h + 1) * LANE:(2 * h + 2) * LANE], cos, sin_s)]
        return (jnp.concatenate(parts, axis=1),)
    dqraw = _rowwise(qrope_t, d.NT, d.rt, d.nlt, [_row(dq), _row(cos), _row(sin_s)],
                     [('row', 2 * d.HQ, BF16)], 'ab_qrope_bwd')[0]

    def dkv_body(i, j, dkc, dv):
        dkr = dkc[:, LANE:2 * LANE]
        for h in range(1, d.H):
            dkr = dkr + dkc[:, (2 * h + 1) * LANE:(2 * h + 2) * LANE]
        parts = [dkc[:, 2 * h * LANE:(2 * h + 1) * LANE] for h in range(d.H)] + [dv.astype(F32)]
        return jnp.concatenate(parts, axis=1), dkr
    dkv, dkr = _rowwise(dkv_body, d.NT, d.rt, d.nlt, [_row(dkc), _row(dvv)], [('row', 2 * d.HQ, BF16), ('row', LANE, F32)],
                        'ab_dkv')
    dqn = _mm([(dqraw, w['w_qb'])], 'nt', F32, 'ab_qb_dx')
    emit('w_qb', _mm([(sv['qn'], dqraw)], 'tn', GRAD_WIRE, 'ab_qb_dw'))
    dkvn = _mm([(dkv, w['w_kvb'])], 'nt', F32, 'ab_kvb_dx')
    emit('w_kvb', _mm([(sv['kvn'], dkv)], 'tn', GRAD_WIRE, 'ab_kvb_dw'))
    ab = d.z0_ax // d.CC
    dab, dcv = _rowwise(lambda i, j, dya, cv, a_b: (dya * cv, dya * a_b), d.NT, d.rt, d.nlt,
                        [_row(dymix, 0, d.CC), _row(sv['cv']), _row(z, ab + 1, d.CC)],
                        [('row', d.CC, BF16), ('row', d.CC, F32)], 'ab_mix_bwd')
    ds = _dwconv(dcv, _w8(w['conv_a'][::-1]), d, F32, 'ab_conv_dx')
    conv_g = _dwconv_wgrad(sv['s'], dcv, d, 'ab_conv_dw')

    def assemble(i, j, dqn, dkvn, dkr, zq, zkv, qw, kw, cos, sin_s, ds, ax, ac, dab):
        _, xq, rq = _rms(zq, qw)
        dzq, dqw = _rms_bwd(dqn, xq, rq, qw)
        _, xk, rk = _rms(zkv, kw)
        dzkv, dkw = _rms_bwd(dkvn, xk, rk, kw)
        parts = [dzq, dzkv, _rope_t(dkr, cos, sin_s)]
        if d.z0_pad:
            parts.append(jnp.zeros((dzq.shape[0], d.z0_pad), F32))
        parts += [ds * ac, dab.astype(F32), ds * ax]
        return jnp.concatenate([t.astype(BF16) for t in parts], axis=1), dqw, dkw
    dz, dqw, dkw = _rowwise(assemble, d.NT, d.rt, d.nlt,
                            [_row(dqn), _row(dkvn), _row(dkr), _row(z, 0, d.QL), _row(z, d.z0_kv // d.KVL, d.KVL),
                             ('full', qnw), ('full', kvnw), _row(cos), _row(sin_s), _row(ds), _row(z, ab, d.CC),
                             _row(z, ab + 2, d.CC), _row(dab)],
                            [('row', d.ZW0, BF16), ('acc', (1, d.QL)), ('acc', (1, d.KVL))], 'ab_dz')
    return dz, dict(conv_a=conv_g[0:3], q_norm=dqw, kv_norm=dkw)


def _log_sigmoid(x):
    return jnp.minimum(x, 0.0) - jnp.log(1.0 + jnp.exp(-jnp.abs(x)))


def _gla_fwd_block(z, w, d):
    wg, bg, onw = w['gate_w'], w['gate_b'], w['o_norm']

    def gates(i, j, lr, wg, bg):
        return (_log_sigmoid(_dot(lr, wg, 'nn') + bg) / GATE_NORMALIZER,)
    g = _rowwise(gates, d.NT, d.rt, d.nlt, [_row(z, d.z1_lr // LANE, LANE), ('full', wg), ('full', bg)],
                 [('row', 2 * d.KEY, F32)], 'gla_gates')[0]
    of, stf = _gla_fwd(z, g, 0, d)
    ob, stb = _gla_fwd(z, g, 1, d)

    def outp(i, j, of, ob, og, ow):
        o = of + ob
        parts = [_rms(o[:, h * d.dv:(h + 1) * d.dv], ow)[0] for h in range(d.GH)]
        return (jnp.concatenate(parts, axis=1) * _silu(og),)
    ymix = _rowwise(outp, d.NT, d.rt, d.nlt, [_row(of), _row(ob), _row(z, d.z1_og // d.VAL, d.VAL), ('full', onw)],
                    [('row', d.VAL, BF16)], 'gla_out')[0]
    return ymix, dict(g=g, of=of, ob=ob, stf=stf, stb=stb)


def _gla_bwd_block(dymix, z, sv, w, d):
    wg, bg, onw = w['gate_w'], w['gate_b'], w['o_norm']

    def outp_bwd(i, j, dy, of, ob, og, ow):
        o = of + ob
        dn = dy * _silu(og)
        dos, ns = [], []
        dow = jnp.zeros((1, d.dv), F32)
        for h in range(d.GH):
            hs = slice(h * d.dv, (h + 1) * d.dv)
            n, xh, r = _rms(o[:, hs], ow)
            do_h, dw_h = _rms_bwd(dn[:, hs], xh, r, ow)
            dos.append(do_h)
            ns.append(n)
            dow = dow + dw_h
        return jnp.concatenate(dos, axis=1), dy * jnp.concatenate(ns, axis=1) * _dsilu(og), dow
    do, dog, dow = _rowwise(outp_bwd, d.NT, d.rt, d.nlt,
                            [_row(dymix), _row(sv['of']), _row(sv['ob']), _row(z, d.z1_og // d.VAL, d.VAL), ('full', onw)],
                            [('row', d.VAL, F32), ('row', d.VAL, BF16), ('acc', (1, d.dv))], 'gla_out_bwd')
    dq0, dk0, dv0, dg0 = _gla_bwd(z, sv['g'], do, sv['stf'], 0, d)
    dq1, dk1, dv1, dg1 = _gla_bwd(z, sv['g'], do, sv['stb'], 1, d)

    def assemble(i, j, dg0, dg1, lr, wg, bg, dq0, dq1, dk0, dk1, dv0, dv1, dog):
        pre = _dot(lr, wg, 'nn') + bg
        e = jnp.exp(-jnp.abs(pre))
        dpre = jnp.concatenate([dg0, dg1], axis=1) * jnp.where(pre >= 0, e, 1.0) / (1.0 + e) / GATE_NORMALIZER
        dlr = _dot(dpre, wg, 'nt')
        parts = [dv0 + dv1, dog.astype(F32), dk0 + dk1, dq0 + dq1, dlr]
        return (jnp.concatenate([t.astype(BF16) for t in parts], axis=1), _dot(lr, dpre, 'tn'),
                jnp.sum(dpre, axis=0, keepdims=True))
    dz, dwg, dbg = _rowwise(assemble, d.NT, d.rt, d.nlt,
                            [_row(dg0), _row(dg1), _row(z, d.z1_lr // LANE, LANE), ('full', wg), ('full', bg), _row(dq0),
                             _row(dq1), _row(dk0), _row(dk1), _row(dv0), _row(dv1), _row(dog)],
                            [('row', d.ZW1, BF16), ('acc', (LANE, 2 * d.KEY)), ('acc', (1, 2 * d.KEY))], 'gla_dz')
    return dz, dict(gate_fw_w=dwg[:GATE_RANK, :d.KEY], gate_bw_w=dwg[GATE_RANK:2 * GATE_RANK, d.KEY:],
                    gate_fw_b=dbg[:, :d.KEY], gate_bw_b=dbg[:, d.KEY:], o_norm=dow)


def _loss_bwd(x, fnw, target, d):
    def body(i, j, x, w, tgt):
        y, xh, r = _rms(x, w)
        e = y - tgt
        dx, dw = _rms_bwd(e * (1.0 / d.D), xh, r, w)
        lat = i < d.nlt
        part = jnp.sum(jnp.sum(e * e, axis=1, keepdims=True), axis=0, keepdims=True) * (0.5 / d.D)
        return (jnp.where(lat, dx, 0.0), jnp.where(lat, jnp.broadcast_to(part, (8, LANE)), 0.0), jnp.where(lat, dw, 0.0))
    return _rowwise(body, d.NT, d.rt, d.nlt, [_row(x), ('full', fnw), ('rowclamp', target, d.nlt - 1, d.D)],
                    [('row', d.D, F32), ('acc', (8, LANE)), ('acc', (1, d.D))], 'loss')


def _layer_fwd(x, mods, w, mixer_fwd, d, tag, rides):
    sh1, sc1, g1, sh2, sc2, g2 = mods
    h = _mod_fwd(x, w['norm1'], sh1, sc1, d, tag + '_mod1')
    z = _mm([(h, _w(w, 'w_in'))], 'nn', F32, tag + '_in', rides=rides)
    ymix, msv = mixer_fwd(z)
    y = _mm([(ymix, _w(w, 'w_out'))], 'nn', F32, tag + '_out')
    x1, h2 = _res_mod_fwd(x, y, g1, w['norm2'], sh2, sc2, d, tag + '_mod2')
    p, u, a, f = _ffn_fwd(h2, w, d, tag + '_ffn', rides)
    x2 = _res_fwd(x1, f, g2, d, tag + '_res')
    return x2, dict(x=x, h=h, z=z, ymix=ymix, msv=msv, y=y, x1=x1, h2=h2, p=p, u=u, a=a, f=f)


def _layer_bwd(dx2, sv, mods, w, mixer_bwd, d, tag, rides, emit):
    sh1, sc1, g1, sh2, sc2, g2 = mods
    df, dg2 = _res_bwd(dx2, sv['f'], g2, d, tag + '_res_bwd')
    dh2, dconv_w, dconv_b = _ffn_bwd(df, sv['h2'], sv['p'], sv['u'], sv['a'], w['ffn_up'], w['ffn_conv_w'], w['ffn_down'],
                                     d, tag + '_ffn', rides, emit)
    dx1, dy, dg1, dn2, dsh2, dsc2 = _res_mod_bwd(dx2, dh2, sv['x1'], sv['y'], g1, w['norm2'], sh2, sc2, d, tag + '_mod2_bwd')
    dymix = _mm([(dy, w['w_out'])], 'nt', F32, tag + '_out_dx')
    emit('w_out', _mm([(sv['ymix'], dy)], 'tn', GRAD_WIRE, tag + '_out_dw'))
    dz, mg = mixer_bwd(dymix, sv['z'], sv['msv'])
    dh = _mm([(dz, w['w_in'])], 'nt', F32, tag + '_in_dx', rides=rides)
    dw_in = _mm([(sv['h'], dz)], 'tn', GRAD_WIRE, tag + '_in_dw', rides=rides)
    emit('w_in', dw_in)
    dx, dn1, dsh1, dsc1 = _mod_bwd_call(dx1, dh, sv['x'], w['norm1'], sh1, sc1, d, tag + '_mod1_bwd')
    grads = dict(mg, ffn_conv_w=dconv_w, ffn_conv_b=dconv_b, norm1=dn1, norm2=dn2)
    return dx, grads, [dsh1, dsc1, dg1, dsh2, dsc2, dg2]


def _pad_flat(v, mult=LANE):
    v = v.reshape(-1)
    return jnp.pad(v, (0, (-v.shape[0]) % mult))


def _pack(entries, row_mult):
    flat, offs, pos = [], [], 0
    for v in entries:
        f = _pad_flat(v)
        flat.append(f)
        offs.append(pos)
        pos += f.shape[0]
    tot = jnp.concatenate(flat)
    tot = jnp.pad(tot, (0, (-pos) % (row_mult * LANE)))
    return tot.reshape(-1, LANE), offs


def _unpack(packed, offs, shapes):
    flat = packed.reshape(-1)
    out = []
    for off, shp in zip(offs, shapes):
        n = 1
        for s in shp:
            n *= s
        out.append(flat[off:off + n].reshape(shp))
    return out


def _step(a):
    d = _dims()
    dm = d.D
    me = 4 * lax.axis_index("x") + 2 * lax.axis_index("y") + lax.axis_index("c")
    sds = jax.ShapeDtypeStruct

    rides = _Rides()
    fwd_hosts = {'l0_w_out': 'l0_in', 'l0_ffn_up': 'flash_fwd', 'l0_ffn_down': 'flash_fwd', 'l1_w_in': 'flash_fwd',
                 'l1_w_out': 'l0_ffn_up', 'l1_ffn_up': 'l0_ffn_up', 'l1_ffn_down': 'l0_ffn_down'}

    def gathered(name):
        shard = _cast_bf16(a[name], 'cast_' + name)
        if name in fwd_hosts:
            rides.add(fwd_hosts[name], name, shard, False)
            return lambda: rides.done[name]
        g = _exchange(shard, False, 'ag_' + name)
        return lambda: g

    def cols(name, relayout=lambda t: t):
        g = gathered(name)
        k, n = a[name].shape
        return lambda: relayout(g().transpose(1, 0, 2).reshape(k, N_DEV * n))

    def rows(name):
        g = gathered(name)
        return lambda: g().reshape(N_DEV * a[name].shape[0], a[name].shape[1])

    small_names = ['l0_conv_a', 'l0_ffn_conv_w', 'l1_ffn_conv_w', 'l1_gate_fw_w', 'l1_gate_bw_w']
    spack, soffs = _pack([a[n] for n in small_names], 8)
    sg = _exchange(spack, False, 'ag_small')
    small_w = {}
    for n, off in zip(small_names, soffs):
        r, c = a[n].shape
        shards = sg.reshape(N_DEV, -1)[:, off:off + r * c].reshape(N_DEV, r, c)
        small_w[n] = shards.transpose(1, 0, 2).reshape(r, N_DEV * c)

    w0 = dict(norm1=a['l0_norm1'].reshape(1, dm), norm2=a['l0_norm2'].reshape(1, dm),
              w_in=cols('l0_w_in', lambda t: _win0_to_kernel(t, d)), w_qb=cols('l0_w_qb', lambda t: _wqb_to_kernel(t, d))(),
              w_kvb=cols('l0_w_kvb', lambda t: _wkvb_to_kernel(t, d))(), w_out=rows('l0_w_out'),
              q_norm=a['l0_q_norm'].reshape(1, -1), kv_norm=a['l0_kv_norm'].reshape(1, -1), conv_a=small_w['l0_conv_a'],
              ffn_up=cols('l0_ffn_up'), ffn_conv_w=small_w['l0_ffn_conv_w'], ffn_conv_b=a['l0_ffn_conv_b'],
              ffn_down=rows('l0_ffn_down'))
    w1 = dict(norm1=a['l1_norm1'].reshape(1, dm), norm2=a['l1_norm2'].reshape(1, dm),
              w_in=cols('l1_w_in', lambda t: _win1_to_kernel(t, d)), w_out=rows('l1_w_out'),
              gate_w=_gate_weight(small_w['l1_gate_fw_w'], small_w['l1_gate_bw_w'], d),
              gate_b=jnp.concatenate([a['l1_gate_fw_b'], a['l1_gate_bw_b']]).reshape(1, -1),
              o_norm=a['l1_o_norm'].reshape(1, -1),
              ffn_up=cols('l1_ffn_up'), ffn_conv_w=small_w['l1_ffn_conv_w'], ffn_conv_b=a['l1_ffn_conv_b'],
              ffn_down=rows('l1_ffn_down'))

    c8 = _exchange(a['c'], False, 'ag_c').reshape(N_DEV, dm)
    c16 = jnp.concatenate([c8, a['c_ctx'].reshape(1, dm), jnp.zeros((7, dm), F32)], axis=0)
    act16, dact16 = _small(lambda v: (_silu(v), _dsilu(v)), [c16], [sds((16, dm), BF16), sds((16, dm), F32)], 'cond_silu')
    n6 = N_MOD * dm // N_DEV
    mod_sh = [_mm([(act16, a[f'l{l}_ada_w'])], 'nn', F32, f'ada{l}') for l in (0, 1)]
    mod_all = _exchange(jnp.concatenate(mod_sh, axis=1), False, 'ag_mod')
    mods = []
    for l in (0, 1):
        full = mod_all[:, :, l * n6:(l + 1) * n6].transpose(1, 0, 2).reshape(16, N_MOD * dm)
        mine = jnp.concatenate([lax.dynamic_slice_in_dim(full, me, 1, 0), full[8:9]], axis=0)
        m2 = _small(lambda r, b: (r + b,), [mine, a[f'l{l}_ada_b'].reshape(1, -1)], [sds((2, N_MOD * dm), F32)], f'ada{l}_bias')[0]
        mods.append([m2[:, k * dm:(k + 1) * dm].reshape(2, 1, dm) for k in range(N_MOD)])

    cos, sin_s = _rope_tables(d)
    x0 = jnp.concatenate([a['x'][0], a['ctx'][0]], axis=0)
    x2, sv0 = _layer_fwd(x0, mods[0], w0, lambda z: _ab_fwd(z, w0, cos, sin_s, d, rides), d, 'l0', rides)
    x4, sv1 = _layer_fwd(x2, mods[1], w1, lambda z: _gla_fwd_block(z, w1, d), d, 'l1', rides)
    dx4, loss_acc, dfn = _loss_bwd(x4, a['final_norm'].reshape(1, dm), a['loss_target'][0], d)

    bwd_hosts = {'l1_ffn_down': 'l1_ffn_up_dx', 'l1_ffn_up': 'l1_in_dx', 'l1_w_out': 'l1_in_dw', 'l1_w_in': 'l0_ffn_down_dx',
                 'l0_ffn_down': 'l0_ffn_up_dx', 'l0_ffn_up': 'flash_bwd', 'l0_w_out': 'flash_bwd'}
    from_kernel = {'l0_w_in': _win0_from_kernel, 'l0_w_qb': _wqb_from_kernel, 'l0_w_kvb': _wkvb_from_kernel,
                   'l1_w_in': _win1_from_kernel}
    slabs = {}

    def emitter(layer):
        def emit(wkey, dw):
            name = f'l{layer}_{wkey}'
            if name in from_kernel:
                dw = from_kernel[name](dw, d)
            if a[name].shape[0] == dw.shape[0]:
                stacked = dw.reshape(dw.shape[0], N_DEV, dw.shape[1] // N_DEV).transpose(1, 0, 2)
            else:
                stacked = dw.reshape(N_DEV, dw.shape[0] // N_DEV, dw.shape[1])
            if name in bwd_hosts:
                rides.add(bwd_hosts[name], 'rs_' + name, stacked, True)
                slabs[name] = lambda: rides.done['rs_' + name]
            else:
                got = _exchange(stacked, True, 'rs_' + name)
                slabs[name] = lambda: got
        return emit

    dx2, g1, dmod1 = _layer_bwd(dx4, sv1, mods[1], w1, lambda dy, z, msv: _gla_bwd_block(dy, z, msv, w1, d), d, 'l1',
                                rides, emitter(1))
    dx0, g0, dmod0 = _layer_bwd(dx2, sv0, mods[0], w0,
                                lambda dy, z, msv: _ab_bwd(dy, z, msv, w0, cos, sin_s, d, rides, emitter(0)), d, 'l0',
                                rides, emitter(0))

    dm_rows = jnp.concatenate([jnp.concatenate([t.reshape(2, dm) for t in dmod], axis=1) for dmod in (dmod0, dmod1)], axis=0)
    dm_all = _exchange(dm_rows, False, 'ag_dmod')
    lat = dm_all[:, 0::2].transpose(1, 0, 2)
    ctxs = dm_all[:, 1::2].transpose(1, 0, 2)

    def ada_prep(lat, ctxs):
        csum = jnp.sum(ctxs, axis=1, keepdims=True)
        row = lax.broadcasted_iota(jnp.int32, (1, 8, 1), 1)
        g16 = jnp.concatenate([lat, jnp.where(row == 0, csum, 0.0)], axis=1)
        return g16, jnp.sum(lat, axis=1, keepdims=True) + csum
    g16, gb = _small(ada_prep, [lat, ctxs], [sds((2, 16, N_MOD * dm), F32), sds((2, 1, N_MOD * dm), F32)], 'ada_bwd_prep')
    g16_sh = [lax.dynamic_slice_in_dim(g16[l], me * n6, n6, 1) for l in (0, 1)]
    grad_ada_w = [_mm([(act16, g16_sh[l])], 'tn', F32, f'ada{l}_dw') for l in (0, 1)]
    dact = _mm([(g16_sh[0], a['l0_ada_w']), (g16_sh[1], a['l1_ada_w'])], 'nt', F32, 'ada_dact')
    dcc = _small(lambda t, s: (t * s,), [dact[8:9], dact16[8:9]], [sds((1, dm), F32)], 'cctx_grad')[0]

    res = {}
    for name in slabs:
        res[name] = _adam(slabs[name](), a[name], a['m_' + name], a['v_' + name], 'adam_' + name)
    for l in (0, 1):
        name = f'l{l}_ada_w'
        res[name] = _adam(grad_ada_w[l][None], a[name], a['m_' + name], a['v_' + name], 'adam_' + name)

    part = {'loss': loss_acc[0:1, 0:1], 'c_ctx': dcc, 'final_norm': dfn,
            'l0_norm1': g0['norm1'], 'l0_norm2': g0['norm2'], 'l0_q_norm': g0['q_norm'], 'l0_kv_norm': g0['kv_norm'],
            'l0_conv_a': g0['conv_a'], 'l0_ffn_conv_w': g0['ffn_conv_w'], 'l0_ffn_conv_b': g0['ffn_conv_b'],
            'l1_norm1': g1['norm1'], 'l1_norm2': g1['norm2'], 'l1_o_norm': g1['o_norm'],
            'l1_gate_fw_w': g1['gate_fw_w'], 'l1_gate_bw_w': g1['gate_bw_w'], 'l1_gate_fw_b': g1['gate_fw_b'],
            'l1_gate_bw_b': g1['gate_bw_b'], 'l1_ffn_conv_w': g1['ffn_conv_w'], 'l1_ffn_conv_b': g1['ffn_conv_b']}
    pkeys = list(part)
    ppack, poffs = _pack([part[k] for k in pkeys], 8)
    psum = _sum_parts(_exchange(ppack, False, 'ag_small_grads'), 'sum_small_grads')
    tot = dict(zip(pkeys, _unpack(psum, poffs, [part[k].shape for k in pkeys])))
    loss = tot['loss'].reshape(())
    sgrad = {}
    for n in _WEIGHTS:
        if n in res:
            continue
        if n.endswith('ada_b'):
            sgrad[n] = gb[int(n[1])].reshape(a[n].shape)
        elif n in small_names:
            c = a[n].shape[1]
            sgrad[n] = lax.dynamic_slice_in_dim(tot[n], me * c, c, 1)
        else:
            sgrad[n] = tot[n].reshape(a[n].shape)
    snames = list(sgrad)
    packs = [_pack([src[n] for n in snames], 8)[0] for src in
             (sgrad, {n: a[n] for n in snames}, {n: a['m_' + n] for n in snames}, {n: a['v_' + n] for n in snames})]
    offs = _pack([sgrad[n] for n in snames], 8)[1]
    outs = _adam(packs[0][None], packs[1], packs[2], packs[3], 'adam_small')
    for k in range(4):
        for n, val in zip(snames, _unpack(outs[k], offs, [a[n].shape for n in snames])):
            res.setdefault(n, [None] * 4)[k] = val

    grad_x = dx0[:d.T].reshape(1, d.T, dm)
    return (loss, grad_x, *[res[n][0] for n in _WEIGHTS], *[res[n][1] for n in _WEIGHTS], *[res[n][2] for n in _WEIGHTS],
            *[res[n][3] for n in _WEIGHTS])


def kernel(*args):
    return _step(dict(zip(_ARGS, args, strict=True)))
```

```python
import functools
import types

import jax
import jax.numpy as jnp
from jax import lax
from jax.experimental import pallas as pl
from jax.experimental.pallas import tpu as pltpu

D_MODEL = 2048
SEQ = 8192
GRID_W = 64
CTX_LEN = 256
EPS = 1e-6
N_MOD = 6
MLA_HEADS = 8
QK_NOPE = 128
QK_ROPE = 64
V_HEAD = 128
Q_LORA = 512
KV_LORA = 256
ROPE_THETA = 10000.0
GLA_HEADS = 4
GATE_RANK = 16
GATE_NORMALIZER = 16.0
CHUNK = 64
D_FF = 5632
ADAM_LR = 0.001
ADAM_B1 = 0.9
ADAM_B2 = 0.999
ADAM_EPS = 1e-08
ADAM_WD = 0.01
ADAM_STEP = 10

N_DEV = 8
LANE = 128
VMEM_LIMIT = 56 * 1024 * 1024
CONV_COLS = 2816
GLU_COLS = 1408

F32 = jnp.float32
BF16 = jnp.bfloat16
GRAD_WIRE = jnp.bfloat16
MESH_ID = pl.DeviceIdType.MESH

_FWD = ['x', 'c', 'ctx', 'c_ctx', 'l0_ada_w', 'l0_ada_b', 'l0_norm1', 'l0_w_in', 'l0_conv_a', 'l0_q_norm', 'l0_w_qb',
        'l0_kv_norm', 'l0_w_kvb', 'l0_w_out', 'l0_norm2', 'l0_ffn_up', 'l0_ffn_conv_w', 'l0_ffn_conv_b', 'l0_ffn_down',
        'l1_ada_w', 'l1_ada_b', 'l1_norm1', 'l1_w_in', 'l1_gate_fw_w', 'l1_gate_fw_b', 'l1_gate_bw_w', 'l1_gate_bw_b',
        'l1_o_norm', 'l1_w_out', 'l1_norm2', 'l1_ffn_up', 'l1_ffn_conv_w', 'l1_ffn_conv_b', 'l1_ffn_down', 'final_norm']
_WEIGHTS = _FWD[3:]
_ARGS = _FWD + ['loss_target'] + ['m_' + n for n in _WEIGHTS] + ['v_' + n for n in _WEIGHTS]


def _dims():
    d = types.SimpleNamespace()
    d.D, d.T, d.TC = D_MODEL, SEQ, CTX_LEN
    d.NT = d.T + d.TC
    d.rt = 256 if d.TC % 256 == 0 else 128
    d.nlt = d.T // d.rt
    d.H = MLA_HEADS
    d.QL, d.KVL = Q_LORA, KV_LORA
    d.CC = D_MODEL // 2
    d.z0_kv = d.QL
    d.z0_kr = d.QL + d.KVL
    d.z0_pad = (-(d.QL + d.KVL + LANE)) % d.CC
    d.z0_ax = d.QL + d.KVL + LANE + d.z0_pad
    d.ZW0 = d.z0_ax + 3 * d.CC
    d.AB_COLS = d.KVL + QK_ROPE + d.QL + 3 * d.CC
    d.HQ = d.H * LANE
    d.GH = GLA_HEADS
    d.KEY = D_MODEL // 2
    d.VAL = D_MODEL
    d.dk = d.KEY // d.GH
    d.dv = d.VAL // d.GH
    d.z1_og = d.VAL
    d.z1_k = 2 * d.VAL
    d.z1_q = 2 * d.VAL + d.KEY
    d.z1_lr = 2 * d.VAL + 2 * d.KEY
    d.ZW1 = d.z1_lr + LANE
    d.GLA_COLS = 2 * d.KEY + 2 * d.VAL + 2 * GATE_RANK
    d.FF = D_FF
    d.tq = min(256, d.TC)
    return d


def _tile(n, pref, align=LANE):
    if n <= pref:
        return n
    t = (pref // align) * align
    while t >= align:
        if n % t == 0:
            return t
        t -= align
    raise ValueError(f"no tile for {n}")


def _cparams(sem):
    return pltpu.CompilerParams(dimension_semantics=sem, vmem_limit_bytes=VMEM_LIMIT)


def _dot(a, b, mode):
    dims = {'nn': (((1,), (0,)), ((), ())), 'nt': (((1,), (1,)), ((), ())), 'tn': (((0,), (0,)), ((), ()))}[mode]
    return lax.dot_general(a.astype(BF16), b.astype(BF16), dims, preferred_element_type=F32)


class _Rides:
    def __init__(self):
        self.pending, self.done = {}, {}

    def add(self, host, key, x, stacked):
        self.pending.setdefault(host, []).append((key, x, stacked))

    def take(self, host):
        return self.pending.pop(host, [])


def _ride_plumbing(riders):
    n = len(riders)
    if not n:
        return [], [], [], []
    spec = pl.BlockSpec(memory_space=pl.ANY)
    shapes = [jax.ShapeDtypeStruct((N_DEV,) + tuple(x.shape[1:] if st else x.shape), x.dtype) for _, x, st in riders]
    sems = [pltpu.SemaphoreType.DMA((n * (N_DEV - 1),)), pltpu.SemaphoreType.DMA((n * (N_DEV - 1),)),
            pltpu.SemaphoreType.DMA((n,))]
    return [spec] * n, shapes, [spec] * n, sems


def _ride_copies(stacked_flags, x_refs, o_refs, send_sems, recv_sems, local_sems):
    ix, iy, ic = lax.axis_index("x"), lax.axis_index("y"), lax.axis_index("c")
    me = 4 * ix + 2 * iy + ic
    local, sends, recvs = [], [], []
    for r, (stacked, x_ref, o_ref) in enumerate(zip(stacked_flags, x_refs, o_refs)):
        def src(p, x_ref=x_ref, stacked=stacked):
            return x_ref.at[p] if stacked else x_ref

        local.append(pltpu.make_async_copy(src(me), o_ref.at[me], local_sems.at[r]))
        for k in range(1, N_DEV):
            px, py, pc = (ix + ((k >> 2) & 1)) % 2, (iy + ((k >> 1) & 1)) % 2, (ic + (k & 1)) % 2
            peer = 4 * px + 2 * py + pc
            s = r * (N_DEV - 1) + k - 1
            sends.append(pltpu.make_async_remote_copy(
                src_ref=src(peer), dst_ref=o_ref.at[me], send_sem=send_sems.at[s], recv_sem=recv_sems.at[s],
                device_id=(px, py, pc), device_id_type=MESH_ID))
            recvs.append(pltpu.make_async_remote_copy(
                src_ref=src(peer), dst_ref=o_ref.at[peer], send_sem=send_sems.at[s], recv_sem=recv_sems.at[s],
                device_id=(px, py, pc), device_id_type=MESH_ID))

    def start():
        for cp in local + sends:
            cp.start()

    def wait():
        for cp in recvs:
            cp.wait_recv()
        for cp in sends:
            cp.wait_send()
        for cp in local:
            cp.wait()

    return start, wait


def _mm(pairs, mode, out_dtype, name, tm=768, tn=1024, tk=2816, rides=None):
    riders = rides.take(name) if rides is not None else []
    nr = len(riders)
    r_in, r_shapes, r_out, r_sems = _ride_plumbing(riders)
    a0, b0 = pairs[0]
    if mode == 'nn':
        (m, k), n = a0.shape, b0.shape[1]
    elif mode == 'nt':
        (m, k), n = a0.shape, b0.shape[0]
    else:
        (k, m), n = a0.shape, b0.shape[1]
    tm, tn, tk = _tile(m, tm), _tile(n, tn), _tile(k, tk)
    nk = k // tk
    if mode == 'nn':
        a_spec = pl.BlockSpec((tm, tk), lambda i, j, kk: (i, kk))
        b_spec = pl.BlockSpec((tk, tn), lambda i, j, kk: (kk, j))
    elif mode == 'nt':
        a_spec = pl.BlockSpec((tm, tk), lambda i, j, kk: (i, kk))
        b_spec = pl.BlockSpec((tn, tk), lambda i, j, kk: (j, kk))
    else:
        a_spec = pl.BlockSpec((tk, tm), lambda i, j, kk: (kk, i))
        b_spec = pl.BlockSpec((tk, tn), lambda i, j, kk: (kk, j))
    npairs = len(pairs)

    nin = 2 * npairs
    gi, gj = m // tm, n // tn

    def body(*refs):
        o_ref, acc_ref = refs[nin + nr], refs[nin + 2 * nr + 1]
        i, j, kk = pl.program_id(0), pl.program_id(1), pl.program_id(2)
        if nr:
            start, wait = _ride_copies([st for _, _, st in riders], refs[nin:nin + nr], refs[nin + nr + 1:nin + 2 * nr + 1],
                                       *refs[nin + 2 * nr + 2:])
            pl.when(jnp.logical_and(jnp.logical_and(i == 0, j == 0), kk == 0))(start)

        def dots():
            s = None
            for p in range(npairs):
                t = _dot(refs[2 * p][...], refs[2 * p + 1][...], mode)
                s = t if s is None else s + t
            return s

        if nk == 1:
            o_ref[...] = dots().astype(o_ref.dtype)
        else:
            @pl.when(kk == 0)
            def _():
                acc_ref[...] = dots()

            @pl.when(jnp.logical_and(kk > 0, kk < nk - 1))
            def _():
                acc_ref[...] += dots()

            @pl.when(kk == nk - 1)
            def _():
                o_ref[...] = (acc_ref[...] + dots()).astype(o_ref.dtype)

        if nr:
            pl.when(jnp.logical_and(jnp.logical_and(i == gi - 1, j == gj - 1), kk == nk - 1))(wait)

    flat = [t for ab in pairs for t in ab]
    res = pl.pallas_call(
        body, name=name,
        out_shape=[jax.ShapeDtypeStruct((m, n), out_dtype)] + r_shapes,
        grid=(gi, gj, nk),
        in_specs=[a_spec, b_spec] * npairs + r_in,
        out_specs=[pl.BlockSpec((tm, tn), lambda i, j, kk: (i, j))] + r_out,
        scratch_shapes=[pltpu.VMEM((tm, tn), F32)] + r_sems,
        compiler_params=_cparams(("arbitrary",) * 3 if nr else ("parallel", "parallel", "arbitrary")),
    )(*flat, *[x for _, x, _ in riders])
    for (key, _, _), arr in zip(riders, res[1:]):
        rides.done[key] = arr
    return res[0]


def _rowwise(body, nrows, rt, nlt, ins, outs, name, ncol=1):
    ntiles = nrows // rt
    in_specs, args = [], []
    for spec in ins:
        kind, arr = spec[0], spec[1]
        if kind == 'row':
            in_specs.append(pl.BlockSpec((rt, spec[3]), functools.partial(lambda i, j, cb: (i, cb), cb=spec[2])))
        elif kind == 'rowc':
            in_specs.append(pl.BlockSpec((rt, spec[3]), functools.partial(lambda i, j, cb: (i, cb + j), cb=spec[2])))
        elif kind == 'rowm':
            in_specs.append(pl.BlockSpec(
                (rt, spec[3]), functools.partial(lambda i, j, cb, md: (i, cb + j % md), cb=spec[2], md=spec[4])))
        elif kind == 'rowclamp':
            in_specs.append(pl.BlockSpec(
                (rt, spec[3]), functools.partial(lambda i, j, mb: (jnp.minimum(i, mb), 0), mb=spec[2])))
        elif kind == 'row3':
            in_specs.append(pl.BlockSpec((arr.shape[0], rt, arr.shape[2]), lambda i, j: (0, i, 0)))
        elif kind == 'seg':
            in_specs.append(pl.BlockSpec((None, 1, arr.shape[2]), lambda i, j: (i // nlt, 0, 0)))
        else:
            in_specs.append(pl.BlockSpec(arr.shape, functools.partial(lambda i, j, nd: (0,) * nd, nd=arr.ndim)))
        args.append(arr)
    out_shapes, out_specs = [], []
    for spec in outs:
        kind = spec[0]
        if kind == 'row':
            out_shapes.append(jax.ShapeDtypeStruct((nrows, spec[1]), spec[2]))
            out_specs.append(pl.BlockSpec((rt, spec[1]), lambda i, j: (i, 0)))
        elif kind == 'rowc':
            out_shapes.append(jax.ShapeDtypeStruct((nrows, spec[1]), spec[3]))
            out_specs.append(pl.BlockSpec((rt, spec[2]), lambda i, j: (i, j)))
        elif kind == 'acc':
            out_shapes.append(jax.ShapeDtypeStruct(spec[1], F32))
            out_specs.append(pl.BlockSpec(spec[1], functools.partial(lambda i, j, nd: (0,) * nd, nd=len(spec[1]))))
        else:
            out_shapes.append(jax.ShapeDtypeStruct((2, 1, spec[1]), F32))
            out_specs.append(pl.BlockSpec((None, 1, spec[1]), lambda i, j: (i // nlt, 0, 0)))
    n_in = len(ins)
    has_acc = any(s[0] in ('acc', 'segacc') for s in outs)

    def kern(*refs):
        i = pl.program_id(0)
        j = pl.program_id(1)
        vals = [r[...] for r in refs[:n_in]]
        res = body(i, j, *vals)
        for spec, ref, val in zip(outs, refs[n_in:], res):
            if spec[0] in ('row', 'rowc'):
                ref[...] = val.astype(ref.dtype)
            else:
                first = (i == 0) if spec[0] == 'acc' else jnp.logical_or(i == 0, i == nlt)

                @pl.when(first)
                def _(ref=ref, val=val):
                    ref[...] = val

                @pl.when(jnp.logical_not(first))
                def _(ref=ref, val=val):
                    ref[...] += val

    return pl.pallas_call(
        kern, name=name, out_shape=tuple(out_shapes), grid=(ntiles, ncol),
        in_specs=in_specs, out_specs=tuple(out_specs),
        compiler_params=_cparams(("arbitrary", "arbitrary") if has_acc else ("parallel", "parallel")),
    )(*args)


def _small(body, args, out_shapes, name):
    n_in = len(args)

    def kern(*refs):
        res = body(*[r[...] for r in refs[:n_in]])
        for ref, val in zip(refs[n_in:], res):
            ref[...] = val.astype(ref.dtype)

    return pl.pallas_call(
        kern, name=name, out_shape=tuple(out_shapes),
        in_specs=[pl.BlockSpec(memory_space=pltpu.VMEM)] * n_in,
        out_specs=tuple(pl.BlockSpec(memory_space=pltpu.VMEM) for _ in out_shapes),
        compiler_params=pltpu.CompilerParams(vmem_limit_bytes=VMEM_LIMIT),
    )(*args)


def _exchange(x, stacked, name):
    r_in, r_shapes, r_out, r_sems = _ride_plumbing([(name, x, stacked)])

    def body(x_ref, o_ref, send_sems, recv_sems, local_sems):
        start, wait = _ride_copies([stacked], [x_ref], [o_ref], send_sems, recv_sems, local_sems)
        start()
        wait()

    return pl.pallas_call(body, name=name, out_shape=r_shapes[0], in_specs=r_in, out_specs=r_out[0], scratch_shapes=r_sems)(x)


def _cast_bf16(x, name):
    r, c = x.shape
    tr = _tile(r, 256, 8)
    return pl.pallas_call(
        lambda x_ref, o_ref: o_ref.__setitem__(Ellipsis, x_ref[...].astype(BF16)), name=name,
        out_shape=jax.ShapeDtypeStruct((r, c), BF16), grid=(r // tr,),
        in_specs=[pl.BlockSpec((tr, c), lambda i: (i, 0))], out_specs=pl.BlockSpec((tr, c), lambda i: (i, 0)),
        compiler_params=_cparams(("parallel",)),
    )(x)


def _adam(parts, w, m, v, name):
    p, r, c = parts.shape
    tr = _tile(r, 64, 8)

    def body(p_ref, w_ref, m_ref, v_ref, g_ref, d_ref, nm_ref, nv_ref):
        g = p_ref[0].astype(F32)
        for q in range(1, p):
            g = g + p_ref[q].astype(F32)
        nm = ADAM_B1 * m_ref[...] + (1.0 - ADAM_B1) * g
        nv = ADAM_B2 * v_ref[...] + (1.0 - ADAM_B2) * (g * g)
        m_hat = nm / (1.0 - ADAM_B1 ** ADAM_STEP)
        v_hat = nv / (1.0 - ADAM_B2 ** ADAM_STEP)
        g_ref[...] = g
        d_ref[...] = -ADAM_LR * (m_hat / (jnp.sqrt(v_hat) + ADAM_EPS) + ADAM_WD * w_ref[...])
        nm_ref[...] = nm
        nv_ref[...] = nv

    spec = pl.BlockSpec((tr, c), lambda i: (i, 0))
    return pl.pallas_call(
        body, name=name, out_shape=tuple(jax.ShapeDtypeStruct((r, c), F32) for _ in range(4)), grid=(r // tr,),
        in_specs=[pl.BlockSpec((p, tr, c), lambda i: (0, i, 0)), spec, spec, spec], out_specs=(spec,) * 4,
        compiler_params=_cparams(("parallel",)),
    )(parts, w, m, v)


def _sum_parts(parts, name):
    p, r, c = parts.shape
    tr = _tile(r, 256, 8)

    def body(p_ref, o_ref):
        g = p_ref[0]
        for q in range(1, p):
            g = g + p_ref[q]
        o_ref[...] = g

    return pl.pallas_call(
        body, name=name, out_shape=jax.ShapeDtypeStruct((r, c), F32), grid=(r // tr,),
        in_specs=[pl.BlockSpec((p, tr, c), lambda i: (0, i, 0))], out_specs=pl.BlockSpec((tr, c), lambda i: (i, 0)),
        compiler_params=_cparams(("parallel",)),
    )(parts)


def _shifted(cur, prev8, next8, i, rt, nlt, ntiles):
    first = jnp.logical_or(i == 0, i == nlt)
    last = jnp.logical_or(i == nlt - 1, i == ntiles - 1)
    prev_row = jnp.where(first, 0.0, prev8[7:8, :])
    next_row = jnp.where(last, 0.0, next8[0:1, :])
    rows = lax.broadcasted_iota(jnp.int32, (rt, 1), 0)
    x_m1 = jnp.where(rows == 0, prev_row, pltpu.roll(cur, 1, 0))
    x_p1 = jnp.where(rows == rt - 1, next_row, pltpu.roll(cur, rt - 1, 0))
    return x_m1, x_p1


def _halo_specs(rt, tc, nrows, col_axis_first):
    r8 = rt // 8
    last8 = nrows // 8 - 1
    if col_axis_first:
        return [pl.BlockSpec((8, tc), lambda j, i: (jnp.maximum(i * r8 - 1, 0), j)),
                pl.BlockSpec((rt, tc), lambda j, i: (i, j)),
                pl.BlockSpec((8, tc), lambda j, i: (jnp.minimum((i + 1) * r8, last8), j))]
    return [pl.BlockSpec((8, tc), lambda i, j: (jnp.maximum(i * r8 - 1, 0), j)),
            pl.BlockSpec((rt, tc), lambda i, j: (i, j)),
            pl.BlockSpec((8, tc), lambda i, j: (jnp.minimum((i + 1) * r8, last8), j))]


def _dwconv(x, w8, d, out_dtype, name):
    nrows, c = x.shape
    rt, nlt = d.rt, d.nlt
    tc = _tile(c, CONV_COLS)
    ntiles = nrows // rt

    def body(p_ref, c_ref, n_ref, w_ref, o_ref):
        i = pl.program_id(0)
        cur = c_ref[...]
        x_m1, x_p1 = _shifted(cur, p_ref[...], n_ref[...], i, rt, nlt, ntiles)
        w = w_ref[...]
        o_ref[...] = (x_m1 * w[0:1] + cur * w[1:2] + x_p1 * w[2:3] + w[3:4]).astype(o_ref.dtype)

    return pl.pallas_call(
        body, name=name, out_shape=jax.ShapeDtypeStruct((nrows, c), out_dtype), grid=(ntiles, c // tc),
        in_specs=_halo_specs(rt, tc, nrows, False) + [pl.BlockSpec((8, tc), lambda i, j: (0, j))],
        out_specs=pl.BlockSpec((rt, tc), lambda i, j: (i, j)),
        compiler_params=_cparams(("parallel", "parallel")),
    )(x, x, x, w8)


def _dwconv_wgrad(x, dy, d, name):
    nrows, c = x.shape
    rt, nlt = d.rt, d.nlt
    tc = _tile(c, CONV_COLS)
    ntiles = nrows // rt

    def body(p_ref, c_ref, n_ref, dy_ref, o_ref):
        i = pl.program_id(1)
        cur = c_ref[...]
        dy = dy_ref[...]
        x_m1, x_p1 = _shifted(cur, p_ref[...], n_ref[...], i, rt, nlt, ntiles)
        sums = [jnp.sum(t * dy, axis=0, keepdims=True) for t in (x_m1, cur, x_p1)] + [jnp.sum(dy, axis=0, keepdims=True)]
        row = lax.broadcasted_iota(jnp.int32, (8, 1), 0)
        part = jnp.zeros((8, tc), F32)
        for k, s in enumerate(sums):
            part = jnp.where(row == k, s, part)

        @pl.when(i == 0)
        def _():
            o_ref[...] = part

        @pl.when(i != 0)
        def _():
            o_ref[...] += part

    return pl.pallas_call(
        body, name=name, out_shape=jax.ShapeDtypeStruct((8, c), F32), grid=(c // tc, ntiles),
        in_specs=_halo_specs(rt, tc, nrows, True) + [pl.BlockSpec((rt, tc), lambda j, i: (i, j))],
        out_specs=pl.BlockSpec((8, tc), lambda j, i: (0, j)),
        compiler_params=_cparams(("parallel", "arbitrary")),
    )(x, x, x, dy)


def _ffn_gate_fwd(p, w8, d, name):
    nrows, ff = p.shape[0], p.shape[1] // 2
    rt, nlt = d.rt, d.nlt
    tc = _tile(ff, GLU_COLS)
    nb = ff // tc
    ntiles = nrows // rt

    def body(gp_ref, gc_ref, gn_ref, vp_ref, vc_ref, vn_ref, wg_ref, wv_ref, o_ref):
        i = pl.program_id(0)
        us = []
        for p_ref, c_ref, n_ref, w_ref in ((gp_ref, gc_ref, gn_ref, wg_ref), (vp_ref, vc_ref, vn_ref, wv_ref)):
            cur, w = c_ref[...], w_ref[...]
            x_m1, x_p1 = _shifted(cur, p_ref[...], n_ref[...], i, rt, nlt, ntiles)
            us.append(x_m1 * w[0:1] + cur * w[1:2] + x_p1 * w[2:3] + w[3:4])
        o_ref[...] = (_silu(us[0]) * us[1]).astype(o_ref.dtype)

    r8, last8 = rt // 8, nrows // 8 - 1

    def halo(off):
        return [pl.BlockSpec((8, tc), lambda i, j: (jnp.maximum(i * r8 - 1, 0), j + off)),
                pl.BlockSpec((rt, tc), lambda i, j: (i, j + off)),
                pl.BlockSpec((8, tc), lambda i, j: (jnp.minimum((i + 1) * r8, last8), j + off))]
    return pl.pallas_call(
        body, name=name, out_shape=jax.ShapeDtypeStruct((nrows, ff), BF16), grid=(ntiles, nb),
        in_specs=halo(0) + halo(nb) + [pl.BlockSpec((8, tc), lambda i, j: (0, j)), pl.BlockSpec((8, tc), lambda i, j: (0, j + nb))],
        out_specs=pl.BlockSpec((rt, tc), lambda i, j: (i, j)),
        compiler_params=_cparams(("parallel", "parallel")),
    )(p, p, p, p, p, p, w8, w8)


def _ffn_gate_bwd(p, da, w8, d, name):
    nrows, ff = da.shape
    rt, nlt = d.rt, d.nlt
    tc = _tile(ff, 512)
    nb = ff // tc
    ntiles = nrows // rt
    ext = rt + 16

    def body(gp_ref, gc_ref, gn_ref, vp_ref, vc_ref, vn_ref, ap_ref, ac_ref, an_ref, wg_ref, wv_ref,
             dpg_ref, dpv_ref, cg_ref, cv_ref):
        i = pl.program_id(1)
        first = jnp.logical_or(i == 0, i == nlt)
        last = jnp.logical_or(i == nlt - 1, i == ntiles - 1)

        def extended(p_ref, c_ref, n_ref):
            return jnp.concatenate([jnp.where(first, 0.0, p_ref[...]), c_ref[...], jnp.where(last, 0.0, n_ref[...])], axis=0)

        def conv(x, w):
            return pltpu.roll(x, 1, 0) * w[0:1] + x * w[1:2] + pltpu.roll(x, ext - 1, 0) * w[2:3]

        wg, wv = wg_ref[...], wv_ref[...]
        pg, pv, da_e = extended(gp_ref, gc_ref, gn_ref), extended(vp_ref, vc_ref, vn_ref), extended(ap_ref, ac_ref, an_ref)
        ug = conv(pg, wg) + wg[3:4]
        uv = conv(pv, wv) + wv[3:4]
        sg = _sigmoid(ug)
        dug = da_e * uv * (sg * (1.0 + ug * (1.0 - sg)))
        duv = da_e * (ug * sg)
        row = lax.broadcasted_iota(jnp.int32, (8, 1), 0)
        for p_e, du, w, dp_ref, c_ref in ((pg, dug, wg, dpg_ref, cg_ref), (pv, duv, wv, dpv_ref, cv_ref)):
            dp = pltpu.roll(du, 1, 0) * w[2:3] + du * w[1:2] + pltpu.roll(du, ext - 1, 0) * w[0:1]
            dp_ref[...] = dp[8:rt + 8].astype(dp_ref.dtype)
            du_c = du[8:rt + 8]
            sums = [jnp.sum(t[8:rt + 8] * du_c, axis=0, keepdims=True)
                    for t in (pltpu.roll(p_e, 1, 0), p_e, pltpu.roll(p_e, ext - 1, 0))] + [jnp.sum(du_c, axis=0, keepdims=True)]
            part = jnp.zeros((8, tc), F32)
            for k, s in enumerate(sums):
                part = jnp.where(row == k, s, part)

            @pl.when(i == 0)
            def _(c_ref=c_ref, part=part):
                c_ref[...] = part

            @pl.when(i != 0)
            def _(c_ref=c_ref, part=part):
                c_ref[...] += part

    r8, last8 = rt // 8, nrows // 8 - 1

    def halo(off):
        return [pl.BlockSpec((8, tc), lambda j, i: (jnp.maximum(i * r8 - 1, 0), j + off)),
                pl.BlockSpec((rt, tc), lambda j, i: (i, j + off)),
                pl.BlockSpec((8, tc), lambda j, i: (jnp.minimum((i + 1) * r8, last8), j + off))]
    tile = pl.BlockSpec((rt, tc), lambda j, i: (i, j))
    acc = pl.BlockSpec((8, tc), lambda j, i: (0, j))
    dpg, dpv, cg, cv = pl.pallas_call(
        body, name=name,
        out_shape=(jax.ShapeDtypeStruct((nrows, ff), BF16), jax.ShapeDtypeStruct((nrows, ff), BF16),
                   jax.ShapeDtypeStruct((8, ff), F32), jax.ShapeDtypeStruct((8, ff), F32)),
        grid=(nb, ntiles),
        in_specs=halo(0) + halo(nb) + halo(0) + [acc, pl.BlockSpec((8, tc), lambda j, i: (0, j + nb))],
        out_specs=(tile, tile, acc, acc),
        compiler_params=_cparams(("parallel", "arbitrary")),
    )(p, p, p, p, p, p, da, da, da, w8, w8)
    return dpg, dpv, jnp.concatenate([cg, cv], axis=1)


def _w8(w3, b=None):
    c = w3.shape[1]
    brow = jnp.zeros((1, c), F32) if b is None else b.reshape(1, c)
    return jnp.concatenate([w3, brow, jnp.zeros((4, c), F32)], axis=0)


def _rms(x, w):
    r = lax.rsqrt(jnp.mean(x * x, axis=-1, keepdims=True) + EPS)
    xh = x * r
    return xh * w, xh, r


def _rms_bwd(dy, xh, r, w):
    dxh = dy * w
    dx = r * (dxh - xh * jnp.mean(dxh * xh, axis=-1, keepdims=True))
    return dx, jnp.sum(dy * xh, axis=0, keepdims=True)


def _mod_bwd(dh, x, w, shift, scale):
    n, xh, r = _rms(x, w)
    dshift = jnp.sum(dh, axis=0, keepdims=True)
    dscale = jnp.sum(dh * n, axis=0, keepdims=True)
    dx, dw = _rms_bwd(dh * (1.0 + scale), xh, r, w)
    return dx, dw, dshift, dscale


def _sigmoid(x):
    return 1.0 / (1.0 + jnp.exp(-x))


def _silu(x):
    return x * _sigmoid(x)


def _dsilu(x):
    s = _sigmoid(x)
    return s * (1.0 + x * (1.0 - s))


def _rope(x, cos, sin_s):
    return x * cos + pltpu.roll(x, LANE // 2, 1) * sin_s


def _rope_t(dy, cos, sin_s):
    return dy * cos + pltpu.roll(dy * sin_s, LANE // 2, 1)


def _flash_fwd(q, kcat, kv, d, rides=None):
    riders = rides.take("flash_fwd") if rides is not None else []
    nr = len(riders)
    r_in, r_shapes, r_out, r_sems = _ride_plumbing(riders)
    nt, h, tq = d.NT, d.H, d.tq
    tkb = _tile(d.T, 4096)
    n_big = d.T // tkb
    nq_lat = d.T // tq
    scale = float(QK_NOPE + QK_ROPE) ** -0.5

    def body(*refs):
        q_ref, k_ref, v_ref = refs[:3]
        o_ref, ob_ref, lse_ref = refs[3 + nr:6 + nr]
        qi = pl.program_id(1)
        if nr:
            start, wait = _ride_copies([st for _, _, st in riders], refs[3:3 + nr], refs[6 + nr:6 + 2 * nr], *refs[6 + 2 * nr:])
            pl.when(jnp.logical_and(pl.program_id(0) == 0, qi == 0))(start)
        q_t = q_ref[...]

        def step(k0, tk, carry):
            m, l, acc = carry
            ks = pl.ds(k0, tk)
            s = _dot(q_t, k_ref[ks, :], 'nt') * scale
            m_new = jnp.maximum(m, jnp.max(s, axis=1, keepdims=True))
            p = jnp.exp(s - m_new)
            alpha = jnp.exp(m - m_new)
            return m_new, alpha * l + jnp.sum(p, axis=1, keepdims=True), alpha * acc + _dot(p, v_ref[ks, :], 'nn')

        init = (jnp.full((tq, 1), -1e30, F32), jnp.zeros((tq, 1), F32), jnp.zeros((tq, LANE), F32))
        trips = jnp.where(qi < nq_lat, n_big, 0)
        carry = lax.fori_loop(0, trips, lambda t, c: step(pl.multiple_of(t * tkb, tkb), tkb, c), init)
        m, l, acc = step(d.T, d.TC, carry)
        o = acc / l
        o_ref[...] = o
        ob_ref[...] = o.astype(BF16)
        lse_ref[...] = jnp.broadcast_to(m + jnp.log(l), (tq, LANE))
        if nr:
            pl.when(jnp.logical_and(pl.program_id(0) == h - 1, qi == nt // tq - 1))(wait)

    out = pl.BlockSpec((tq, LANE), lambda hh, i: (i, hh))
    res = pl.pallas_call(
        body, name="flash_fwd",
        out_shape=[jax.ShapeDtypeStruct((nt, d.HQ), F32), jax.ShapeDtypeStruct((nt, d.HQ), BF16),
                   jax.ShapeDtypeStruct((nt, d.HQ), F32)] + r_shapes,
        grid=(h, nt // tq),
        in_specs=[pl.BlockSpec((tq, 2 * LANE), lambda hh, i: (i, hh)), pl.BlockSpec((nt, 2 * LANE), lambda hh, i: (0, hh)),
                  pl.BlockSpec((nt, LANE), lambda hh, i: (0, h + hh))] + r_in,
        out_specs=[out, out, out] + r_out,
        scratch_shapes=r_sems,
        compiler_params=_cparams(("arbitrary", "arbitrary") if nr else ("parallel", "parallel")),
    )(q, kcat, kv, *[x for _, x, _ in riders])
    for (key, _, _), arr in zip(riders, res[3:]):
        rides.done[key] = arr
    return res[0], res[1], res[2]


def _flash_bwd(q, kcat, kv, do, ld, d, rides=None):
    riders = rides.take("flash_bwd") if rides is not None else []
    nr = len(riders)
    r_in, r_shapes, r_out, r_sems = _ride_plumbing(riders)
    nt, h, tk = d.NT, d.H, d.tq
    tqb =_tile(d.T, 2048)
    n_big = d.T // tqb
    nk_lat = d.T // tk
    scale = float(QK_NOPE + QK_ROPE) ** -0.5

    def body(*refs):
        q_ref, do_ref, ld_ref, k_ref, v_ref = refs[:5]
        dq_ref, dk_ref, dv_ref = refs[5 + nr:8 + nr]
        kt = pl.program_id(1)
        if nr:
            start, wait = _ride_copies([st for _, _, st in riders], refs[5:5 + nr], refs[8 + nr:8 + 2 * nr], *refs[8 + 2 * nr:])
            pl.when(jnp.logical_and(pl.program_id(0) == 0, kt == 0))(start)
        k_t, v_t = k_ref[...], v_ref[...]

        @pl.when(kt == 0)
        def _():
            dq_ref[...] = jnp.zeros_like(dq_ref)

        def step(q0, tq, carry):
            dk, dv = carry
            qs = pl.ds(q0, tq)
            q_t, do_t, ld_t = q_ref[qs, :], do_ref[qs, :], ld_ref[qs, :]
            s = _dot(q_t, k_t, 'nt') * scale
            p = jnp.exp(s - ld_t[:, 0:1])
            ds = p * (_dot(do_t, v_t, 'nt') - ld_t[:, LANE // 2:LANE // 2 + 1]) * scale
            dq_ref[qs, :] += _dot(ds, k_t, 'nn')
            return dk + _dot(ds, q_t, 'tn'), dv + _dot(p, do_t, 'tn')

        init = (jnp.zeros((tk, 2 * LANE), F32), jnp.zeros((tk, LANE), F32))
        carry = lax.fori_loop(0, n_big, lambda t, c: step(pl.multiple_of(t * tqb, tqb), tqb, c), init)
        dk_ref[...] = carry[0]
        dv_ref[...] = carry[1].astype(BF16)

        @pl.when(kt >= nk_lat)
        def _():
            dk, dv = step(d.T, d.TC, carry)
            dk_ref[...] = dk
            dv_ref[...] = dv.astype(BF16)

        if nr:
            pl.when(jnp.logical_and(pl.program_id(0) == h - 1, kt == nt // tk - 1))(wait)

    res = lambda w: pl.BlockSpec((nt, w), lambda hh, i: (0, hh))
    outs = pl.pallas_call(
        body, name="flash_bwd",
        out_shape=[jax.ShapeDtypeStruct((nt, 2 * d.HQ), F32), jax.ShapeDtypeStruct((nt, 2 * d.HQ), F32),
                   jax.ShapeDtypeStruct((nt, d.HQ), BF16)] + r_shapes,
        grid=(h, nt // tk),
        in_specs=[res(2 * LANE), res(LANE), res(LANE), pl.BlockSpec((tk, 2 * LANE), lambda hh, i: (i, hh)),
                  pl.BlockSpec((tk, LANE), lambda hh, i: (i, h + hh))] + r_in,
        out_specs=[res(2 * LANE), pl.BlockSpec((tk, 2 * LANE), lambda hh, i: (i, hh)),
                   pl.BlockSpec((tk, LANE), lambda hh, i: (i, hh))] + r_out,
        scratch_shapes=r_sems,
        compiler_params=_cparams(("arbitrary", "arbitrary") if nr else ("parallel", "arbitrary")),
    )(q, do, ld, kcat, kv, *[x for _, x, _ in riders])
    for (key, _, _), arr in zip(riders, outs[3:]):
        rides.done[key] = arr
    return outs[0], outs[1], outs[2]


def _tri(dirn):
    r = lax.broadcasted_iota(jnp.int32, (CHUNK, CHUNK), 0)
    c = lax.broadcasted_iota(jnp.int32, (CHUNK, CHUNK), 1)
    return (c <= r) if dirn == 0 else (c >= r)


def _exact_mask_dot(mask_bf16, x):
    hi = x.astype(BF16)
    r1 = x - hi.astype(F32)
    mid = r1.astype(BF16)
    lo = (r1 - mid.astype(F32)).astype(BF16)
    dot = lambda t: lax.dot_general(mask_bf16, t, (((1,), (0,)), ((), ())), preferred_element_type=F32)
    return dot(hi) + dot(mid) + dot(lo)


def _gla_terms(q, k, g, dirn, dk):
    mb = _tri(dirn)
    b = _exact_mask_dot(mb.astype(BF16), g)
    tot = jnp.sum(g, axis=0, keepdims=True)
    qe = q * (float(dk) ** -0.5) * jnp.exp(b)
    ke = k * jnp.exp(-b)
    kd = k * jnp.exp(tot - b)
    att = jnp.where(mb, _dot(qe, ke, 'nt'), 0.0)
    return mb, b, tot, qe, ke, kd, att


def _gla_block_index(d, dirn):
    nb = d.NT // d.rt
    if dirn == 0:
        return lambda s: (s + d.nlt) % nb
    return lambda s: nb - 1 - s


def _gla_rows(j, nsub, dirn):
    r0 = (j if dirn == 0 else nsub - 1 - j) * CHUNK
    return slice(r0, r0 + CHUNK)


def _gla_fwd(z, g, dirn, d):
    nt, gh, dk, dv, rb = d.NT, d.GH, d.dk, d.dv, d.rt
    nb, nsub = nt // rb, rb // CHUNK
    bidx = _gla_block_index(d, dirn)
    qb, kb, gb = d.z1_q // dk, d.z1_k // dk, dirn * gh

    def body(q_ref, k_ref, v_ref, g_ref, o_ref, st_ref, state):
        @pl.when(pl.program_id(1) == 0)
        def _():
            state[...] = jnp.zeros_like(state)

        st = state[...]
        for j in range(nsub):
            rs = _gla_rows(j, nsub, dirn)
            st_ref[j] = st
            v = v_ref[rs, :]
            _, _, tot, qe, _, kd, att = _gla_terms(q_ref[rs, :], k_ref[rs, :], g_ref[rs, :], dirn, dk)
            o_ref[rs, :] = _dot(att, v, 'nn') + _dot(qe, st, 'nt')
            st = st * jnp.exp(tot) + _dot(v, kd, 'tn')
        state[...] = st

    col = lambda w, off: pl.BlockSpec((rb, w), functools.partial(lambda hh, s, off: (bidx(s), off + hh), off=off))
    return pl.pallas_call(
        body, name=f"gla_fwd_{dirn}",
        out_shape=(jax.ShapeDtypeStruct((nt, d.VAL), F32), jax.ShapeDtypeStruct((gh, nb * nsub, dv, dk), F32)),
        grid=(gh, nb),
        in_specs=[col(dk, qb), col(dk, kb), col(dv, 0), col(dk, gb)],
        out_specs=(col(dv, 0), pl.BlockSpec((None, nsub, dv, dk), lambda hh, s: (hh, s, 0, 0))),
        scratch_shapes=[pltpu.VMEM((dv, dk), F32)],
        compiler_params=_cparams(("parallel", "arbitrary")),
    )(z, z, z, g)


def _gla_bwd(z, g, do, states, dirn, d, rides=None):
    riders = rides.take(f"gla_bwd_{dirn}") if rides is not None else []
    nr = len(riders)
    r_in, r_shapes, r_out, r_sems = _ride_plumbing(riders)
    nt, gh, dk, dv, rb = d.NT, d.GH, d.dk, d.dv, d.rt
    nb, nsub = nt // rb, rb // CHUNK
    bfwd = _gla_block_index(d, dirn)
    bidx = lambda s: bfwd(nb - 1 - s)
    qb, kb, gb = d.z1_q // dk, d.z1_k // dk, dirn * gh
    qscale = float(dk) ** -0.5

    def body(*refs):
        q_ref, k_ref, v_ref, g_ref, do_ref, st_ref = refs[:6]
        dq_ref, dk_ref, dv_ref, dg_ref = refs[6 + nr:10 + nr]
        dstate = refs[10 + 2 * nr]
        if nr:
            start, wait = _ride_copies([st for _, _, st in riders], refs[6:6 + nr], refs[10 + nr:10 + 2 * nr],
                                       *refs[11 + 2 * nr:])
            pl.when(jnp.logical_and(pl.program_id(0) == 0, pl.program_id(1) == 0))(start)

        @pl.when(pl.program_id(1) == 0)
        def _():
            dstate[...] = jnp.zeros_like(dstate)

        dst = dstate[...]
        for j in reversed(range(nsub)):
            rs = _gla_rows(j, nsub, dirn)
            q, k, v, g_, dout, st = q_ref[rs, :], k_ref[rs, :], v_ref[rs, :], g_ref[rs, :], do_ref[rs, :], st_ref[j]
            mb, b, tot, qe, ke, kd, att = _gla_terms(q, k, g_, dirn, dk)
            etot = jnp.exp(tot)
            datt = jnp.where(mb, _dot(dout, v, 'nt'), 0.0)
            dv_ref[rs, :] = _dot(att, dout, 'tn') + _dot(kd, dst, 'nt')
            dqe = _dot(datt, ke, 'nn') + _dot(dout, st, 'nn')
            dke = _dot(datt, qe, 'tn')
            dkd = _dot(v, dst, 'nn')
            dq_ref[rs, :] = dqe * (qscale * jnp.exp(b))
            dk_ref[rs, :] = dke * jnp.exp(-b) + dkd * jnp.exp(tot - b)
            dkd_kd = dkd * kd
            db = dqe * qe - dke * ke - dkd_kd
            dtot = jnp.sum(dkd_kd, axis=0, keepdims=True) + jnp.sum(dst * st, axis=0, keepdims=True) * etot
            dg_ref[rs, :] = _exact_mask_dot(_tri(1 - dirn).astype(BF16), db) + dtot
            dst = dst * etot + _dot(dout, qe, 'tn')
        dstate[...] = dst
        if nr:
            pl.when(jnp.logical_and(pl.program_id(0) == gh - 1, pl.program_id(1) == nb - 1))(wait)

    col = lambda w, off: pl.BlockSpec((rb, w), functools.partial(lambda hh, s, off: (bidx(s), off + hh), off=off))
    res = pl.pallas_call(
        body, name=f"gla_bwd_{dirn}",
        out_shape=[jax.ShapeDtypeStruct((nt, d.KEY), F32), jax.ShapeDtypeStruct((nt, d.KEY), F32),
                   jax.ShapeDtypeStruct((nt, d.VAL), F32), jax.ShapeDtypeStruct((nt, d.KEY), F32)] + r_shapes,
        grid=(gh, nb),
        in_specs=[col(dk, qb), col(dk, kb), col(dv, 0), col(dk, gb), col(dv, 0),
                  pl.BlockSpec((None, nsub, dv, dk), lambda hh, s: (hh, nb - 1 - s, 0, 0))] + r_in,
        out_specs=[col(dk, 0), col(dk, 0), col(dv, 0), col(dk, 0)] + r_out,
        scratch_shapes=[pltpu.VMEM((dv, dk), F32)] + r_sems,
        compiler_params=_cparams(("arbitrary", "arbitrary") if nr else ("parallel", "arbitrary")),
    )(z, z, z, g, do, states, *[x for _, x, _ in riders])
    for (key, _, _), arr in zip(riders, res[4:]):
        rides.done[key] = arr
    return tuple(res[:4])


def _rope_pad(w):
    q = QK_ROPE // 4
    a1, a2, b1, b2 = (w[..., k * q:(k + 1) * q] for k in range(4))
    z = jnp.zeros(w.shape[:-1] + (LANE // 2 - 2 * q,), w.dtype)
    return jnp.concatenate([a1, b1, z, a2, b2, z], axis=-1)


def _rope_unpad(g):
    q = QK_ROPE // 4
    h = LANE // 2
    return jnp.concatenate([g[..., 0:q], g[..., h:h + q], g[..., q:2 * q], g[..., h + q:h + 2 * q]], axis=-1)


def _win0_to_kernel(w, d):
    kv_lat = w[:, :d.KVL]
    k_rope = w[:, d.KVL:d.KVL + QK_ROPE]
    q_lat = w[:, d.KVL + QK_ROPE:d.KVL + QK_ROPE + d.QL]
    rest = w[:, d.KVL + QK_ROPE + d.QL:]
    parts = [q_lat, kv_lat, _rope_pad(k_rope)]
    if d.z0_pad:
        parts.append(jnp.zeros((w.shape[0], d.z0_pad), w.dtype))
    return jnp.concatenate(parts + [rest], axis=1)


def _win0_from_kernel(g, d):
    return jnp.concatenate([g[:, d.z0_kv:d.z0_kv + d.KVL], _rope_unpad(g[:, d.z0_kr:d.z0_kr + LANE]), g[:, :d.QL],
                            g[:, d.z0_ax:]], axis=1)


def _wqb_to_kernel(w, d):
    wr = w.reshape(d.QL, d.H, QK_NOPE + QK_ROPE)
    return jnp.concatenate([wr[:, :, :QK_NOPE], _rope_pad(wr[:, :, QK_NOPE:])], axis=2).reshape(d.QL, 2 * d.HQ)


def _wqb_from_kernel(g, d):
    gr = g.reshape(d.QL, d.H, QK_NOPE + LANE)
    return jnp.concatenate([gr[:, :, :QK_NOPE], _rope_unpad(gr[:, :, QK_NOPE:])], axis=2).reshape(d.QL, d.H * (QK_NOPE + QK_ROPE))


def _wkvb_to_kernel(w, d):
    return w.reshape(d.KVL, d.H, 2, LANE).transpose(0, 2, 1, 3).reshape(d.KVL, 2 * d.HQ)


def _wkvb_from_kernel(g, d):
    return g.reshape(d.KVL, 2, d.H, LANE).transpose(0, 2, 1, 3).reshape(d.KVL, 2 * d.HQ)


def _win1_to_kernel(w, d):
    k = w[:, :d.KEY]
    v = w[:, d.KEY:d.KEY + d.VAL]
    lr = w[:, d.KEY + d.VAL:d.KEY + d.VAL + 2 * GATE_RANK]
    q = w[:, d.KEY + d.VAL + 2 * GATE_RANK:2 * d.KEY + d.VAL + 2 * GATE_RANK]
    og = w[:, 2 * d.KEY + d.VAL + 2 * GATE_RANK:]
    return jnp.concatenate([v, og, k, q, lr, jnp.zeros((w.shape[0], LANE - 2 * GATE_RANK), w.dtype)], axis=1)


def _win1_from_kernel(g, d):
    return jnp.concatenate([g[:, d.z1_k:d.z1_k + d.KEY], g[:, :d.VAL], g[:, d.z1_lr:d.z1_lr + 2 * GATE_RANK],
                            g[:, d.z1_q:d.z1_q + d.KEY], g[:, d.z1_og:d.z1_og + d.VAL]], axis=1)


def _gate_weight(fw_w, bw_w, d):
    z = jnp.zeros((GATE_RANK, d.KEY), F32)
    return jnp.concatenate([jnp.concatenate([fw_w, z], axis=1), jnp.concatenate([z, bw_w], axis=1),
                            jnp.zeros((LANE - 2 * GATE_RANK, 2 * d.KEY), F32)], axis=0)


def _rope_tables(d):
    t = jnp.arange(d.T)
    inv = ROPE_THETA ** (-jnp.arange(0, QK_ROPE // 2, 2, dtype=F32) / (QK_ROPE // 2))
    ar = (t // GRID_W).astype(F32)[:, None] * inv
    ac = (t % GRID_W).astype(F32)[:, None] * inv
    z = jnp.zeros((d.T, LANE // 2 - 2 * inv.shape[0]), F32)
    ang = jnp.concatenate([ar, ac, z, ar, ac, z], axis=1)
    cos = jnp.concatenate([jnp.cos(ang), jnp.ones((d.TC, LANE), F32)], axis=0)
    sin = jnp.concatenate([jnp.sin(ang), jnp.zeros((d.TC, LANE), F32)], axis=0)
    sgn = jnp.where(jnp.arange(LANE) < LANE // 2, -1.0, 1.0).astype(F32)
    return cos, sin * sgn


def _row(arr, cb=0, width=None):
    return ('row', arr, cb, arr.shape[1] if width is None else width)


def _mod_fwd(x, nw, shift, scale, d, name):
    def body(i, j, x, w, sh, sc):
        return (_rms(x, w)[0] * (1.0 + sc) + sh,)
    return _rowwise(body, d.NT, d.rt, d.nlt, [_row(x), ('full', nw), ('seg', shift), ('seg', scale)],
                    [('row', d.D, BF16)], name)[0]


def _mod_bwd_call(dres, dh, x, nw, shift, scale, d, name):
    def body(i, j, dres, dh, x, w, sh, sc):
        dx, dw, dsh, dsc = _mod_bwd(dh, x, w, sh, sc)
        return dres + dx, dw, dsh, dsc
    return _rowwise(body, d.NT, d.rt, d.nlt, [_row(dres), _row(dh), _row(x), ('full', nw), ('seg', shift), ('seg', scale)],
                    [('row', d.D, F32), ('acc', (1, d.D)), ('segacc', d.D), ('segacc', d.D)], name)


def _res_mod_fwd(x, y, gate, nw, shift, scale, d, name):
    def body(i, j, x, y, g, w, sh, sc):
        x1 = x + g * y
        return x1, _rms(x1, w)[0] * (1.0 + sc) + sh
    return _rowwise(body, d.NT, d.rt, d.nlt, [_row(x), _row(y), ('seg', gate), ('full', nw), ('seg', shift), ('seg', scale)],
                    [('row', d.D, F32), ('row', d.D, BF16)], name)


def _res_mod_bwd(dx2, dh2, x1, y, gate, nw, shift, scale, d, name):
    def body(i, j, dx2, dh2, x1, y, g, w, sh, sc):
        dx, dw, dsh, dsc = _mod_bwd(dh2, x1, w, sh, sc)
        dx1 = dx2 + dx
        return dx1, g * dx1, jnp.sum(dx1 * y, axis=0, keepdims=True), dw, dsh, dsc
    return _rowwise(body, d.NT, d.rt, d.nlt,
                    [_row(dx2), _row(dh2), _row(x1), _row(y), ('seg', gate), ('full', nw), ('seg', shift), ('seg', scale)],
                    [('row', d.D, F32), ('row', d.D, BF16), ('segacc', d.D), ('acc', (1, d.D)), ('segacc', d.D),
                     ('segacc', d.D)], name)


def _res_fwd(x1, f, gate, d, name):
    return _rowwise(lambda i, j, x1, f, g: (x1 + g * f,), d.NT, d.rt, d.nlt, [_row(x1), _row(f), ('seg', gate)],
                    [('row', d.D, F32)], name)[0]


def _res_bwd(dx2, f, gate, d, name):
    def body(i, j, dx2, f, g):
        return g * dx2, jnp.sum(dx2 * f, axis=0, keepdims=True)
    return _rowwise(body, d.NT, d.rt, d.nlt, [_row(dx2), _row(f), ('seg', gate)], [('row', d.D, BF16), ('segacc', d.D)], name)


def _w(w, key):
    if callable(w[key]):
        w[key] = w[key]()
    return w[key]


def _ffn_fwd(h2, w, d, tag, rides):
    p = _mm([(h2, _w(w, 'ffn_up'))], 'nn', F32, tag + '_up', rides=rides)
    a = _ffn_gate_fwd(p, _w8(w['ffn_conv_w'], w['ffn_conv_b']), d, tag + '_gate')
    f = _mm([(a, _w(w, 'ffn_down'))], 'nn', F32, tag + '_down', rides=rides)
    return p, a, f


def _ffn_bwd(df, h2, p, a, w, d, tag, rides, emit):
    w_up = w['ffn_up']
    da = _mm([(df, w['ffn_down'])], 'nt', F32, tag + '_down_dx', rides=rides)
    emit('ffn_down', _mm([(a, df)], 'tn', GRAD_WIRE, tag + '_down_dw'))
    dpg, dpv, conv_g = _ffn_gate_bwd(p, da, _w8(w['ffn_conv_w'], w['ffn_conv_b']), d, tag + '_gate_bwd')
    dh2 = _mm([(dpg, w_up[:, :d.FF]), (dpv, w_up[:, d.FF:])], 'nt', F32, tag + '_up_dx', rides=rides)
    emit('ffn_up', jnp.concatenate([_mm([(h2, dpg)], 'tn', GRAD_WIRE, tag + '_up_dw_gate'),
                                    _mm([(h2, dpv)], 'tn', GRAD_WIRE, tag + '_up_dw_val')], axis=1))
    return dh2, conv_g[0:3], conv_g[3]


def _ab_fwd(z, w, cos, sin_s, d, rides):
    qnw, kvnw = w['q_norm'], w['kv_norm']

    def prep(i, j, zq, zkv, zkr, qw, kw, cos, sin_s):
        return _rms(zq, qw)[0], _rms(zkv, kw)[0], _rope(zkr, cos, sin_s)
    qn, kvn, kr = _rowwise(prep, d.NT, d.rt, d.nlt,
                           [_row(z, 0, d.QL), _row(z, d.z0_kv // d.KVL, d.KVL), _row(z, d.z0_kr // LANE, LANE),
                            ('full', qnw), ('full', kvnw), _row(cos), _row(sin_s)],
                           [('row', d.QL, BF16), ('row', d.KVL, BF16), ('row', LANE, BF16)], 'ab_prep')
    qraw = _mm([(qn, w['w_qb'])], 'nn', F32, 'ab_qb')
    kv = _mm([(kvn, w['w_kvb'])], 'nn', BF16, 'ab_kvb')

    def qrope(i, j, qraw, cos, sin_s):
        parts = []
        for h in range(d.H):
            parts += [qraw[:, 2 * h * LANE:(2 * h + 1) * LANE], _rope(qraw[:, (2 * h + 1) * LANE:(2 * h + 2) * LANE], cos, sin_s)]
        return (jnp.concatenate(parts, axis=1),)
    q = _rowwise(qrope, d.NT, d.rt, d.nlt, [_row(qraw), _row(cos), _row(sin_s)], [('row', 2 * d.HQ, BF16)], 'ab_qrope')[0]

    def kcat_body(i, j, kn, kr):
        parts = []
        for h in range(d.H):
            parts += [kn[:, h * LANE:(h + 1) * LANE], kr]
        return (jnp.concatenate(parts, axis=1),)
    kcat = _rowwise(kcat_body, d.NT, d.rt, d.nlt, [_row(kv, 0, d.HQ), _row(kr)], [('row', 2 * d.HQ, BF16)], 'ab_kcat')[0]
    o, ob, lse = _flash_fwd(q, kcat, kv, d, rides)
    ab = d.z0_ax // d.CC
    s = _rowwise(lambda i, j, ax, ac: (ax * ac,), d.NT, d.rt, d.nlt, [_row(z, ab, d.CC), _row(z, ab + 2, d.CC)],
                 [('row', d.CC, F32)], 'ab_conv_in')[0]
    cv = _dwconv(s, _w8(w['conv_a']), d, F32, 'ab_conv')
    ymix = _rowwise(lambda i, j, a_b, cv, ob: (jnp.concatenate([(a_b * cv).astype(BF16), ob], axis=1),), d.NT, d.rt, d.nlt,
                    [_row(z, ab + 1, d.CC), _row(cv), _row(ob)], [('row', d.CC + d.HQ, BF16)], 'ab_mix')[0]
    return ymix, dict(qn=qn, kvn=kvn, q=q, kcat=kcat, kv=kv, o=o, lse=lse, s=s, cv=cv)


def _ab_bwd(dymix, z, sv, w, cos, sin_s, d, rides, emit):
    qnw, kvnw = w['q_norm'], w['kv_norm']
    assert d.CC % d.HQ == 0

    def dprep(i, j, dmo, o, lse):
        lane = lax.broadcasted_iota(jnp.int32, (1, LANE), 1)
        cols = []
        for h in range(d.H):
            hs = slice(h * LANE, (h + 1) * LANE)
            delta = jnp.sum(dmo[:, hs] * o[:, hs], axis=1, keepdims=True)
            cols.append(jnp.where(lane < LANE // 2, lse[:, hs], delta))
        return dmo, jnp.concatenate(cols, axis=1)
    do, ld = _rowwise(dprep, d.NT, d.rt, d.nlt, [_row(dymix, d.CC // d.HQ, d.HQ), _row(sv['o']), _row(sv['lse'])],
                      [('row', d.HQ, BF16), ('row', d.HQ, F32)], 'ab_do')
    dq, dkc, dvv = _flash_bwd(sv['q'], sv['kcat'], sv['kv'], do, ld, d, rides)

    def qrope_t(i, j, dq, cos, sin_s):
        parts = []
        for h in range(d.H):
            parts += [dq[:, 2 * h * LANE:(2 * h + 1) * LANE], _rope_t(dq[:, (2 * h + 1) * LANE:(2 * h + 2) * LANE], cos, sin_s)]
        return (jnp.concatenate(parts, axis=1),)
    dqraw = _rowwise(qrope_t, d.NT, d.rt, d.nlt, [_row(dq), _row(cos), _row(sin_s)],
                     [('row', 2 * d.HQ, BF16)], 'ab_qrope_bwd')[0]

    def dkv_body(i, j, dkc, dv):
        dkr = dkc[:, LANE:2 * LANE]
        for h in range(1, d.H):
            dkr = dkr + dkc[:, (2 * h + 1) * LANE:(2 * h + 2) * LANE]
        parts = [dkc[:, 2 * h * LANE:(2 * h + 1) * LANE] for h in range(d.H)] + [dv.astype(F32)]
        return jnp.concatenate(parts, axis=1), dkr
    dkv, dkr = _rowwise(dkv_body, d.NT, d.rt, d.nlt, [_row(dkc), _row(dvv)], [('row', 2 * d.HQ, BF16), ('row', LANE, F32)],
                        'ab_dkv')
    dqn = _mm([(dqraw, w['w_qb'])], 'nt', F32, 'ab_qb_dx')
    emit('w_qb', _mm([(sv['qn'], dqraw)], 'tn', GRAD_WIRE, 'ab_qb_dw'))
    dkvn = _mm([(dkv, w['w_kvb'])], 'nt', F32, 'ab_kvb_dx')
    emit('w_kvb', _mm([(sv['kvn'], dkv)], 'tn', GRAD_WIRE, 'ab_kvb_dw'))
    ab = d.z0_ax // d.CC
    dab, dcv = _rowwise(lambda i, j, dya, cv, a_b: (dya * cv, dya * a_b), d.NT, d.rt, d.nlt,
                        [_row(dymix, 0, d.CC), _row(sv['cv']), _row(z, ab + 1, d.CC)],
                        [('row', d.CC, BF16), ('row', d.CC, F32)], 'ab_mix_bwd')
    ds = _dwconv(dcv, _w8(w['conv_a'][::-1]), d, F32, 'ab_conv_dx')
    conv_g = _dwconv_wgrad(sv['s'], dcv, d, 'ab_conv_dw')

    def assemble(i, j, dqn, dkvn, dkr, zq, zkv, qw, kw, cos, sin_s, ds, ax, ac, dab):
        _, xq, rq = _rms(zq, qw)
        dzq, dqw = _rms_bwd(dqn, xq, rq, qw)
        _, xk, rk = _rms(zkv, kw)
        dzkv, dkw = _rms_bwd(dkvn, xk, rk, kw)
        parts = [dzq, dzkv, _rope_t(dkr, cos, sin_s)]
        if d.z0_pad:
            parts.append(jnp.zeros((dzq.shape[0], d.z0_pad), F32))
        parts += [ds * ac, dab.astype(F32), ds * ax]
        return jnp.concatenate([t.astype(BF16) for t in parts], axis=1), dqw, dkw
    dz, dqw, dkw = _rowwise(assemble, d.NT, d.rt, d.nlt,
                            [_row(dqn), _row(dkvn), _row(dkr), _row(z, 0, d.QL), _row(z, d.z0_kv // d.KVL, d.KVL),
                             ('full', qnw), ('full', kvnw), _row(cos), _row(sin_s), _row(ds), _row(z, ab, d.CC),
                             _row(z, ab + 2, d.CC), _row(dab)],
                            [('row', d.ZW0, BF16), ('acc', (1, d.QL)), ('acc', (1, d.KVL))], 'ab_dz')
    return dz, dict(conv_a=conv_g[0:3], q_norm=dqw, kv_norm=dkw)


def _log_sigmoid(x):
    return jnp.minimum(x, 0.0) - jnp.log(1.0 + jnp.exp(-jnp.abs(x)))


def _gla_fwd_block(z, w, d):
    wg, bg, onw = w['gate_w'], w['gate_b'], w['o_norm']

    def gates(i, j, lr, wg, bg):
        return (_log_sigmoid(_dot(lr, wg, 'nn') + bg) / GATE_NORMALIZER,)
    g = _rowwise(gates, d.NT, d.rt, d.nlt, [_row(z, d.z1_lr // LANE, LANE), ('full', wg), ('full', bg)],
                 [('row', 2 * d.KEY, F32)], 'gla_gates')[0]
    of, stf = _gla_fwd(z, g, 0, d)
    ob, stb = _gla_fwd(z, g, 1, d)

    def outp(i, j, of, ob, og, ow):
        o = of + ob
        parts = [_rms(o[:, h * d.dv:(h + 1) * d.dv], ow)[0] for h in range(d.GH)]
        return (jnp.concatenate(parts, axis=1) * _silu(og),)
    ymix = _rowwise(outp, d.NT, d.rt, d.nlt, [_row(of), _row(ob), _row(z, d.z1_og // d.VAL, d.VAL), ('full', onw)],
                    [('row', d.VAL, BF16)], 'gla_out')[0]
    return ymix, dict(g=g, of=of, ob=ob, stf=stf, stb=stb)


def _gla_bwd_block(dymix, z, sv, w, d, rides):
    wg, bg, onw = w['gate_w'], w['gate_b'], w['o_norm']

    def outp_bwd(i, j, dy, of, ob, og, ow):
        o = of + ob
        dn = dy * _silu(og)
        dos, ns = [], []
        dow = jnp.zeros((1, d.dv), F32)
        for h in range(d.GH):
            hs = slice(h * d.dv, (h + 1) * d.dv)
            n, xh, r = _rms(o[:, hs], ow)
            do_h, dw_h = _rms_bwd(dn[:, hs], xh, r, ow)
            dos.append(do_h)
            ns.append(n)
            dow = dow + dw_h
        return jnp.concatenate(dos, axis=1), dy * jnp.concatenate(ns, axis=1) * _dsilu(og), dow
    do, dog, dow = _rowwise(outp_bwd, d.NT, d.rt, d.nlt,
                            [_row(dymix), _row(sv['of']), _row(sv['ob']), _row(z, d.z1_og // d.VAL, d.VAL), ('full', onw)],
                            [('row', d.VAL, F32), ('row', d.VAL, BF16), ('acc', (1, d.dv))], 'gla_out_bwd')
    dq0, dk0, dv0, dg0 = _gla_bwd(z, sv['g'], do, sv['stf'], 0, d, rides)
    dq1, dk1, dv1, dg1 = _gla_bwd(z, sv['g'], do, sv['stb'], 1, d, rides)

    def assemble(i, j, dg0, dg1, lr, wg, bg, dq0, dq1, dk0, dk1, dv0, dv1, dog):
        pre = _dot(lr, wg, 'nn') + bg
        e = jnp.exp(-jnp.abs(pre))
        dpre = jnp.concatenate([dg0, dg1], axis=1) * jnp.where(pre >= 0, e, 1.0) / (1.0 + e) / GATE_NORMALIZER
        dlr = _dot(dpre, wg, 'nt')
        parts = [dv0 + dv1, dog.astype(F32), dk0 + dk1, dq0 + dq1, dlr]
        return (jnp.concatenate([t.astype(BF16) for t in parts], axis=1), _dot(lr, dpre, 'tn'),
                jnp.sum(dpre, axis=0, keepdims=True))
    dz, dwg, dbg = _rowwise(assemble, d.NT, d.rt, d.nlt,
                            [_row(dg0), _row(dg1), _row(z, d.z1_lr // LANE, LANE), ('full', wg), ('full', bg), _row(dq0),
                             _row(dq1), _row(dk0), _row(dk1), _row(dv0), _row(dv1), _row(dog)],
                            [('row', d.ZW1, BF16), ('acc', (LANE, 2 * d.KEY)), ('acc', (1, 2 * d.KEY))], 'gla_dz')
    return dz, dict(gate_fw_w=dwg[:GATE_RANK, :d.KEY], gate_bw_w=dwg[GATE_RANK:2 * GATE_RANK, d.KEY:],
                    gate_fw_b=dbg[:, :d.KEY], gate_bw_b=dbg[:, d.KEY:], o_norm=dow)


def _loss_bwd(x, fnw, target, d):
    def body(i, j, x, w, tgt):
        y, xh, r = _rms(x, w)
        e = y - tgt
        dx, dw = _rms_bwd(e * (1.0 / d.D), xh, r, w)
        lat = i < d.nlt
        part = jnp.sum(jnp.sum(e * e, axis=1, keepdims=True), axis=0, keepdims=True) * (0.5 / d.D)
        return (jnp.where(lat, dx, 0.0), jnp.where(lat, jnp.broadcast_to(part, (8, LANE)), 0.0), jnp.where(lat, dw, 0.0))
    return _rowwise(body, d.NT, d.rt, d.nlt, [_row(x), ('full', fnw), ('rowclamp', target, d.nlt - 1, d.D)],
                    [('row', d.D, F32), ('acc', (8, LANE)), ('acc', (1, d.D))], 'loss')


def _layer_fwd(x, mods, w, mixer_fwd, d, tag, rides):
    sh1, sc1, g1, sh2, sc2, g2 = mods
    h = _mod_fwd(x, w['norm1'], sh1, sc1, d, tag + '_mod1')
    z = _mm([(h, _w(w, 'w_in'))], 'nn', F32, tag + '_in', rides=rides)
    ymix, msv = mixer_fwd(z)
    y = _mm([(ymix, _w(w, 'w_out'))], 'nn', F32, tag + '_out')
    x1, h2 = _res_mod_fwd(x, y, g1, w['norm2'], sh2, sc2, d, tag + '_mod2')
    p, a, f = _ffn_fwd(h2, w, d, tag + '_ffn', rides)
    x2 = _res_fwd(x1, f, g2, d, tag + '_res')
    return x2, dict(x=x, h=h, z=z, ymix=ymix, msv=msv, y=y, x1=x1, h2=h2, p=p, a=a, f=f)


def _layer_bwd(dx2, sv, mods, w, mixer_bwd, d, tag, rides, emit):
    sh1, sc1, g1, sh2, sc2, g2 = mods
    df, dg2 = _res_bwd(dx2, sv['f'], g2, d, tag + '_res_bwd')
    dh2, dconv_w, dconv_b = _ffn_bwd(df, sv['h2'], sv['p'], sv['a'], w, d, tag + '_ffn', rides, emit)
    dx1, dy, dg1, dn2, dsh2, dsc2 = _res_mod_bwd(dx2, dh2, sv['x1'], sv['y'], g1, w['norm2'], sh2, sc2, d, tag + '_mod2_bwd')
    dymix = _mm([(dy, w['w_out'])], 'nt', F32, tag + '_out_dx')
    emit('w_out', _mm([(sv['ymix'], dy)], 'tn', GRAD_WIRE, tag + '_out_dw'))
    dz, mg = mixer_bwd(dymix, sv['z'], sv['msv'])
    emit('w_in', _mm([(sv['h'], dz)], 'tn', GRAD_WIRE, tag + '_in_dw', rides=rides))
    dh = _mm([(dz, w['w_in'])], 'nt', F32, tag + '_in_dx', rides=rides)
    dx, dn1, dsh1, dsc1 = _mod_bwd_call(dx1, dh, sv['x'], w['norm1'], sh1, sc1, d, tag + '_mod1_bwd')
    grads = dict(mg, ffn_conv_w=dconv_w, ffn_conv_b=dconv_b, norm1=dn1, norm2=dn2)
    return dx, grads, [dsh1, dsc1, dg1, dsh2, dsc2, dg2]


def _pad_flat(v, mult=LANE):
    v = v.reshape(-1)
    return jnp.pad(v, (0, (-v.shape[0]) % mult))


def _pack(entries, row_mult):
    flat, offs, pos = [], [], 0
    for v in entries:
        f = _pad_flat(v)
        flat.append(f)
        offs.append(pos)
        pos += f.shape[0]
    tot = jnp.concatenate(flat)
    tot = jnp.pad(tot, (0, (-pos) % (row_mult * LANE)))
    return tot.reshape(-1, LANE), offs


def _unpack(packed, offs, shapes):
    flat = packed.reshape(-1)
    out = []
    for off, shp in zip(offs, shapes):
        n = 1
        for s in shp:
            n *= s
        out.append(flat[off:off + n].reshape(shp))
    return out


def _step(a):
    d = _dims()
    dm = d.D
    me = 4 * lax.axis_index("x") + 2 * lax.axis_index("y") + lax.axis_index("c")
    sds = jax.ShapeDtypeStruct

    rides = _Rides()
    fwd_hosts = {'l0_w_out': 'l0_in', 'l0_ffn_up': 'flash_fwd', 'l0_ffn_down': 'flash_fwd', 'l1_w_in': 'flash_fwd',
                 'l1_w_out': 'l0_ffn_up', 'l1_ffn_up': 'l0_ffn_up', 'l1_ffn_down': 'l0_ffn_down'}

    def gathered(name):
        shard = _cast_bf16(a[name], 'cast_' + name)
        if name in fwd_hosts:
            rides.add(fwd_hosts[name], name, shard, False)
            return lambda: rides.done[name]
        g = _exchange(shard, False, 'ag_' + name)
        return lambda: g

    def cols(name, relayout=lambda t: t):
        g = gathered(name)
        k, n = a[name].shape
        return lambda: relayout(g().transpose(1, 0, 2).reshape(k, N_DEV * n))

    def rows(name):
        g = gathered(name)
        return lambda: g().reshape(N_DEV * a[name].shape[0], a[name].shape[1])

    small_names = ['l0_conv_a', 'l0_ffn_conv_w', 'l1_ffn_conv_w', 'l1_gate_fw_w', 'l1_gate_bw_w']
    spack, soffs = _pack([a[n] for n in small_names], 8)
    sg = _exchange(spack, False, 'ag_small')
    small_w = {}
    for n, off in zip(small_names, soffs):
        r, c = a[n].shape
        shards = sg.reshape(N_DEV, -1)[:, off:off + r * c].reshape(N_DEV, r, c)
        small_w[n] = shards.transpose(1, 0, 2).reshape(r, N_DEV * c)

    w0 = dict(norm1=a['l0_norm1'].reshape(1, dm), norm2=a['l0_norm2'].reshape(1, dm),
              w_in=cols('l0_w_in', lambda t: _win0_to_kernel(t, d)), w_qb=cols('l0_w_qb', lambda t: _wqb_to_kernel(t, d))(),
              w_kvb=cols('l0_w_kvb', lambda t: _wkvb_to_kernel(t, d))(), w_out=rows('l0_w_out'),
              q_norm=a['l0_q_norm'].reshape(1, -1), kv_norm=a['l0_kv_norm'].reshape(1, -1), conv_a=small_w['l0_conv_a'],
              ffn_up=cols('l0_ffn_up'), ffn_conv_w=small_w['l0_ffn_conv_w'], ffn_conv_b=a['l0_ffn_conv_b'],
              ffn_down=rows('l0_ffn_down'))
    w1 = dict(norm1=a['l1_norm1'].reshape(1, dm), norm2=a['l1_norm2'].reshape(1, dm),
              w_in=cols('l1_w_in', lambda t: _win1_to_kernel(t, d)), w_out=rows('l1_w_out'),
              gate_w=_gate_weight(small_w['l1_gate_fw_w'], small_w['l1_gate_bw_w'], d),
              gate_b=jnp.concatenate([a['l1_gate_fw_b'], a['l1_gate_bw_b']]).reshape(1, -1),
              o_norm=a['l1_o_norm'].reshape(1, -1),
              ffn_up=cols('l1_ffn_up'), ffn_conv_w=small_w['l1_ffn_conv_w'], ffn_conv_b=a['l1_ffn_conv_b'],
              ffn_down=rows('l1_ffn_down'))

    c8 = _exchange(a['c'], False, 'ag_c').reshape(N_DEV, dm)
    c16 = jnp.concatenate([c8, a['c_ctx'].reshape(1, dm), jnp.zeros((7, dm), F32)], axis=0)
    act16, dact16 = _small(lambda v: (_silu(v), _dsilu(v)), [c16], [sds((16, dm), BF16), sds((16, dm), F32)], 'cond_silu')
    n6 = N_MOD * dm // N_DEV
    mod_sh = [_mm([(act16, a[f'l{l}_ada_w'])], 'nn', F32, f'ada{l}') for l in (0, 1)]
    mod_all = _exchange(jnp.concatenate(mod_sh, axis=1), False, 'ag_mod')
    mods = []
    for l in (0, 1):
        full = mod_all[:, :, l * n6:(l + 1) * n6].transpose(1, 0, 2).reshape(16, N_MOD * dm)
        mine = jnp.concatenate([lax.dynamic_slice_in_dim(full, me, 1, 0), full[8:9]], axis=0)
        m2 = _small(lambda r, b: (r + b,), [mine, a[f'l{l}_ada_b'].reshape(1, -1)], [sds((2, N_MOD * dm), F32)], f'ada{l}_bias')[0]
        mods.append([m2[:, k * dm:(k + 1) * dm].reshape(2, 1, dm) for k in range(N_MOD)])

    cos, sin_s = _rope_tables(d)
    x0 = jnp.concatenate([a['x'][0], a['ctx'][0]], axis=0)
    x2, sv0 = _layer_fwd(x0, mods[0], w0, lambda z: _ab_fwd(z, w0, cos, sin_s, d, rides), d, 'l0', rides)
    x4, sv1 = _layer_fwd(x2, mods[1], w1, lambda z: _gla_fwd_block(z, w1, d), d, 'l1', rides)
    dx4, loss_acc, dfn = _loss_bwd(x4, a['final_norm'].reshape(1, dm), a['loss_target'][0], d)

    bwd_hosts = {'l1_ffn_down': 'l1_ffn_up_dx', 'l1_ffn_up': 'gla_bwd_0', 'l1_w_out': 'gla_bwd_1', 'l1_w_in': 'l1_in_dx',
                 'l0_ffn_down': 'l0_ffn_up_dx', 'l0_ffn_up': 'flash_bwd', 'l0_w_out': 'flash_bwd', 'l0_w_qb': 'l0_in_dw',
                 'l0_w_kvb': 'l0_in_dw', 'l0_w_in': 'l0_in_dx'}
    from_kernel = {'l0_w_in': _win0_from_kernel, 'l0_w_qb': _wqb_from_kernel, 'l0_w_kvb': _wkvb_from_kernel,
                   'l1_w_in': _win1_from_kernel}
    slabs = {}

    def emitter(layer):
        def emit(wkey, dw):
            name = f'l{layer}_{wkey}'
            if name in from_kernel:
                dw = from_kernel[name](dw, d)
            if a[name].shape[0] == dw.shape[0]:
                stacked = dw.reshape(dw.shape[0], N_DEV, dw.shape[1] // N_DEV).transpose(1, 0, 2)
            else:
                stacked = dw.reshape(N_DEV, dw.shape[0] // N_DEV, dw.shape[1])
            if name in bwd_hosts:
                rides.add(bwd_hosts[name], 'rs_' + name, stacked, True)
                slabs[name] = lambda: rides.done['rs_' + name]
            else:
                got = _exchange(stacked, True, 'rs_' + name)
                slabs[name] = lambda: got
        return emit

    dx2, g1, dmod1 = _layer_bwd(dx4, sv1, mods[1], w1, lambda dy, z, msv: _gla_bwd_block(dy, z, msv, w1, d, rides), d, 'l1',
                                rides, emitter(1))
    dx0, g0, dmod0 = _layer_bwd(dx2, sv0, mods[0], w0,
                                lambda dy, z, msv: _ab_bwd(dy, z, msv, w0, cos, sin_s, d, rides, emitter(0)), d, 'l0',
                                rides, emitter(0))

    dm_rows = jnp.concatenate([jnp.concatenate([t.reshape(2, dm) for t in dmod], axis=1) for dmod in (dmod0, dmod1)], axis=0)
    dm_all = _exchange(dm_rows, False, 'ag_dmod')
    lat = dm_all[:, 0::2].transpose(1, 0, 2)
    ctxs = dm_all[:, 1::2].transpose(1, 0, 2)

    def ada_prep(lat, ctxs):
        csum = jnp.sum(ctxs, axis=1, keepdims=True)
        row = lax.broadcasted_iota(jnp.int32, (1, 8, 1), 1)
        g16 = jnp.concatenate([lat, jnp.where(row == 0, csum, 0.0)], axis=1)
        return g16, jnp.sum(lat, axis=1, keepdims=True) + csum
    g16, gb = _small(ada_prep, [lat, ctxs], [sds((2, 16, N_MOD * dm), F32), sds((2, 1, N_MOD * dm), F32)], 'ada_bwd_prep')
    g16_sh = [lax.dynamic_slice_in_dim(g16[l], me * n6, n6, 1) for l in (0, 1)]
    grad_ada_w = [_mm([(act16, g16_sh[l])], 'tn', F32, f'ada{l}_dw') for l in (0, 1)]
    dact = _mm([(g16_sh[0], a['l0_ada_w']), (g16_sh[1], a['l1_ada_w'])], 'nt', F32, 'ada_dact')
    dcc = _small(lambda t, s: (t * s,), [dact[8:9], dact16[8:9]], [sds((1, dm), F32)], 'cctx_grad')[0]

    res = {}
    for name in slabs:
        res[name] = _adam(slabs[name](), a[name], a['m_' + name], a['v_' + name], 'adam_' + name)
    for l in (0, 1):
        name = f'l{l}_ada_w'
        res[name] = _adam(grad_ada_w[l][None], a[name], a['m_' + name], a['v_' + name], 'adam_' + name)

    part = {'loss': loss_acc[0:1, 0:1], 'c_ctx': dcc, 'final_norm': dfn,
            'l0_norm1': g0['norm1'], 'l0_norm2': g0['norm2'], 'l0_q_norm': g0['q_norm'], 'l0_kv_norm': g0['kv_norm'],
            'l0_conv_a': g0['conv_a'], 'l0_ffn_conv_w': g0['ffn_conv_w'], 'l0_ffn_conv_b': g0['ffn_conv_b'],
            'l1_norm1': g1['norm1'], 'l1_norm2': g1['norm2'], 'l1_o_norm': g1['o_norm'],
            'l1_gate_fw_w': g1['gate_fw_w'], 'l1_gate_bw_w': g1['gate_bw_w'], 'l1_gate_fw_b': g1['gate_fw_b'],
            'l1_gate_bw_b': g1['gate_bw_b'], 'l1_ffn_conv_w': g1['ffn_conv_w'], 'l1_ffn_conv_b': g1['ffn_conv_b']}
    pkeys = list(part)
    ppack, poffs = _pack([part[k] for k in pkeys], 8)
    psum = _sum_parts(_exchange(ppack, False, 'ag_small_grads'), 'sum_small_grads')
    tot = dict(zip(pkeys, _unpack(psum, poffs, [part[k].shape for k in pkeys])))
    loss = tot['loss'].reshape(())
    sgrad = {}
    for n in _WEIGHTS:
        if n in res:
            continue
        if n.endswith('ada_b'):
            sgrad[n] = gb[int(n[1])].reshape(a[n].shape)
        elif n in small_names:
            c = a[n].shape[1]
            sgrad[n] = lax.dynamic_slice_in_dim(tot[n], me * c, c, 1)
        else:
            sgrad[n] = tot[n].reshape(a[n].shape)
    snames = list(sgrad)
    packs = [_pack([src[n] for n in snames], 8)[0] for src in
             (sgrad, {n: a[n] for n in snames}, {n: a['m_' + n] for n in snames}, {n: a['v_' + n] for n in snames})]
    offs = _pack([sgrad[n] for n in snames], 8)[1]
    outs = _adam(packs[0][None], packs[1], packs[2], packs[3], 'adam_small')
    for k in range(4):
        for n, val in zip(snames, _unpack(outs[k], offs, [a[n].shape for n in snames])):
            res.setdefault(n, [None] * 4)[k] = val

    grad_x = dx0[:d.T].reshape(1, d.T, dm)
    return (loss, grad_x, *[res[n][0] for n in _WEIGHTS], *[res[n][1] for n in _WEIGHTS], *[res[n][2] for n in _WEIGHTS],
            *[res[n][3] for n in _WEIGHTS])


def kernel(*args):
    return _step(dict(zip(_ARGS, args, strict=True)))
```

```python
import functools
import types

import jax
import jax.numpy as jnp
from jax import lax
from jax.experimental import pallas as pl
from jax.experimental.pallas import tpu as pltpu

D_MODEL = 2048
SEQ = 8192
GRID_W = 64
CTX_LEN = 256
EPS = 1e-6
N_MOD = 6
MLA_HEADS = 8
QK_NOPE = 128
QK_ROPE = 64
V_HEAD = 128
Q_LORA = 512
KV_LORA = 256
ROPE_THETA = 10000.0
GLA_HEADS = 4
GATE_RANK = 16
GATE_NORMALIZER = 16.0
CHUNK = 64
D_FF = 5632
ADAM_LR = 0.001
ADAM_B1 = 0.9
ADAM_B2 = 0.999
ADAM_EPS = 1e-08
ADAM_WD = 0.01
ADAM_STEP = 10

N_DEV = 8
LANE = 128
VMEM_LIMIT = 56 * 1024 * 1024
CONV_COLS = 2816
GLU_COLS = 1408
LN2 = 0.6931471805599453
ATTN_SCALE = float(QK_NOPE + QK_ROPE) ** -0.5

F32 = jnp.float32
BF16 = jnp.bfloat16
GRAD_WIRE = jnp.bfloat16
MESH_ID = pl.DeviceIdType.MESH

_FWD = ['x', 'c', 'ctx', 'c_ctx', 'l0_ada_w', 'l0_ada_b', 'l0_norm1', 'l0_w_in', 'l0_conv_a', 'l0_q_norm', 'l0_w_qb',
        'l0_kv_norm', 'l0_w_kvb', 'l0_w_out', 'l0_norm2', 'l0_ffn_up', 'l0_ffn_conv_w', 'l0_ffn_conv_b', 'l0_ffn_down',
        'l1_ada_w', 'l1_ada_b', 'l1_norm1', 'l1_w_in', 'l1_gate_fw_w', 'l1_gate_fw_b', 'l1_gate_bw_w', 'l1_gate_bw_b',
        'l1_o_norm', 'l1_w_out', 'l1_norm2', 'l1_ffn_up', 'l1_ffn_conv_w', 'l1_ffn_conv_b', 'l1_ffn_down', 'final_norm']
_WEIGHTS = _FWD[3:]
_ARGS = _FWD + ['loss_target'] + ['m_' + n for n in _WEIGHTS] + ['v_' + n for n in _WEIGHTS]


def _dims():
    d = types.SimpleNamespace()
    d.D, d.T, d.TC = D_MODEL, SEQ, CTX_LEN
    d.NT = d.T + d.TC
    d.rt = 256 if d.TC % 256 == 0 else 128
    d.nlt = d.T // d.rt
    d.H = MLA_HEADS
    d.QL, d.KVL = Q_LORA, KV_LORA
    d.CC = D_MODEL // 2
    d.z0_kv = d.QL
    d.z0_kr = d.QL + d.KVL
    d.z0_pad = (-(d.QL + d.KVL + LANE)) % d.CC
    d.z0_ax = d.QL + d.KVL + LANE + d.z0_pad
    d.ZW0 = d.z0_ax + 3 * d.CC
    d.AB_COLS = d.KVL + QK_ROPE + d.QL + 3 * d.CC
    d.HQ = d.H * LANE
    d.GH = GLA_HEADS
    d.KEY = D_MODEL // 2
    d.VAL = D_MODEL
    d.dk = d.KEY // d.GH
    d.dv = d.VAL // d.GH
    d.z1_og = d.VAL
    d.z1_k = 2 * d.VAL
    d.z1_q = 2 * d.VAL + d.KEY
    d.z1_lr = 2 * d.VAL + 2 * d.KEY
    d.ZW1 = d.z1_lr + LANE
    d.GLA_COLS = 2 * d.KEY + 2 * d.VAL + 2 * GATE_RANK
    d.FF = D_FF
    d.tq = min(256, d.TC)
    return d


def _tile(n, pref, align=LANE):
    if n <= pref:
        return n
    t = (pref // align) * align
    while t >= align:
        if n % t == 0:
            return t
        t -= align
    raise ValueError(f"no tile for {n}")


def _cparams(sem):
    return pltpu.CompilerParams(dimension_semantics=sem, vmem_limit_bytes=VMEM_LIMIT)


def _dot(a, b, mode):
    dims = {'nn': (((1,), (0,)), ((), ())), 'nt': (((1,), (1,)), ((), ())), 'tn': (((0,), (0,)), ((), ()))}[mode]
    return lax.dot_general(a.astype(BF16), b.astype(BF16), dims, preferred_element_type=F32)


class _Rides:
    def __init__(self):
        self.pending, self.done = {}, {}

    def add(self, host, key, x, stacked):
        self.pending.setdefault(host, []).append((key, x, stacked))

    def take(self, host):
        return self.pending.pop(host, [])


def _ride_plumbing(riders):
    n = len(riders)
    if not n:
        return [], [], [], []
    spec = pl.BlockSpec(memory_space=pl.ANY)
    shapes = [jax.ShapeDtypeStruct((N_DEV,) + tuple(x.shape[1:] if st else x.shape), x.dtype) for _, x, st in riders]
    sems = [pltpu.SemaphoreType.DMA((n * (N_DEV - 1),)), pltpu.SemaphoreType.DMA((n * (N_DEV - 1),)),
            pltpu.SemaphoreType.DMA((n,))]
    return [spec] * n, shapes, [spec] * n, sems


def _ride_copies(stacked_flags, x_refs, o_refs, send_sems, recv_sems, local_sems):
    ix, iy, ic = lax.axis_index("x"), lax.axis_index("y"), lax.axis_index("c")
    me = 4 * ix + 2 * iy + ic
    local, sends, recvs = [], [], []
    for r, (stacked, x_ref, o_ref) in enumerate(zip(stacked_flags, x_refs, o_refs)):
        def src(p, x_ref=x_ref, stacked=stacked):
            return x_ref.at[p] if stacked else x_ref

        local.append(pltpu.make_async_copy(src(me), o_ref.at[me], local_sems.at[r]))
        for k in range(1, N_DEV):
            px, py, pc = (ix + ((k >> 2) & 1)) % 2, (iy + ((k >> 1) & 1)) % 2, (ic + (k & 1)) % 2
            peer = 4 * px + 2 * py + pc
            s = r * (N_DEV - 1) + k - 1
            sends.append(pltpu.make_async_remote_copy(
                src_ref=src(peer), dst_ref=o_ref.at[me], send_sem=send_sems.at[s], recv_sem=recv_sems.at[s],
                device_id=(px, py, pc), device_id_type=MESH_ID))
            recvs.append(pltpu.make_async_remote_copy(
                src_ref=src(peer), dst_ref=o_ref.at[peer], send_sem=send_sems.at[s], recv_sem=recv_sems.at[s],
                device_id=(px, py, pc), device_id_type=MESH_ID))

    def start():
        for cp in local + sends:
            cp.start()

    def wait():
        for cp in recvs:
            cp.wait_recv()
        for cp in sends:
            cp.wait_send()
        for cp in local:
            cp.wait()

    return start, wait


def _mm(pairs, mode, out_dtype, name, tm=768, tn=1024, tk=2816, rides=None):
    riders = rides.take(name) if rides is not None else []
    nr = len(riders)
    r_in, r_shapes, r_out, r_sems = _ride_plumbing(riders)
    a0, b0 = pairs[0]
    if mode == 'nn':
        (m, k), n = a0.shape, b0.shape[1]
    elif mode == 'nt':
        (m, k), n = a0.shape, b0.shape[0]
    else:
        (k, m), n = a0.shape, b0.shape[1]
    tm, tn, tk = _tile(m, tm), _tile(n, tn), _tile(k, tk)
    nk = k // tk
    if mode == 'nn':
        a_spec = pl.BlockSpec((tm, tk), lambda i, j, kk: (i, kk))
        b_spec = pl.BlockSpec((tk, tn), lambda i, j, kk: (kk, j))
    elif mode == 'nt':
        a_spec = pl.BlockSpec((tm, tk), lambda i, j, kk: (i, kk))
        b_spec = pl.BlockSpec((tn, tk), lambda i, j, kk: (j, kk))
    else:
        a_spec = pl.BlockSpec((tk, tm), lambda i, j, kk: (kk, i))
        b_spec = pl.BlockSpec((tk, tn), lambda i, j, kk: (kk, j))
    npairs = len(pairs)

    nin = 2 * npairs
    gi, gj = m // tm, n // tn

    def body(*refs):
        o_ref, acc_ref = refs[nin + nr], refs[nin + 2 * nr + 1]
        i, j, kk = pl.program_id(0), pl.program_id(1), pl.program_id(2)
        if nr:
            start, wait = _ride_copies([st for _, _, st in riders], refs[nin:nin + nr], refs[nin + nr + 1:nin + 2 * nr + 1],
                                       *refs[nin + 2 * nr + 2:])
            pl.when(jnp.logical_and(jnp.logical_and(i == 0, j == 0), kk == 0))(start)

        def dots():
            s = None
            for p in range(npairs):
                t = _dot(refs[2 * p][...], refs[2 * p + 1][...], mode)
                s = t if s is None else s + t
            return s

        if nk == 1:
            o_ref[...] = dots().astype(o_ref.dtype)
        else:
            @pl.when(kk == 0)
            def _():
                acc_ref[...] = dots()

            @pl.when(jnp.logical_and(kk > 0, kk < nk - 1))
            def _():
                acc_ref[...] += dots()

            @pl.when(kk == nk - 1)
            def _():
                o_ref[...] = (acc_ref[...] + dots()).astype(o_ref.dtype)

        if nr:
            pl.when(jnp.logical_and(jnp.logical_and(i == gi - 1, j == gj - 1), kk == nk - 1))(wait)

    flat = [t for ab in pairs for t in ab]
    res = pl.pallas_call(
        body, name=name,
        out_shape=[jax.ShapeDtypeStruct((m, n), out_dtype)] + r_shapes,
        grid=(gi, gj, nk),
        in_specs=[a_spec, b_spec] * npairs + r_in,
        out_specs=[pl.BlockSpec((tm, tn), lambda i, j, kk: (i, j))] + r_out,
        scratch_shapes=[pltpu.VMEM((tm, tn), F32)] + r_sems,
        compiler_params=_cparams(("arbitrary",) * 3 if nr else ("parallel", "parallel", "arbitrary")),
    )(*flat, *[x for _, x, _ in riders])
    for (key, _, _), arr in zip(riders, res[1:]):
        rides.done[key] = arr
    return res[0]


def _rowwise(body, nrows, rt, nlt, ins, outs, name, ncol=1):
    ntiles = nrows // rt
    in_specs, args = [], []
    for spec in ins:
        kind, arr = spec[0], spec[1]
        if kind == 'row':
            in_specs.append(pl.BlockSpec((rt, spec[3]), functools.partial(lambda i, j, cb: (i, cb), cb=spec[2])))
        elif kind == 'rowc':
            in_specs.append(pl.BlockSpec((rt, spec[3]), functools.partial(lambda i, j, cb: (i, cb + j), cb=spec[2])))
        elif kind == 'rowm':
            in_specs.append(pl.BlockSpec(
                (rt, spec[3]), functools.partial(lambda i, j, cb, md: (i, cb + j % md), cb=spec[2], md=spec[4])))
        elif kind == 'rowclamp':
            in_specs.append(pl.BlockSpec(
                (rt, spec[3]), functools.partial(lambda i, j, mb: (jnp.minimum(i, mb), 0), mb=spec[2])))
        elif kind == 'row3':
            in_specs.append(pl.BlockSpec((arr.shape[0], rt, arr.shape[2]), lambda i, j: (0, i, 0)))
        elif kind == 'seg':
            in_specs.append(pl.BlockSpec((None, 1, arr.shape[2]), lambda i, j: (i // nlt, 0, 0)))
        else:
            in_specs.append(pl.BlockSpec(arr.shape, functools.partial(lambda i, j, nd: (0,) * nd, nd=arr.ndim)))
        args.append(arr)
    out_shapes, out_specs = [], []
    for spec in outs:
        kind = spec[0]
        if kind == 'row':
            out_shapes.append(jax.ShapeDtypeStruct((nrows, spec[1]), spec[2]))
            out_specs.append(pl.BlockSpec((rt, spec[1]), lambda i, j: (i, 0)))
        elif kind == 'rowc':
            out_shapes.append(jax.ShapeDtypeStruct((nrows, spec[1]), spec[3]))
            out_specs.append(pl.BlockSpec((rt, spec[2]), lambda i, j: (i, j)))
        elif kind == 'acc':
            out_shapes.append(jax.ShapeDtypeStruct(spec[1], F32))
            out_specs.append(pl.BlockSpec(spec[1], functools.partial(lambda i, j, nd: (0,) * nd, nd=len(spec[1]))))
        else:
            out_shapes.append(jax.ShapeDtypeStruct((2, 1, spec[1]), F32))
            out_specs.append(pl.BlockSpec((None, 1, spec[1]), lambda i, j: (i // nlt, 0, 0)))
    n_in = len(ins)
    has_acc = any(s[0] in ('acc', 'segacc') for s in outs)

    def kern(*refs):
        i = pl.program_id(0)
        j = pl.program_id(1)
        vals = [r[...] for r in refs[:n_in]]
        res = body(i, j, *vals)
        for spec, ref, val in zip(outs, refs[n_in:], res):
            if spec[0] in ('row', 'rowc'):
                ref[...] = val.astype(ref.dtype)
            else:
                first = (i == 0) if spec[0] == 'acc' else jnp.logical_or(i == 0, i == nlt)

                @pl.when(first)
                def _(ref=ref, val=val):
                    ref[...] = val

                @pl.when(jnp.logical_not(first))
                def _(ref=ref, val=val):
                    ref[...] += val

    return pl.pallas_call(
        kern, name=name, out_shape=tuple(out_shapes), grid=(ntiles, ncol),
        in_specs=in_specs, out_specs=tuple(out_specs),
        compiler_params=_cparams(("arbitrary", "arbitrary") if has_acc else ("parallel", "parallel")),
    )(*args)


def _small(body, args, out_shapes, name):
    n_in = len(args)

    def kern(*refs):
        res = body(*[r[...] for r in refs[:n_in]])
        for ref, val in zip(refs[n_in:], res):
            ref[...] = val.astype(ref.dtype)

    return pl.pallas_call(
        kern, name=name, out_shape=tuple(out_shapes),
        in_specs=[pl.BlockSpec(memory_space=pltpu.VMEM)] * n_in,
        out_specs=tuple(pl.BlockSpec(memory_space=pltpu.VMEM) for _ in out_shapes),
        compiler_params=pltpu.CompilerParams(vmem_limit_bytes=VMEM_LIMIT),
    )(*args)


def _exchange(x, stacked, name):
    r_in, r_shapes, r_out, r_sems = _ride_plumbing([(name, x, stacked)])

    def body(x_ref, o_ref, send_sems, recv_sems, local_sems):
        start, wait = _ride_copies([stacked], [x_ref], [o_ref], send_sems, recv_sems, local_sems)
        start()
        wait()

    return pl.pallas_call(body, name=name, out_shape=r_shapes[0], in_specs=r_in, out_specs=r_out[0], scratch_shapes=r_sems)(x)


def _cast_bf16(x, name):
    r, c = x.shape
    tr = _tile(r, 256, 8)
    return pl.pallas_call(
        lambda x_ref, o_ref: o_ref.__setitem__(Ellipsis, x_ref[...].astype(BF16)), name=name,
        out_shape=jax.ShapeDtypeStruct((r, c), BF16), grid=(r // tr,),
        in_specs=[pl.BlockSpec((tr, c), lambda i: (i, 0))], out_specs=pl.BlockSpec((tr, c), lambda i: (i, 0)),
        compiler_params=_cparams(("parallel",)),
    )(x)


def _adam(parts, w, m, v, name):
    p, r, c = parts.shape
    tr = _tile(r, 64, 8)

    def body(p_ref, w_ref, m_ref, v_ref, g_ref, d_ref, nm_ref, nv_ref):
        g = p_ref[0].astype(F32)
        for q in range(1, p):
            g = g + p_ref[q].astype(F32)
        nm = ADAM_B1 * m_ref[...] + (1.0 - ADAM_B1) * g
        nv = ADAM_B2 * v_ref[...] + (1.0 - ADAM_B2) * (g * g)
        m_hat = nm / (1.0 - ADAM_B1 ** ADAM_STEP)
        v_hat = nv / (1.0 - ADAM_B2 ** ADAM_STEP)
        g_ref[...] = g
        d_ref[...] = -ADAM_LR * (m_hat / (jnp.sqrt(v_hat) + ADAM_EPS) + ADAM_WD * w_ref[...])
        nm_ref[...] = nm
        nv_ref[...] = nv

    spec = pl.BlockSpec((tr, c), lambda i: (i, 0))
    return pl.pallas_call(
        body, name=name, out_shape=tuple(jax.ShapeDtypeStruct((r, c), F32) for _ in range(4)), grid=(r // tr,),
        in_specs=[pl.BlockSpec((p, tr, c), lambda i: (0, i, 0)), spec, spec, spec], out_specs=(spec,) * 4,
        compiler_params=_cparams(("parallel",)),
    )(parts, w, m, v)


def _sum_parts(parts, name):
    p, r, c = parts.shape
    tr = _tile(r, 256, 8)

    def body(p_ref, o_ref):
        g = p_ref[0]
        for q in range(1, p):
            g = g + p_ref[q]
        o_ref[...] = g

    return pl.pallas_call(
        body, name=name, out_shape=jax.ShapeDtypeStruct((r, c), F32), grid=(r // tr,),
        in_specs=[pl.BlockSpec((p, tr, c), lambda i: (0, i, 0))], out_specs=pl.BlockSpec((tr, c), lambda i: (i, 0)),
        compiler_params=_cparams(("parallel",)),
    )(parts)


def _shifted(cur, prev8, next8, i, rt, nlt, ntiles):
    first = jnp.logical_or(i == 0, i == nlt)
    last = jnp.logical_or(i == nlt - 1, i == ntiles - 1)
    prev_row = jnp.where(first, 0.0, prev8[7:8, :])
    next_row = jnp.where(last, 0.0, next8[0:1, :])
    rows = lax.broadcasted_iota(jnp.int32, (rt, 1), 0)
    x_m1 = jnp.where(rows == 0, prev_row, pltpu.roll(cur, 1, 0))
    x_p1 = jnp.where(rows == rt - 1, next_row, pltpu.roll(cur, rt - 1, 0))
    return x_m1, x_p1


def _halo_specs(rt, tc, nrows, col_axis_first):
    r8 = rt // 8
    last8 = nrows // 8 - 1
    if col_axis_first:
        return [pl.BlockSpec((8, tc), lambda j, i: (jnp.maximum(i * r8 - 1, 0), j)),
                pl.BlockSpec((rt, tc), lambda j, i: (i, j)),
                pl.BlockSpec((8, tc), lambda j, i: (jnp.minimum((i + 1) * r8, last8), j))]
    return [pl.BlockSpec((8, tc), lambda i, j: (jnp.maximum(i * r8 - 1, 0), j)),
            pl.BlockSpec((rt, tc), lambda i, j: (i, j)),
            pl.BlockSpec((8, tc), lambda i, j: (jnp.minimum((i + 1) * r8, last8), j))]


def _dwconv(x, w8, d, out_dtype, name):
    nrows, c = x.shape
    rt, nlt = d.rt, d.nlt
    tc = _tile(c, CONV_COLS)
    ntiles = nrows // rt

    def body(p_ref, c_ref, n_ref, w_ref, o_ref):
        i = pl.program_id(0)
        cur = c_ref[...]
        x_m1, x_p1 = _shifted(cur, p_ref[...], n_ref[...], i, rt, nlt, ntiles)
        w = w_ref[...]
        o_ref[...] = (x_m1 * w[0:1] + cur * w[1:2] + x_p1 * w[2:3] + w[3:4]).astype(o_ref.dtype)

    return pl.pallas_call(
        body, name=name, out_shape=jax.ShapeDtypeStruct((nrows, c), out_dtype), grid=(ntiles, c // tc),
        in_specs=_halo_specs(rt, tc, nrows, False) + [pl.BlockSpec((8, tc), lambda i, j: (0, j))],
        out_specs=pl.BlockSpec((rt, tc), lambda i, j: (i, j)),
        compiler_params=_cparams(("parallel", "parallel")),
    )(x, x, x, w8)


def _dwconv_wgrad(x, dy, d, name):
    nrows, c = x.shape
    rt, nlt = d.rt, d.nlt
    tc = _tile(c, CONV_COLS)
    ntiles = nrows // rt

    def body(p_ref, c_ref, n_ref, dy_ref, o_ref):
        i = pl.program_id(1)
        cur = c_ref[...]
        dy = dy_ref[...]
        x_m1, x_p1 = _shifted(cur, p_ref[...], n_ref[...], i, rt, nlt, ntiles)
        sums = [jnp.sum(t * dy, axis=0, keepdims=True) for t in (x_m1, cur, x_p1)] + [jnp.sum(dy, axis=0, keepdims=True)]
        row = lax.broadcasted_iota(jnp.int32, (8, 1), 0)
        part = jnp.zeros((8, tc), F32)
        for k, s in enumerate(sums):
            part = jnp.where(row == k, s, part)

        @pl.when(i == 0)
        def _():
            o_ref[...] = part

        @pl.when(i != 0)
        def _():
            o_ref[...] += part

    return pl.pallas_call(
        body, name=name, out_shape=jax.ShapeDtypeStruct((8, c), F32), grid=(c // tc, ntiles),
        in_specs=_halo_specs(rt, tc, nrows, True) + [pl.BlockSpec((rt, tc), lambda j, i: (i, j))],
        out_specs=pl.BlockSpec((8, tc), lambda j, i: (0, j)),
        compiler_params=_cparams(("parallel", "arbitrary")),
    )(x, x, x, dy)


def _ffn_gate_fwd(p, w8, d, name):
    nrows, ff = p.shape[0], p.shape[1] // 2
    rt, nlt = d.rt, d.nlt
    tc = _tile(ff, GLU_COLS)
    nb = ff // tc
    ntiles = nrows // rt

    def body(gp_ref, gc_ref, gn_ref, vp_ref, vc_ref, vn_ref, wg_ref, wv_ref, o_ref):
        i = pl.program_id(0)
        us = []
        for p_ref, c_ref, n_ref, w_ref in ((gp_ref, gc_ref, gn_ref, wg_ref), (vp_ref, vc_ref, vn_ref, wv_ref)):
            cur, w = c_ref[...], w_ref[...]
            x_m1, x_p1 = _shifted(cur, p_ref[...], n_ref[...], i, rt, nlt, ntiles)
            us.append(x_m1 * w[0:1] + cur * w[1:2] + x_p1 * w[2:3] + w[3:4])
        o_ref[...] = (_silu(us[0]) * us[1]).astype(o_ref.dtype)

    r8, last8 = rt // 8, nrows // 8 - 1

    def halo(off):
        return [pl.BlockSpec((8, tc), lambda i, j: (jnp.maximum(i * r8 - 1, 0), j + off)),
                pl.BlockSpec((rt, tc), lambda i, j: (i, j + off)),
                pl.BlockSpec((8, tc), lambda i, j: (jnp.minimum((i + 1) * r8, last8), j + off))]
    return pl.pallas_call(
        body, name=name, out_shape=jax.ShapeDtypeStruct((nrows, ff), BF16), grid=(ntiles, nb),
        in_specs=halo(0) + halo(nb) + [pl.BlockSpec((8, tc), lambda i, j: (0, j)), pl.BlockSpec((8, tc), lambda i, j: (0, j + nb))],
        out_specs=pl.BlockSpec((rt, tc), lambda i, j: (i, j)),
        compiler_params=_cparams(("parallel", "parallel")),
    )(p, p, p, p, p, p, w8, w8)


def _ffn_gate_bwd(p, da, w8, d, name):
    nrows, ff = da.shape
    rt, nlt = d.rt, d.nlt
    tc = _tile(ff, 512)
    nb = ff // tc
    ntiles = nrows // rt
    ext = rt + 16

    def body(gp_ref, gc_ref, gn_ref, vp_ref, vc_ref, vn_ref, ap_ref, ac_ref, an_ref, wg_ref, wv_ref,
             dpg_ref, dpv_ref, cg_ref, cv_ref):
        i = pl.program_id(1)
        first = jnp.logical_or(i == 0, i == nlt)
        last = jnp.logical_or(i == nlt - 1, i == ntiles - 1)

        def extended(p_ref, c_ref, n_ref):
            return jnp.concatenate([jnp.where(first, 0.0, p_ref[...]), c_ref[...], jnp.where(last, 0.0, n_ref[...])], axis=0)

        def conv(x, w):
            return pltpu.roll(x, 1, 0) * w[0:1] + x * w[1:2] + pltpu.roll(x, ext - 1, 0) * w[2:3]

        wg, wv = wg_ref[...], wv_ref[...]
        pg, pv, da_e = extended(gp_ref, gc_ref, gn_ref), extended(vp_ref, vc_ref, vn_ref), extended(ap_ref, ac_ref, an_ref)
        ug = conv(pg, wg) + wg[3:4]
        uv = conv(pv, wv) + wv[3:4]
        sg = _sigmoid(ug)
        dug = da_e * uv * (sg * (1.0 + ug * (1.0 - sg)))
        duv = da_e * (ug * sg)
        row = lax.broadcasted_iota(jnp.int32, (8, 1), 0)
        for p_e, du, w, dp_ref, c_ref in ((pg, dug, wg, dpg_ref, cg_ref), (pv, duv, wv, dpv_ref, cv_ref)):
            dp = pltpu.roll(du, 1, 0) * w[2:3] + du * w[1:2] + pltpu.roll(du, ext - 1, 0) * w[0:1]
            dp_ref[...] = dp[8:rt + 8].astype(dp_ref.dtype)
            du_c = du[8:rt + 8]
            sums = [jnp.sum(t[8:rt + 8] * du_c, axis=0, keepdims=True)
                    for t in (pltpu.roll(p_e, 1, 0), p_e, pltpu.roll(p_e, ext - 1, 0))] + [jnp.sum(du_c, axis=0, keepdims=True)]
            part = jnp.zeros((8, tc), F32)
            for k, s in enumerate(sums):
                part = jnp.where(row == k, s, part)

            @pl.when(i == 0)
            def _(c_ref=c_ref, part=part):
                c_ref[...] = part

            @pl.when(i != 0)
            def _(c_ref=c_ref, part=part):
                c_ref[...] += part

    r8, last8 = rt // 8, nrows // 8 - 1

    def halo(off):
        return [pl.BlockSpec((8, tc), lambda j, i: (jnp.maximum(i * r8 - 1, 0), j + off)),
                pl.BlockSpec((rt, tc), lambda j, i: (i, j + off)),
                pl.BlockSpec((8, tc), lambda j, i: (jnp.minimum((i + 1) * r8, last8), j + off))]
    tile = pl.BlockSpec((rt, tc), lambda j, i: (i, j))
    acc = pl.BlockSpec((8, tc), lambda j, i: (0, j))
    dpg, dpv, cg, cv = pl.pallas_call(
        body, name=name,
        out_shape=(jax.ShapeDtypeStruct((nrows, ff), BF16), jax.ShapeDtypeStruct((nrows, ff), BF16),
                   jax.ShapeDtypeStruct((8, ff), F32), jax.ShapeDtypeStruct((8, ff), F32)),
        grid=(nb, ntiles),
        in_specs=halo(0) + halo(nb) + halo(0) + [acc, pl.BlockSpec((8, tc), lambda j, i: (0, j + nb))],
        out_specs=(tile, tile, acc, acc),
        compiler_params=_cparams(("parallel", "arbitrary")),
    )(p, p, p, p, p, p, da, da, da, w8, w8)
    return dpg, dpv, jnp.concatenate([cg, cv], axis=1)


def _w8(w3, b=None):
    c = w3.shape[1]
    brow = jnp.zeros((1, c), F32) if b is None else b.reshape(1, c)
    return jnp.concatenate([w3, brow, jnp.zeros((4, c), F32)], axis=0)


def _rms(x, w):
    r = lax.rsqrt(jnp.mean(x * x, axis=-1, keepdims=True) + EPS)
    xh = x * r
    return xh * w, xh, r


def _rms_bwd(dy, xh, r, w):
    dxh = dy * w
    dx = r * (dxh - xh * jnp.mean(dxh * xh, axis=-1, keepdims=True))
    return dx, jnp.sum(dy * xh, axis=0, keepdims=True)


def _mod_bwd(dh, x, w, shift, scale):
    n, xh, r = _rms(x, w)
    dshift = jnp.sum(dh, axis=0, keepdims=True)
    dscale = jnp.sum(dh * n, axis=0, keepdims=True)
    dx, dw = _rms_bwd(dh * (1.0 + scale), xh, r, w)
    return dx, dw, dshift, dscale


def _sigmoid(x):
    return 1.0 / (1.0 + jnp.exp(-x))


def _silu(x):
    return x * _sigmoid(x)


def _dsilu(x):
    s = _sigmoid(x)
    return s * (1.0 + x * (1.0 - s))


def _rope(x, cos, sin_s):
    return x * cos + pltpu.roll(x, LANE // 2, 1) * sin_s


def _rope_t(dy, cos, sin_s):
    return dy * cos + pltpu.roll(dy * sin_s, LANE // 2, 1)


def _flash_fwd(q, kcat, kv, d, rides=None):
    riders = rides.take("flash_fwd") if rides is not None else []
    nr = len(riders)
    r_in, r_shapes, r_out, r_sems = _ride_plumbing(riders)
    nt, h, tq = d.NT, d.H, d.tq
    tkb = _tile(d.T, 4096)
    n_big = d.T // tkb
    nq_lat = d.T // tq

    def body(*refs):
        q_ref, k_ref, v_ref = refs[:3]
        o_ref, ob_ref, lse_ref = refs[3 + nr:6 + nr]
        qi = pl.program_id(1)
        if nr:
            start, wait = _ride_copies([st for _, _, st in riders], refs[3:3 + nr], refs[6 + nr:6 + 2 * nr], *refs[6 + 2 * nr:])
            pl.when(jnp.logical_and(pl.program_id(0) == 0, qi == 0))(start)
        q_t = q_ref[...]

        def step(k0, tk, carry):
            m, l, acc = carry
            ks = pl.ds(k0, tk)
            s = _dot(q_t, k_ref[ks, :], 'nt')
            m_new = jnp.maximum(m, jnp.max(s, axis=1, keepdims=True))
            p = jnp.exp2(s - m_new)
            alpha = jnp.exp2(m - m_new)
            return m_new, alpha * l + jnp.sum(p, axis=1, keepdims=True), alpha * acc + _dot(p, v_ref[ks, :], 'nn')

        init = (jnp.full((tq, 1), -1e30, F32), jnp.zeros((tq, 1), F32), jnp.zeros((tq, LANE), F32))
        trips = jnp.where(qi < nq_lat, n_big, 0)
        carry = lax.fori_loop(0, trips, lambda t, c: step(pl.multiple_of(t * tkb, tkb), tkb, c), init)
        m, l, acc = step(d.T, d.TC, carry)
        o = acc / l
        o_ref[...] = o
        ob_ref[...] = o.astype(BF16)
        lse_ref[...] = jnp.broadcast_to(m + jnp.log2(l), (tq, LANE))
        if nr:
            pl.when(jnp.logical_and(pl.program_id(0) == h - 1, qi == nt // tq - 1))(wait)

    out = pl.BlockSpec((tq, LANE), lambda hh, i: (i, hh))
    res = pl.pallas_call(
        body, name="flash_fwd",
        out_shape=[jax.ShapeDtypeStruct((nt, d.HQ), F32), jax.ShapeDtypeStruct((nt, d.HQ), BF16),
                   jax.ShapeDtypeStruct((nt, d.HQ), F32)] + r_shapes,
        grid=(h, nt // tq),
        in_specs=[pl.BlockSpec((tq, 2 * LANE), lambda hh, i: (i, hh)), pl.BlockSpec((nt, 2 * LANE), lambda hh, i: (0, hh)),
                  pl.BlockSpec((nt, LANE), lambda hh, i: (0, h + hh))] + r_in,
        out_specs=[out, out, out] + r_out,
        scratch_shapes=r_sems,
        compiler_params=_cparams(("arbitrary", "arbitrary") if nr else ("parallel", "parallel")),
    )(q, kcat, kv, *[x for _, x, _ in riders])
    for (key, _, _), arr in zip(riders, res[3:]):
        rides.done[key] = arr
    return res[0], res[1], res[2]


def _flash_bwd(q, kcat, kv, do, ld, d, rides=None):
    riders = rides.take("flash_bwd") if rides is not None else []
    nr = len(riders)
    r_in, r_shapes, r_out, r_sems = _ride_plumbing(riders)
    nt, h, tk = d.NT, d.H, d.tq
    tqb =_tile(d.T, 2048)
    n_big = d.T // tqb
    nk_lat = d.T // tk

    def body(*refs):
        q_ref, do_ref, ld_ref, k_ref, v_ref = refs[:5]
        dq_ref, dk_ref, dv_ref = refs[5 + nr:8 + nr]
        kt = pl.program_id(1)
        if nr:
            start, wait = _ride_copies([st for _, _, st in riders], refs[5:5 + nr], refs[8 + nr:8 + 2 * nr], *refs[8 + 2 * nr:])
            pl.when(jnp.logical_and(pl.program_id(0) == 0, kt == 0))(start)
        k_t, v_t = k_ref[...], v_ref[...]

        @pl.when(kt == 0)
        def _():
            dq_ref[...] = jnp.zeros_like(dq_ref)

        def step(q0, tq, carry):
            dk, dv = carry
            qs = pl.ds(q0, tq)
            q_t, do_t, ld_t = q_ref[qs, :], do_ref[qs, :], ld_ref[qs, :]
            p = jnp.exp2(_dot(q_t, k_t, 'nt') - ld_t[:, 0:1])
            ds = p * (_dot(do_t, v_t, 'nt') - ld_t[:, LANE // 2:LANE // 2 + 1])
            dq_ref[qs, :] += _dot(ds, k_t, 'nn')
            return dk + _dot(ds, q_t, 'tn'), dv + _dot(p, do_t, 'tn')

        init = (jnp.zeros((tk, 2 * LANE), F32), jnp.zeros((tk, LANE), F32))
        carry = lax.fori_loop(0, n_big, lambda t, c: step(pl.multiple_of(t * tqb, tqb), tqb, c), init)
        dk_ref[...] = carry[0] * LN2
        dv_ref[...] = carry[1].astype(BF16)

        @pl.when(kt >= nk_lat)
        def _():
            dk, dv = step(d.T, d.TC, carry)
            dk_ref[...] = dk * LN2
            dv_ref[...] = dv.astype(BF16)

        if nr:
            pl.when(jnp.logical_and(pl.program_id(0) == h - 1, kt == nt // tk - 1))(wait)

    res = lambda w: pl.BlockSpec((nt, w), lambda hh, i: (0, hh))
    outs = pl.pallas_call(
        body, name="flash_bwd",
        out_shape=[jax.ShapeDtypeStruct((nt, 2 * d.HQ), F32), jax.ShapeDtypeStruct((nt, 2 * d.HQ), F32),
                   jax.ShapeDtypeStruct((nt, d.HQ), BF16)] + r_shapes,
        grid=(h, nt // tk),
        in_specs=[res(2 * LANE), res(LANE), res(LANE), pl.BlockSpec((tk, 2 * LANE), lambda hh, i: (i, hh)),
                  pl.BlockSpec((tk, LANE), lambda hh, i: (i, h + hh))] + r_in,
        out_specs=[res(2 * LANE), pl.BlockSpec((tk, 2 * LANE), lambda hh, i: (i, hh)),
                   pl.BlockSpec((tk, LANE), lambda hh, i: (i, hh))] + r_out,
        scratch_shapes=r_sems,
        compiler_params=_cparams(("arbitrary", "arbitrary") if nr else ("parallel", "arbitrary")),
    )(q, do, ld, kcat, kv, *[x for _, x, _ in riders])
    for (key, _, _), arr in zip(riders, outs[3:]):
        rides.done[key] = arr
    return outs[0], outs[1], outs[2]


def _tri(dirn):
    r = lax.broadcasted_iota(jnp.int32, (CHUNK, CHUNK), 0)
    c = lax.broadcasted_iota(jnp.int32, (CHUNK, CHUNK), 1)
    return (c <= r) if dirn == 0 else (c >= r)


def _exact_mask_dot(mask_bf16, x):
    hi = x.astype(BF16)
    r1 = x - hi.astype(F32)
    mid = r1.astype(BF16)
    lo = (r1 - mid.astype(F32)).astype(BF16)
    dot = lambda t: lax.dot_general(mask_bf16, t, (((1,), (0,)), ((), ())), preferred_element_type=F32)
    return dot(hi) + dot(mid) + dot(lo)


def _gla_terms(q, k, g, dirn, dk):
    mb = _tri(dirn)
    b = _exact_mask_dot(mb.astype(BF16), g)
    tot = jnp.sum(g, axis=0, keepdims=True)
    qe = q * (float(dk) ** -0.5) * jnp.exp(b)
    ke = k * jnp.exp(-b)
    kd = k * jnp.exp(tot - b)
    att = jnp.where(mb, _dot(qe, ke, 'nt'), 0.0)
    return mb, b, tot, qe, ke, kd, att


def _gla_block_index(d, dirn):
    nb = d.NT // d.rt
    if dirn == 0:
        return lambda s: (s + d.nlt) % nb
    return lambda s: nb - 1 - s


def _gla_rows(j, nsub, dirn):
    r0 = (j if dirn == 0 else nsub - 1 - j) * CHUNK
    return slice(r0, r0 + CHUNK)


def _gla_fwd(z, g, d):
    nt, gh, dk, dv, rb = d.NT, d.GH, d.dk, d.dv, d.rt
    nb, nsub = nt // rb, rb // CHUNK
    qb, kb = d.z1_q // dk, d.z1_k // dk

    def body(*refs):
        @pl.when(pl.program_id(1) == 0)
        def _():
            for dirn in (0, 1):
                refs[12 + dirn][...] = jnp.zeros_like(refs[12 + dirn])

        for dirn in (0, 1):
            q_ref, k_ref, v_ref, g_ref = refs[4 * dirn:4 * dirn + 4]
            o_ref, st_ref, state = refs[8 + 2 * dirn], refs[9 + 2 * dirn], refs[12 + dirn]
            st = state[...]
            for j in range(nsub):
                rs = _gla_rows(j, nsub, dirn)
                st_ref[j] = st
                v = v_ref[rs, :]
                _, _, tot, qe, _, kd, att = _gla_terms(q_ref[rs, :], k_ref[rs, :], g_ref[rs, :], dirn, dk)
                o_ref[rs, :] = _dot(att, v, 'nn') + _dot(qe, st, 'nt')
                st = st * jnp.exp(tot) + _dot(v, kd, 'tn')
            state[...] = st

    def col(dirn, w, off):
        bidx = _gla_block_index(d, dirn)
        return pl.BlockSpec((rb, w), lambda hh, s: (bidx(s), off + hh))

    in_specs, out_shapes, out_specs = [], [], []
    for dirn in (0, 1):
        in_specs += [col(dirn, dk, qb), col(dirn, dk, kb), col(dirn, dv, 0), col(dirn, dk, dirn * gh)]
        out_shapes += [jax.ShapeDtypeStruct((nt, d.VAL), F32), jax.ShapeDtypeStruct((gh, nb * nsub, dv, dk), F32)]
        out_specs += [col(dirn, dv, 0), pl.BlockSpec((None, nsub, dv, dk), lambda hh, s: (hh, s, 0, 0))]
    return pl.pallas_call(
        body, name="gla_fwd", out_shape=out_shapes, grid=(gh, nb), in_specs=in_specs, out_specs=out_specs,
        scratch_shapes=[pltpu.VMEM((dv, dk), F32), pltpu.VMEM((dv, dk), F32)],
        compiler_params=_cparams(("parallel", "arbitrary")),
    )(z, z, z, g, z, z, z, g)


def _gla_bwd(z, g, do, states_fw, states_bw, d, rides=None):
    riders = rides.take("gla_bwd") if rides is not None else []
    nr = len(riders)
    r_in, r_shapes, r_out, r_sems = _ride_plumbing(riders)
    nt, gh, dk, dv, rb = d.NT, d.GH, d.dk, d.dv, d.rt
    nb, nsub = nt // rb, rb // CHUNK
    qb, kb = d.z1_q // dk, d.z1_k // dk
    qscale = float(dk) ** -0.5
    n_in, n_out = 12, 8

    def body(*refs):
        outs = refs[n_in + nr:n_in + nr + n_out]
        dstates = refs[n_in + n_out + 2 * nr:n_in + n_out + 2 * nr + 2]
        if nr:
            start, wait = _ride_copies([st for _, _, st in riders], refs[n_in:n_in + nr],
                                       refs[n_in + nr + n_out:n_in + n_out + 2 * nr], *refs[n_in + n_out + 2 * nr + 2:])
            pl.when(jnp.logical_and(pl.program_id(0) == 0, pl.program_id(1) == 0))(start)

        @pl.when(pl.program_id(1) == 0)
        def _():
            for dstate in dstates:
                dstate[...] = jnp.zeros_like(dstate)

        for dirn in (0, 1):
            q_ref, k_ref, v_ref, g_ref, do_ref, st_ref = refs[6 * dirn:6 * dirn + 6]
            dq_ref, dk_ref, dv_ref, dg_ref = outs[4 * dirn:4 * dirn + 4]
            dst = dstates[dirn][...]
            for j in reversed(range(nsub)):
                rs = _gla_rows(j, nsub, dirn)
                q, k, v, g_, dout, st = q_ref[rs, :], k_ref[rs, :], v_ref[rs, :], g_ref[rs, :], do_ref[rs, :], st_ref[j]
                mb, b, tot, qe, ke, kd, att = _gla_terms(q, k, g_, dirn, dk)
                etot = jnp.exp(tot)
                datt = jnp.where(mb, _dot(dout, v, 'nt'), 0.0)
                dv_ref[rs, :] = _dot(att, dout, 'tn') + _dot(kd, dst, 'nt')
                dqe = _dot(datt, ke, 'nn') + _dot(dout, st, 'nn')
                dke = _dot(datt, qe, 'tn')
                dkd = _dot(v, dst, 'nn')
                dq_ref[rs, :] = dqe * (qscale * jnp.exp(b))
                dk_ref[rs, :] = dke * jnp.exp(-b) + dkd * jnp.exp(tot - b)
                dkd_kd = dkd * kd
                db = dqe * qe - dke * ke - dkd_kd
                dtot = jnp.sum(dkd_kd, axis=0, keepdims=True) + jnp.sum(dst * st, axis=0, keepdims=True) * etot
                dg_ref[rs, :] = _exact_mask_dot(_tri(1 - dirn).astype(BF16), db) + dtot
                dst = dst * etot + _dot(dout, qe, 'tn')
            dstates[dirn][...] = dst
        if nr:
            pl.when(jnp.logical_and(pl.program_id(0) == gh - 1, pl.program_id(1) == nb - 1))(wait)

    def col(dirn, w, off):
        bfwd = _gla_block_index(d, dirn)
        return pl.BlockSpec((rb, w), lambda hh, s: (bfwd(nb - 1 - s), off + hh))

    in_specs, out_shapes, out_specs = [], [], []
    for dirn in (0, 1):
        in_specs += [col(dirn, dk, qb), col(dirn, dk, kb), col(dirn, dv, 0), col(dirn, dk, dirn * gh), col(dirn, dv, 0),
                     pl.BlockSpec((None, nsub, dv, dk), lambda hh, s: (hh, nb - 1 - s, 0, 0))]
        out_shapes += [jax.ShapeDtypeStruct((nt, d.KEY), F32), jax.ShapeDtypeStruct((nt, d.KEY), F32),
                       jax.ShapeDtypeStruct((nt, d.VAL), F32), jax.ShapeDtypeStruct((nt, d.KEY), F32)]
        out_specs += [col(dirn, dk, 0), col(dirn, dk, 0), col(dirn, dv, 0), col(dirn, dk, 0)]
    res = pl.pallas_call(
        body, name="gla_bwd", out_shape=out_shapes + r_shapes, grid=(gh, nb), in_specs=in_specs + r_in,
        out_specs=out_specs + r_out,
        scratch_shapes=[pltpu.VMEM((dv, dk), F32), pltpu.VMEM((dv, dk), F32)] + r_sems,
        compiler_params=_cparams(("arbitrary", "arbitrary") if nr else ("parallel", "arbitrary")),
    )(z, z, z, g, do, states_fw, z, z, z, g, do, states_bw, *[x for _, x, _ in riders])
    for (key, _, _), arr in zip(riders, res[n_out:]):
        rides.done[key] = arr
    return res[0:4], res[4:8]


def _rope_pad(w):
    q = QK_ROPE // 4
    a1, a2, b1, b2 = (w[..., k * q:(k + 1) * q] for k in range(4))
    z = jnp.zeros(w.shape[:-1] + (LANE // 2 - 2 * q,), w.dtype)
    return jnp.concatenate([a1, b1, z, a2, b2, z], axis=-1)


def _rope_unpad(g):
    q = QK_ROPE // 4
    h = LANE // 2
    return jnp.concatenate([g[..., 0:q], g[..., h:h + q], g[..., q:2 * q], g[..., h + q:h + 2 * q]], axis=-1)


def _win0_to_kernel(w, d):
    kv_lat = w[:, :d.KVL]
    k_rope = w[:, d.KVL:d.KVL + QK_ROPE]
    q_lat = w[:, d.KVL + QK_ROPE:d.KVL + QK_ROPE + d.QL]
    rest = w[:, d.KVL + QK_ROPE + d.QL:]
    parts = [q_lat, kv_lat, _rope_pad(k_rope)]
    if d.z0_pad:
        parts.append(jnp.zeros((w.shape[0], d.z0_pad), w.dtype))
    return jnp.concatenate(parts + [rest], axis=1)


def _win0_from_kernel(g, d):
    return jnp.concatenate([g[:, d.z0_kv:d.z0_kv + d.KVL], _rope_unpad(g[:, d.z0_kr:d.z0_kr + LANE]), g[:, :d.QL],
                            g[:, d.z0_ax:]], axis=1)


def _wqb_to_kernel(w, d):
    wr = w.reshape(d.QL, d.H, QK_NOPE + QK_ROPE)
    return jnp.concatenate([wr[:, :, :QK_NOPE], _rope_pad(wr[:, :, QK_NOPE:])], axis=2).reshape(d.QL, 2 * d.HQ)


def _wqb_from_kernel(g, d):
    gr = g.reshape(d.QL, d.H, QK_NOPE + LANE)
    return jnp.concatenate([gr[:, :, :QK_NOPE], _rope_unpad(gr[:, :, QK_NOPE:])], axis=2).reshape(d.QL, d.H * (QK_NOPE + QK_ROPE))


def _wkvb_to_kernel(w, d):
    return w.reshape(d.KVL, d.H, 2, LANE).transpose(0, 2, 1, 3).reshape(d.KVL, 2 * d.HQ)


def _wkvb_from_kernel(g, d):
    return g.reshape(d.KVL, 2, d.H, LANE).transpose(0, 2, 1, 3).reshape(d.KVL, 2 * d.HQ)


def _win1_to_kernel(w, d):
    k = w[:, :d.KEY]
    v = w[:, d.KEY:d.KEY + d.VAL]
    lr = w[:, d.KEY + d.VAL:d.KEY + d.VAL + 2 * GATE_RANK]
    q = w[:, d.KEY + d.VAL + 2 * GATE_RANK:2 * d.KEY + d.VAL + 2 * GATE_RANK]
    og = w[:, 2 * d.KEY + d.VAL + 2 * GATE_RANK:]
    return jnp.concatenate([v, og, k, q, lr, jnp.zeros((w.shape[0], LANE - 2 * GATE_RANK), w.dtype)], axis=1)


def _win1_from_kernel(g, d):
    return jnp.concatenate([g[:, d.z1_k:d.z1_k + d.KEY], g[:, :d.VAL], g[:, d.z1_lr:d.z1_lr + 2 * GATE_RANK],
                            g[:, d.z1_q:d.z1_q + d.KEY], g[:, d.z1_og:d.z1_og + d.VAL]], axis=1)


def _gate_weight(fw_w, bw_w, d):
    z = jnp.zeros((GATE_RANK, d.KEY), F32)
    return jnp.concatenate([jnp.concatenate([fw_w, z], axis=1), jnp.concatenate([z, bw_w], axis=1),
                            jnp.zeros((LANE - 2 * GATE_RANK, 2 * d.KEY), F32)], axis=0)


def _rope_tables(d):
    t = jnp.arange(d.T)
    inv = ROPE_THETA ** (-jnp.arange(0, QK_ROPE // 2, 2, dtype=F32) / (QK_ROPE // 2))
    ar = (t // GRID_W).astype(F32)[:, None] * inv
    ac = (t % GRID_W).astype(F32)[:, None] * inv
    z = jnp.zeros((d.T, LANE // 2 - 2 * inv.shape[0]), F32)
    ang = jnp.concatenate([ar, ac, z, ar, ac, z], axis=1)
    cos = jnp.concatenate([jnp.cos(ang), jnp.ones((d.TC, LANE), F32)], axis=0)
    sin = jnp.concatenate([jnp.sin(ang), jnp.zeros((d.TC, LANE), F32)], axis=0)
    sgn = jnp.where(jnp.arange(LANE) < LANE // 2, -1.0, 1.0).astype(F32)
    return cos, sin * sgn


def _row(arr, cb=0, width=None):
    return ('row', arr, cb, arr.shape[1] if width is None else width)


def _mod_fwd(x, nw, shift, scale, d, name):
    def body(i, j, x, w, sh, sc):
        return (_rms(x, w)[0] * (1.0 + sc) + sh,)
    return _rowwise(body, d.NT, d.rt, d.nlt, [_row(x), ('full', nw), ('seg', shift), ('seg', scale)],
                    [('row', d.D, BF16)], name)[0]


def _mod_bwd_call(dres, dh, x, nw, shift, scale, d, name):
    def body(i, j, dres, dh, x, w, sh, sc):
        dx, dw, dsh, dsc = _mod_bwd(dh, x, w, sh, sc)
        return dres + dx, dw, dsh, dsc
    return _rowwise(body, d.NT, d.rt, d.nlt, [_row(dres), _row(dh), _row(x), ('full', nw), ('seg', shift), ('seg', scale)],
                    [('row', d.D, F32), ('acc', (1, d.D)), ('segacc', d.D), ('segacc', d.D)], name)


def _res_mod_fwd(x, y, gate, nw, shift, scale, d, name):
    def body(i, j, x, y, g, w, sh, sc):
        x1 = x + g * y
        return x1, _rms(x1, w)[0] * (1.0 + sc) + sh
    return _rowwise(body, d.NT, d.rt, d.nlt, [_row(x), _row(y), ('seg', gate), ('full', nw), ('seg', shift), ('seg', scale)],
                    [('row', d.D, F32), ('row', d.D, BF16)], name)


def _res_mod_bwd(dx2, dh2, x1, y, gate, nw, shift, scale, d, name):
    def body(i, j, dx2, dh2, x1, y, g, w, sh, sc):
        dx, dw, dsh, dsc = _mod_bwd(dh2, x1, w, sh, sc)
        dx1 = dx2 + dx
        return dx1, g * dx1, jnp.sum(dx1 * y, axis=0, keepdims=True), dw, dsh, dsc
    return _rowwise(body, d.NT, d.rt, d.nlt,
                    [_row(dx2), _row(dh2), _row(x1), _row(y), ('seg', gate), ('full', nw), ('seg', shift), ('seg', scale)],
                    [('row', d.D, F32), ('row', d.D, BF16), ('segacc', d.D), ('acc', (1, d.D)), ('segacc', d.D),
                     ('segacc', d.D)], name)


def _res_fwd(x1, f, gate, d, name):
    return _rowwise(lambda i, j, x1, f, g: (x1 + g * f,), d.NT, d.rt, d.nlt, [_row(x1), _row(f), ('seg', gate)],
                    [('row', d.D, F32)], name)[0]


def _res_bwd(dx2, f, gate, d, name):
    def body(i, j, dx2, f, g):
        return g * dx2, jnp.sum(dx2 * f, axis=0, keepdims=True)
    return _rowwise(body, d.NT, d.rt, d.nlt, [_row(dx2), _row(f), ('seg', gate)], [('row', d.D, BF16), ('segacc', d.D)], name)


def _w(w, key):
    if callable(w[key]):
        w[key] = w[key]()
    return w[key]


def _ffn_fwd(h2, w, d, tag, rides):
    p = _mm([(h2, _w(w, 'ffn_up'))], 'nn', F32, tag + '_up', rides=rides)
    a = _ffn_gate_fwd(p, _w8(w['ffn_conv_w'], w['ffn_conv_b']), d, tag + '_gate')
    f = _mm([(a, _w(w, 'ffn_down'))], 'nn', F32, tag + '_down', rides=rides)
    return p, a, f


def _ffn_bwd(df, h2, p, a, w, d, tag, rides, emit):
    w_up = w['ffn_up']
    da = _mm([(df, w['ffn_down'])], 'nt', F32, tag + '_down_dx', rides=rides)
    emit('ffn_down', _mm([(a, df)], 'tn', GRAD_WIRE, tag + '_down_dw'))
    dpg, dpv, conv_g = _ffn_gate_bwd(p, da, _w8(w['ffn_conv_w'], w['ffn_conv_b']), d, tag + '_gate_bwd')
    dh2 = _mm([(dpg, w_up[:, :d.FF]), (dpv, w_up[:, d.FF:])], 'nt', F32, tag + '_up_dx', rides=rides)
    emit('ffn_up', jnp.concatenate([_mm([(h2, dpg)], 'tn', GRAD_WIRE, tag + '_up_dw_gate', tn=GLU_COLS),
                                    _mm([(h2, dpv)], 'tn', GRAD_WIRE, tag + '_up_dw_val', tn=GLU_COLS)], axis=1))
    return dh2, conv_g[0:3], conv_g[3]


def _ab_fwd(z, w, cos, sin_s, d, rides):
    qnw, kvnw = w['q_norm'], w['kv_norm']

    def prep(i, j, zq, zkv, zkr, qw, kw, cos, sin_s):
        return _rms(zq, qw)[0], _rms(zkv, kw)[0], _rope(zkr, cos, sin_s)
    qn, kvn, kr = _rowwise(prep, d.NT, d.rt, d.nlt,
                           [_row(z, 0, d.QL), _row(z, d.z0_kv // d.KVL, d.KVL), _row(z, d.z0_kr // LANE, LANE),
                            ('full', qnw), ('full', kvnw), _row(cos), _row(sin_s)],
                           [('row', d.QL, BF16), ('row', d.KVL, BF16), ('row', LANE, BF16)], 'ab_prep')
    qraw = _mm([(qn, w['w_qb'])], 'nn', F32, 'ab_qb')
    kv = _mm([(kvn, w['w_kvb'])], 'nn', BF16, 'ab_kvb')

    def qrope(i, j, qraw, cos, sin_s):
        parts = []
        for h in range(d.H):
            parts += [qraw[:, 2 * h * LANE:(2 * h + 1) * LANE], _rope(qraw[:, (2 * h + 1) * LANE:(2 * h + 2) * LANE], cos, sin_s)]
        return (jnp.concatenate(parts, axis=1) * (ATTN_SCALE / LN2),)
    q = _rowwise(qrope, d.NT, d.rt, d.nlt, [_row(qraw), _row(cos), _row(sin_s)], [('row', 2 * d.HQ, BF16)], 'ab_qrope')[0]

    def kcat_body(i, j, kn, kr):
        parts = []
        for h in range(d.H):
            parts += [kn[:, h * LANE:(h + 1) * LANE], kr]
        return (jnp.concatenate(parts, axis=1),)
    kcat = _rowwise(kcat_body, d.NT, d.rt, d.nlt, [_row(kv, 0, d.HQ), _row(kr)], [('row', 2 * d.HQ, BF16)], 'ab_kcat')[0]
    o, ob, lse = _flash_fwd(q, kcat, kv, d, rides)
    ab = d.z0_ax // d.CC
    s = _rowwise(lambda i, j, ax, ac: (ax * ac,), d.NT, d.rt, d.nlt, [_row(z, ab, d.CC), _row(z, ab + 2, d.CC)],
                 [('row', d.CC, F32)], 'ab_conv_in')[0]
    cv = _dwconv(s, _w8(w['conv_a']), d, F32, 'ab_conv')
    ymix = _rowwise(lambda i, j, a_b, cv, ob: (jnp.concatenate([(a_b * cv).astype(BF16), ob], axis=1),), d.NT, d.rt, d.nlt,
                    [_row(z, ab + 1, d.CC), _row(cv), _row(ob)], [('row', d.CC + d.HQ, BF16)], 'ab_mix')[0]
    return ymix, dict(qn=qn, kvn=kvn, q=q, kcat=kcat, kv=kv, o=o, lse=lse, s=s, cv=cv)


def _ab_bwd(dymix, z, sv, w, cos, sin_s, d, rides, emit):
    qnw, kvnw = w['q_norm'], w['kv_norm']
    assert d.CC % d.HQ == 0

    def dprep(i, j, dmo, o, lse):
        lane = lax.broadcasted_iota(jnp.int32, (1, LANE), 1)
        cols = []
        for h in range(d.H):
            hs = slice(h * LANE, (h + 1) * LANE)
            delta = jnp.sum(dmo[:, hs] * o[:, hs], axis=1, keepdims=True)
            cols.append(jnp.where(lane < LANE // 2, lse[:, hs], delta))
        return dmo, jnp.concatenate(cols, axis=1)
    do, ld = _rowwise(dprep, d.NT, d.rt, d.nlt, [_row(dymix, d.CC // d.HQ, d.HQ), _row(sv['o']), _row(sv['lse'])],
                      [('row', d.HQ, BF16), ('row', d.HQ, F32)], 'ab_do')
    dq, dkc, dvv = _flash_bwd(sv['q'], sv['kcat'], sv['kv'], do, ld, d, rides)

    def qrope_t(i, j, dq, cos, sin_s):
        dq = dq * ATTN_SCALE
        parts = []
        for h in range(d.H):
            parts += [dq[:, 2 * h * LANE:(2 * h + 1) * LANE], _rope_t(dq[:, (2 * h + 1) * LANE:(2 * h + 2) * LANE], cos, sin_s)]
        return (jnp.concatenate(parts, axis=1),)
    dqraw = _rowwise(qrope_t, d.NT, d.rt, d.nlt, [_row(dq), _row(cos), _row(sin_s)],
                     [('row', 2 * d.HQ, BF16)], 'ab_qrope_bwd')[0]

    def dkv_body(i, j, dkc, dv):
        dkr = dkc[:, LANE:2 * LANE]
        for h in range(1, d.H):
            dkr = dkr + dkc[:, (2 * h + 1) * LANE:(2 * h + 2) * LANE]
        parts = [dkc[:, 2 * h * LANE:(2 * h + 1) * LANE] for h in range(d.H)] + [dv.astype(F32)]
        return jnp.concatenate(parts, axis=1), dkr
    dkv, dkr = _rowwise(dkv_body, d.NT, d.rt, d.nlt, [_row(dkc), _row(dvv)], [('row', 2 * d.HQ, BF16), ('row', LANE, F32)],
                        'ab_dkv')
    dqn = _mm([(dqraw, w['w_qb'])], 'nt', F32, 'ab_qb_dx')
    emit('w_qb', _mm([(sv['qn'], dqraw)], 'tn', GRAD_WIRE, 'ab_qb_dw'))
    dkvn = _mm([(dkv, w['w_kvb'])], 'nt', F32, 'ab_kvb_dx')
    emit('w_kvb', _mm([(sv['kvn'], dkv)], 'tn', GRAD_WIRE, 'ab_kvb_dw'))
    ab = d.z0_ax // d.CC
    dab, dcv = _rowwise(lambda i, j, dya, cv, a_b: (dya * cv, dya * a_b), d.NT, d.rt, d.nlt,
                        [_row(dymix, 0, d.CC), _row(sv['cv']), _row(z, ab + 1, d.CC)],
                        [('row', d.CC, BF16), ('row', d.CC, F32)], 'ab_mix_bwd')
    ds = _dwconv(dcv, _w8(w['conv_a'][::-1]), d, F32, 'ab_conv_dx')
    conv_g = _dwconv_wgrad(sv['s'], dcv, d, 'ab_conv_dw')

    def assemble(i, j, dqn, dkvn, dkr, zq, zkv, qw, kw, cos, sin_s, ds, ax, ac, dab):
        _, xq, rq = _rms(zq, qw)
        dzq, dqw = _rms_bwd(dqn, xq, rq, qw)
        _, xk, rk = _rms(zkv, kw)
        dzkv, dkw = _rms_bwd(dkvn, xk, rk, kw)
        parts = [dzq, dzkv, _rope_t(dkr, cos, sin_s)]
        if d.z0_pad:
            parts.append(jnp.zeros((dzq.shape[0], d.z0_pad), F32))
        parts += [ds * ac, dab.astype(F32), ds * ax]
        return jnp.concatenate([t.astype(BF16) for t in parts], axis=1), dqw, dkw
    dz, dqw, dkw = _rowwise(assemble, d.NT, d.rt, d.nlt,
                            [_row(dqn), _row(dkvn), _row(dkr), _row(z, 0, d.QL), _row(z, d.z0_kv // d.KVL, d.KVL),
                             ('full', qnw), ('full', kvnw), _row(cos), _row(sin_s), _row(ds), _row(z, ab, d.CC),
                             _row(z, ab + 2, d.CC), _row(dab)],
                            [('row', d.ZW0, BF16), ('acc', (1, d.QL)), ('acc', (1, d.KVL))], 'ab_dz')
    return dz, dict(conv_a=conv_g[0:3], q_norm=dqw, kv_norm=dkw)


def _log_sigmoid(x):
    return jnp.minimum(x, 0.0) - jnp.log(1.0 + jnp.exp(-jnp.abs(x)))


def _gla_fwd_block(z, w, d):
    wg, bg, onw = w['gate_w'], w['gate_b'], w['o_norm']

    def gates(i, j, lr, wg, bg):
        return (_log_sigmoid(_dot(lr, wg, 'nn') + bg) / GATE_NORMALIZER,)
    g = _rowwise(gates, d.NT, d.rt, d.nlt, [_row(z, d.z1_lr // LANE, LANE), ('full', wg), ('full', bg)],
                 [('row', 2 * d.KEY, F32)], 'gla_gates')[0]
    of, stf, ob, stb = _gla_fwd(z, g, d)

    def outp(i, j, of, ob, og, ow):
        o = of + ob
        parts = [_rms(o[:, h * d.dv:(h + 1) * d.dv], ow)[0] for h in range(d.GH)]
        return (jnp.concatenate(parts, axis=1) * _silu(og),)
    ymix = _rowwise(outp, d.NT, d.rt, d.nlt, [_row(of), _row(ob), _row(z, d.z1_og // d.VAL, d.VAL), ('full', onw)],
                    [('row', d.VAL, BF16)], 'gla_out')[0]
    return ymix, dict(g=g, of=of, ob=ob, stf=stf, stb=stb)


def _gla_bwd_block(dymix, z, sv, w, d, rides):
    wg, bg, onw = w['gate_w'], w['gate_b'], w['o_norm']

    def outp_bwd(i, j, dy, of, ob, og, ow):
        o = of + ob
        dn = dy * _silu(og)
        dos, ns = [], []
        dow = jnp.zeros((1, d.dv), F32)
        for h in range(d.GH):
            hs = slice(h * d.dv, (h + 1) * d.dv)
            n, xh, r = _rms(o[:, hs], ow)
            do_h, dw_h = _rms_bwd(dn[:, hs], xh, r, ow)
            dos.append(do_h)
            ns.append(n)
            dow = dow + dw_h
        return jnp.concatenate(dos, axis=1), dy * jnp.concatenate(ns, axis=1) * _dsilu(og), dow
    do, dog, dow = _rowwise(outp_bwd, d.NT, d.rt, d.nlt,
                            [_row(dymix), _row(sv['of']), _row(sv['ob']), _row(z, d.z1_og // d.VAL, d.VAL), ('full', onw)],
                            [('row', d.VAL, F32), ('row', d.VAL, BF16), ('acc', (1, d.dv))], 'gla_out_bwd')
    (dq0, dk0, dv0, dg0), (dq1, dk1, dv1, dg1) = _gla_bwd(z, sv['g'], do, sv['stf'], sv['stb'], d, rides)

    def assemble(i, j, dg0, dg1, lr, wg, bg, dq0, dq1, dk0, dk1, dv0, dv1, dog):
        pre = _dot(lr, wg, 'nn') + bg
        e = jnp.exp(-jnp.abs(pre))
        dpre = jnp.concatenate([dg0, dg1], axis=1) * jnp.where(pre >= 0, e, 1.0) / (1.0 + e) / GATE_NORMALIZER
        dlr = _dot(dpre, wg, 'nt')
        parts = [dv0 + dv1, dog.astype(F32), dk0 + dk1, dq0 + dq1, dlr]
        return (jnp.concatenate([t.astype(BF16) for t in parts], axis=1), _dot(lr, dpre, 'tn'),
                jnp.sum(dpre, axis=0, keepdims=True))
    dz, dwg, dbg = _rowwise(assemble, d.NT, d.rt, d.nlt,
                            [_row(dg0), _row(dg1), _row(z, d.z1_lr // LANE, LANE), ('full', wg), ('full', bg), _row(dq0),
                             _row(dq1), _row(dk0), _row(dk1), _row(dv0), _row(dv1), _row(dog)],
                            [('row', d.ZW1, BF16), ('acc', (LANE, 2 * d.KEY)), ('acc', (1, 2 * d.KEY))], 'gla_dz')
    return dz, dict(gate_fw_w=dwg[:GATE_RANK, :d.KEY], gate_bw_w=dwg[GATE_RANK:2 * GATE_RANK, d.KEY:],
                    gate_fw_b=dbg[:, :d.KEY], gate_bw_b=dbg[:, d.KEY:], o_norm=dow)


def _loss_bwd(x, fnw, target, d):
    def body(i, j, x, w, tgt):
        y, xh, r = _rms(x, w)
        e = y - tgt
        dx, dw = _rms_bwd(e * (1.0 / d.D), xh, r, w)
        lat = i < d.nlt
        part = jnp.sum(jnp.sum(e * e, axis=1, keepdims=True), axis=0, keepdims=True) * (0.5 / d.D)
        return (jnp.where(lat, dx, 0.0), jnp.where(lat, jnp.broadcast_to(part, (8, LANE)), 0.0), jnp.where(lat, dw, 0.0))
    return _rowwise(body, d.NT, d.rt, d.nlt, [_row(x), ('full', fnw), ('rowclamp', target, d.nlt - 1, d.D)],
                    [('row', d.D, F32), ('acc', (8, LANE)), ('acc', (1, d.D))], 'loss')


def _layer_fwd(x, mods, w, mixer_fwd, d, tag, rides):
    sh1, sc1, g1, sh2, sc2, g2 = mods
    h = _mod_fwd(x, w['norm1'], sh1, sc1, d, tag + '_mod1')
    z = _mm([(h, _w(w, 'w_in'))], 'nn', F32, tag + '_in', rides=rides)
    ymix, msv = mixer_fwd(z)
    y = _mm([(ymix, _w(w, 'w_out'))], 'nn', F32, tag + '_out')
    x1, h2 = _res_mod_fwd(x, y, g1, w['norm2'], sh2, sc2, d, tag + '_mod2')
    p, a, f = _ffn_fwd(h2, w, d, tag + '_ffn', rides)
    x2 = _res_fwd(x1, f, g2, d, tag + '_res')
    return x2, dict(x=x, h=h, z=z, ymix=ymix, msv=msv, y=y, x1=x1, h2=h2, p=p, a=a, f=f)


def _layer_bwd(dx2, sv, mods, w, mixer_bwd, d, tag, rides, emit):
    sh1, sc1, g1, sh2, sc2, g2 = mods
    df, dg2 = _res_bwd(dx2, sv['f'], g2, d, tag + '_res_bwd')
    dh2, dconv_w, dconv_b = _ffn_bwd(df, sv['h2'], sv['p'], sv['a'], w, d, tag + '_ffn', rides, emit)
    dx1, dy, dg1, dn2, dsh2, dsc2 = _res_mod_bwd(dx2, dh2, sv['x1'], sv['y'], g1, w['norm2'], sh2, sc2, d, tag + '_mod2_bwd')
    dymix = _mm([(dy, w['w_out'])], 'nt', F32, tag + '_out_dx')
    emit('w_out', _mm([(sv['ymix'], dy)], 'tn', GRAD_WIRE, tag + '_out_dw'))
    dz, mg = mixer_bwd(dymix, sv['z'], sv['msv'])
    emit('w_in', _mm([(sv['h'], dz)], 'tn', GRAD_WIRE, tag + '_in_dw', rides=rides))
    dh = _mm([(dz, w['w_in'])], 'nt', F32, tag + '_in_dx', rides=rides)
    dx, dn1, dsh1, dsc1 = _mod_bwd_call(dx1, dh, sv['x'], w['norm1'], sh1, sc1, d, tag + '_mod1_bwd')
    grads = dict(mg, ffn_conv_w=dconv_w, ffn_conv_b=dconv_b, norm1=dn1, norm2=dn2)
    return dx, grads, [dsh1, dsc1, dg1, dsh2, dsc2, dg2]


def _pad_flat(v, mult=LANE):
    v = v.reshape(-1)
    return jnp.pad(v, (0, (-v.shape[0]) % mult))


def _pack(entries, row_mult):
    flat, offs, pos = [], [], 0
    for v in entries:
        f = _pad_flat(v)
        flat.append(f)
        offs.append(pos)
        pos += f.shape[0]
    tot = jnp.concatenate(flat)
    tot = jnp.pad(tot, (0, (-pos) % (row_mult * LANE)))
    return tot.reshape(-1, LANE), offs


def _unpack(packed, offs, shapes):
    flat = packed.reshape(-1)
    out = []
    for off, shp in zip(offs, shapes):
        n = 1
        for s in shp:
            n *= s
        out.append(flat[off:off + n].reshape(shp))
    return out


def _step(a):
    d = _dims()
    dm = d.D
    me = 4 * lax.axis_index("x") + 2 * lax.axis_index("y") + lax.axis_index("c")
    sds = jax.ShapeDtypeStruct

    rides = _Rides()
    fwd_hosts = {'l0_w_out': 'l0_in', 'l0_ffn_up': 'flash_fwd', 'l0_ffn_down': 'flash_fwd', 'l1_w_in': 'flash_fwd',
                 'l1_w_out': 'l0_ffn_up', 'l1_ffn_up': 'l0_ffn_up', 'l1_ffn_down': 'l0_ffn_down'}

    def gathered(name):
        shard = _cast_bf16(a[name], 'cast_' + name)
        if name in fwd_hosts:
            rides.add(fwd_hosts[name], name, shard, False)
            return lambda: rides.done[name]
        g = _exchange(shard, False, 'ag_' + name)
        return lambda: g

    def cols(name, relayout=lambda t: t):
        g = gathered(name)
        k, n = a[name].shape
        return lambda: relayout(g().transpose(1, 0, 2).reshape(k, N_DEV * n))

    def rows(name):
        g = gathered(name)
        return lambda: g().reshape(N_DEV * a[name].shape[0], a[name].shape[1])

    small_names = ['l0_conv_a', 'l0_ffn_conv_w', 'l1_ffn_conv_w', 'l1_gate_fw_w', 'l1_gate_bw_w']
    spack, soffs = _pack([a[n] for n in small_names], 8)
    sg = _exchange(spack, False, 'ag_small')
    small_w = {}
    for n, off in zip(small_names, soffs):
        r, c = a[n].shape
        shards = sg.reshape(N_DEV, -1)[:, off:off + r * c].reshape(N_DEV, r, c)
        small_w[n] = shards.transpose(1, 0, 2).reshape(r, N_DEV * c)

    w0 = dict(norm1=a['l0_norm1'].reshape(1, dm), norm2=a['l0_norm2'].reshape(1, dm),
              w_in=cols('l0_w_in', lambda t: _win0_to_kernel(t, d)), w_qb=cols('l0_w_qb', lambda t: _wqb_to_kernel(t, d))(),
              w_kvb=cols('l0_w_kvb', lambda t: _wkvb_to_kernel(t, d))(), w_out=rows('l0_w_out'),
              q_norm=a['l0_q_norm'].reshape(1, -1), kv_norm=a['l0_kv_norm'].reshape(1, -1), conv_a=small_w['l0_conv_a'],
              ffn_up=cols('l0_ffn_up'), ffn_conv_w=small_w['l0_ffn_conv_w'], ffn_conv_b=a['l0_ffn_conv_b'],
              ffn_down=rows('l0_ffn_down'))
    w1 = dict(norm1=a['l1_norm1'].reshape(1, dm), norm2=a['l1_norm2'].reshape(1, dm),
              w_in=cols('l1_w_in', lambda t: _win1_to_kernel(t, d)), w_out=rows('l1_w_out'),
              gate_w=_gate_weight(small_w['l1_gate_fw_w'], small_w['l1_gate_bw_w'], d),
              gate_b=jnp.concatenate([a['l1_gate_fw_b'], a['l1_gate_bw_b']]).reshape(1, -1),
              o_norm=a['l1_o_norm'].reshape(1, -1),
              ffn_up=cols('l1_ffn_up'), ffn_conv_w=small_w['l1_ffn_conv_w'], ffn_conv_b=a['l1_ffn_conv_b'],
              ffn_down=rows('l1_ffn_down'))

    c8 = _exchange(a['c'], False, 'ag_c').reshape(N_DEV, dm)
    c16 = jnp.concatenate([c8, a['c_ctx'].reshape(1, dm), jnp.zeros((7, dm), F32)], axis=0)
    act16, dact16 = _small(lambda v: (_silu(v), _dsilu(v)), [c16], [sds((16, dm), BF16), sds((16, dm), F32)], 'cond_silu')
    n6 = N_MOD * dm // N_DEV
    mod_sh = [_mm([(act16, a[f'l{l}_ada_w'])], 'nn', F32, f'ada{l}') for l in (0, 1)]
    mod_all = _exchange(jnp.concatenate(mod_sh, axis=1), False, 'ag_mod')
    mods = []
    for l in (0, 1):
        full = mod_all[:, :, l * n6:(l + 1) * n6].transpose(1, 0, 2).reshape(16, N_MOD * dm)
        mine = jnp.concatenate([lax.dynamic_slice_in_dim(full, me, 1, 0), full[8:9]], axis=0)
        m2 = _small(lambda r, b: (r + b,), [mine, a[f'l{l}_ada_b'].reshape(1, -1)], [sds((2, N_MOD * dm), F32)], f'ada{l}_bias')[0]
        mods.append([m2[:, k * dm:(k + 1) * dm].reshape(2, 1, dm) for k in range(N_MOD)])

    cos, sin_s = _rope_tables(d)
    x0 = jnp.concatenate([a['x'][0], a['ctx'][0]], axis=0)
    x2, sv0 = _layer_fwd(x0, mods[0], w0, lambda z: _ab_fwd(z, w0, cos, sin_s, d, rides), d, 'l0', rides)
    x4, sv1 = _layer_fwd(x2, mods[1], w1, lambda z: _gla_fwd_block(z, w1, d), d, 'l1', rides)
    dx4, loss_acc, dfn = _loss_bwd(x4, a['final_norm'].reshape(1, dm), a['loss_target'][0], d)

    bwd_hosts = {'l1_ffn_down': 'l1_ffn_up_dx', 'l1_ffn_up': 'gla_bwd', 'l1_w_out': 'gla_bwd', 'l1_w_in': 'l1_in_dx',
                 'l0_ffn_down': 'l0_ffn_up_dx', 'l0_ffn_up': 'flash_bwd', 'l0_w_out': 'flash_bwd', 'l0_w_qb': 'l0_in_dw',
                 'l0_w_kvb': 'l0_in_dw', 'l0_w_in': 'l0_in_dx'}
    from_kernel = {'l0_w_in': _win0_from_kernel, 'l0_w_qb': _wqb_from_kernel, 'l0_w_kvb': _wkvb_from_kernel,
                   'l1_w_in': _win1_from_kernel}
    slabs = {}

    def emitter(layer):
        def emit(wkey, dw):
            name = f'l{layer}_{wkey}'
            if name in from_kernel:
                dw = from_kernel[name](dw, d)
            if a[name].shape[0] == dw.shape[0]:
                stacked = dw.reshape(dw.shape[0], N_DEV, dw.shape[1] // N_DEV).transpose(1, 0, 2)
            else:
                stacked = dw.reshape(N_DEV, dw.shape[0] // N_DEV, dw.shape[1])
            if name in bwd_hosts:
                rides.add(bwd_hosts[name], 'rs_' + name, stacked, True)
                slabs[name] = lambda: rides.done['rs_' + name]
            else:
                got = _exchange(stacked, True, 'rs_' + name)
                slabs[name] = lambda: got
        return emit

    dx2, g1, dmod1 = _layer_bwd(dx4, sv1, mods[1], w1, lambda dy, z, msv: _gla_bwd_block(dy, z, msv, w1, d, rides), d, 'l1',
                                rides, emitter(1))
    dx0, g0, dmod0 = _layer_bwd(dx2, sv0, mods[0], w0,
                                lambda dy, z, msv: _ab_bwd(dy, z, msv, w0, cos, sin_s, d, rides, emitter(0)), d, 'l0',
                                rides, emitter(0))

    dm_rows = jnp.concatenate([jnp.concatenate([t.reshape(2, dm) for t in dmod], axis=1) for dmod in (dmod0, dmod1)], axis=0)
    dm_all = _exchange(dm_rows, False, 'ag_dmod')
    lat = dm_all[:, 0::2].transpose(1, 0, 2)
    ctxs = dm_all[:, 1::2].transpose(1, 0, 2)

    def ada_prep(lat, ctxs):
        csum = jnp.sum(ctxs, axis=1, keepdims=True)
        row = lax.broadcasted_iota(jnp.int32, (1, 8, 1), 1)
        g16 = jnp.concatenate([lat, jnp.where(row == 0, csum, 0.0)], axis=1)
        return g16, jnp.sum(lat, axis=1, keepdims=True) + csum
    g16, gb = _small(ada_prep, [lat, ctxs], [sds((2, 16, N_MOD * dm), F32), sds((2, 1, N_MOD * dm), F32)], 'ada_bwd_prep')
    g16_sh = [lax.dynamic_slice_in_dim(g16[l], me * n6, n6, 1) for l in (0, 1)]
    grad_ada_w = [_mm([(act16, g16_sh[l])], 'tn', F32, f'ada{l}_dw') for l in (0, 1)]
    dact = _mm([(g16_sh[0], a['l0_ada_w']), (g16_sh[1], a['l1_ada_w'])], 'nt', F32, 'ada_dact')
    dcc = _small(lambda t, s: (t * s,), [dact[8:9], dact16[8:9]], [sds((1, dm), F32)], 'cctx_grad')[0]

    res = {}
    for name in slabs:
        res[name] = _adam(slabs[name](), a[name], a['m_' + name], a['v_' + name], 'adam_' + name)
    for l in (0, 1):
        name = f'l{l}_ada_w'
        res[name] = _adam(grad_ada_w[l][None], a[name], a['m_' + name], a['v_' + name], 'adam_' + name)

    part = {'loss': loss_acc[0:1, 0:1], 'c_ctx': dcc, 'final_norm': dfn,
            'l0_norm1': g0['norm1'], 'l0_norm2': g0['norm2'], 'l0_q_norm': g0['q_norm'], 'l0_kv_norm': g0['kv_norm'],
            'l0_conv_a': g0['conv_a'], 'l0_ffn_conv_w': g0['ffn_conv_w'], 'l0_ffn_conv_b': g0['ffn_conv_b'],
            'l1_norm1': g1['norm1'], 'l1_norm2': g1['norm2'], 'l1_o_norm': g1['o_norm'],
            'l1_gate_fw_w': g1['gate_fw_w'], 'l1_gate_bw_w': g1['gate_bw_w'], 'l1_gate_fw_b': g1['gate_fw_b'],
            'l1_gate_bw_b': g1['gate_bw_b'], 'l1_ffn_conv_w': g1['ffn_conv_w'], 'l1_ffn_conv_b': g1['ffn_conv_b']}
    pkeys = list(part)
    ppack, poffs = _pack([part[k] for k in pkeys], 8)
    psum = _sum_parts(_exchange(ppack, False, 'ag_small_grads'), 'sum_small_grads')
    tot = dict(zip(pkeys, _unpack(psum, poffs, [part[k].shape for k in pkeys])))
    loss = tot['loss'].reshape(())
    sgrad = {}
    for n in _WEIGHTS:
        if n in res:
            continue
        if n.endswith('ada_b'):
            sgrad[n] = gb[int(n[1])].reshape(a[n].shape)
        elif n in small_names:
            c = a[n].shape[1]
            sgrad[n] = lax.dynamic_slice_in_dim(tot[n], me * c, c, 1)
        else:
            sgrad[n] = tot[n].reshape(a[n].shape)
    snames = list(sgrad)
    packs = [_pack([src[n] for n in snames], 8)[0] for src in
             (sgrad, {n: a[n] for n in snames}, {n: a['m_' + n] for n in snames}, {n: a['v_' + n] for n in snames})]
    offs = _pack([sgrad[n] for n in snames], 8)[1]
    outs = _adam(packs[0][None], packs[1], packs[2], packs[3], 'adam_small')
    for k in range(4):
        for n, val in zip(snames, _unpack(outs[k], offs, [a[n].shape for n in snames])):
            res.setdefault(n, [None] * 4)[k] = val

    grad_x = dx0[:d.T].reshape(1, d.T, dm)
    return (loss, grad_x, *[res[n][0] for n in _WEIGHTS], *[res[n][1] for n in _WEIGHTS], *[res[n][2] for n in _WEIGHTS],
            *[res[n][3] for n in _WEIGHTS])


def kernel(*args):
    return _step(dict(zip(_ARGS, args, strict=True)))
```

```python
import functools
import types

import jax
import jax.numpy as jnp
from jax import lax
from jax.experimental import pallas as pl
from jax.experimental.pallas import tpu as pltpu

D_MODEL = 2048
SEQ = 8192
GRID_W = 64
CTX_LEN = 256
EPS = 1e-6
N_MOD = 6
MLA_HEADS = 8
QK_NOPE = 128
QK_ROPE = 64
V_HEAD = 128
Q_LORA = 512
KV_LORA = 256
ROPE_THETA = 10000.0
GLA_HEADS = 4
GATE_RANK = 16
GATE_NORMALIZER = 16.0
CHUNK = 64
D_FF = 5632
ADAM_LR = 0.001
ADAM_B1 = 0.9
ADAM_B2 = 0.999
ADAM_EPS = 1e-08
ADAM_WD = 0.01
ADAM_STEP = 10

N_DEV = 8
LANE = 128
VMEM_LIMIT = 56 * 1024 * 1024
CONV_COLS = 2816
GLU_COLS = 1408
LN2 = 0.6931471805599453
ATTN_SCALE = float(QK_NOPE + QK_ROPE) ** -0.5

F32 = jnp.float32
BF16 = jnp.bfloat16
GRAD_WIRE = jnp.bfloat16
MESH_ID = pl.DeviceIdType.MESH

_FWD = ['x', 'c', 'ctx', 'c_ctx', 'l0_ada_w', 'l0_ada_b', 'l0_norm1', 'l0_w_in', 'l0_conv_a', 'l0_q_norm', 'l0_w_qb',
        'l0_kv_norm', 'l0_w_kvb', 'l0_w_out', 'l0_norm2', 'l0_ffn_up', 'l0_ffn_conv_w', 'l0_ffn_conv_b', 'l0_ffn_down',
        'l1_ada_w', 'l1_ada_b', 'l1_norm1', 'l1_w_in', 'l1_gate_fw_w', 'l1_gate_fw_b', 'l1_gate_bw_w', 'l1_gate_bw_b',
        'l1_o_norm', 'l1_w_out', 'l1_norm2', 'l1_ffn_up', 'l1_ffn_conv_w', 'l1_ffn_conv_b', 'l1_ffn_down', 'final_norm']
_WEIGHTS = _FWD[3:]
_ARGS = _FWD + ['loss_target'] + ['m_' + n for n in _WEIGHTS] + ['v_' + n for n in _WEIGHTS]


def _dims():
    d = types.SimpleNamespace()
    d.D, d.T, d.TC = D_MODEL, SEQ, CTX_LEN
    d.NT = d.T + d.TC
    d.rt = 256 if d.TC % 256 == 0 else 128
    d.nlt = d.T // d.rt
    d.H = MLA_HEADS
    d.QL, d.KVL = Q_LORA, KV_LORA
    d.CC = D_MODEL // 2
    d.z0_kv = d.QL
    d.z0_kr = d.QL + d.KVL
    d.z0_pad = (-(d.QL + d.KVL + LANE)) % d.CC
    d.z0_ax = d.QL + d.KVL + LANE + d.z0_pad
    d.ZW0 = d.z0_ax + 3 * d.CC
    d.AB_COLS = d.KVL + QK_ROPE + d.QL + 3 * d.CC
    d.HQ = d.H * LANE
    d.GH = GLA_HEADS
    d.KEY = D_MODEL // 2
    d.VAL = D_MODEL
    d.dk = d.KEY // d.GH
    d.dv = d.VAL // d.GH
    d.z1_og = d.VAL
    d.z1_k = 2 * d.VAL
    d.z1_q = 2 * d.VAL + d.KEY
    d.z1_lr = 2 * d.VAL + 2 * d.KEY
    d.ZW1 = d.z1_lr + LANE
    d.GLA_COLS = 2 * d.KEY + 2 * d.VAL + 2 * GATE_RANK
    d.FF = D_FF
    d.tq = min(256, d.TC)
    return d


def _tile(n, pref, align=LANE):
    if n <= pref:
        return n
    t = (pref // align) * align
    while t >= align:
        if n % t == 0:
            return t
        t -= align
    raise ValueError(f"no tile for {n}")


def _cparams(sem):
    return pltpu.CompilerParams(dimension_semantics=sem, vmem_limit_bytes=VMEM_LIMIT)


def _dot(a, b, mode):
    dims = {'nn': (((1,), (0,)), ((), ())), 'nt': (((1,), (1,)), ((), ())), 'tn': (((0,), (0,)), ((), ()))}[mode]
    return lax.dot_general(a.astype(BF16), b.astype(BF16), dims, preferred_element_type=F32)


class _Rides:
    def __init__(self):
        self.pending, self.done = {}, {}

    def add(self, host, key, x, stacked):
        self.pending.setdefault(host, []).append((key, x, stacked))

    def take(self, host):
        return self.pending.pop(host, [])


def _ride_plumbing(riders):
    n = len(riders)
    if not n:
        return [], [], [], []
    spec = pl.BlockSpec(memory_space=pl.ANY)
    shapes = [jax.ShapeDtypeStruct((N_DEV,) + tuple(x.shape[1:] if st else x.shape), x.dtype) for _, x, st in riders]
    sems = [pltpu.SemaphoreType.DMA((n * (N_DEV - 1),)), pltpu.SemaphoreType.DMA((n * (N_DEV - 1),)),
            pltpu.SemaphoreType.DMA((n,))]
    return [spec] * n, shapes, [spec] * n, sems


def _ride_copies(stacked_flags, x_refs, o_refs, send_sems, recv_sems, local_sems):
    ix, iy, ic = lax.axis_index("x"), lax.axis_index("y"), lax.axis_index("c")
    me = 4 * ix + 2 * iy + ic
    local, sends, recvs = [], [], []
    for r, (stacked, x_ref, o_ref) in enumerate(zip(stacked_flags, x_refs, o_refs)):
        def src(p, x_ref=x_ref, stacked=stacked):
            return x_ref.at[p] if stacked else x_ref

        local.append(pltpu.make_async_copy(src(me), o_ref.at[me], local_sems.at[r]))
        for k in range(1, N_DEV):
            px, py, pc = (ix + ((k >> 2) & 1)) % 2, (iy + ((k >> 1) & 1)) % 2, (ic + (k & 1)) % 2
            peer = 4 * px + 2 * py + pc
            s = r * (N_DEV - 1) + k - 1
            sends.append(pltpu.make_async_remote_copy(
                src_ref=src(peer), dst_ref=o_ref.at[me], send_sem=send_sems.at[s], recv_sem=recv_sems.at[s],
                device_id=(px, py, pc), device_id_type=MESH_ID))
            recvs.append(pltpu.make_async_remote_copy(
                src_ref=src(peer), dst_ref=o_ref.at[peer], send_sem=send_sems.at[s], recv_sem=recv_sems.at[s],
                device_id=(px, py, pc), device_id_type=MESH_ID))

    def start():
        for cp in local + sends:
            cp.start()

    def wait():
        for cp in recvs:
            cp.wait_recv()
        for cp in sends:
            cp.wait_send()
        for cp in local:
            cp.wait()

    return start, wait


def _mm(pairs, mode, out_dtype, name, tm=768, tn=1024, tk=2816, rides=None):
    riders = rides.take(name) if rides is not None else []
    nr = len(riders)
    r_in, r_shapes, r_out, r_sems = _ride_plumbing(riders)
    a0, b0 = pairs[0]
    if mode == 'nn':
        (m, k), n = a0.shape, b0.shape[1]
    elif mode == 'nt':
        (m, k), n = a0.shape, b0.shape[0]
    else:
        (k, m), n = a0.shape, b0.shape[1]
    tm, tn, tk = _tile(m, tm), _tile(n, tn), _tile(k, tk)
    nk = k // tk
    if mode == 'nn':
        a_spec = pl.BlockSpec((tm, tk), lambda i, j, kk: (i, kk))
        b_spec = pl.BlockSpec((tk, tn), lambda i, j, kk: (kk, j))
    elif mode == 'nt':
        a_spec = pl.BlockSpec((tm, tk), lambda i, j, kk: (i, kk))
        b_spec = pl.BlockSpec((tn, tk), lambda i, j, kk: (j, kk))
    else:
        a_spec = pl.BlockSpec((tk, tm), lambda i, j, kk: (kk, i))
        b_spec = pl.BlockSpec((tk, tn), lambda i, j, kk: (kk, j))
    npairs = len(pairs)

    nin = 2 * npairs
    gi, gj = m // tm, n // tn

    def body(*refs):
        o_ref, acc_ref = refs[nin + nr], refs[nin + 2 * nr + 1]
        i, j, kk = pl.program_id(0), pl.program_id(1), pl.program_id(2)
        if nr:
            start, wait = _ride_copies([st for _, _, st in riders], refs[nin:nin + nr], refs[nin + nr + 1:nin + 2 * nr + 1],
                                       *refs[nin + 2 * nr + 2:])
            pl.when(jnp.logical_and(jnp.logical_and(i == 0, j == 0), kk == 0))(start)

        def dots():
            s = None
            for p in range(npairs):
                t = _dot(refs[2 * p][...], refs[2 * p + 1][...], mode)
                s = t if s is None else s + t
            return s

        if nk == 1:
            o_ref[...] = dots().astype(o_ref.dtype)
        else:
            @pl.when(kk == 0)
            def _():
                acc_ref[...] = dots()

            @pl.when(jnp.logical_and(kk > 0, kk < nk - 1))
            def _():
                acc_ref[...] += dots()

            @pl.when(kk == nk - 1)
            def _():
                o_ref[...] = (acc_ref[...] + dots()).astype(o_ref.dtype)

        if nr:
            pl.when(jnp.logical_and(jnp.logical_and(i == gi - 1, j == gj - 1), kk == nk - 1))(wait)

    flat = [t for ab in pairs for t in ab]
    res = pl.pallas_call(
        body, name=name,
        out_shape=[jax.ShapeDtypeStruct((m, n), out_dtype)] + r_shapes,
        grid=(gi, gj, nk),
        in_specs=[a_spec, b_spec] * npairs + r_in,
        out_specs=[pl.BlockSpec((tm, tn), lambda i, j, kk: (i, j))] + r_out,
        scratch_shapes=[pltpu.VMEM((tm, tn), F32)] + r_sems,
        compiler_params=_cparams(("arbitrary",) * 3 if nr else ("parallel", "parallel", "arbitrary")),
    )(*flat, *[x for _, x, _ in riders])
    for (key, _, _), arr in zip(riders, res[1:]):
        rides.done[key] = arr
    return res[0]


def _rowwise(body, nrows, rt, nlt, ins, outs, name, ncol=1):
    ntiles = nrows // rt
    in_specs, args = [], []
    for spec in ins:
        kind, arr = spec[0], spec[1]
        if kind == 'row':
            in_specs.append(pl.BlockSpec((rt, spec[3]), functools.partial(lambda i, j, cb: (i, cb), cb=spec[2])))
        elif kind == 'rowc':
            in_specs.append(pl.BlockSpec((rt, spec[3]), functools.partial(lambda i, j, cb: (i, cb + j), cb=spec[2])))
        elif kind == 'rowm':
            in_specs.append(pl.BlockSpec(
                (rt, spec[3]), functools.partial(lambda i, j, cb, md: (i, cb + j % md), cb=spec[2], md=spec[4])))
        elif kind == 'rowclamp':
            in_specs.append(pl.BlockSpec(
                (rt, spec[3]), functools.partial(lambda i, j, mb: (jnp.minimum(i, mb), 0), mb=spec[2])))
        elif kind == 'row3':
            in_specs.append(pl.BlockSpec((arr.shape[0], rt, arr.shape[2]), lambda i, j: (0, i, 0)))
        elif kind == 'seg':
            in_specs.append(pl.BlockSpec((None, 1, arr.shape[2]), lambda i, j: (i // nlt, 0, 0)))
        else:
            in_specs.append(pl.BlockSpec(arr.shape, functools.partial(lambda i, j, nd: (0,) * nd, nd=arr.ndim)))
        args.append(arr)
    out_shapes, out_specs = [], []
    for spec in outs:
        kind = spec[0]
        if kind == 'row':
            out_shapes.append(jax.ShapeDtypeStruct((nrows, spec[1]), spec[2]))
            out_specs.append(pl.BlockSpec((rt, spec[1]), lambda i, j: (i, 0)))
        elif kind == 'rowc':
            out_shapes.append(jax.ShapeDtypeStruct((nrows, spec[1]), spec[3]))
            out_specs.append(pl.BlockSpec((rt, spec[2]), lambda i, j: (i, j)))
        elif kind == 'acc':
            out_shapes.append(jax.ShapeDtypeStruct(spec[1], F32))
            out_specs.append(pl.BlockSpec(spec[1], functools.partial(lambda i, j, nd: (0,) * nd, nd=len(spec[1]))))
        else:
            out_shapes.append(jax.ShapeDtypeStruct((2, 1, spec[1]), F32))
            out_specs.append(pl.BlockSpec((None, 1, spec[1]), lambda i, j: (i // nlt, 0, 0)))
    n_in = len(ins)
    has_acc = any(s[0] in ('acc', 'segacc') for s in outs)

    def kern(*refs):
        i = pl.program_id(0)
        j = pl.program_id(1)
        vals = [r[...] for r in refs[:n_in]]
        res = body(i, j, *vals)
        for spec, ref, val in zip(outs, refs[n_in:], res):
            if spec[0] in ('row', 'rowc'):
                ref[...] = val.astype(ref.dtype)
            else:
                first = (i == 0) if spec[0] == 'acc' else jnp.logical_or(i == 0, i == nlt)

                @pl.when(first)
                def _(ref=ref, val=val):
                    ref[...] = val

                @pl.when(jnp.logical_not(first))
                def _(ref=ref, val=val):
                    ref[...] += val

    return pl.pallas_call(
        kern, name=name, out_shape=tuple(out_shapes), grid=(ntiles, ncol),
        in_specs=in_specs, out_specs=tuple(out_specs),
        compiler_params=_cparams(("arbitrary", "arbitrary") if has_acc else ("parallel", "parallel")),
    )(*args)


def _small(body, args, out_shapes, name):
    n_in = len(args)

    def kern(*refs):
        res = body(*[r[...] for r in refs[:n_in]])
        for ref, val in zip(refs[n_in:], res):
            ref[...] = val.astype(ref.dtype)

    return pl.pallas_call(
        kern, name=name, out_shape=tuple(out_shapes),
        in_specs=[pl.BlockSpec(memory_space=pltpu.VMEM)] * n_in,
        out_specs=tuple(pl.BlockSpec(memory_space=pltpu.VMEM) for _ in out_shapes),
        compiler_params=pltpu.CompilerParams(vmem_limit_bytes=VMEM_LIMIT),
    )(*args)


def _exchange(x, stacked, name):
    r_in, r_shapes, r_out, r_sems = _ride_plumbing([(name, x, stacked)])

    def body(x_ref, o_ref, send_sems, recv_sems, local_sems):
        start, wait = _ride_copies([stacked], [x_ref], [o_ref], send_sems, recv_sems, local_sems)
        start()
        wait()

    return pl.pallas_call(body, name=name, out_shape=r_shapes[0], in_specs=r_in, out_specs=r_out[0], scratch_shapes=r_sems)(x)


def _cast_bf16(x, name):
    r, c = x.shape
    tr = _tile(r, 256, 8)
    return pl.pallas_call(
        lambda x_ref, o_ref: o_ref.__setitem__(Ellipsis, x_ref[...].astype(BF16)), name=name,
        out_shape=jax.ShapeDtypeStruct((r, c), BF16), grid=(r // tr,),
        in_specs=[pl.BlockSpec((tr, c), lambda i: (i, 0))], out_specs=pl.BlockSpec((tr, c), lambda i: (i, 0)),
        compiler_params=_cparams(("parallel",)),
    )(x)


def _adam(parts, w, m, v, name):
    p, r, c = parts.shape
    tr = _tile(r, 64, 8)

    def body(p_ref, w_ref, m_ref, v_ref, g_ref, d_ref, nm_ref, nv_ref):
        g = p_ref[0].astype(F32)
        for q in range(1, p):
            g = g + p_ref[q].astype(F32)
        nm = ADAM_B1 * m_ref[...] + (1.0 - ADAM_B1) * g
        nv = ADAM_B2 * v_ref[...] + (1.0 - ADAM_B2) * (g * g)
        m_hat = nm / (1.0 - ADAM_B1 ** ADAM_STEP)
        v_hat = nv / (1.0 - ADAM_B2 ** ADAM_STEP)
        g_ref[...] = g
        d_ref[...] = -ADAM_LR * (m_hat / (jnp.sqrt(v_hat) + ADAM_EPS) + ADAM_WD * w_ref[...])
        nm_ref[...] = nm
        nv_ref[...] = nv

    spec = pl.BlockSpec((tr, c), lambda i: (i, 0))
    return pl.pallas_call(
        body, name=name, out_shape=tuple(jax.ShapeDtypeStruct((r, c), F32) for _ in range(4)), grid=(r // tr,),
        in_specs=[pl.BlockSpec((p, tr, c), lambda i: (0, i, 0)), spec, spec, spec], out_specs=(spec,) * 4,
        compiler_params=_cparams(("parallel",)),
    )(parts, w, m, v)


def _sum_parts(parts, name):
    p, r, c = parts.shape
    tr = _tile(r, 256, 8)

    def body(p_ref, o_ref):
        g = p_ref[0]
        for q in range(1, p):
            g = g + p_ref[q]
        o_ref[...] = g

    return pl.pallas_call(
        body, name=name, out_shape=jax.ShapeDtypeStruct((r, c), F32), grid=(r // tr,),
        in_specs=[pl.BlockSpec((p, tr, c), lambda i: (0, i, 0))], out_specs=pl.BlockSpec((tr, c), lambda i: (i, 0)),
        compiler_params=_cparams(("parallel",)),
    )(parts)


def _shifted(cur, prev8, next8, i, rt, nlt, ntiles):
    first = jnp.logical_or(i == 0, i == nlt)
    last = jnp.logical_or(i == nlt - 1, i == ntiles - 1)
    prev_row = jnp.where(first, 0.0, prev8[7:8, :])
    next_row = jnp.where(last, 0.0, next8[0:1, :])
    rows = lax.broadcasted_iota(jnp.int32, (rt, 1), 0)
    x_m1 = jnp.where(rows == 0, prev_row, pltpu.roll(cur, 1, 0))
    x_p1 = jnp.where(rows == rt - 1, next_row, pltpu.roll(cur, rt - 1, 0))
    return x_m1, x_p1


def _halo_specs(rt, tc, nrows, col_axis_first):
    r8 = rt // 8
    last8 = nrows // 8 - 1
    if col_axis_first:
        return [pl.BlockSpec((8, tc), lambda j, i: (jnp.maximum(i * r8 - 1, 0), j)),
                pl.BlockSpec((rt, tc), lambda j, i: (i, j)),
                pl.BlockSpec((8, tc), lambda j, i: (jnp.minimum((i + 1) * r8, last8), j))]
    return [pl.BlockSpec((8, tc), lambda i, j: (jnp.maximum(i * r8 - 1, 0), j)),
            pl.BlockSpec((rt, tc), lambda i, j: (i, j)),
            pl.BlockSpec((8, tc), lambda i, j: (jnp.minimum((i + 1) * r8, last8), j))]


def _dwconv(x, w8, d, out_dtype, name):
    nrows, c = x.shape
    rt, nlt = d.rt, d.nlt
    tc = _tile(c, CONV_COLS)
    ntiles = nrows // rt

    def body(p_ref, c_ref, n_ref, w_ref, o_ref):
        i = pl.program_id(0)
        cur = c_ref[...]
        x_m1, x_p1 = _shifted(cur, p_ref[...], n_ref[...], i, rt, nlt, ntiles)
        w = w_ref[...]
        o_ref[...] = (x_m1 * w[0:1] + cur * w[1:2] + x_p1 * w[2:3] + w[3:4]).astype(o_ref.dtype)

    return pl.pallas_call(
        body, name=name, out_shape=jax.ShapeDtypeStruct((nrows, c), out_dtype), grid=(ntiles, c // tc),
        in_specs=_halo_specs(rt, tc, nrows, False) + [pl.BlockSpec((8, tc), lambda i, j: (0, j))],
        out_specs=pl.BlockSpec((rt, tc), lambda i, j: (i, j)),
        compiler_params=_cparams(("parallel", "parallel")),
    )(x, x, x, w8)


def _dwconv_wgrad(x, dy, d, name):
    nrows, c = x.shape
    rt, nlt = d.rt, d.nlt
    tc = _tile(c, CONV_COLS)
    ntiles = nrows // rt

    def body(p_ref, c_ref, n_ref, dy_ref, o_ref):
        i = pl.program_id(1)
        cur = c_ref[...]
        dy = dy_ref[...]
        x_m1, x_p1 = _shifted(cur, p_ref[...], n_ref[...], i, rt, nlt, ntiles)
        sums = [jnp.sum(t * dy, axis=0, keepdims=True) for t in (x_m1, cur, x_p1)] + [jnp.sum(dy, axis=0, keepdims=True)]
        row = lax.broadcasted_iota(jnp.int32, (8, 1), 0)
        part = jnp.zeros((8, tc), F32)
        for k, s in enumerate(sums):
            part = jnp.where(row == k, s, part)

        @pl.when(i == 0)
        def _():
            o_ref[...] = part

        @pl.when(i != 0)
        def _():
            o_ref[...] += part

    return pl.pallas_call(
        body, name=name, out_shape=jax.ShapeDtypeStruct((8, c), F32), grid=(c // tc, ntiles),
        in_specs=_halo_specs(rt, tc, nrows, True) + [pl.BlockSpec((rt, tc), lambda j, i: (i, j))],
        out_specs=pl.BlockSpec((8, tc), lambda j, i: (0, j)),
        compiler_params=_cparams(("parallel", "arbitrary")),
    )(x, x, x, dy)


def _ffn_gate_fwd(p, w8, d, name):
    nrows, ff = p.shape[0], p.shape[1] // 2
    rt, nlt = d.rt, d.nlt
    tc = _tile(ff, GLU_COLS)
    nb = ff // tc
    ntiles = nrows // rt

    def body(gp_ref, gc_ref, gn_ref, vp_ref, vc_ref, vn_ref, wg_ref, wv_ref, o_ref):
        i = pl.program_id(0)
        us = []
        for p_ref, c_ref, n_ref, w_ref in ((gp_ref, gc_ref, gn_ref, wg_ref), (vp_ref, vc_ref, vn_ref, wv_ref)):
            cur, w = c_ref[...], w_ref[...]
            x_m1, x_p1 = _shifted(cur, p_ref[...], n_ref[...], i, rt, nlt, ntiles)
            us.append(x_m1 * w[0:1] + cur * w[1:2] + x_p1 * w[2:3] + w[3:4])
        o_ref[...] = (_silu(us[0]) * us[1]).astype(o_ref.dtype)

    r8, last8 = rt // 8, nrows // 8 - 1

    def halo(off):
        return [pl.BlockSpec((8, tc), lambda i, j: (jnp.maximum(i * r8 - 1, 0), j + off)),
                pl.BlockSpec((rt, tc), lambda i, j: (i, j + off)),
                pl.BlockSpec((8, tc), lambda i, j: (jnp.minimum((i + 1) * r8, last8), j + off))]
    return pl.pallas_call(
        body, name=name, out_shape=jax.ShapeDtypeStruct((nrows, ff), BF16), grid=(ntiles, nb),
        in_specs=halo(0) + halo(nb) + [pl.BlockSpec((8, tc), lambda i, j: (0, j)), pl.BlockSpec((8, tc), lambda i, j: (0, j + nb))],
        out_specs=pl.BlockSpec((rt, tc), lambda i, j: (i, j)),
        compiler_params=_cparams(("parallel", "parallel")),
    )(p, p, p, p, p, p, w8, w8)


def _ffn_gate_bwd(p, da, w8, d, name):
    nrows, ff = da.shape
    rt, nlt = d.rt, d.nlt
    tc = _tile(ff, 512)
    nb = ff // tc
    ntiles = nrows // rt
    ext = rt + 16

    def body(gp_ref, gc_ref, gn_ref, vp_ref, vc_ref, vn_ref, ap_ref, ac_ref, an_ref, wg_ref, wv_ref,
             dpg_ref, dpv_ref, cg_ref, cv_ref):
        i = pl.program_id(1)
        first = jnp.logical_or(i == 0, i == nlt)
        last = jnp.logical_or(i == nlt - 1, i == ntiles - 1)

        def extended(p_ref, c_ref, n_ref):
            return jnp.concatenate([jnp.where(first, 0.0, p_ref[...]), c_ref[...], jnp.where(last, 0.0, n_ref[...])], axis=0)

        def conv(x, w):
            return pltpu.roll(x, 1, 0) * w[0:1] + x * w[1:2] + pltpu.roll(x, ext - 1, 0) * w[2:3]

        wg, wv = wg_ref[...], wv_ref[...]
        pg, pv, da_e = extended(gp_ref, gc_ref, gn_ref), extended(vp_ref, vc_ref, vn_ref), extended(ap_ref, ac_ref, an_ref)
        ug = conv(pg, wg) + wg[3:4]
        uv = conv(pv, wv) + wv[3:4]
        sg = _sigmoid(ug)
        dug = da_e * uv * (sg * (1.0 + ug * (1.0 - sg)))
        duv = da_e * (ug * sg)
        row = lax.broadcasted_iota(jnp.int32, (8, 1), 0)
        for p_e, du, w, dp_ref, c_ref in ((pg, dug, wg, dpg_ref, cg_ref), (pv, duv, wv, dpv_ref, cv_ref)):
            dp = pltpu.roll(du, 1, 0) * w[2:3] + du * w[1:2] + pltpu.roll(du, ext - 1, 0) * w[0:1]
            dp_ref[...] = dp[8:rt + 8].astype(dp_ref.dtype)
            du_c = du[8:rt + 8]
            sums = [jnp.sum(t[8:rt + 8] * du_c, axis=0, keepdims=True)
                    for t in (pltpu.roll(p_e, 1, 0), p_e, pltpu.roll(p_e, ext - 1, 0))] + [jnp.sum(du_c, axis=0, keepdims=True)]
            part = jnp.zeros((8, tc), F32)
            for k, s in enumerate(sums):
                part = jnp.where(row == k, s, part)

            @pl.when(i == 0)
            def _(c_ref=c_ref, part=part):
                c_ref[...] = part

            @pl.when(i != 0)
            def _(c_ref=c_ref, part=part):
                c_ref[...] += part

    r8, last8 = rt // 8, nrows // 8 - 1

    def halo(off):
        return [pl.BlockSpec((8, tc), lambda j, i: (jnp.maximum(i * r8 - 1, 0), j + off)),
                pl.BlockSpec((rt, tc), lambda j, i: (i, j + off)),
                pl.BlockSpec((8, tc), lambda j, i: (jnp.minimum((i + 1) * r8, last8), j + off))]
    tile = pl.BlockSpec((rt, tc), lambda j, i: (i, j))
    acc = pl.BlockSpec((8, tc), lambda j, i: (0, j))
    dpg, dpv, cg, cv = pl.pallas_call(
        body, name=name,
        out_shape=(jax.ShapeDtypeStruct((nrows, ff), BF16), jax.ShapeDtypeStruct((nrows, ff), BF16),
                   jax.ShapeDtypeStruct((8, ff), F32), jax.ShapeDtypeStruct((8, ff), F32)),
        grid=(nb, ntiles),
        in_specs=halo(0) + halo(nb) + halo(0) + [acc, pl.BlockSpec((8, tc), lambda j, i: (0, j + nb))],
        out_specs=(tile, tile, acc, acc),
        compiler_params=_cparams(("parallel", "arbitrary")),
    )(p, p, p, p, p, p, da, da, da, w8, w8)
    return dpg, dpv, jnp.concatenate([cg, cv], axis=1)


def _w8(w3, b=None):
    c = w3.shape[1]
    brow = jnp.zeros((1, c), F32) if b is None else b.reshape(1, c)
    return jnp.concatenate([w3, brow, jnp.zeros((4, c), F32)], axis=0)


def _rms(x, w):
    r = lax.rsqrt(jnp.mean(x * x, axis=-1, keepdims=True) + EPS)
    xh = x * r
    return xh * w, xh, r


def _rms_bwd(dy, xh, r, w):
    dxh = dy * w
    dx = r * (dxh - xh * jnp.mean(dxh * xh, axis=-1, keepdims=True))
    return dx, jnp.sum(dy * xh, axis=0, keepdims=True)


def _mod_bwd(dh, x, w, shift, scale):
    n, xh, r = _rms(x, w)
    dshift = jnp.sum(dh, axis=0, keepdims=True)
    dscale = jnp.sum(dh * n, axis=0, keepdims=True)
    dx, dw = _rms_bwd(dh * (1.0 + scale), xh, r, w)
    return dx, dw, dshift, dscale


def _sigmoid(x):
    return 1.0 / (1.0 + jnp.exp(-x))


def _silu(x):
    return x * _sigmoid(x)


def _dsilu(x):
    s = _sigmoid(x)
    return s * (1.0 + x * (1.0 - s))


def _rope(x, cos, sin_s):
    return x * cos + pltpu.roll(x, LANE // 2, 1) * sin_s


def _rope_t(dy, cos, sin_s):
    return dy * cos + pltpu.roll(dy * sin_s, LANE // 2, 1)


def _flash_fwd(q, kcat, kv, d, rides=None):
    riders = rides.take("flash_fwd") if rides is not None else []
    nr = len(riders)
    r_in, r_shapes, r_out, r_sems = _ride_plumbing(riders)
    nt, h, tq = d.NT, d.H, d.tq
    tkb = _tile(d.T, 4096)
    n_big = d.T // tkb
    nq_lat = d.T // tq

    def body(*refs):
        q_ref, k_ref, v_ref = refs[:3]
        o_ref, ob_ref, lse_ref = refs[3 + nr:6 + nr]
        qi = pl.program_id(1)
        if nr:
            start, wait = _ride_copies([st for _, _, st in riders], refs[3:3 + nr], refs[6 + nr:6 + 2 * nr], *refs[6 + 2 * nr:])
            pl.when(jnp.logical_and(pl.program_id(0) == 0, qi == 0))(start)
        q_t = q_ref[...]

        def step(k0, tk, carry):
            m, l, acc = carry
            ks = pl.ds(k0, tk)
            s = _dot(q_t, k_ref[ks, :], 'nt')
            m_new = jnp.maximum(m, jnp.max(s, axis=1, keepdims=True))
            p = jnp.exp2(s - m_new)
            alpha = jnp.exp2(m - m_new)
            return m_new, alpha * l + jnp.sum(p, axis=1, keepdims=True), alpha * acc + _dot(p, v_ref[ks, :], 'nn')

        init = (jnp.full((tq, 1), -1e30, F32), jnp.zeros((tq, 1), F32), jnp.zeros((tq, LANE), F32))
        trips = jnp.where(qi < nq_lat, n_big, 0)
        carry = lax.fori_loop(0, trips, lambda t, c: step(pl.multiple_of(t * tkb, tkb), tkb, c), init)
        m, l, acc = step(d.T, d.TC, carry)
        o = acc / l
        o_ref[...] = o
        ob_ref[...] = o.astype(BF16)
        lse_ref[...] = jnp.broadcast_to(m + jnp.log2(l), (tq, LANE))
        if nr:
            pl.when(jnp.logical_and(pl.program_id(0) == h - 1, qi == nt // tq - 1))(wait)

    out = pl.BlockSpec((tq, LANE), lambda hh, i: (i, hh))
    res = pl.pallas_call(
        body, name="flash_fwd",
        out_shape=[jax.ShapeDtypeStruct((nt, d.HQ), F32), jax.ShapeDtypeStruct((nt, d.HQ), BF16),
                   jax.ShapeDtypeStruct((nt, d.HQ), F32)] + r_shapes,
        grid=(h, nt // tq),
        in_specs=[pl.BlockSpec((tq, 2 * LANE), lambda hh, i: (i, hh)), pl.BlockSpec((nt, 2 * LANE), lambda hh, i: (0, hh)),
                  pl.BlockSpec((nt, LANE), lambda hh, i: (0, h + hh))] + r_in,
        out_specs=[out, out, out] + r_out,
        scratch_shapes=r_sems,
        compiler_params=_cparams(("arbitrary", "arbitrary") if nr else ("parallel", "parallel")),
    )(q, kcat, kv, *[x for _, x, _ in riders])
    for (key, _, _), arr in zip(riders, res[3:]):
        rides.done[key] = arr
    return res[0], res[1], res[2]


def _flash_bwd(q, kcat, kv, do, ld, d, rides=None):
    riders = rides.take("flash_bwd") if rides is not None else []
    nr = len(riders)
    r_in, r_shapes, r_out, r_sems = _ride_plumbing(riders)
    nt, h, tk = d.NT, d.H, d.tq
    tqb =_tile(d.T, 2048)
    n_big = d.T // tqb
    nk_lat = d.T // tk

    def body(*refs):
        q_ref, do_ref, ld_ref, k_ref, v_ref = refs[:5]
        dq_ref, dk_ref, dv_ref = refs[5 + nr:8 + nr]
        kt = pl.program_id(1)
        if nr:
            start, wait = _ride_copies([st for _, _, st in riders], refs[5:5 + nr], refs[8 + nr:8 + 2 * nr], *refs[8 + 2 * nr:])
            pl.when(jnp.logical_and(pl.program_id(0) == 0, kt == 0))(start)
        k_t, v_t = k_ref[...], v_ref[...]

        @pl.when(kt == 0)
        def _():
            dq_ref[...] = jnp.zeros_like(dq_ref)

        def step(q0, tq, carry):
            dk, dv = carry
            qs = pl.ds(q0, tq)
            q_t, do_t, ld_t = q_ref[qs, :], do_ref[qs, :], ld_ref[qs, :]
            p = jnp.exp2(_dot(q_t, k_t, 'nt') - ld_t[:, 0:1])
            ds = p * (_dot(do_t, v_t, 'nt') - ld_t[:, LANE // 2:LANE // 2 + 1])
            dq_ref[qs, :] += _dot(ds, k_t, 'nn')
            return dk + _dot(ds, q_t, 'tn'), dv + _dot(p, do_t, 'tn')

        init = (jnp.zeros((tk, 2 * LANE), F32), jnp.zeros((tk, LANE), F32))
        carry = lax.fori_loop(0, n_big, lambda t, c: step(pl.multiple_of(t * tqb, tqb), tqb, c), init)
        dk_ref[...] = carry[0] * LN2
        dv_ref[...] = carry[1].astype(BF16)

        @pl.when(kt >= nk_lat)
        def _():
            dk, dv = step(d.T, d.TC, carry)
            dk_ref[...] = dk * LN2
            dv_ref[...] = dv.astype(BF16)

        if nr:
            pl.when(jnp.logical_and(pl.program_id(0) == h - 1, kt == nt // tk - 1))(wait)

    res = lambda w: pl.BlockSpec((nt, w), lambda hh, i: (0, hh))
    outs = pl.pallas_call(
        body, name="flash_bwd",
        out_shape=[jax.ShapeDtypeStruct((nt, 2 * d.HQ), F32), jax.ShapeDtypeStruct((nt, 2 * d.HQ), F32),
                   jax.ShapeDtypeStruct((nt, d.HQ), BF16)] + r_shapes,
        grid=(h, nt // tk),
        in_specs=[res(2 * LANE), res(LANE), res(LANE), pl.BlockSpec((tk, 2 * LANE), lambda hh, i: (i, hh)),
                  pl.BlockSpec((tk, LANE), lambda hh, i: (i, h + hh))] + r_in,
        out_specs=[res(2 * LANE), pl.BlockSpec((tk, 2 * LANE), lambda hh, i: (i, hh)),
                   pl.BlockSpec((tk, LANE), lambda hh, i: (i, hh))] + r_out,
        scratch_shapes=r_sems,
        compiler_params=_cparams(("arbitrary", "arbitrary") if nr else ("parallel", "arbitrary")),
    )(q, do, ld, kcat, kv, *[x for _, x, _ in riders])
    for (key, _, _), arr in zip(riders, outs[3:]):
        rides.done[key] = arr
    return outs[0], outs[1], outs[2]


def _tri(dirn):
    r = lax.broadcasted_iota(jnp.int32, (CHUNK, CHUNK), 0)
    c = lax.broadcasted_iota(jnp.int32, (CHUNK, CHUNK), 1)
    return (c <= r) if dirn == 0 else (c >= r)


def _exact_mask_dot(mask_bf16, x):
    hi = x.astype(BF16)
    r1 = x - hi.astype(F32)
    mid = r1.astype(BF16)
    lo = (r1 - mid.astype(F32)).astype(BF16)
    dot = lambda t: lax.dot_general(mask_bf16, t, (((1,), (0,)), ((), ())), preferred_element_type=F32)
    return dot(hi) + dot(mid) + dot(lo)


def _gla_terms(q, k, g, dirn, dk):
    mb = _tri(dirn)
    b = _exact_mask_dot(mb.astype(BF16), g)
    tot = jnp.sum(g, axis=0, keepdims=True)
    qe = q * (float(dk) ** -0.5) * jnp.exp(b)
    ke = k * jnp.exp(-b)
    kd = k * jnp.exp(tot - b)
    att = jnp.where(mb, _dot(qe, ke, 'nt'), 0.0)
    return mb, b, tot, qe, ke, kd, att


def _gla_block_index(d, dirn):
    nb = d.NT // d.rt
    if dirn == 0:
        return lambda s: (s + d.nlt) % nb
    return lambda s: nb - 1 - s


def _gla_rows(j, nsub, dirn):
    r0 = (j if dirn == 0 else nsub - 1 - j) * CHUNK
    return slice(r0, r0 + CHUNK)


def _gla_fwd(z, g, d):
    nt, gh, dk, dv, rb = d.NT, d.GH, d.dk, d.dv, d.rt
    nb, nsub = nt // rb, rb // CHUNK
    qb, kb = d.z1_q // dk, d.z1_k // dk

    def body(*refs):
        @pl.when(pl.program_id(1) == 0)
        def _():
            for dirn in (0, 1):
                refs[12 + dirn][...] = jnp.zeros_like(refs[12 + dirn])

        for dirn in (0, 1):
            q_ref, k_ref, v_ref, g_ref = refs[4 * dirn:4 * dirn + 4]
            o_ref, st_ref, state = refs[8 + 2 * dirn], refs[9 + 2 * dirn], refs[12 + dirn]
            pre = []
            for j in range(nsub):
                rs = _gla_rows(j, nsub, dirn)
                v = v_ref[rs, :]
                _, _, tot, qe, _, kd, att = _gla_terms(q_ref[rs, :], k_ref[rs, :], g_ref[rs, :], dirn, dk)
                o_ref[rs, :] = _dot(att, v, 'nn')
                pre.append((rs, qe, jnp.exp(tot), _dot(v, kd, 'tn')))
            st = state[...]
            for j, (rs, qe, decay, update) in enumerate(pre):
                st_ref[j] = st
                o_ref[rs, :] += _dot(qe, st, 'nt')
                st = st * decay + update
            state[...] = st

    def col(dirn, w, off):
        bidx = _gla_block_index(d, dirn)
        return pl.BlockSpec((rb, w), lambda hh, s: (bidx(s), off + hh))

    in_specs, out_shapes, out_specs = [], [], []
    for dirn in (0, 1):
        in_specs += [col(dirn, dk, qb), col(dirn, dk, kb), col(dirn, dv, 0), col(dirn, dk, dirn * gh)]
        out_shapes += [jax.ShapeDtypeStruct((nt, d.VAL), F32), jax.ShapeDtypeStruct((gh, nb * nsub, dv, dk), F32)]
        out_specs += [col(dirn, dv, 0), pl.BlockSpec((None, nsub, dv, dk), lambda hh, s: (hh, s, 0, 0))]
    return pl.pallas_call(
        body, name="gla_fwd", out_shape=out_shapes, grid=(gh, nb), in_specs=in_specs, out_specs=out_specs,
        scratch_shapes=[pltpu.VMEM((dv, dk), F32), pltpu.VMEM((dv, dk), F32)],
        compiler_params=_cparams(("parallel", "arbitrary")),
    )(z, z, z, g, z, z, z, g)


def _gla_bwd(z, g, do, states_fw, states_bw, d, rides=None):
    riders = rides.take("gla_bwd") if rides is not None else []
    nr = len(riders)
    r_in, r_shapes, r_out, r_sems = _ride_plumbing(riders)
    nt, gh, dk, dv, rb = d.NT, d.GH, d.dk, d.dv, d.rt
    nb, nsub = nt // rb, rb // CHUNK
    qb, kb = d.z1_q // dk, d.z1_k // dk
    qscale = float(dk) ** -0.5
    n_in, n_out = 12, 8

    def body(*refs):
        outs = refs[n_in + nr:n_in + nr + n_out]
        dstates = refs[n_in + n_out + 2 * nr:n_in + n_out + 2 * nr + 2]
        if nr:
            start, wait = _ride_copies([st for _, _, st in riders], refs[n_in:n_in + nr],
                                       refs[n_in + nr + n_out:n_in + n_out + 2 * nr], *refs[n_in + n_out + 2 * nr + 2:])
            pl.when(jnp.logical_and(pl.program_id(0) == 0, pl.program_id(1) == 0))(start)

        @pl.when(pl.program_id(1) == 0)
        def _():
            for dstate in dstates:
                dstate[...] = jnp.zeros_like(dstate)

        for dirn in (0, 1):
            q_ref, k_ref, v_ref, g_ref, do_ref, st_ref = refs[6 * dirn:6 * dirn + 6]
            dq_ref, dk_ref, dv_ref, dg_ref = outs[4 * dirn:4 * dirn + 4]
            dst = dstates[dirn][...]
            for j in reversed(range(nsub)):
                rs = _gla_rows(j, nsub, dirn)
                q, k, v, g_, dout, st = q_ref[rs, :], k_ref[rs, :], v_ref[rs, :], g_ref[rs, :], do_ref[rs, :], st_ref[j]
                mb, b, tot, qe, ke, kd, att = _gla_terms(q, k, g_, dirn, dk)
                etot = jnp.exp(tot)
                datt = jnp.where(mb, _dot(dout, v, 'nt'), 0.0)
                dv_ref[rs, :] = _dot(att, dout, 'tn') + _dot(kd, dst, 'nt')
                dqe = _dot(datt, ke, 'nn') + _dot(dout, st, 'nn')
                dke = _dot(datt, qe, 'tn')
                dkd = _dot(v, dst, 'nn')
                dq_ref[rs, :] = dqe * (qscale * jnp.exp(b))
                dk_ref[rs, :] = dke * jnp.exp(-b) + dkd * jnp.exp(tot - b)
                dkd_kd = dkd * kd
                db = dqe * qe - dke * ke - dkd_kd
                dtot = jnp.sum(dkd_kd, axis=0, keepdims=True) + jnp.sum(dst * st, axis=0, keepdims=True) * etot
                dg_ref[rs, :] = _exact_mask_dot(_tri(1 - dirn).astype(BF16), db) + dtot
                dst = dst * etot + _dot(dout, qe, 'tn')
            dstates[dirn][...] = dst
        if nr:
            pl.when(jnp.logical_and(pl.program_id(0) == gh - 1, pl.program_id(1) == nb - 1))(wait)

    def col(dirn, w, off):
        bfwd = _gla_block_index(d, dirn)
        return pl.BlockSpec((rb, w), lambda hh, s: (bfwd(nb - 1 - s), off + hh))

    in_specs, out_shapes, out_specs = [], [], []
    for dirn in (0, 1):
        in_specs += [col(dirn, dk, qb), col(dirn, dk, kb), col(dirn, dv, 0), col(dirn, dk, dirn * gh), col(dirn, dv, 0),
                     pl.BlockSpec((None, nsub, dv, dk), lambda hh, s: (hh, nb - 1 - s, 0, 0))]
        out_shapes += [jax.ShapeDtypeStruct((nt, d.KEY), F32), jax.ShapeDtypeStruct((nt, d.KEY), F32),
                       jax.ShapeDtypeStruct((nt, d.VAL), F32), jax.ShapeDtypeStruct((nt, d.KEY), F32)]
        out_specs += [col(dirn, dk, 0), col(dirn, dk, 0), col(dirn, dv, 0), col(dirn, dk, 0)]
    res = pl.pallas_call(
        body, name="gla_bwd", out_shape=out_shapes + r_shapes, grid=(gh, nb), in_specs=in_specs + r_in,
        out_specs=out_specs + r_out,
        scratch_shapes=[pltpu.VMEM((dv, dk), F32), pltpu.VMEM((dv, dk), F32)] + r_sems,
        compiler_params=_cparams(("arbitrary", "arbitrary") if nr else ("parallel", "arbitrary")),
    )(z, z, z, g, do, states_fw, z, z, z, g, do, states_bw, *[x for _, x, _ in riders])
    for (key, _, _), arr in zip(riders, res[n_out:]):
        rides.done[key] = arr
    return res[0:4], res[4:8]


def _rope_pad(w):
    q = QK_ROPE // 4
    a1, a2, b1, b2 = (w[..., k * q:(k + 1) * q] for k in range(4))
    z = jnp.zeros(w.shape[:-1] + (LANE // 2 - 2 * q,), w.dtype)
    return jnp.concatenate([a1, b1, z, a2, b2, z], axis=-1)


def _rope_unpad(g):
    q = QK_ROPE // 4
    h = LANE // 2
    return jnp.concatenate([g[..., 0:q], g[..., h:h + q], g[..., q:2 * q], g[..., h + q:h + 2 * q]], axis=-1)


def _win0_to_kernel(w, d):
    kv_lat = w[:, :d.KVL]
    k_rope = w[:, d.KVL:d.KVL + QK_ROPE]
    q_lat = w[:, d.KVL + QK_ROPE:d.KVL + QK_ROPE + d.QL]
    rest = w[:, d.KVL + QK_ROPE + d.QL:]
    parts = [q_lat, kv_lat, _rope_pad(k_rope)]
    if d.z0_pad:
        parts.append(jnp.zeros((w.shape[0], d.z0_pad), w.dtype))
    return jnp.concatenate(parts + [rest], axis=1)


def _win0_from_kernel(g, d):
    return jnp.concatenate([g[:, d.z0_kv:d.z0_kv + d.KVL], _rope_unpad(g[:, d.z0_kr:d.z0_kr + LANE]), g[:, :d.QL],
                            g[:, d.z0_ax:]], axis=1)


def _wqb_to_kernel(w, d):
    wr = w.reshape(d.QL, d.H, QK_NOPE + QK_ROPE)
    return jnp.concatenate([wr[:, :, :QK_NOPE], _rope_pad(wr[:, :, QK_NOPE:])], axis=2).reshape(d.QL, 2 * d.HQ)


def _wqb_from_kernel(g, d):
    gr = g.reshape(d.QL, d.H, QK_NOPE + LANE)
    return jnp.concatenate([gr[:, :, :QK_NOPE], _rope_unpad(gr[:, :, QK_NOPE:])], axis=2).reshape(d.QL, d.H * (QK_NOPE + QK_ROPE))


def _wkvb_to_kernel(w, d):
    return w.reshape(d.KVL, d.H, 2, LANE).transpose(0, 2, 1, 3).reshape(d.KVL, 2 * d.HQ)


def _wkvb_from_kernel(g, d):
    return g.reshape(d.KVL, 2, d.H, LANE).transpose(0, 2, 1, 3).reshape(d.KVL, 2 * d.HQ)


def _win1_to_kernel(w, d):
    k = w[:, :d.KEY]
    v = w[:, d.KEY:d.KEY + d.VAL]
    lr = w[:, d.KEY + d.VAL:d.KEY + d.VAL + 2 * GATE_RANK]
    q = w[:, d.KEY + d.VAL + 2 * GATE_RANK:2 * d.KEY + d.VAL + 2 * GATE_RANK]
    og = w[:, 2 * d.KEY + d.VAL + 2 * GATE_RANK:]
    return jnp.concatenate([v, og, k, q, lr, jnp.zeros((w.shape[0], LANE - 2 * GATE_RANK), w.dtype)], axis=1)


def _win1_from_kernel(g, d):
    return jnp.concatenate([g[:, d.z1_k:d.z1_k + d.KEY], g[:, :d.VAL], g[:, d.z1_lr:d.z1_lr + 2 * GATE_RANK],
                            g[:, d.z1_q:d.z1_q + d.KEY], g[:, d.z1_og:d.z1_og + d.VAL]], axis=1)


def _gate_weight(fw_w, bw_w, d):
    z = jnp.zeros((GATE_RANK, d.KEY), F32)
    return jnp.concatenate([jnp.concatenate([fw_w, z], axis=1), jnp.concatenate([z, bw_w], axis=1),
                            jnp.zeros((LANE - 2 * GATE_RANK, 2 * d.KEY), F32)], axis=0)


def _rope_tables(d):
    t = jnp.arange(d.T)
    inv = ROPE_THETA ** (-jnp.arange(0, QK_ROPE // 2, 2, dtype=F32) / (QK_ROPE // 2))
    ar = (t // GRID_W).astype(F32)[:, None] * inv
    ac = (t % GRID_W).astype(F32)[:, None] * inv
    z = jnp.zeros((d.T, LANE // 2 - 2 * inv.shape[0]), F32)
    ang = jnp.concatenate([ar, ac, z, ar, ac, z], axis=1)
    cos = jnp.concatenate([jnp.cos(ang), jnp.ones((d.TC, LANE), F32)], axis=0)
    sin = jnp.concatenate([jnp.sin(ang), jnp.zeros((d.TC, LANE), F32)], axis=0)
    sgn = jnp.where(jnp.arange(LANE) < LANE // 2, -1.0, 1.0).astype(F32)
    return cos, sin * sgn


def _row(arr, cb=0, width=None):
    return ('row', arr, cb, arr.shape[1] if width is None else width)


def _mod_fwd(x, nw, shift, scale, d, name):
    def body(i, j, x, w, sh, sc):
        return (_rms(x, w)[0] * (1.0 + sc) + sh,)
    return _rowwise(body, d.NT, d.rt, d.nlt, [_row(x), ('full', nw), ('seg', shift), ('seg', scale)],
                    [('row', d.D, BF16)], name)[0]


def _mod_bwd_call(dres, dh, x, nw, shift, scale, d, name):
    def body(i, j, dres, dh, x, w, sh, sc):
        dx, dw, dsh, dsc = _mod_bwd(dh, x, w, sh, sc)
        return dres + dx, dw, dsh, dsc
    return _rowwise(body, d.NT, d.rt, d.nlt, [_row(dres), _row(dh), _row(x), ('full', nw), ('seg', shift), ('seg', scale)],
                    [('row', d.D, F32), ('acc', (1, d.D)), ('segacc', d.D), ('segacc', d.D)], name)


def _res_mod_fwd(x, y, gate, nw, shift, scale, d, name):
    def body(i, j, x, y, g, w, sh, sc):
        x1 = x + g * y
        return x1, _rms(x1, w)[0] * (1.0 + sc) + sh
    return _rowwise(body, d.NT, d.rt, d.nlt, [_row(x), _row(y), ('seg', gate), ('full', nw), ('seg', shift), ('seg', scale)],
                    [('row', d.D, F32), ('row', d.D, BF16)], name)


def _res_mod_bwd(dx2, dh2, x1, y, gate, nw, shift, scale, d, name):
    def body(i, j, dx2, dh2, x1, y, g, w, sh, sc):
        dx, dw, dsh, dsc = _mod_bwd(dh2, x1, w, sh, sc)
        dx1 = dx2 + dx
        return dx1, g * dx1, jnp.sum(dx1 * y, axis=0, keepdims=True), dw, dsh, dsc
    return _rowwise(body, d.NT, d.rt, d.nlt,
                    [_row(dx2), _row(dh2), _row(x1), _row(y), ('seg', gate), ('full', nw), ('seg', shift), ('seg', scale)],
                    [('row', d.D, F32), ('row', d.D, BF16), ('segacc', d.D), ('acc', (1, d.D)), ('segacc', d.D),
                     ('segacc', d.D)], name)


def _res_fwd(x1, f, gate, d, name):
    return _rowwise(lambda i, j, x1, f, g: (x1 + g * f,), d.NT, d.rt, d.nlt, [_row(x1), _row(f), ('seg', gate)],
                    [('row', d.D, F32)], name)[0]


def _res_bwd(dx2, f, gate, d, name):
    def body(i, j, dx2, f, g):
        return g * dx2, jnp.sum(dx2 * f, axis=0, keepdims=True)
    return _rowwise(body, d.NT, d.rt, d.nlt, [_row(dx2), _row(f), ('seg', gate)], [('row', d.D, BF16), ('segacc', d.D)], name)


def _w(w, key):
    if callable(w[key]):
        w[key] = w[key]()
    return w[key]


def _ffn_fwd(h2, w, d, tag, rides):
    p = _mm([(h2, _w(w, 'ffn_up'))], 'nn', F32, tag + '_up', rides=rides)
    a = _ffn_gate_fwd(p, _w8(w['ffn_conv_w'], w['ffn_conv_b']), d, tag + '_gate')
    f = _mm([(a, _w(w, 'ffn_down'))], 'nn', F32, tag + '_down', rides=rides)
    return p, a, f


def _ffn_bwd(df, h2, p, a, w, d, tag, rides, emit):
    w_up = w['ffn_up']
    da = _mm([(df, w['ffn_down'])], 'nt', F32, tag + '_down_dx', tn=GLU_COLS, rides=rides)
    emit('ffn_down', _mm([(a, df)], 'tn', GRAD_WIRE, tag + '_down_dw'))
    dpg, dpv, conv_g = _ffn_gate_bwd(p, da, _w8(w['ffn_conv_w'], w['ffn_conv_b']), d, tag + '_gate_bwd')
    dh2 = _mm([(dpg, w_up[:, :d.FF]), (dpv, w_up[:, d.FF:])], 'nt', F32, tag + '_up_dx', rides=rides)
    emit('ffn_up', jnp.concatenate([_mm([(h2, dpg)], 'tn', GRAD_WIRE, tag + '_up_dw_gate', tn=GLU_COLS),
                                    _mm([(h2, dpv)], 'tn', GRAD_WIRE, tag + '_up_dw_val', tn=GLU_COLS)], axis=1))
    return dh2, conv_g[0:3], conv_g[3]


def _ab_fwd(z, w, cos, sin_s, d, rides):
    qnw, kvnw = w['q_norm'], w['kv_norm']

    def prep(i, j, zq, zkv, zkr, qw, kw, cos, sin_s):
        return _rms(zq, qw)[0], _rms(zkv, kw)[0], _rope(zkr, cos, sin_s)
    qn, kvn, kr = _rowwise(prep, d.NT, d.rt, d.nlt,
                           [_row(z, 0, d.QL), _row(z, d.z0_kv // d.KVL, d.KVL), _row(z, d.z0_kr // LANE, LANE),
                            ('full', qnw), ('full', kvnw), _row(cos), _row(sin_s)],
                           [('row', d.QL, BF16), ('row', d.KVL, BF16), ('row', LANE, BF16)], 'ab_prep')
    qraw = _mm([(qn, w['w_qb'])], 'nn', F32, 'ab_qb')
    kv = _mm([(kvn, w['w_kvb'])], 'nn', BF16, 'ab_kvb')

    def qrope(i, j, qraw, cos, sin_s):
        parts = []
        for h in range(d.H):
            parts += [qraw[:, 2 * h * LANE:(2 * h + 1) * LANE], _rope(qraw[:, (2 * h + 1) * LANE:(2 * h + 2) * LANE], cos, sin_s)]
        return (jnp.concatenate(parts, axis=1) * (ATTN_SCALE / LN2),)
    q = _rowwise(qrope, d.NT, d.rt, d.nlt, [_row(qraw), _row(cos), _row(sin_s)], [('row', 2 * d.HQ, BF16)], 'ab_qrope')[0]

    def kcat_body(i, j, kn, kr):
        parts = []
        for h in range(d.H):
            parts += [kn[:, h * LANE:(h + 1) * LANE], kr]
        return (jnp.concatenate(parts, axis=1),)
    kcat = _rowwise(kcat_body, d.NT, d.rt, d.nlt, [_row(kv, 0, d.HQ), _row(kr)], [('row', 2 * d.HQ, BF16)], 'ab_kcat')[0]
    o, ob, lse = _flash_fwd(q, kcat, kv, d, rides)
    ab = d.z0_ax // d.CC
    s = _rowwise(lambda i, j, ax, ac: (ax * ac,), d.NT, d.rt, d.nlt, [_row(z, ab, d.CC), _row(z, ab + 2, d.CC)],
                 [('row', d.CC, F32)], 'ab_conv_in')[0]
    cv = _dwconv(s, _w8(w['conv_a']), d, F32, 'ab_conv')
    ymix = _rowwise(lambda i, j, a_b, cv, ob: (jnp.concatenate([(a_b * cv).astype(BF16), ob], axis=1),), d.NT, d.rt, d.nlt,
                    [_row(z, ab + 1, d.CC), _row(cv), _row(ob)], [('row', d.CC + d.HQ, BF16)], 'ab_mix')[0]
    return ymix, dict(qn=qn, kvn=kvn, q=q, kcat=kcat, kv=kv, o=o, lse=lse, s=s, cv=cv)


def _ab_bwd(dymix, z, sv, w, cos, sin_s, d, rides, emit):
    qnw, kvnw = w['q_norm'], w['kv_norm']
    assert d.CC % d.HQ == 0

    def dprep(i, j, dmo, o, lse):
        lane = lax.broadcasted_iota(jnp.int32, (1, LANE), 1)
        cols = []
        for h in range(d.H):
            hs = slice(h * LANE, (h + 1) * LANE)
            delta = jnp.sum(dmo[:, hs] * o[:, hs], axis=1, keepdims=True)
            cols.append(jnp.where(lane < LANE // 2, lse[:, hs], delta))
        return dmo, jnp.concatenate(cols, axis=1)
    do, ld = _rowwise(dprep, d.NT, d.rt, d.nlt, [_row(dymix, d.CC // d.HQ, d.HQ), _row(sv['o']), _row(sv['lse'])],
                      [('row', d.HQ, BF16), ('row', d.HQ, F32)], 'ab_do')
    dq, dkc, dvv = _flash_bwd(sv['q'], sv['kcat'], sv['kv'], do, ld, d, rides)

    def qrope_t(i, j, dq, cos, sin_s):
        dq = dq * ATTN_SCALE
        parts = []
        for h in range(d.H):
            parts += [dq[:, 2 * h * LANE:(2 * h + 1) * LANE], _rope_t(dq[:, (2 * h + 1) * LANE:(2 * h + 2) * LANE], cos, sin_s)]
        return (jnp.concatenate(parts, axis=1),)
    dqraw = _rowwise(qrope_t, d.NT, d.rt, d.nlt, [_row(dq), _row(cos), _row(sin_s)],
                     [('row', 2 * d.HQ, BF16)], 'ab_qrope_bwd')[0]

    def dkv_body(i, j, dkc, dv):
        dkr = dkc[:, LANE:2 * LANE]
        for h in range(1, d.H):
            dkr = dkr + dkc[:, (2 * h + 1) * LANE:(2 * h + 2) * LANE]
        parts = [dkc[:, 2 * h * LANE:(2 * h + 1) * LANE] for h in range(d.H)] + [dv.astype(F32)]
        return jnp.concatenate(parts, axis=1), dkr
    dkv, dkr = _rowwise(dkv_body, d.NT, d.rt, d.nlt, [_row(dkc), _row(dvv)], [('row', 2 * d.HQ, BF16), ('row', LANE, F32)],
                        'ab_dkv')
    dqn = _mm([(dqraw, w['w_qb'])], 'nt', F32, 'ab_qb_dx')
    emit('w_qb', _mm([(sv['qn'], dqraw)], 'tn', GRAD_WIRE, 'ab_qb_dw'))
    dkvn = _mm([(dkv, w['w_kvb'])], 'nt', F32, 'ab_kvb_dx')
    emit('w_kvb', _mm([(sv['kvn'], dkv)], 'tn', GRAD_WIRE, 'ab_kvb_dw'))
    ab = d.z0_ax // d.CC
    dab, dcv = _rowwise(lambda i, j, dya, cv, a_b: (dya * cv, dya * a_b), d.NT, d.rt, d.nlt,
                        [_row(dymix, 0, d.CC), _row(sv['cv']), _row(z, ab + 1, d.CC)],
                        [('row', d.CC, BF16), ('row', d.CC, F32)], 'ab_mix_bwd')
    ds = _dwconv(dcv, _w8(w['conv_a'][::-1]), d, F32, 'ab_conv_dx')
    conv_g = _dwconv_wgrad(sv['s'], dcv, d, 'ab_conv_dw')

    def assemble(i, j, dqn, dkvn, dkr, zq, zkv, qw, kw, cos, sin_s, ds, ax, ac, dab):
        _, xq, rq = _rms(zq, qw)
        dzq, dqw = _rms_bwd(dqn, xq, rq, qw)
        _, xk, rk = _rms(zkv, kw)
        dzkv, dkw = _rms_bwd(dkvn, xk, rk, kw)
        parts = [dzq, dzkv, _rope_t(dkr, cos, sin_s)]
        if d.z0_pad:
            parts.append(jnp.zeros((dzq.shape[0], d.z0_pad), F32))
        parts += [ds * ac, dab.astype(F32), ds * ax]
        return jnp.concatenate([t.astype(BF16) for t in parts], axis=1), dqw, dkw
    dz, dqw, dkw = _rowwise(assemble, d.NT, d.rt, d.nlt,
                            [_row(dqn), _row(dkvn), _row(dkr), _row(z, 0, d.QL), _row(z, d.z0_kv // d.KVL, d.KVL),
                             ('full', qnw), ('full', kvnw), _row(cos), _row(sin_s), _row(ds), _row(z, ab, d.CC),
                             _row(z, ab + 2, d.CC), _row(dab)],
                            [('row', d.ZW0, BF16), ('acc', (1, d.QL)), ('acc', (1, d.KVL))], 'ab_dz')
    return dz, dict(conv_a=conv_g[0:3], q_norm=dqw, kv_norm=dkw)


def _log_sigmoid(x):
    return jnp.minimum(x, 0.0) - jnp.log(1.0 + jnp.exp(-jnp.abs(x)))


def _gla_fwd_block(z, w, d):
    wg, bg, onw = w['gate_w'], w['gate_b'], w['o_norm']

    def gates(i, j, lr, wg, bg):
        return (_log_sigmoid(_dot(lr, wg, 'nn') + bg) / GATE_NORMALIZER,)
    g = _rowwise(gates, d.NT, d.rt, d.nlt, [_row(z, d.z1_lr // LANE, LANE), ('full', wg), ('full', bg)],
                 [('row', 2 * d.KEY, F32)], 'gla_gates')[0]
    of, stf, ob, stb = _gla_fwd(z, g, d)

    def outp(i, j, of, ob, og, ow):
        o = of + ob
        parts = [_rms(o[:, h * d.dv:(h + 1) * d.dv], ow)[0] for h in range(d.GH)]
        return (jnp.concatenate(parts, axis=1) * _silu(og),)
    ymix = _rowwise(outp, d.NT, d.rt, d.nlt, [_row(of), _row(ob), _row(z, d.z1_og // d.VAL, d.VAL), ('full', onw)],
                    [('row', d.VAL, BF16)], 'gla_out')[0]
    return ymix, dict(g=g, of=of, ob=ob, stf=stf, stb=stb)


def _gla_bwd_block(dymix, z, sv, w, d, rides):
    wg, bg, onw = w['gate_w'], w['gate_b'], w['o_norm']

    def outp_bwd(i, j, dy, of, ob, og, ow):
        o = of + ob
        dn = dy * _silu(og)
        dos, ns = [], []
        dow = jnp.zeros((1, d.dv), F32)
        for h in range(d.GH):
            hs = slice(h * d.dv, (h + 1) * d.dv)
            n, xh, r = _rms(o[:, hs], ow)
            do_h, dw_h = _rms_bwd(dn[:, hs], xh, r, ow)
            dos.append(do_h)
            ns.append(n)
            dow = dow + dw_h
        return jnp.concatenate(dos, axis=1), dy * jnp.concatenate(ns, axis=1) * _dsilu(og), dow
    do, dog, dow = _rowwise(outp_bwd, d.NT, d.rt, d.nlt,
                            [_row(dymix), _row(sv['of']), _row(sv['ob']), _row(z, d.z1_og // d.VAL, d.VAL), ('full', onw)],
                            [('row', d.VAL, F32), ('row', d.VAL, BF16), ('acc', (1, d.dv))], 'gla_out_bwd')
    (dq0, dk0, dv0, dg0), (dq1, dk1, dv1, dg1) = _gla_bwd(z, sv['g'], do, sv['stf'], sv['stb'], d, rides)

    def assemble(i, j, dg0, dg1, lr, wg, bg, dq0, dq1, dk0, dk1, dv0, dv1, dog):
        pre = _dot(lr, wg, 'nn') + bg
        e = jnp.exp(-jnp.abs(pre))
        dpre = jnp.concatenate([dg0, dg1], axis=1) * jnp.where(pre >= 0, e, 1.0) / (1.0 + e) / GATE_NORMALIZER
        dlr = _dot(dpre, wg, 'nt')
        parts = [dv0 + dv1, dog.astype(F32), dk0 + dk1, dq0 + dq1, dlr]
        return (jnp.concatenate([t.astype(BF16) for t in parts], axis=1), _dot(lr, dpre, 'tn'),
                jnp.sum(dpre, axis=0, keepdims=True))
    dz, dwg, dbg = _rowwise(assemble, d.NT, d.rt, d.nlt,
                            [_row(dg0), _row(dg1), _row(z, d.z1_lr // LANE, LANE), ('full', wg), ('full', bg), _row(dq0),
                             _row(dq1), _row(dk0), _row(dk1), _row(dv0), _row(dv1), _row(dog)],
                            [('row', d.ZW1, BF16), ('acc', (LANE, 2 * d.KEY)), ('acc', (1, 2 * d.KEY))], 'gla_dz')
    return dz, dict(gate_fw_w=dwg[:GATE_RANK, :d.KEY], gate_bw_w=dwg[GATE_RANK:2 * GATE_RANK, d.KEY:],
                    gate_fw_b=dbg[:, :d.KEY], gate_bw_b=dbg[:, d.KEY:], o_norm=dow)


def _loss_bwd(x, fnw, target, d):
    def body(i, j, x, w, tgt):
        y, xh, r = _rms(x, w)
        e = y - tgt
        dx, dw = _rms_bwd(e * (1.0 / d.D), xh, r, w)
        lat = i < d.nlt
        part = jnp.sum(jnp.sum(e * e, axis=1, keepdims=True), axis=0, keepdims=True) * (0.5 / d.D)
        return (jnp.where(lat, dx, 0.0), jnp.where(lat, jnp.broadcast_to(part, (8, LANE)), 0.0), jnp.where(lat, dw, 0.0))
    return _rowwise(body, d.NT, d.rt, d.nlt, [_row(x), ('full', fnw), ('rowclamp', target, d.nlt - 1, d.D)],
                    [('row', d.D, F32), ('acc', (8, LANE)), ('acc', (1, d.D))], 'loss')


def _layer_fwd(x, mods, w, mixer_fwd, d, tag, rides):
    sh1, sc1, g1, sh2, sc2, g2 = mods
    h = _mod_fwd(x, w['norm1'], sh1, sc1, d, tag + '_mod1')
    z = _mm([(h, _w(w, 'w_in'))], 'nn', F32, tag + '_in', rides=rides)
    ymix, msv = mixer_fwd(z)
    y = _mm([(ymix, _w(w, 'w_out'))], 'nn', F32, tag + '_out')
    x1, h2 = _res_mod_fwd(x, y, g1, w['norm2'], sh2, sc2, d, tag + '_mod2')
    p, a, f = _ffn_fwd(h2, w, d, tag + '_ffn', rides)
    x2 = _res_fwd(x1, f, g2, d, tag + '_res')
    return x2, dict(x=x, h=h, z=z, ymix=ymix, msv=msv, y=y, x1=x1, h2=h2, p=p, a=a, f=f)


def _layer_bwd(dx2, sv, mods, w, mixer_bwd, d, tag, rides, emit):
    sh1, sc1, g1, sh2, sc2, g2 = mods
    df, dg2 = _res_bwd(dx2, sv['f'], g2, d, tag + '_res_bwd')
    dh2, dconv_w, dconv_b = _ffn_bwd(df, sv['h2'], sv['p'], sv['a'], w, d, tag + '_ffn', rides, emit)
    dx1, dy, dg1, dn2, dsh2, dsc2 = _res_mod_bwd(dx2, dh2, sv['x1'], sv['y'], g1, w['norm2'], sh2, sc2, d, tag + '_mod2_bwd')
    dymix = _mm([(dy, w['w_out'])], 'nt', F32, tag + '_out_dx')
    emit('w_out', _mm([(sv['ymix'], dy)], 'tn', GRAD_WIRE, tag + '_out_dw'))
    dz, mg = mixer_bwd(dymix, sv['z'], sv['msv'])
    emit('w_in', _mm([(sv['h'], dz)], 'tn', GRAD_WIRE, tag + '_in_dw', rides=rides))
    dh = _mm([(dz, w['w_in'])], 'nt', F32, tag + '_in_dx', rides=rides)
    dx, dn1, dsh1, dsc1 = _mod_bwd_call(dx1, dh, sv['x'], w['norm1'], sh1, sc1, d, tag + '_mod1_bwd')
    grads = dict(mg, ffn_conv_w=dconv_w, ffn_conv_b=dconv_b, norm1=dn1, norm2=dn2)
    return dx, grads, [dsh1, dsc1, dg1, dsh2, dsc2, dg2]


def _pad_flat(v, mult=LANE):
    v = v.reshape(-1)
    return jnp.pad(v, (0, (-v.shape[0]) % mult))


def _pack(entries, row_mult):
    flat, offs, pos = [], [], 0
    for v in entries:
        f = _pad_flat(v)
        flat.append(f)
        offs.append(pos)
        pos += f.shape[0]
    tot = jnp.concatenate(flat)
    tot = jnp.pad(tot, (0, (-pos) % (row_mult * LANE)))
    return tot.reshape(-1, LANE), offs


def _unpack(packed, offs, shapes):
    flat = packed.reshape(-1)
    out = []
    for off, shp in zip(offs, shapes):
        n = 1
        for s in shp:
            n *= s
        out.append(flat[off:off + n].reshape(shp))
    return out


def _step(a):
    d = _dims()
    dm = d.D
    me = 4 * lax.axis_index("x") + 2 * lax.axis_index("y") + lax.axis_index("c")
    sds = jax.ShapeDtypeStruct

    rides = _Rides()
    fwd_hosts = {'l0_w_out': 'l0_in', 'l0_ffn_up': 'flash_fwd', 'l0_ffn_down': 'flash_fwd', 'l1_ffn_up': 'flash_fwd',
                 'l1_w_out': 'l0_ffn_up', 'l1_w_in': 'l0_ffn_up', 'l1_ffn_down': 'l0_ffn_down'}

    def gathered(name):
        shard = _cast_bf16(a[name], 'cast_' + name)
        if name in fwd_hosts:
            rides.add(fwd_hosts[name], name, shard, False)
            return lambda: rides.done[name]
        g = _exchange(shard, False, 'ag_' + name)
        return lambda: g

    def cols(name, relayout=lambda t: t):
        g = gathered(name)
        k, n = a[name].shape
        return lambda: relayout(g().transpose(1, 0, 2).reshape(k, N_DEV * n))

    def rows(name):
        g = gathered(name)
        return lambda: g().reshape(N_DEV * a[name].shape[0], a[name].shape[1])

    small_names = ['l0_conv_a', 'l0_ffn_conv_w', 'l1_ffn_conv_w', 'l1_gate_fw_w', 'l1_gate_bw_w']
    spack, soffs = _pack([a[n] for n in small_names], 8)
    sg = _exchange(spack, False, 'ag_small')
    small_w = {}
    for n, off in zip(small_names, soffs):
        r, c = a[n].shape
        shards = sg.reshape(N_DEV, -1)[:, off:off + r * c].reshape(N_DEV, r, c)
        small_w[n] = shards.transpose(1, 0, 2).reshape(r, N_DEV * c)

    w0 = dict(norm1=a['l0_norm1'].reshape(1, dm), norm2=a['l0_norm2'].reshape(1, dm),
              w_in=cols('l0_w_in', lambda t: _win0_to_kernel(t, d)), w_qb=cols('l0_w_qb', lambda t: _wqb_to_kernel(t, d))(),
              w_kvb=cols('l0_w_kvb', lambda t: _wkvb_to_kernel(t, d))(), w_out=rows('l0_w_out'),
              q_norm=a['l0_q_norm'].reshape(1, -1), kv_norm=a['l0_kv_norm'].reshape(1, -1), conv_a=small_w['l0_conv_a'],
              ffn_up=cols('l0_ffn_up'), ffn_conv_w=small_w['l0_ffn_conv_w'], ffn_conv_b=a['l0_ffn_conv_b'],
              ffn_down=rows('l0_ffn_down'))
    w1 = dict(norm1=a['l1_norm1'].reshape(1, dm), norm2=a['l1_norm2'].reshape(1, dm),
              w_in=cols('l1_w_in', lambda t: _win1_to_kernel(t, d)), w_out=rows('l1_w_out'),
              gate_w=_gate_weight(small_w['l1_gate_fw_w'], small_w['l1_gate_bw_w'], d),
              gate_b=jnp.concatenate([a['l1_gate_fw_b'], a['l1_gate_bw_b']]).reshape(1, -1),
              o_norm=a['l1_o_norm'].reshape(1, -1),
              ffn_up=cols('l1_ffn_up'), ffn_conv_w=small_w['l1_ffn_conv_w'], ffn_conv_b=a['l1_ffn_conv_b'],
              ffn_down=rows('l1_ffn_down'))

    c8 = _exchange(a['c'], False, 'ag_c').reshape(N_DEV, dm)
    c16 = jnp.concatenate([c8, a['c_ctx'].reshape(1, dm), jnp.zeros((7, dm), F32)], axis=0)
    act16, dact16 = _small(lambda v: (_silu(v), _dsilu(v)), [c16], [sds((16, dm), BF16), sds((16, dm), F32)], 'cond_silu')
    n6 = N_MOD * dm // N_DEV
    mod_sh = [_mm([(act16, a[f'l{l}_ada_w'])], 'nn', F32, f'ada{l}') for l in (0, 1)]
    mod_all = _exchange(jnp.concatenate(mod_sh, axis=1), False, 'ag_mod')
    mods = []
    for l in (0, 1):
        full = mod_all[:, :, l * n6:(l + 1) * n6].transpose(1, 0, 2).reshape(16, N_MOD * dm)
        mine = jnp.concatenate([lax.dynamic_slice_in_dim(full, me, 1, 0), full[8:9]], axis=0)
        m2 = _small(lambda r, b: (r + b,), [mine, a[f'l{l}_ada_b'].reshape(1, -1)], [sds((2, N_MOD * dm), F32)], f'ada{l}_bias')[0]
        mods.append([m2[:, k * dm:(k + 1) * dm].reshape(2, 1, dm) for k in range(N_MOD)])

    cos, sin_s = _rope_tables(d)
    x0 = jnp.concatenate([a['x'][0], a['ctx'][0]], axis=0)
    x2, sv0 = _layer_fwd(x0, mods[0], w0, lambda z: _ab_fwd(z, w0, cos, sin_s, d, rides), d, 'l0', rides)
    x4, sv1 = _layer_fwd(x2, mods[1], w1, lambda z: _gla_fwd_block(z, w1, d), d, 'l1', rides)
    dx4, loss_acc, dfn = _loss_bwd(x4, a['final_norm'].reshape(1, dm), a['loss_target'][0], d)

    bwd_hosts = {'l1_ffn_down': 'l1_ffn_up_dx', 'l1_ffn_up': 'gla_bwd', 'l1_w_out': 'gla_bwd', 'l1_w_in': 'l1_in_dx',
                 'l0_ffn_down': 'l0_ffn_up_dx', 'l0_ffn_up': 'flash_bwd', 'l0_w_out': 'flash_bwd', 'l0_w_qb': 'l0_in_dw',
                 'l0_w_kvb': 'l0_in_dw', 'l0_w_in': 'l0_in_dx'}
    from_kernel = {'l0_w_in': _win0_from_kernel, 'l0_w_qb': _wqb_from_kernel, 'l0_w_kvb': _wkvb_from_kernel,
                   'l1_w_in': _win1_from_kernel}
    slabs = {}

    def emitter(layer):
        def emit(wkey, dw):
            name = f'l{layer}_{wkey}'
            if name in from_kernel:
                dw = from_kernel[name](dw, d)
            if a[name].shape[0] == dw.shape[0]:
                stacked = dw.reshape(dw.shape[0], N_DEV, dw.shape[1] // N_DEV).transpose(1, 0, 2)
            else:
                stacked = dw.reshape(N_DEV, dw.shape[0] // N_DEV, dw.shape[1])
            if name in bwd_hosts:
                rides.add(bwd_hosts[name], 'rs_' + name, stacked, True)
                slabs[name] = lambda: rides.done['rs_' + name]
            else:
                got = _exchange(stacked, True, 'rs_' + name)
                slabs[name] = lambda: got
        return emit

    dx2, g1, dmod1 = _layer_bwd(dx4, sv1, mods[1], w1, lambda dy, z, msv: _gla_bwd_block(dy, z, msv, w1, d, rides), d, 'l1',
                                rides, emitter(1))
    dx0, g0, dmod0 = _layer_bwd(dx2, sv0, mods[0], w0,
                                lambda dy, z, msv: _ab_bwd(dy, z, msv, w0, cos, sin_s, d, rides, emitter(0)), d, 'l0',
                                rides, emitter(0))

    dm_rows = jnp.concatenate([jnp.concatenate([t.reshape(2, dm) for t in dmod], axis=1) for dmod in (dmod0, dmod1)], axis=0)
    dm_all = _exchange(dm_rows, False, 'ag_dmod')
    lat = dm_all[:, 0::2].transpose(1, 0, 2)
    ctxs = dm_all[:, 1::2].transpose(1, 0, 2)

    def ada_prep(lat, ctxs):
        csum = jnp.sum(ctxs, axis=1, keepdims=True)
        row = lax.broadcasted_iota(jnp.int32, (1, 8, 1), 1)
        g16 = jnp.concatenate([lat, jnp.where(row == 0, csum, 0.0)], axis=1)
        return g16, jnp.sum(lat, axis=1, keepdims=True) + csum
    g16, gb = _small(ada_prep, [lat, ctxs], [sds((2, 16, N_MOD * dm), F32), sds((2, 1, N_MOD * dm), F32)], 'ada_bwd_prep')
    g16_sh = [lax.dynamic_slice_in_dim(g16[l], me * n6, n6, 1) for l in (0, 1)]
    grad_ada_w = [_mm([(act16, g16_sh[l])], 'tn', F32, f'ada{l}_dw') for l in (0, 1)]
    dact = _mm([(g16_sh[0], a['l0_ada_w']), (g16_sh[1], a['l1_ada_w'])], 'nt', F32, 'ada_dact')
    dcc = _small(lambda t, s: (t * s,), [dact[8:9], dact16[8:9]], [sds((1, dm), F32)], 'cctx_grad')[0]

    res = {}
    for name in slabs:
        res[name] = _adam(slabs[name](), a[name], a['m_' + name], a['v_' + name], 'adam_' + name)
    for l in (0, 1):
        name = f'l{l}_ada_w'
        res[name] = _adam(grad_ada_w[l][None], a[name], a['m_' + name], a['v_' + name], 'adam_' + name)

    part = {'loss': loss_acc[0:1, 0:1], 'c_ctx': dcc, 'final_norm': dfn,
            'l0_norm1': g0['norm1'], 'l0_norm2': g0['norm2'], 'l0_q_norm': g0['q_norm'], 'l0_kv_norm': g0['kv_norm'],
            'l0_conv_a': g0['conv_a'], 'l0_ffn_conv_w': g0['ffn_conv_w'], 'l0_ffn_conv_b': g0['ffn_conv_b'],
            'l1_norm1': g1['norm1'], 'l1_norm2': g1['norm2'], 'l1_o_norm': g1['o_norm'],
            'l1_gate_fw_w': g1['gate_fw_w'], 'l1_gate_bw_w': g1['gate_bw_w'], 'l1_gate_fw_b': g1['gate_fw_b'],
            'l1_gate_bw_b': g1['gate_bw_b'], 'l1_ffn_conv_w': g1['ffn_conv_w'], 'l1_ffn_conv_b': g1['ffn_conv_b']}
    pkeys = list(part)
    ppack, poffs = _pack([part[k] for k in pkeys], 8)
    psum = _sum_parts(_exchange(ppack, False, 'ag_small_grads'), 'sum_small_grads')
    tot = dict(zip(pkeys, _unpack(psum, poffs, [part[k].shape for k in pkeys])))
    loss = tot['loss'].reshape(())
    sgrad = {}
    for n in _WEIGHTS:
        if n in res:
            continue
        if n.endswith('ada_b'):
            sgrad[n] = gb[int(n[1])].reshape(a[n].shape)
        elif n in small_names:
            c = a[n].shape[1]
            sgrad[n] = lax.dynamic_slice_in_dim(tot[n], me * c, c, 1)
        else:
            sgrad[n] = tot[n].reshape(a[n].shape)
    snames = list(sgrad)
    packs = [_pack([src[n] for n in snames], 8)[0] for src in
             (sgrad, {n: a[n] for n in snames}, {n: a['m_' + n] for n in snames}, {n: a['v_' + n] for n in snames})]
    offs = _pack([sgrad[n] for n in snames], 8)[1]
    outs = _adam(packs[0][None], packs[1], packs[2], packs[3], 'adam_small')
    for k in range(4):
        for n, val in zip(snames, _unpack(outs[k], offs, [a[n].shape for n in snames])):
            res.setdefault(n, [None] * 4)[k] = val

    grad_x = dx0[:d.T].reshape(1, d.T, dm)
    return (loss, grad_x, *[res[n][0] for n in _WEIGHTS], *[res[n][1] for n in _WEIGHTS], *[res[n][2] for n in _WEIGHTS],
            *[res[n][3] for n in _WEIGHTS])


def kernel(*args):
    return _step(dict(zip(_ARGS, args, strict=True)))
```

```python
import functools
import types

import jax
import jax.numpy as jnp
from jax import lax
from jax.experimental import pallas as pl
from jax.experimental.pallas import tpu as pltpu

D_MODEL = 2048
SEQ = 8192
GRID_W = 64
CTX_LEN = 256
EPS = 1e-6
N_MOD = 6
MLA_HEADS = 8
QK_NOPE = 128
QK_ROPE = 64
V_HEAD = 128
Q_LORA = 512
KV_LORA = 256
ROPE_THETA = 10000.0
GLA_HEADS = 4
GATE_RANK = 16
GATE_NORMALIZER = 16.0
CHUNK = 64
D_FF = 5632
ADAM_LR = 0.001
ADAM_B1 = 0.9
ADAM_B2 = 0.999
ADAM_EPS = 1e-08
ADAM_WD = 0.01
ADAM_STEP = 10

N_DEV = 8
LANE = 128
VMEM_LIMIT = 56 * 1024 * 1024
CONV_COLS = 2816
GLU_COLS = 1408
LN2 = 0.6931471805599453
ATTN_SCALE = float(QK_NOPE + QK_ROPE) ** -0.5

F32 = jnp.float32
BF16 = jnp.bfloat16
GRAD_WIRE = jnp.bfloat16
MESH_ID = pl.DeviceIdType.MESH

_FWD = ['x', 'c', 'ctx', 'c_ctx', 'l0_ada_w', 'l0_ada_b', 'l0_norm1', 'l0_w_in', 'l0_conv_a', 'l0_q_norm', 'l0_w_qb',
        'l0_kv_norm', 'l0_w_kvb', 'l0_w_out', 'l0_norm2', 'l0_ffn_up', 'l0_ffn_conv_w', 'l0_ffn_conv_b', 'l0_ffn_down',
        'l1_ada_w', 'l1_ada_b', 'l1_norm1', 'l1_w_in', 'l1_gate_fw_w', 'l1_gate_fw_b', 'l1_gate_bw_w', 'l1_gate_bw_b',
        'l1_o_norm', 'l1_w_out', 'l1_norm2', 'l1_ffn_up', 'l1_ffn_conv_w', 'l1_ffn_conv_b', 'l1_ffn_down', 'final_norm']
_WEIGHTS = _FWD[3:]
_ARGS = _FWD + ['loss_target'] + ['m_' + n for n in _WEIGHTS] + ['v_' + n for n in _WEIGHTS]


def _dims():
    d = types.SimpleNamespace()
    d.D, d.T, d.TC = D_MODEL, SEQ, CTX_LEN
    d.NT = d.T + d.TC
    d.rt = 256 if d.TC % 256 == 0 else 128
    d.nlt = d.T // d.rt
    d.H = MLA_HEADS
    d.QL, d.KVL = Q_LORA, KV_LORA
    d.CC = D_MODEL // 2
    d.z0_kv = d.QL
    d.z0_kr = d.QL + d.KVL
    d.z0_pad = (-(d.QL + d.KVL + LANE)) % d.CC
    d.z0_ax = d.QL + d.KVL + LANE + d.z0_pad
    d.ZW0 = d.z0_ax + 3 * d.CC
    d.AB_COLS = d.KVL + QK_ROPE + d.QL + 3 * d.CC
    d.HQ = d.H * LANE
    d.GH = GLA_HEADS
    d.KEY = D_MODEL // 2
    d.VAL = D_MODEL
    d.dk = d.KEY // d.GH
    d.dv = d.VAL // d.GH
    d.z1_og = d.VAL
    d.z1_k = 2 * d.VAL
    d.z1_q = 2 * d.VAL + d.KEY
    d.z1_lr = 2 * d.VAL + 2 * d.KEY
    d.ZW1 = d.z1_lr + LANE
    d.GLA_COLS = 2 * d.KEY + 2 * d.VAL + 2 * GATE_RANK
    d.FF = D_FF
    d.tq = min(256, d.TC)
    return d


def _tile(n, pref, align=LANE):
    if n <= pref:
        return n
    t = (pref // align) * align
    while t >= align:
        if n % t == 0:
            return t
        t -= align
    raise ValueError(f"no tile for {n}")


def _cparams(sem):
    return pltpu.CompilerParams(dimension_semantics=sem, vmem_limit_bytes=VMEM_LIMIT)


def _dot(a, b, mode):
    dims = {'nn': (((1,), (0,)), ((), ())), 'nt': (((1,), (1,)), ((), ())), 'tn': (((0,), (0,)), ((), ()))}[mode]
    return lax.dot_general(a.astype(BF16), b.astype(BF16), dims, preferred_element_type=F32)


class _Rides:
    def __init__(self):
        self.pending, self.done = {}, {}

    def add(self, host, key, x, stacked):
        self.pending.setdefault(host, []).append((key, x, stacked))

    def take(self, host):
        return self.pending.pop(host, [])


def _ride_plumbing(riders):
    n = len(riders)
    if not n:
        return [], [], [], []
    spec = pl.BlockSpec(memory_space=pl.ANY)
    shapes = [jax.ShapeDtypeStruct((N_DEV,) + tuple(x.shape[1:] if st else x.shape), x.dtype) for _, x, st in riders]
    sems = [pltpu.SemaphoreType.DMA((n * (N_DEV - 1),)), pltpu.SemaphoreType.DMA((n * (N_DEV - 1),)),
            pltpu.SemaphoreType.DMA((n,))]
    return [spec] * n, shapes, [spec] * n, sems


def _ride_copies(stacked_flags, x_refs, o_refs, send_sems, recv_sems, local_sems):
    ix, iy, ic = lax.axis_index("x"), lax.axis_index("y"), lax.axis_index("c")
    me = 4 * ix + 2 * iy + ic
    local, sends, recvs = [], [], []
    for r, (stacked, x_ref, o_ref) in enumerate(zip(stacked_flags, x_refs, o_refs)):
        def src(p, x_ref=x_ref, stacked=stacked):
            return x_ref.at[p] if stacked else x_ref

        local.append(pltpu.make_async_copy(src(me), o_ref.at[me], local_sems.at[r]))
        for k in range(1, N_DEV):
            px, py, pc = (ix + ((k >> 2) & 1)) % 2, (iy + ((k >> 1) & 1)) % 2, (ic + (k & 1)) % 2
            peer = 4 * px + 2 * py + pc
            s = r * (N_DEV - 1) + k - 1
            sends.append(pltpu.make_async_remote_copy(
                src_ref=src(peer), dst_ref=o_ref.at[me], send_sem=send_sems.at[s], recv_sem=recv_sems.at[s],
                device_id=(px, py, pc), device_id_type=MESH_ID))
            recvs.append(pltpu.make_async_remote_copy(
                src_ref=src(peer), dst_ref=o_ref.at[peer], send_sem=send_sems.at[s], recv_sem=recv_sems.at[s],
                device_id=(px, py, pc), device_id_type=MESH_ID))

    def start():
        for cp in local + sends:
            cp.start()

    def wait():
        for cp in recvs:
            cp.wait_recv()
        for cp in sends:
            cp.wait_send()
        for cp in local:
            cp.wait()

    return start, wait


def _mm(pairs, mode, out_dtype, name, tm=768, tn=1024, tk=2816, rides=None):
    riders = rides.take(name) if rides is not None else []
    nr = len(riders)
    r_in, r_shapes, r_out, r_sems = _ride_plumbing(riders)
    a0, b0 = pairs[0]
    if mode == 'nn':
        (m, k), n = a0.shape, b0.shape[1]
    elif mode == 'nt':
        (m, k), n = a0.shape, b0.shape[0]
    else:
        (k, m), n = a0.shape, b0.shape[1]
    tm, tn, tk = _tile(m, tm), _tile(n, tn), _tile(k, tk)
    nk = k // tk
    if mode == 'nn':
        a_spec = pl.BlockSpec((tm, tk), lambda i, j, kk: (i, kk))
        b_spec = pl.BlockSpec((tk, tn), lambda i, j, kk: (kk, j))
    elif mode == 'nt':
        a_spec = pl.BlockSpec((tm, tk), lambda i, j, kk: (i, kk))
        b_spec = pl.BlockSpec((tn, tk), lambda i, j, kk: (j, kk))
    else:
        a_spec = pl.BlockSpec((tk, tm), lambda i, j, kk: (kk, i))
        b_spec = pl.BlockSpec((tk, tn), lambda i, j, kk: (kk, j))
    npairs = len(pairs)

    nin = 2 * npairs
    gi, gj = m // tm, n // tn

    def body(*refs):
        o_ref, acc_ref = refs[nin + nr], refs[nin + 2 * nr + 1]
        i, j, kk = pl.program_id(0), pl.program_id(1), pl.program_id(2)
        if nr:
            start, wait = _ride_copies([st for _, _, st in riders], refs[nin:nin + nr], refs[nin + nr + 1:nin + 2 * nr + 1],
                                       *refs[nin + 2 * nr + 2:])
            pl.when(jnp.logical_and(jnp.logical_and(i == 0, j == 0), kk == 0))(start)

        def dots():
            s = None
            for p in range(npairs):
                t = _dot(refs[2 * p][...], refs[2 * p + 1][...], mode)
                s = t if s is None else s + t
            return s

        if nk == 1:
            o_ref[...] = dots().astype(o_ref.dtype)
        else:
            @pl.when(kk == 0)
            def _():
                acc_ref[...] = dots()

            @pl.when(jnp.logical_and(kk > 0, kk < nk - 1))
            def _():
                acc_ref[...] += dots()

            @pl.when(kk == nk - 1)
            def _():
                o_ref[...] = (acc_ref[...] + dots()).astype(o_ref.dtype)

        if nr:
            pl.when(jnp.logical_and(jnp.logical_and(i == gi - 1, j == gj - 1), kk == nk - 1))(wait)

    flat = [t for ab in pairs for t in ab]
    res = pl.pallas_call(
        body, name=name,
        out_shape=[jax.ShapeDtypeStruct((m, n), out_dtype)] + r_shapes,
        grid=(gi, gj, nk),
        in_specs=[a_spec, b_spec] * npairs + r_in,
        out_specs=[pl.BlockSpec((tm, tn), lambda i, j, kk: (i, j))] + r_out,
        scratch_shapes=[pltpu.VMEM((tm, tn), F32)] + r_sems,
        compiler_params=_cparams(("arbitrary",) * 3 if nr else ("parallel", "parallel", "arbitrary")),
    )(*flat, *[x for _, x, _ in riders])
    for (key, _, _), arr in zip(riders, res[1:]):
        rides.done[key] = arr
    return res[0]


def _rowwise(body, nrows, rt, nlt, ins, outs, name, ncol=1):
    ntiles = nrows // rt
    in_specs, args = [], []
    for spec in ins:
        kind, arr = spec[0], spec[1]
        if kind == 'row':
            in_specs.append(pl.BlockSpec((rt, spec[3]), functools.partial(lambda i, j, cb: (i, cb), cb=spec[2])))
        elif kind == 'rowc':
            in_specs.append(pl.BlockSpec((rt, spec[3]), functools.partial(lambda i, j, cb: (i, cb + j), cb=spec[2])))
        elif kind == 'rowm':
            in_specs.append(pl.BlockSpec(
                (rt, spec[3]), functools.partial(lambda i, j, cb, md: (i, cb + j % md), cb=spec[2], md=spec[4])))
        elif kind == 'rowclamp':
            in_specs.append(pl.BlockSpec(
                (rt, spec[3]), functools.partial(lambda i, j, mb: (jnp.minimum(i, mb), 0), mb=spec[2])))
        elif kind == 'row3':
            in_specs.append(pl.BlockSpec((arr.shape[0], rt, arr.shape[2]), lambda i, j: (0, i, 0)))
        elif kind == 'seg':
            in_specs.append(pl.BlockSpec((None, 1, arr.shape[2]), lambda i, j: (i // nlt, 0, 0)))
        else:
            in_specs.append(pl.BlockSpec(arr.shape, functools.partial(lambda i, j, nd: (0,) * nd, nd=arr.ndim)))
        args.append(arr)
    out_shapes, out_specs = [], []
    for spec in outs:
        kind = spec[0]
        if kind == 'row':
            out_shapes.append(jax.ShapeDtypeStruct((nrows, spec[1]), spec[2]))
            out_specs.append(pl.BlockSpec((rt, spec[1]), lambda i, j: (i, 0)))
        elif kind == 'rowc':
            out_shapes.append(jax.ShapeDtypeStruct((nrows, spec[1]), spec[3]))
            out_specs.append(pl.BlockSpec((rt, spec[2]), lambda i, j: (i, j)))
        elif kind == 'acc':
            out_shapes.append(jax.ShapeDtypeStruct(spec[1], F32))
            out_specs.append(pl.BlockSpec(spec[1], functools.partial(lambda i, j, nd: (0,) * nd, nd=len(spec[1]))))
        else:
            out_shapes.append(jax.ShapeDtypeStruct((2, 1, spec[1]), F32))
            out_specs.append(pl.BlockSpec((None, 1, spec[1]), lambda i, j: (i // nlt, 0, 0)))
    n_in = len(ins)
    has_acc = any(s[0] in ('acc', 'segacc') for s in outs)

    def kern(*refs):
        i = pl.program_id(0)
        j = pl.program_id(1)
        vals = [r[...] for r in refs[:n_in]]
        res = body(i, j, *vals)
        for spec, ref, val in zip(outs, refs[n_in:], res):
            if spec[0] in ('row', 'rowc'):
                ref[...] = val.astype(ref.dtype)
            else:
                first = (i == 0) if spec[0] == 'acc' else jnp.logical_or(i == 0, i == nlt)

                @pl.when(first)
                def _(ref=ref, val=val):
                    ref[...] = val

                @pl.when(jnp.logical_not(first))
                def _(ref=ref, val=val):
                    ref[...] += val

    return pl.pallas_call(
        kern, name=name, out_shape=tuple(out_shapes), grid=(ntiles, ncol),
        in_specs=in_specs, out_specs=tuple(out_specs),
        compiler_params=_cparams(("arbitrary", "arbitrary") if has_acc else ("parallel", "parallel")),
    )(*args)


def _small(body, args, out_shapes, name):
    n_in = len(args)

    def kern(*refs):
        res = body(*[r[...] for r in refs[:n_in]])
        for ref, val in zip(refs[n_in:], res):
            ref[...] = val.astype(ref.dtype)

    return pl.pallas_call(
        kern, name=name, out_shape=tuple(out_shapes),
        in_specs=[pl.BlockSpec(memory_space=pltpu.VMEM)] * n_in,
        out_specs=tuple(pl.BlockSpec(memory_space=pltpu.VMEM) for _ in out_shapes),
        compiler_params=pltpu.CompilerParams(vmem_limit_bytes=VMEM_LIMIT),
    )(*args)


def _exchange(x, stacked, name):
    r_in, r_shapes, r_out, r_sems = _ride_plumbing([(name, x, stacked)])

    def body(x_ref, o_ref, send_sems, recv_sems, local_sems):
        start, wait = _ride_copies([stacked], [x_ref], [o_ref], send_sems, recv_sems, local_sems)
        start()
        wait()

    return pl.pallas_call(body, name=name, out_shape=r_shapes[0], in_specs=r_in, out_specs=r_out[0], scratch_shapes=r_sems)(x)


def _cast_bf16(x, name):
    r, c = x.shape
    tr = _tile(r, 256, 8)
    return pl.pallas_call(
        lambda x_ref, o_ref: o_ref.__setitem__(Ellipsis, x_ref[...].astype(BF16)), name=name,
        out_shape=jax.ShapeDtypeStruct((r, c), BF16), grid=(r // tr,),
        in_specs=[pl.BlockSpec((tr, c), lambda i: (i, 0))], out_specs=pl.BlockSpec((tr, c), lambda i: (i, 0)),
        compiler_params=_cparams(("parallel",)),
    )(x)


def _adam(parts, w, m, v, name):
    p, r, c = parts.shape
    tr = _tile(r, 64, 8)

    def body(p_ref, w_ref, m_ref, v_ref, g_ref, d_ref, nm_ref, nv_ref):
        g = p_ref[0].astype(F32)
        for q in range(1, p):
            g = g + p_ref[q].astype(F32)
        nm = ADAM_B1 * m_ref[...] + (1.0 - ADAM_B1) * g
        nv = ADAM_B2 * v_ref[...] + (1.0 - ADAM_B2) * (g * g)
        m_hat = nm / (1.0 - ADAM_B1 ** ADAM_STEP)
        v_hat = nv / (1.0 - ADAM_B2 ** ADAM_STEP)
        g_ref[...] = g
        d_ref[...] = -ADAM_LR * (m_hat / (jnp.sqrt(v_hat) + ADAM_EPS) + ADAM_WD * w_ref[...])
        nm_ref[...] = nm
        nv_ref[...] = nv

    spec = pl.BlockSpec((tr, c), lambda i: (i, 0))
    return pl.pallas_call(
        body, name=name, out_shape=tuple(jax.ShapeDtypeStruct((r, c), F32) for _ in range(4)), grid=(r // tr,),
        in_specs=[pl.BlockSpec((p, tr, c), lambda i: (0, i, 0)), spec, spec, spec], out_specs=(spec,) * 4,
        compiler_params=_cparams(("parallel",)),
    )(parts, w, m, v)


def _sum_parts(parts, name):
    p, r, c = parts.shape
    tr = _tile(r, 256, 8)

    def body(p_ref, o_ref):
        g = p_ref[0]
        for q in range(1, p):
            g = g + p_ref[q]
        o_ref[...] = g

    return pl.pallas_call(
        body, name=name, out_shape=jax.ShapeDtypeStruct((r, c), F32), grid=(r // tr,),
        in_specs=[pl.BlockSpec((p, tr, c), lambda i: (0, i, 0))], out_specs=pl.BlockSpec((tr, c), lambda i: (i, 0)),
        compiler_params=_cparams(("parallel",)),
    )(parts)


def _shifted(cur, prev8, next8, i, rt, nlt, ntiles):
    first = jnp.logical_or(i == 0, i == nlt)
    last = jnp.logical_or(i == nlt - 1, i == ntiles - 1)
    prev_row = jnp.where(first, 0.0, prev8[7:8, :])
    next_row = jnp.where(last, 0.0, next8[0:1, :])
    rows = lax.broadcasted_iota(jnp.int32, (rt, 1), 0)
    x_m1 = jnp.where(rows == 0, prev_row, pltpu.roll(cur, 1, 0))
    x_p1 = jnp.where(rows == rt - 1, next_row, pltpu.roll(cur, rt - 1, 0))
    return x_m1, x_p1


def _halo_specs(rt, tc, nrows, col_axis_first):
    r8 = rt // 8
    last8 = nrows // 8 - 1
    if col_axis_first:
        return [pl.BlockSpec((8, tc), lambda j, i: (jnp.maximum(i * r8 - 1, 0), j)),
                pl.BlockSpec((rt, tc), lambda j, i: (i, j)),
                pl.BlockSpec((8, tc), lambda j, i: (jnp.minimum((i + 1) * r8, last8), j))]
    return [pl.BlockSpec((8, tc), lambda i, j: (jnp.maximum(i * r8 - 1, 0), j)),
            pl.BlockSpec((rt, tc), lambda i, j: (i, j)),
            pl.BlockSpec((8, tc), lambda i, j: (jnp.minimum((i + 1) * r8, last8), j))]


def _dwconv(x, w8, d, out_dtype, name):
    nrows, c = x.shape
    rt, nlt = d.rt, d.nlt
    tc = _tile(c, CONV_COLS)
    ntiles = nrows // rt

    def body(p_ref, c_ref, n_ref, w_ref, o_ref):
        i = pl.program_id(0)
        cur = c_ref[...]
        x_m1, x_p1 = _shifted(cur, p_ref[...], n_ref[...], i, rt, nlt, ntiles)
        w = w_ref[...]
        o_ref[...] = (x_m1 * w[0:1] + cur * w[1:2] + x_p1 * w[2:3] + w[3:4]).astype(o_ref.dtype)

    return pl.pallas_call(
        body, name=name, out_shape=jax.ShapeDtypeStruct((nrows, c), out_dtype), grid=(ntiles, c // tc),
        in_specs=_halo_specs(rt, tc, nrows, False) + [pl.BlockSpec((8, tc), lambda i, j: (0, j))],
        out_specs=pl.BlockSpec((rt, tc), lambda i, j: (i, j)),
        compiler_params=_cparams(("parallel", "parallel")),
    )(x, x, x, w8)


def _dwconv_wgrad(x, dy, d, name):
    nrows, c = x.shape
    rt, nlt = d.rt, d.nlt
    tc = _tile(c, CONV_COLS)
    ntiles = nrows // rt

    def body(p_ref, c_ref, n_ref, dy_ref, o_ref):
        i = pl.program_id(1)
        cur = c_ref[...]
        dy = dy_ref[...]
        x_m1, x_p1 = _shifted(cur, p_ref[...], n_ref[...], i, rt, nlt, ntiles)
        sums = [jnp.sum(t * dy, axis=0, keepdims=True) for t in (x_m1, cur, x_p1)] + [jnp.sum(dy, axis=0, keepdims=True)]
        row = lax.broadcasted_iota(jnp.int32, (8, 1), 0)
        part = jnp.zeros((8, tc), F32)
        for k, s in enumerate(sums):
            part = jnp.where(row == k, s, part)

        @pl.when(i == 0)
        def _():
            o_ref[...] = part

        @pl.when(i != 0)
        def _():
            o_ref[...] += part

    return pl.pallas_call(
        body, name=name, out_shape=jax.ShapeDtypeStruct((8, c), F32), grid=(c // tc, ntiles),
        in_specs=_halo_specs(rt, tc, nrows, True) + [pl.BlockSpec((rt, tc), lambda j, i: (i, j))],
        out_specs=pl.BlockSpec((8, tc), lambda j, i: (0, j)),
        compiler_params=_cparams(("parallel", "arbitrary")),
    )(x, x, x, dy)


def _ffn_gate_fwd(p, w8, d, name):
    nrows, ff = p.shape[0], p.shape[1] // 2
    rt, nlt = d.rt, d.nlt
    tc = _tile(ff, GLU_COLS)
    nb = ff // tc
    ntiles = nrows // rt

    def body(gp_ref, gc_ref, gn_ref, vp_ref, vc_ref, vn_ref, wg_ref, wv_ref, o_ref):
        i = pl.program_id(0)
        us = []
        for p_ref, c_ref, n_ref, w_ref in ((gp_ref, gc_ref, gn_ref, wg_ref), (vp_ref, vc_ref, vn_ref, wv_ref)):
            cur, w = c_ref[...], w_ref[...]
            x_m1, x_p1 = _shifted(cur, p_ref[...], n_ref[...], i, rt, nlt, ntiles)
            us.append(x_m1 * w[0:1] + cur * w[1:2] + x_p1 * w[2:3] + w[3:4])
        o_ref[...] = (_silu(us[0]) * us[1]).astype(o_ref.dtype)

    r8, last8 = rt // 8, nrows // 8 - 1

    def halo(off):
        return [pl.BlockSpec((8, tc), lambda i, j: (jnp.maximum(i * r8 - 1, 0), j + off)),
                pl.BlockSpec((rt, tc), lambda i, j: (i, j + off)),
                pl.BlockSpec((8, tc), lambda i, j: (jnp.minimum((i + 1) * r8, last8), j + off))]
    return pl.pallas_call(
        body, name=name, out_shape=jax.ShapeDtypeStruct((nrows, ff), BF16), grid=(ntiles, nb),
        in_specs=halo(0) + halo(nb) + [pl.BlockSpec((8, tc), lambda i, j: (0, j)), pl.BlockSpec((8, tc), lambda i, j: (0, j + nb))],
        out_specs=pl.BlockSpec((rt, tc), lambda i, j: (i, j)),
        compiler_params=_cparams(("parallel", "parallel")),
    )(p, p, p, p, p, p, w8, w8)


def _ffn_gate_bwd(p, da, w8, d, name):
    nrows, ff = da.shape
    rt, nlt = d.rt, d.nlt
    tc = _tile(ff, GLU_COLS)
    nb = ff // tc
    ntiles = nrows // rt
    ext = rt + 16

    def body(gp_ref, gc_ref, gn_ref, vp_ref, vc_ref, vn_ref, ap_ref, ac_ref, an_ref, wg_ref, wv_ref,
             dpg_ref, dpv_ref, cg_ref, cv_ref):
        i = pl.program_id(1)
        first = jnp.logical_or(i == 0, i == nlt)
        last = jnp.logical_or(i == nlt - 1, i == ntiles - 1)

        def extended(p_ref, c_ref, n_ref):
            return jnp.concatenate([jnp.where(first, 0.0, p_ref[...]), c_ref[...], jnp.where(last, 0.0, n_ref[...])], axis=0)

        def conv(x, w):
            return pltpu.roll(x, 1, 0) * w[0:1] + x * w[1:2] + pltpu.roll(x, ext - 1, 0) * w[2:3]

        wg, wv = wg_ref[...], wv_ref[...]
        pg, pv, da_e = extended(gp_ref, gc_ref, gn_ref), extended(vp_ref, vc_ref, vn_ref), extended(ap_ref, ac_ref, an_ref)
        ug = conv(pg, wg) + wg[3:4]
        uv = conv(pv, wv) + wv[3:4]
        sg = _sigmoid(ug)
        dug = da_e * uv * (sg * (1.0 + ug * (1.0 - sg)))
        duv = da_e * (ug * sg)
        row = lax.broadcasted_iota(jnp.int32, (8, 1), 0)
        for p_e, du, w, dp_ref, c_ref in ((pg, dug, wg, dpg_ref, cg_ref), (pv, duv, wv, dpv_ref, cv_ref)):
            dp = pltpu.roll(du, 1, 0) * w[2:3] + du * w[1:2] + pltpu.roll(du, ext - 1, 0) * w[0:1]
            dp_ref[...] = dp[8:rt + 8].astype(dp_ref.dtype)
            du_c = du[8:rt + 8]
            sums = [jnp.sum(t[8:rt + 8] * du_c, axis=0, keepdims=True)
                    for t in (pltpu.roll(p_e, 1, 0), p_e, pltpu.roll(p_e, ext - 1, 0))] + [jnp.sum(du_c, axis=0, keepdims=True)]
            part = jnp.zeros((8, tc), F32)
            for k, s in enumerate(sums):
                part = jnp.where(row == k, s, part)

            @pl.when(i == 0)
            def _(c_ref=c_ref, part=part):
                c_ref[...] = part

            @pl.when(i != 0)
            def _(c_ref=c_ref, part=part):
                c_ref[...] += part

    r8, last8 = rt // 8, nrows // 8 - 1

    def halo(off):
        return [pl.BlockSpec((8, tc), lambda j, i: (jnp.maximum(i * r8 - 1, 0), j + off)),
                pl.BlockSpec((rt, tc), lambda j, i: (i, j + off)),
                pl.BlockSpec((8, tc), lambda j, i: (jnp.minimum((i + 1) * r8, last8), j + off))]
    tile = pl.BlockSpec((rt, tc), lambda j, i: (i, j))
    acc = pl.BlockSpec((8, tc), lambda j, i: (0, j))
    dpg, dpv, cg, cv = pl.pallas_call(
        body, name=name,
        out_shape=(jax.ShapeDtypeStruct((nrows, ff), BF16), jax.ShapeDtypeStruct((nrows, ff), BF16),
                   jax.ShapeDtypeStruct((8, ff), F32), jax.ShapeDtypeStruct((8, ff), F32)),
        grid=(nb, ntiles),
        in_specs=halo(0) + halo(nb) + halo(0) + [acc, pl.BlockSpec((8, tc), lambda j, i: (0, j + nb))],
        out_specs=(tile, tile, acc, acc),
        compiler_params=_cparams(("parallel", "arbitrary")),
    )(p, p, p, p, p, p, da, da, da, w8, w8)
    return dpg, dpv, jnp.concatenate([cg, cv], axis=1)


def _w8(w3, b=None):
    c = w3.shape[1]
    brow = jnp.zeros((1, c), F32) if b is None else b.reshape(1, c)
    return jnp.concatenate([w3, brow, jnp.zeros((4, c), F32)], axis=0)


def _rms(x, w):
    r = lax.rsqrt(jnp.mean(x * x, axis=-1, keepdims=True) + EPS)
    xh = x * r
    return xh * w, xh, r


def _rms_bwd(dy, xh, r, w):
    dxh = dy * w
    dx = r * (dxh - xh * jnp.mean(dxh * xh, axis=-1, keepdims=True))
    return dx, jnp.sum(dy * xh, axis=0, keepdims=True)


def _mod_bwd(dh, x, w, shift, scale):
    n, xh, r = _rms(x, w)
    dshift = jnp.sum(dh, axis=0, keepdims=True)
    dscale = jnp.sum(dh * n, axis=0, keepdims=True)
    dx, dw = _rms_bwd(dh * (1.0 + scale), xh, r, w)
    return dx, dw, dshift, dscale


def _sigmoid(x):
    return 1.0 / (1.0 + jnp.exp(-x))


def _silu(x):
    return x * _sigmoid(x)


def _dsilu(x):
    s = _sigmoid(x)
    return s * (1.0 + x * (1.0 - s))


def _rope(x, cos, sin_s):
    return x * cos + pltpu.roll(x, LANE // 2, 1) * sin_s


def _rope_t(dy, cos, sin_s):
    return dy * cos + pltpu.roll(dy * sin_s, LANE // 2, 1)


def _flash_fwd(q, kcat, kv, d, rides=None):
    riders = rides.take("flash_fwd") if rides is not None else []
    nr = len(riders)
    r_in, r_shapes, r_out, r_sems = _ride_plumbing(riders)
    nt, h, tq = d.NT, d.H, d.tq
    tkb = _tile(d.T, 4096)
    n_big = d.T // tkb
    nq_lat = d.T // tq

    def body(*refs):
        q_ref, k_ref, v_ref = refs[:3]
        o_ref, ob_ref, lse_ref = refs[3 + nr:6 + nr]
        qi = pl.program_id(1)
        if nr:
            start, wait = _ride_copies([st for _, _, st in riders], refs[3:3 + nr], refs[6 + nr:6 + 2 * nr], *refs[6 + 2 * nr:])
            pl.when(jnp.logical_and(pl.program_id(0) == 0, qi == 0))(start)
        q_t = q_ref[...]

        def step(k0, tk, carry):
            m, l, acc = carry
            ks = pl.ds(k0, tk)
            s = _dot(q_t, k_ref[ks, :], 'nt')
            m_new = jnp.maximum(m, jnp.max(s, axis=1, keepdims=True))
            p = jnp.exp2(s - m_new)
            alpha = jnp.exp2(m - m_new)
            return m_new, alpha * l + jnp.sum(p, axis=1, keepdims=True), alpha * acc + _dot(p, v_ref[ks, :], 'nn')

        init = (jnp.full((tq, 1), -1e30, F32), jnp.zeros((tq, 1), F32), jnp.zeros((tq, LANE), F32))
        trips = jnp.where(qi < nq_lat, n_big, 0)
        carry = lax.fori_loop(0, trips, lambda t, c: step(pl.multiple_of(t * tkb, tkb), tkb, c), init)
        m, l, acc = step(d.T, d.TC, carry)
        o = acc / l
        o_ref[...] = o
        ob_ref[...] = o.astype(BF16)
        lse_ref[...] = jnp.broadcast_to(m + jnp.log2(l), (tq, LANE))
        if nr:
            pl.when(jnp.logical_and(pl.program_id(0) == h - 1, qi == nt // tq - 1))(wait)

    out = pl.BlockSpec((tq, LANE), lambda hh, i: (i, hh))
    res = pl.pallas_call(
        body, name="flash_fwd",
        out_shape=[jax.ShapeDtypeStruct((nt, d.HQ), F32), jax.ShapeDtypeStruct((nt, d.HQ), BF16),
                   jax.ShapeDtypeStruct((nt, d.HQ), F32)] + r_shapes,
        grid=(h, nt // tq),
        in_specs=[pl.BlockSpec((tq, 2 * LANE), lambda hh, i: (i, hh)), pl.BlockSpec((nt, 2 * LANE), lambda hh, i: (0, hh)),
                  pl.BlockSpec((nt, LANE), lambda hh, i: (0, h + hh))] + r_in,
        out_specs=[out, out, out] + r_out,
        scratch_shapes=r_sems,
        compiler_params=_cparams(("arbitrary", "arbitrary") if nr else ("parallel", "parallel")),
    )(q, kcat, kv, *[x for _, x, _ in riders])
    for (key, _, _), arr in zip(riders, res[3:]):
        rides.done[key] = arr
    return res[0], res[1], res[2]


def _flash_bwd(q, kcat, kv, do, ld, d, rides=None):
    riders = rides.take("flash_bwd") if rides is not None else []
    nr = len(riders)
    r_in, r_shapes, r_out, r_sems = _ride_plumbing(riders)
    nt, h, tk = d.NT, d.H, d.tq
    tqb =_tile(d.T, 2048)
    n_big = d.T // tqb
    nk_lat = d.T // tk

    def body(*refs):
        q_ref, do_ref, ld_ref, k_ref, v_ref = refs[:5]
        dq_ref, dk_ref, dv_ref = refs[5 + nr:8 + nr]
        kt = pl.program_id(1)
        if nr:
            start, wait = _ride_copies([st for _, _, st in riders], refs[5:5 + nr], refs[8 + nr:8 + 2 * nr], *refs[8 + 2 * nr:])
            pl.when(jnp.logical_and(pl.program_id(0) == 0, kt == 0))(start)
        k_t, v_t = k_ref[...], v_ref[...]

        @pl.when(kt == 0)
        def _():
            dq_ref[...] = jnp.zeros_like(dq_ref)

        def step(q0, tq, carry):
            dk, dv = carry
            qs = pl.ds(q0, tq)
            q_t, do_t, ld_t = q_ref[qs, :], do_ref[qs, :], ld_ref[qs, :]
            p = jnp.exp2(_dot(q_t, k_t, 'nt') - ld_t[:, 0:1])
            ds = p * (_dot(do_t, v_t, 'nt') - ld_t[:, LANE // 2:LANE // 2 + 1])
            dq_ref[qs, :] += _dot(ds, k_t, 'nn')
            return dk + _dot(ds, q_t, 'tn'), dv + _dot(p, do_t, 'tn')

        init = (jnp.zeros((tk, 2 * LANE), F32), jnp.zeros((tk, LANE), F32))
        carry = lax.fori_loop(0, n_big, lambda t, c: step(pl.multiple_of(t * tqb, tqb), tqb, c), init)
        dk_ref[...] = carry[0] * LN2
        dv_ref[...] = carry[1].astype(BF16)

        @pl.when(kt >= nk_lat)
        def _():
            dk, dv = step(d.T, d.TC, carry)
            dk_ref[...] = dk * LN2
            dv_ref[...] = dv.astype(BF16)

        if nr:
            pl.when(jnp.logical_and(pl.program_id(0) == h - 1, kt == nt // tk - 1))(wait)

    res = lambda w: pl.BlockSpec((nt, w), lambda hh, i: (0, hh))
    outs = pl.pallas_call(
        body, name="flash_bwd",
        out_shape=[jax.ShapeDtypeStruct((nt, 2 * d.HQ), F32), jax.ShapeDtypeStruct((nt, 2 * d.HQ), F32),
                   jax.ShapeDtypeStruct((nt, d.HQ), BF16)] + r_shapes,
        grid=(h, nt // tk),
        in_specs=[res(2 * LANE), res(LANE), res(LANE), pl.BlockSpec((tk, 2 * LANE), lambda hh, i: (i, hh)),
                  pl.BlockSpec((tk, LANE), lambda hh, i: (i, h + hh))] + r_in,
        out_specs=[res(2 * LANE), pl.BlockSpec((tk, 2 * LANE), lambda hh, i: (i, hh)),
                   pl.BlockSpec((tk, LANE), lambda hh, i: (i, hh))] + r_out,
        scratch_shapes=r_sems,
        compiler_params=_cparams(("arbitrary", "arbitrary") if nr else ("parallel", "arbitrary")),
    )(q, do, ld, kcat, kv, *[x for _, x, _ in riders])
    for (key, _, _), arr in zip(riders, outs[3:]):
        rides.done[key] = arr
    return outs[0], outs[1], outs[2]


def _tri(dirn):
    r = lax.broadcasted_iota(jnp.int32, (CHUNK, CHUNK), 0)
    c = lax.broadcasted_iota(jnp.int32, (CHUNK, CHUNK), 1)
    return (c <= r) if dirn == 0 else (c >= r)


def _exact_mask_dot(mask_bf16, x):
    hi = x.astype(BF16)
    r1 = x - hi.astype(F32)
    mid = r1.astype(BF16)
    lo = (r1 - mid.astype(F32)).astype(BF16)
    dot = lambda t: lax.dot_general(mask_bf16, t, (((1,), (0,)), ((), ())), preferred_element_type=F32)
    return dot(hi) + dot(mid) + dot(lo)


def _gla_terms(q, k, g, dirn, dk):
    mb = _tri(dirn)
    b = _exact_mask_dot(mb.astype(BF16), g)
    tot = jnp.sum(g, axis=0, keepdims=True)
    qe = q * (float(dk) ** -0.5) * jnp.exp(b)
    ke = k * jnp.exp(-b)
    kd = k * jnp.exp(tot - b)
    att = jnp.where(mb, _dot(qe, ke, 'nt'), 0.0)
    return mb, b, tot, qe, ke, kd, att


def _gla_block_index(d, dirn):
    nb = d.NT // d.rt
    if dirn == 0:
        return lambda s: (s + d.nlt) % nb
    return lambda s: nb - 1 - s


def _gla_rows(j, nsub, dirn):
    r0 = (j if dirn == 0 else nsub - 1 - j) * CHUNK
    return slice(r0, r0 + CHUNK)


def _gla_fwd(z, g, d):
    nt, gh, dk, dv, rb = d.NT, d.GH, d.dk, d.dv, d.rt
    nb, nsub = nt // rb, rb // CHUNK
    qb, kb = d.z1_q // dk, d.z1_k // dk

    def body(*refs):
        @pl.when(pl.program_id(1) == 0)
        def _():
            for dirn in (0, 1):
                refs[12 + dirn][...] = jnp.zeros_like(refs[12 + dirn])

        for dirn in (0, 1):
            q_ref, k_ref, v_ref, g_ref = refs[4 * dirn:4 * dirn + 4]
            o_ref, st_ref, state = refs[8 + 2 * dirn], refs[9 + 2 * dirn], refs[12 + dirn]
            pre = []
            for j in range(nsub):
                rs = _gla_rows(j, nsub, dirn)
                v = v_ref[rs, :]
                _, _, tot, qe, _, kd, att = _gla_terms(q_ref[rs, :], k_ref[rs, :], g_ref[rs, :], dirn, dk)
                o_ref[rs, :] = _dot(att, v, 'nn')
                pre.append((rs, qe, jnp.exp(tot), _dot(v, kd, 'tn')))
            st = state[...]
            for j, (rs, qe, decay, update) in enumerate(pre):
                st_ref[j] = st
                o_ref[rs, :] += _dot(qe, st, 'nt')
                st = st * decay + update
            state[...] = st

    def col(dirn, w, off):
        bidx = _gla_block_index(d, dirn)
        return pl.BlockSpec((rb, w), lambda hh, s: (bidx(s), off + hh))

    in_specs, out_shapes, out_specs = [], [], []
    for dirn in (0, 1):
        in_specs += [col(dirn, dk, qb), col(dirn, dk, kb), col(dirn, dv, 0), col(dirn, dk, dirn * gh)]
        out_shapes += [jax.ShapeDtypeStruct((nt, d.VAL), F32), jax.ShapeDtypeStruct((gh, nb * nsub, dv, dk), F32)]
        out_specs += [col(dirn, dv, 0), pl.BlockSpec((None, nsub, dv, dk), lambda hh, s: (hh, s, 0, 0))]
    return pl.pallas_call(
        body, name="gla_fwd", out_shape=out_shapes, grid=(gh, nb), in_specs=in_specs, out_specs=out_specs,
        scratch_shapes=[pltpu.VMEM((dv, dk), F32), pltpu.VMEM((dv, dk), F32)],
        compiler_params=_cparams(("parallel", "arbitrary")),
    )(z, z, z, g, z, z, z, g)


def _gla_bwd(z, g, do, states_fw, states_bw, d, rides=None):
    riders = rides.take("gla_bwd") if rides is not None else []
    nr = len(riders)
    r_in, r_shapes, r_out, r_sems = _ride_plumbing(riders)
    nt, gh, dk, dv, rb = d.NT, d.GH, d.dk, d.dv, d.rt
    nb, nsub = nt // rb, rb // CHUNK
    qb, kb = d.z1_q // dk, d.z1_k // dk
    qscale = float(dk) ** -0.5
    n_in, n_out = 12, 8

    def body(*refs):
        outs = refs[n_in + nr:n_in + nr + n_out]
        dstates = refs[n_in + n_out + 2 * nr:n_in + n_out + 2 * nr + 2]
        if nr:
            start, wait = _ride_copies([st for _, _, st in riders], refs[n_in:n_in + nr],
                                       refs[n_in + nr + n_out:n_in + n_out + 2 * nr], *refs[n_in + n_out + 2 * nr + 2:])
            pl.when(jnp.logical_and(pl.program_id(0) == 0, pl.program_id(1) == 0))(start)

        @pl.when(pl.program_id(1) == 0)
        def _():
            for dstate in dstates:
                dstate[...] = jnp.zeros_like(dstate)

        for dirn in (0, 1):
            q_ref, k_ref, v_ref, g_ref, do_ref, st_ref = refs[6 * dirn:6 * dirn + 6]
            dq_ref, dk_ref, dv_ref, dg_ref = outs[4 * dirn:4 * dirn + 4]
            dst = dstates[dirn][...]
            for j in reversed(range(nsub)):
                rs = _gla_rows(j, nsub, dirn)
                q, k, v, g_, dout, st = q_ref[rs, :], k_ref[rs, :], v_ref[rs, :], g_ref[rs, :], do_ref[rs, :], st_ref[j]
                mb, b, tot, qe, ke, kd, att = _gla_terms(q, k, g_, dirn, dk)
                etot = jnp.exp(tot)
                datt = jnp.where(mb, _dot(dout, v, 'nt'), 0.0)
                dv_ref[rs, :] = _dot(att, dout, 'tn') + _dot(kd, dst, 'nt')
                dqe = _dot(datt, ke, 'nn') + _dot(dout, st, 'nn')
                dke = _dot(datt, qe, 'tn')
                dkd = _dot(v, dst, 'nn')
                dq_ref[rs, :] = dqe * (qscale * jnp.exp(b))
                dk_ref[rs, :] = dke * jnp.exp(-b) + dkd * jnp.exp(tot - b)
                dkd_kd = dkd * kd
                db = dqe * qe - dke * ke - dkd_kd
                dtot = jnp.sum(dkd_kd, axis=0, keepdims=True) + jnp.sum(dst * st, axis=0, keepdims=True) * etot
                dg_ref[rs, :] = _exact_mask_dot(_tri(1 - dirn).astype(BF16), db) + dtot
                dst = dst * etot + _dot(dout, qe, 'tn')
            dstates[dirn][...] = dst
        if nr:
            pl.when(jnp.logical_and(pl.program_id(0) == gh - 1, pl.program_id(1) == nb - 1))(wait)

    def col(dirn, w, off):
        bfwd = _gla_block_index(d, dirn)
        return pl.BlockSpec((rb, w), lambda hh, s: (bfwd(nb - 1 - s), off + hh))

    in_specs, out_shapes, out_specs = [], [], []
    for dirn in (0, 1):
        in_specs += [col(dirn, dk, qb), col(dirn, dk, kb), col(dirn, dv, 0), col(dirn, dk, dirn * gh), col(dirn, dv, 0),
                     pl.BlockSpec((None, nsub, dv, dk), lambda hh, s: (hh, nb - 1 - s, 0, 0))]
        out_shapes += [jax.ShapeDtypeStruct((nt, d.KEY), F32), jax.ShapeDtypeStruct((nt, d.KEY), F32),
                       jax.ShapeDtypeStruct((nt, d.VAL), F32), jax.ShapeDtypeStruct((nt, d.KEY), F32)]
        out_specs += [col(dirn, dk, 0), col(dirn, dk, 0), col(dirn, dv, 0), col(dirn, dk, 0)]
    res = pl.pallas_call(
        body, name="gla_bwd", out_shape=out_shapes + r_shapes, grid=(gh, nb), in_specs=in_specs + r_in,
        out_specs=out_specs + r_out,
        scratch_shapes=[pltpu.VMEM((dv, dk), F32), pltpu.VMEM((dv, dk), F32)] + r_sems,
        compiler_params=_cparams(("arbitrary", "arbitrary") if nr else ("parallel", "arbitrary")),
    )(z, z, z, g, do, states_fw, z, z, z, g, do, states_bw, *[x for _, x, _ in riders])
    for (key, _, _), arr in zip(riders, res[n_out:]):
        rides.done[key] = arr
    return res[0:4], res[4:8]


def _rope_pad(w):
    q = QK_ROPE // 4
    a1, a2, b1, b2 = (w[..., k * q:(k + 1) * q] for k in range(4))
    z = jnp.zeros(w.shape[:-1] + (LANE // 2 - 2 * q,), w.dtype)
    return jnp.concatenate([a1, b1, z, a2, b2, z], axis=-1)


def _rope_unpad(g):
    q = QK_ROPE // 4
    h = LANE // 2
    return jnp.concatenate([g[..., 0:q], g[..., h:h + q], g[..., q:2 * q], g[..., h + q:h + 2 * q]], axis=-1)


def _win0_to_kernel(w, d):
    kv_lat = w[:, :d.KVL]
    k_rope = w[:, d.KVL:d.KVL + QK_ROPE]
    q_lat = w[:, d.KVL + QK_ROPE:d.KVL + QK_ROPE + d.QL]
    rest = w[:, d.KVL + QK_ROPE + d.QL:]
    parts = [q_lat, kv_lat, _rope_pad(k_rope)]
    if d.z0_pad:
        parts.append(jnp.zeros((w.shape[0], d.z0_pad), w.dtype))
    return jnp.concatenate(parts + [rest], axis=1)


def _win0_from_kernel(g, d):
    return jnp.concatenate([g[:, d.z0_kv:d.z0_kv + d.KVL], _rope_unpad(g[:, d.z0_kr:d.z0_kr + LANE]), g[:, :d.QL],
                            g[:, d.z0_ax:]], axis=1)


def _wqb_to_kernel(w, d):
    wr = w.reshape(d.QL, d.H, QK_NOPE + QK_ROPE)
    return jnp.concatenate([wr[:, :, :QK_NOPE], _rope_pad(wr[:, :, QK_NOPE:])], axis=2).reshape(d.QL, 2 * d.HQ)


def _wqb_from_kernel(g, d):
    gr = g.reshape(d.QL, d.H, QK_NOPE + LANE)
    return jnp.concatenate([gr[:, :, :QK_NOPE], _rope_unpad(gr[:, :, QK_NOPE:])], axis=2).reshape(d.QL, d.H * (QK_NOPE + QK_ROPE))


def _wkvb_to_kernel(w, d):
    return w.reshape(d.KVL, d.H, 2, LANE).transpose(0, 2, 1, 3).reshape(d.KVL, 2 * d.HQ)


def _wkvb_from_kernel(g, d):
    return g.reshape(d.KVL, 2, d.H, LANE).transpose(0, 2, 1, 3).reshape(d.KVL, 2 * d.HQ)


def _win1_to_kernel(w, d):
    k = w[:, :d.KEY]
    v = w[:, d.KEY:d.KEY + d.VAL]
    lr = w[:, d.KEY + d.VAL:d.KEY + d.VAL + 2 * GATE_RANK]
    q = w[:, d.KEY + d.VAL + 2 * GATE_RANK:2 * d.KEY + d.VAL + 2 * GATE_RANK]
    og = w[:, 2 * d.KEY + d.VAL + 2 * GATE_RANK:]
    return jnp.concatenate([v, og, k, q, lr, jnp.zeros((w.shape[0], LANE - 2 * GATE_RANK), w.dtype)], axis=1)


def _win1_from_kernel(g, d):
    return jnp.concatenate([g[:, d.z1_k:d.z1_k + d.KEY], g[:, :d.VAL], g[:, d.z1_lr:d.z1_lr + 2 * GATE_RANK],
                            g[:, d.z1_q:d.z1_q + d.KEY], g[:, d.z1_og:d.z1_og + d.VAL]], axis=1)


def _gate_weight(fw_w, bw_w, d):
    z = jnp.zeros((GATE_RANK, d.KEY), F32)
    return jnp.concatenate([jnp.concatenate([fw_w, z], axis=1), jnp.concatenate([z, bw_w], axis=1),
                            jnp.zeros((LANE - 2 * GATE_RANK, 2 * d.KEY), F32)], axis=0)


def _rope_tables(d):
    t = jnp.arange(d.T)
    inv = ROPE_THETA ** (-jnp.arange(0, QK_ROPE // 2, 2, dtype=F32) / (QK_ROPE // 2))
    ar = (t // GRID_W).astype(F32)[:, None] * inv
    ac = (t % GRID_W).astype(F32)[:, None] * inv
    z = jnp.zeros((d.T, LANE // 2 - 2 * inv.shape[0]), F32)
    ang = jnp.concatenate([ar, ac, z, ar, ac, z], axis=1)
    cos = jnp.concatenate([jnp.cos(ang), jnp.ones((d.TC, LANE), F32)], axis=0)
    sin = jnp.concatenate([jnp.sin(ang), jnp.zeros((d.TC, LANE), F32)], axis=0)
    sgn = jnp.where(jnp.arange(LANE) < LANE // 2, -1.0, 1.0).astype(F32)
    return cos, sin * sgn


def _row(arr, cb=0, width=None):
    return ('row', arr, cb, arr.shape[1] if width is None else width)


def _mod_fwd(x, nw, shift, scale, d, name):
    def body(i, j, x, w, sh, sc):
        return (_rms(x, w)[0] * (1.0 + sc) + sh,)
    return _rowwise(body, d.NT, d.rt, d.nlt, [_row(x), ('full', nw), ('seg', shift), ('seg', scale)],
                    [('row', d.D, BF16)], name)[0]


def _mod_bwd_call(dres, dh, x, nw, shift, scale, d, name):
    def body(i, j, dres, dh, x, w, sh, sc):
        dx, dw, dsh, dsc = _mod_bwd(dh, x, w, sh, sc)
        return dres + dx, dw, dsh, dsc
    return _rowwise(body, d.NT, d.rt, d.nlt, [_row(dres), _row(dh), _row(x), ('full', nw), ('seg', shift), ('seg', scale)],
                    [('row', d.D, F32), ('acc', (1, d.D)), ('segacc', d.D), ('segacc', d.D)], name)


def _res_mod_fwd(x, y, gate, nw, shift, scale, d, name):
    def body(i, j, x, y, g, w, sh, sc):
        x1 = x + g * y
        return x1, _rms(x1, w)[0] * (1.0 + sc) + sh
    return _rowwise(body, d.NT, d.rt, d.nlt, [_row(x), _row(y), ('seg', gate), ('full', nw), ('seg', shift), ('seg', scale)],
                    [('row', d.D, F32), ('row', d.D, BF16)], name)


def _res_mod_bwd(dx2, dh2, x1, y, gate, nw, shift, scale, d, name):
    def body(i, j, dx2, dh2, x1, y, g, w, sh, sc):
        dx, dw, dsh, dsc = _mod_bwd(dh2, x1, w, sh, sc)
        dx1 = dx2 + dx
        return dx1, g * dx1, jnp.sum(dx1 * y, axis=0, keepdims=True), dw, dsh, dsc
    return _rowwise(body, d.NT, d.rt, d.nlt,
                    [_row(dx2), _row(dh2), _row(x1), _row(y), ('seg', gate), ('full', nw), ('seg', shift), ('seg', scale)],
                    [('row', d.D, F32), ('row', d.D, BF16), ('segacc', d.D), ('acc', (1, d.D)), ('segacc', d.D),
                     ('segacc', d.D)], name)


def _res_fwd(x1, f, gate, d, name):
    return _rowwise(lambda i, j, x1, f, g: (x1 + g * f,), d.NT, d.rt, d.nlt, [_row(x1), _row(f), ('seg', gate)],
                    [('row', d.D, F32)], name)[0]


def _res_bwd(dx2, f, gate, d, name):
    def body(i, j, dx2, f, g):
        return g * dx2, jnp.sum(dx2 * f, axis=0, keepdims=True)
    return _rowwise(body, d.NT, d.rt, d.nlt, [_row(dx2), _row(f), ('seg', gate)], [('row', d.D, BF16), ('segacc', d.D)], name)


def _w(w, key):
    if callable(w[key]):
        w[key] = w[key]()
    return w[key]


def _ffn_fwd(h2, w, d, tag, rides):
    p = _mm([(h2, _w(w, 'ffn_up'))], 'nn', F32, tag + '_up', rides=rides)
    a = _ffn_gate_fwd(p, _w8(w['ffn_conv_w'], w['ffn_conv_b']), d, tag + '_gate')
    f = _mm([(a, _w(w, 'ffn_down'))], 'nn', F32, tag + '_down', rides=rides)
    return p, a, f


def _ffn_bwd(df, h2, p, a, w, d, tag, rides, emit):
    w_up = w['ffn_up']
    da = _mm([(df, w['ffn_down'])], 'nt', F32, tag + '_down_dx', tn=GLU_COLS, rides=rides)
    emit('ffn_down', _mm([(a, df)], 'tn', GRAD_WIRE, tag + '_down_dw'))
    dpg, dpv, conv_g = _ffn_gate_bwd(p, da, _w8(w['ffn_conv_w'], w['ffn_conv_b']), d, tag + '_gate_bwd')
    dh2 = _mm([(dpg, w_up[:, :d.FF]), (dpv, w_up[:, d.FF:])], 'nt', F32, tag + '_up_dx', rides=rides)
    emit('ffn_up', jnp.concatenate([_mm([(h2, dpg)], 'tn', GRAD_WIRE, tag + '_up_dw_gate', tn=GLU_COLS),
                                    _mm([(h2, dpv)], 'tn', GRAD_WIRE, tag + '_up_dw_val', tn=GLU_COLS)], axis=1))
    return dh2, conv_g[0:3], conv_g[3]


def _ab_fwd(z, w, cos, sin_s, d, rides):
    qnw, kvnw = w['q_norm'], w['kv_norm']

    def prep(i, j, zq, zkv, zkr, qw, kw, cos, sin_s):
        return _rms(zq, qw)[0], _rms(zkv, kw)[0], _rope(zkr, cos, sin_s)
    qn, kvn, kr = _rowwise(prep, d.NT, d.rt, d.nlt,
                           [_row(z, 0, d.QL), _row(z, d.z0_kv // d.KVL, d.KVL), _row(z, d.z0_kr // LANE, LANE),
                            ('full', qnw), ('full', kvnw), _row(cos), _row(sin_s)],
                           [('row', d.QL, BF16), ('row', d.KVL, BF16), ('row', LANE, BF16)], 'ab_prep')
    qraw = _mm([(qn, w['w_qb'])], 'nn', F32, 'ab_qb')
    kv = _mm([(kvn, w['w_kvb'])], 'nn', BF16, 'ab_kvb')

    def qrope(i, j, qraw, cos, sin_s):
        parts = []
        for h in range(d.H):
            parts += [qraw[:, 2 * h * LANE:(2 * h + 1) * LANE], _rope(qraw[:, (2 * h + 1) * LANE:(2 * h + 2) * LANE], cos, sin_s)]
        return (jnp.concatenate(parts, axis=1) * (ATTN_SCALE / LN2),)
    q = _rowwise(qrope, d.NT, d.rt, d.nlt, [_row(qraw), _row(cos), _row(sin_s)], [('row', 2 * d.HQ, BF16)], 'ab_qrope')[0]

    def kcat_body(i, j, kn, kr):
        parts = []
        for h in range(d.H):
            parts += [kn[:, h * LANE:(h + 1) * LANE], kr]
        return (jnp.concatenate(parts, axis=1),)
    kcat = _rowwise(kcat_body, d.NT, d.rt, d.nlt, [_row(kv, 0, d.HQ), _row(kr)], [('row', 2 * d.HQ, BF16)], 'ab_kcat')[0]
    o, ob, lse = _flash_fwd(q, kcat, kv, d, rides)
    ab = d.z0_ax // d.CC
    s = _rowwise(lambda i, j, ax, ac: (ax * ac,), d.NT, d.rt, d.nlt, [_row(z, ab, d.CC), _row(z, ab + 2, d.CC)],
                 [('row', d.CC, F32)], 'ab_conv_in')[0]
    cv = _dwconv(s, _w8(w['conv_a']), d, F32, 'ab_conv')
    ymix = _rowwise(lambda i, j, a_b, cv, ob: (jnp.concatenate([(a_b * cv).astype(BF16), ob], axis=1),), d.NT, d.rt, d.nlt,
                    [_row(z, ab + 1, d.CC), _row(cv), _row(ob)], [('row', d.CC + d.HQ, BF16)], 'ab_mix')[0]
    return ymix, dict(qn=qn, kvn=kvn, q=q, kcat=kcat, kv=kv, o=o, lse=lse, s=s, cv=cv)


def _ab_bwd(dymix, z, sv, w, cos, sin_s, d, rides, emit):
    qnw, kvnw = w['q_norm'], w['kv_norm']
    assert d.CC % d.HQ == 0

    def dprep(i, j, dmo, o, lse):
        lane = lax.broadcasted_iota(jnp.int32, (1, LANE), 1)
        cols = []
        for h in range(d.H):
            hs = slice(h * LANE, (h + 1) * LANE)
            delta = jnp.sum(dmo[:, hs] * o[:, hs], axis=1, keepdims=True)
            cols.append(jnp.where(lane < LANE // 2, lse[:, hs], delta))
        return dmo, jnp.concatenate(cols, axis=1)
    do, ld = _rowwise(dprep, d.NT, d.rt, d.nlt, [_row(dymix, d.CC // d.HQ, d.HQ), _row(sv['o']), _row(sv['lse'])],
                      [('row', d.HQ, BF16), ('row', d.HQ, F32)], 'ab_do')
    dq, dkc, dvv = _flash_bwd(sv['q'], sv['kcat'], sv['kv'], do, ld, d, rides)

    def qrope_t(i, j, dq, cos, sin_s):
        dq = dq * ATTN_SCALE
        parts = []
        for h in range(d.H):
            parts += [dq[:, 2 * h * LANE:(2 * h + 1) * LANE], _rope_t(dq[:, (2 * h + 1) * LANE:(2 * h + 2) * LANE], cos, sin_s)]
        return (jnp.concatenate(parts, axis=1),)
    dqraw = _rowwise(qrope_t, d.NT, d.rt, d.nlt, [_row(dq), _row(cos), _row(sin_s)],
                     [('row', 2 * d.HQ, BF16)], 'ab_qrope_bwd')[0]

    def dkv_body(i, j, dkc, dv):
        dkr = dkc[:, LANE:2 * LANE]
        for h in range(1, d.H):
            dkr = dkr + dkc[:, (2 * h + 1) * LANE:(2 * h + 2) * LANE]
        parts = [dkc[:, 2 * h * LANE:(2 * h + 1) * LANE] for h in range(d.H)] + [dv.astype(F32)]
        return jnp.concatenate(parts, axis=1), dkr
    dkv, dkr = _rowwise(dkv_body, d.NT, d.rt, d.nlt, [_row(dkc), _row(dvv)], [('row', 2 * d.HQ, BF16), ('row', LANE, F32)],
                        'ab_dkv')
    dqn = _mm([(dqraw, w['w_qb'])], 'nt', F32, 'ab_qb_dx')
    emit('w_qb', _mm([(sv['qn'], dqraw)], 'tn', GRAD_WIRE, 'ab_qb_dw'))
    dkvn = _mm([(dkv, w['w_kvb'])], 'nt', F32, 'ab_kvb_dx')
    emit('w_kvb', _mm([(sv['kvn'], dkv)], 'tn', GRAD_WIRE, 'ab_kvb_dw'))
    ab = d.z0_ax // d.CC
    dab, dcv = _rowwise(lambda i, j, dya, cv, a_b: (dya * cv, dya * a_b), d.NT, d.rt, d.nlt,
                        [_row(dymix, 0, d.CC), _row(sv['cv']), _row(z, ab + 1, d.CC)],
                        [('row', d.CC, BF16), ('row', d.CC, F32)], 'ab_mix_bwd')
    ds = _dwconv(dcv, _w8(w['conv_a'][::-1]), d, F32, 'ab_conv_dx')
    conv_g = _dwconv_wgrad(sv['s'], dcv, d, 'ab_conv_dw')

    def assemble(i, j, dqn, dkvn, dkr, zq, zkv, qw, kw, cos, sin_s, ds, ax, ac, dab):
        _, xq, rq = _rms(zq, qw)
        dzq, dqw = _rms_bwd(dqn, xq, rq, qw)
        _, xk, rk = _rms(zkv, kw)
        dzkv, dkw = _rms_bwd(dkvn, xk, rk, kw)
        parts = [dzq, dzkv, _rope_t(dkr, cos, sin_s)]
        if d.z0_pad:
            parts.append(jnp.zeros((dzq.shape[0], d.z0_pad), F32))
        parts += [ds * ac, dab.astype(F32), ds * ax]
        return jnp.concatenate([t.astype(BF16) for t in parts], axis=1), dqw, dkw
    dz, dqw, dkw = _rowwise(assemble, d.NT, d.rt, d.nlt,
                            [_row(dqn), _row(dkvn), _row(dkr), _row(z, 0, d.QL), _row(z, d.z0_kv // d.KVL, d.KVL),
                             ('full', qnw), ('full', kvnw), _row(cos), _row(sin_s), _row(ds), _row(z, ab, d.CC),
                             _row(z, ab + 2, d.CC), _row(dab)],
                            [('row', d.ZW0, BF16), ('acc', (1, d.QL)), ('acc', (1, d.KVL))], 'ab_dz')
    return dz, dict(conv_a=conv_g[0:3], q_norm=dqw, kv_norm=dkw)


def _log_sigmoid(x):
    return jnp.minimum(x, 0.0) - jnp.log(1.0 + jnp.exp(-jnp.abs(x)))


def _gla_fwd_block(z, w, d):
    wg, bg, onw = w['gate_w'], w['gate_b'], w['o_norm']

    def gates(i, j, lr, wg, bg):
        return (_log_sigmoid(_dot(lr, wg, 'nn') + bg) / GATE_NORMALIZER,)
    g = _rowwise(gates, d.NT, d.rt, d.nlt, [_row(z, d.z1_lr // LANE, LANE), ('full', wg), ('full', bg)],
                 [('row', 2 * d.KEY, F32)], 'gla_gates')[0]
    of, stf, ob, stb = _gla_fwd(z, g, d)

    def outp(i, j, of, ob, og, ow):
        o = of + ob
        parts = [_rms(o[:, h * d.dv:(h + 1) * d.dv], ow)[0] for h in range(d.GH)]
        return (jnp.concatenate(parts, axis=1) * _silu(og),)
    ymix = _rowwise(outp, d.NT, d.rt, d.nlt, [_row(of), _row(ob), _row(z, d.z1_og // d.VAL, d.VAL), ('full', onw)],
                    [('row', d.VAL, BF16)], 'gla_out')[0]
    return ymix, dict(g=g, of=of, ob=ob, stf=stf, stb=stb)


def _gla_bwd_block(dymix, z, sv, w, d, rides):
    wg, bg, onw = w['gate_w'], w['gate_b'], w['o_norm']

    def outp_bwd(i, j, dy, of, ob, og, ow):
        o = of + ob
        dn = dy * _silu(og)
        dos, ns = [], []
        dow = jnp.zeros((1, d.dv), F32)
        for h in range(d.GH):
            hs = slice(h * d.dv, (h + 1) * d.dv)
            n, xh, r = _rms(o[:, hs], ow)
            do_h, dw_h = _rms_bwd(dn[:, hs], xh, r, ow)
            dos.append(do_h)
            ns.append(n)
            dow = dow + dw_h
        return jnp.concatenate(dos, axis=1), dy * jnp.concatenate(ns, axis=1) * _dsilu(og), dow
    do, dog, dow = _rowwise(outp_bwd, d.NT, d.rt, d.nlt,
                            [_row(dymix), _row(sv['of']), _row(sv['ob']), _row(z, d.z1_og // d.VAL, d.VAL), ('full', onw)],
                            [('row', d.VAL, F32), ('row', d.VAL, BF16), ('acc', (1, d.dv))], 'gla_out_bwd')
    (dq0, dk0, dv0, dg0), (dq1, dk1, dv1, dg1) = _gla_bwd(z, sv['g'], do, sv['stf'], sv['stb'], d, rides)

    def assemble(i, j, dg0, dg1, lr, wg, bg, dq0, dq1, dk0, dk1, dv0, dv1, dog):
        pre = _dot(lr, wg, 'nn') + bg
        e = jnp.exp(-jnp.abs(pre))
        dpre = jnp.concatenate([dg0, dg1], axis=1) * jnp.where(pre >= 0, e, 1.0) / (1.0 + e) / GATE_NORMALIZER
        dlr = _dot(dpre, wg, 'nt')
        parts = [dv0 + dv1, dog.astype(F32), dk0 + dk1, dq0 + dq1, dlr]
        return (jnp.concatenate([t.astype(BF16) for t in parts], axis=1), _dot(lr, dpre, 'tn'),
                jnp.sum(dpre, axis=0, keepdims=True))
    dz, dwg, dbg = _rowwise(assemble, d.NT, d.rt, d.nlt,
                            [_row(dg0), _row(dg1), _row(z, d.z1_lr // LANE, LANE), ('full', wg), ('full', bg), _row(dq0),
                             _row(dq1), _row(dk0), _row(dk1), _row(dv0), _row(dv1), _row(dog)],
                            [('row', d.ZW1, BF16), ('acc', (LANE, 2 * d.KEY)), ('acc', (1, 2 * d.KEY))], 'gla_dz')
    return dz, dict(gate_fw_w=dwg[:GATE_RANK, :d.KEY], gate_bw_w=dwg[GATE_RANK:2 * GATE_RANK, d.KEY:],
                    gate_fw_b=dbg[:, :d.KEY], gate_bw_b=dbg[:, d.KEY:], o_norm=dow)


def _loss_bwd(x, fnw, target, d):
    def body(i, j, x, w, tgt):
        y, xh, r = _rms(x, w)
        e = y - tgt
        dx, dw = _rms_bwd(e * (1.0 / d.D), xh, r, w)
        lat = i < d.nlt
        part = jnp.sum(jnp.sum(e * e, axis=1, keepdims=True), axis=0, keepdims=True) * (0.5 / d.D)
        return (jnp.where(lat, dx, 0.0), jnp.where(lat, jnp.broadcast_to(part, (8, LANE)), 0.0), jnp.where(lat, dw, 0.0))
    return _rowwise(body, d.NT, d.rt, d.nlt, [_row(x), ('full', fnw), ('rowclamp', target, d.nlt - 1, d.D)],
                    [('row', d.D, F32), ('acc', (8, LANE)), ('acc', (1, d.D))], 'loss')


def _layer_fwd(x, mods, w, mixer_fwd, d, tag, rides):
    sh1, sc1, g1, sh2, sc2, g2 = mods
    h = _mod_fwd(x, w['norm1'], sh1, sc1, d, tag + '_mod1')
    z = _mm([(h, _w(w, 'w_in'))], 'nn', F32, tag + '_in', rides=rides)
    ymix, msv = mixer_fwd(z)
    y = _mm([(ymix, _w(w, 'w_out'))], 'nn', F32, tag + '_out')
    x1, h2 = _res_mod_fwd(x, y, g1, w['norm2'], sh2, sc2, d, tag + '_mod2')
    p, a, f = _ffn_fwd(h2, w, d, tag + '_ffn', rides)
    x2 = _res_fwd(x1, f, g2, d, tag + '_res')
    return x2, dict(x=x, h=h, z=z, ymix=ymix, msv=msv, y=y, x1=x1, h2=h2, p=p, a=a, f=f)


def _layer_bwd(dx2, sv, mods, w, mixer_bwd, d, tag, rides, emit):
    sh1, sc1, g1, sh2, sc2, g2 = mods
    df, dg2 = _res_bwd(dx2, sv['f'], g2, d, tag + '_res_bwd')
    dh2, dconv_w, dconv_b = _ffn_bwd(df, sv['h2'], sv['p'], sv['a'], w, d, tag + '_ffn', rides, emit)
    dx1, dy, dg1, dn2, dsh2, dsc2 = _res_mod_bwd(dx2, dh2, sv['x1'], sv['y'], g1, w['norm2'], sh2, sc2, d, tag + '_mod2_bwd')
    dymix = _mm([(dy, w['w_out'])], 'nt', F32, tag + '_out_dx')
    emit('w_out', _mm([(sv['ymix'], dy)], 'tn', GRAD_WIRE, tag + '_out_dw'))
    dz, mg = mixer_bwd(dymix, sv['z'], sv['msv'])
    emit('w_in', _mm([(sv['h'], dz)], 'tn', GRAD_WIRE, tag + '_in_dw', rides=rides))
    dh = _mm([(dz, w['w_in'])], 'nt', F32, tag + '_in_dx', rides=rides)
    dx, dn1, dsh1, dsc1 = _mod_bwd_call(dx1, dh, sv['x'], w['norm1'], sh1, sc1, d, tag + '_mod1_bwd')
    grads = dict(mg, ffn_conv_w=dconv_w, ffn_conv_b=dconv_b, norm1=dn1, norm2=dn2)
    return dx, grads, [dsh1, dsc1, dg1, dsh2, dsc2, dg2]


def _pad_flat(v, mult=LANE):
    v = v.reshape(-1)
    return jnp.pad(v, (0, (-v.shape[0]) % mult))


def _pack(entries, row_mult):
    flat, offs, pos = [], [], 0
    for v in entries:
        f = _pad_flat(v)
        flat.append(f)
        offs.append(pos)
        pos += f.shape[0]
    tot = jnp.concatenate(flat)
    tot = jnp.pad(tot, (0, (-pos) % (row_mult * LANE)))
    return tot.reshape(-1, LANE), offs


def _unpack(packed, offs, shapes):
    flat = packed.reshape(-1)
    out = []
    for off, shp in zip(offs, shapes):
        n = 1
        for s in shp:
            n *= s
        out.append(flat[off:off + n].reshape(shp))
    return out


def _step(a):
    d = _dims()
    dm = d.D
    me = 4 * lax.axis_index("x") + 2 * lax.axis_index("y") + lax.axis_index("c")
    sds = jax.ShapeDtypeStruct

    rides = _Rides()
    fwd_hosts = {'l0_w_out': 'l0_in', 'l0_ffn_up': 'flash_fwd', 'l0_ffn_down': 'flash_fwd', 'l1_ffn_up': 'flash_fwd',
                 'l1_w_out': 'l0_ffn_up', 'l1_w_in': 'l0_ffn_up', 'l1_ffn_down': 'l0_ffn_down'}

    def gathered(name):
        shard = _cast_bf16(a[name], 'cast_' + name)
        if name in fwd_hosts:
            rides.add(fwd_hosts[name], name, shard, False)
            return lambda: rides.done[name]
        g = _exchange(shard, False, 'ag_' + name)
        return lambda: g

    def cols(name, relayout=lambda t: t):
        g = gathered(name)
        k, n = a[name].shape
        return lambda: relayout(g().transpose(1, 0, 2).reshape(k, N_DEV * n))

    def rows(name):
        g = gathered(name)
        return lambda: g().reshape(N_DEV * a[name].shape[0], a[name].shape[1])

    small_names = ['l0_conv_a', 'l0_ffn_conv_w', 'l1_ffn_conv_w', 'l1_gate_fw_w', 'l1_gate_bw_w']
    spack, soffs = _pack([a[n] for n in small_names], 8)
    sg = _exchange(spack, False, 'ag_small')
    small_w = {}
    for n, off in zip(small_names, soffs):
        r, c = a[n].shape
        shards = sg.reshape(N_DEV, -1)[:, off:off + r * c].reshape(N_DEV, r, c)
        small_w[n] = shards.transpose(1, 0, 2).reshape(r, N_DEV * c)

    w0 = dict(norm1=a['l0_norm1'].reshape(1, dm), norm2=a['l0_norm2'].reshape(1, dm),
              w_in=cols('l0_w_in', lambda t: _win0_to_kernel(t, d)), w_qb=cols('l0_w_qb', lambda t: _wqb_to_kernel(t, d))(),
              w_kvb=cols('l0_w_kvb', lambda t: _wkvb_to_kernel(t, d))(), w_out=rows('l0_w_out'),
              q_norm=a['l0_q_norm'].reshape(1, -1), kv_norm=a['l0_kv_norm'].reshape(1, -1), conv_a=small_w['l0_conv_a'],
              ffn_up=cols('l0_ffn_up'), ffn_conv_w=small_w['l0_ffn_conv_w'], ffn_conv_b=a['l0_ffn_conv_b'],
              ffn_down=rows('l0_ffn_down'))
    w1 = dict(norm1=a['l1_norm1'].reshape(1, dm), norm2=a['l1_norm2'].reshape(1, dm),
              w_in=cols('l1_w_in', lambda t: _win1_to_kernel(t, d)), w_out=rows('l1_w_out'),
              gate_w=_gate_weight(small_w['l1_gate_fw_w'], small_w['l1_gate_bw_w'], d),
              gate_b=jnp.concatenate([a['l1_gate_fw_b'], a['l1_gate_bw_b']]).reshape(1, -1),
              o_norm=a['l1_o_norm'].reshape(1, -1),
              ffn_up=cols('l1_ffn_up'), ffn_conv_w=small_w['l1_ffn_conv_w'], ffn_conv_b=a['l1_ffn_conv_b'],
              ffn_down=rows('l1_ffn_down'))

    c8 = _exchange(a['c'], False, 'ag_c').reshape(N_DEV, dm)
    c16 = jnp.concatenate([c8, a['c_ctx'].reshape(1, dm), jnp.zeros((7, dm), F32)], axis=0)
    act16, dact16 = _small(lambda v: (_silu(v), _dsilu(v)), [c16], [sds((16, dm), BF16), sds((16, dm), F32)], 'cond_silu')
    n6 = N_MOD * dm // N_DEV
    mod_sh = [_mm([(act16, a[f'l{l}_ada_w'])], 'nn', F32, f'ada{l}') for l in (0, 1)]
    mod_all = _exchange(jnp.concatenate(mod_sh, axis=1), False, 'ag_mod')
    mods = []
    for l in (0, 1):
        full = mod_all[:, :, l * n6:(l + 1) * n6].transpose(1, 0, 2).reshape(16, N_MOD * dm)
        mine = jnp.concatenate([lax.dynamic_slice_in_dim(full, me, 1, 0), full[8:9]], axis=0)
        m2 = _small(lambda r, b: (r + b,), [mine, a[f'l{l}_ada_b'].reshape(1, -1)], [sds((2, N_MOD * dm), F32)], f'ada{l}_bias')[0]
        mods.append([m2[:, k * dm:(k + 1) * dm].reshape(2, 1, dm) for k in range(N_MOD)])

    cos, sin_s = _rope_tables(d)
    x0 = jnp.concatenate([a['x'][0], a['ctx'][0]], axis=0)
    x2, sv0 = _layer_fwd(x0, mods[0], w0, lambda z: _ab_fwd(z, w0, cos, sin_s, d, rides), d, 'l0', rides)
    x4, sv1 = _layer_fwd(x2, mods[1], w1, lambda z: _gla_fwd_block(z, w1, d), d, 'l1', rides)
    dx4, loss_acc, dfn = _loss_bwd(x4, a['final_norm'].reshape(1, dm), a['loss_target'][0], d)

    bwd_hosts = {'l1_ffn_down': 'l1_ffn_up_dx', 'l1_ffn_up': 'gla_bwd', 'l1_w_out': 'gla_bwd', 'l1_w_in': 'l1_in_dx',
                 'l0_ffn_down': 'l0_ffn_up_dx', 'l0_ffn_up': 'flash_bwd', 'l0_w_out': 'flash_bwd', 'l0_w_qb': 'l0_in_dw',
                 'l0_w_kvb': 'l0_in_dw', 'l0_w_in': 'l0_in_dx'}
    from_kernel = {'l0_w_in': _win0_from_kernel, 'l0_w_qb': _wqb_from_kernel, 'l0_w_kvb': _wkvb_from_kernel,
                   'l1_w_in': _win1_from_kernel}
    slabs = {}

    def emitter(layer):
        def emit(wkey, dw):
            name = f'l{layer}_{wkey}'
            if name in from_kernel:
                dw = from_kernel[name](dw, d)
            if a[name].shape[0] == dw.shape[0]:
                stacked = dw.reshape(dw.shape[0], N_DEV, dw.shape[1] // N_DEV).transpose(1, 0, 2)
            else:
                stacked = dw.reshape(N_DEV, dw.shape[0] // N_DEV, dw.shape[1])
            if name in bwd_hosts:
                rides.add(bwd_hosts[name], 'rs_' + name, stacked, True)
                slabs[name] = lambda: rides.done['rs_' + name]
            else:
                got = _exchange(stacked, True, 'rs_' + name)
                slabs[name] = lambda: got
        return emit

    dx2, g1, dmod1 = _layer_bwd(dx4, sv1, mods[1], w1, lambda dy, z, msv: _gla_bwd_block(dy, z, msv, w1, d, rides), d, 'l1',
                                rides, emitter(1))
    dx0, g0, dmod0 = _layer_bwd(dx2, sv0, mods[0], w0,
                                lambda dy, z, msv: _ab_bwd(dy, z, msv, w0, cos, sin_s, d, rides, emitter(0)), d, 'l0',
                                rides, emitter(0))

    dm_rows = jnp.concatenate([jnp.concatenate([t.reshape(2, dm) for t in dmod], axis=1) for dmod in (dmod0, dmod1)], axis=0)
    dm_all = _exchange(dm_rows, False, 'ag_dmod')
    lat = dm_all[:, 0::2].transpose(1, 0, 2)
    ctxs = dm_all[:, 1::2].transpose(1, 0, 2)

    def ada_prep(lat, ctxs):
        csum = jnp.sum(ctxs, axis=1, keepdims=True)
        row = lax.broadcasted_iota(jnp.int32, (1, 8, 1), 1)
        g16 = jnp.concatenate([lat, jnp.where(row == 0, csum, 0.0)], axis=1)
        return g16, jnp.sum(lat, axis=1, keepdims=True) + csum
    g16, gb = _small(ada_prep, [lat, ctxs], [sds((2, 16, N_MOD * dm), F32), sds((2, 1, N_MOD * dm), F32)], 'ada_bwd_prep')
    g16_sh = [lax.dynamic_slice_in_dim(g16[l], me * n6, n6, 1) for l in (0, 1)]
    grad_ada_w = [_mm([(act16, g16_sh[l])], 'tn', F32, f'ada{l}_dw') for l in (0, 1)]
    dact = _mm([(g16_sh[0], a['l0_ada_w']), (g16_sh[1], a['l1_ada_w'])], 'nt', F32, 'ada_dact')
    dcc = _small(lambda t, s: (t * s,), [dact[8:9], dact16[8:9]], [sds((1, dm), F32)], 'cctx_grad')[0]

    res = {}
    for name in slabs:
        res[name] = _adam(slabs[name](), a[name], a['m_' + name], a['v_' + name], 'adam_' + name)
    for l in (0, 1):
        name = f'l{l}_ada_w'
        res[name] = _adam(grad_ada_w[l][None], a[name], a['m_' + name], a['v_' + name], 'adam_' + name)

    part = {'loss': loss_acc[0:1, 0:1], 'c_ctx': dcc, 'final_norm': dfn,
            'l0_norm1': g0['norm1'], 'l0_norm2': g0['norm2'], 'l0_q_norm': g0['q_norm'], 'l0_kv_norm': g0['kv_norm'],
            'l0_conv_a': g0['conv_a'], 'l0_ffn_conv_w': g0['ffn_conv_w'], 'l0_ffn_conv_b': g0['ffn_conv_b'],
            'l1_norm1': g1['norm1'], 'l1_norm2': g1['norm2'], 'l1_o_norm': g1['o_norm'],
            'l1_gate_fw_w': g1['gate_fw_w'], 'l1_gate_bw_w': g1['gate_bw_w'], 'l1_gate_fw_b': g1['gate_fw_b'],
            'l1_gate_bw_b': g1['gate_bw_b'], 'l1_ffn_conv_w': g1['ffn_conv_w'], 'l1_ffn_conv_b': g1['ffn_conv_b']}
    pkeys = list(part)
    ppack, poffs = _pack([part[k] for k in pkeys], 8)
    psum = _sum_parts(_exchange(ppack, False, 'ag_small_grads'), 'sum_small_grads')
    tot = dict(zip(pkeys, _unpack(psum, poffs, [part[k].shape for k in pkeys])))
    loss = tot['loss'].reshape(())
    sgrad = {}
    for n in _WEIGHTS:
        if n in res:
            continue
        if n.endswith('ada_b'):
            sgrad[n] = gb[int(n[1])].reshape(a[n].shape)
        elif n in small_names:
            c = a[n].shape[1]
            sgrad[n] = lax.dynamic_slice_in_dim(tot[n], me * c, c, 1)
        else:
            sgrad[n] = tot[n].reshape(a[n].shape)
    snames = list(sgrad)
    packs = [_pack([src[n] for n in snames], 8)[0] for src in
             (sgrad, {n: a[n] for n in snames}, {n: a['m_' + n] for n in snames}, {n: a['v_' + n] for n in snames})]
    offs = _pack([sgrad[n] for n in snames], 8)[1]
    outs = _adam(packs[0][None], packs[1], packs[2], packs[3], 'adam_small')
    for k in range(4):
        for n, val in zip(snames, _unpack(outs[k], offs, [a[n].shape for n in snames])):
            res.setdefault(n, [None] * 4)[k] = val

    grad_x = dx0[:d.T].reshape(1, d.T, dm)
    return (loss, grad_x, *[res[n][0] for n in _WEIGHTS], *[res[n][1] for n in _WEIGHTS], *[res[n][2] for n in _WEIGHTS],
            *[res[n][3] for n in _WEIGHTS])


def kernel(*args):
    return _step(dict(zip(_ARGS, args, strict=True)))
```

```python
import functools
import types

import jax
import jax.numpy as jnp
from jax import lax
from jax.experimental import pallas as pl
from jax.experimental.pallas import tpu as pltpu

D_MODEL = 2048
SEQ = 8192
GRID_W = 64
CTX_LEN = 256
EPS = 1e-6
N_MOD = 6
MLA_HEADS = 8
QK_NOPE = 128
QK_ROPE = 64
V_HEAD = 128
Q_LORA = 512
KV_LORA = 256
ROPE_THETA = 10000.0
GLA_HEADS = 4
GATE_RANK = 16
GATE_NORMALIZER = 16.0
CHUNK = 64
D_FF = 5632
ADAM_LR = 0.001
ADAM_B1 = 0.9
ADAM_B2 = 0.999
ADAM_EPS = 1e-08
ADAM_WD = 0.01
ADAM_STEP = 10

N_DEV = 8
LANE = 128
VMEM_LIMIT = 56 * 1024 * 1024
CONV_COLS = 2816
GLU_COLS = 1408
FLASH_CHUNK = 1024
LN2 = 0.6931471805599453
ATTN_SCALE = float(QK_NOPE + QK_ROPE) ** -0.5

F32 = jnp.float32
BF16 = jnp.bfloat16
GRAD_WIRE = jnp.bfloat16
MESH_ID = pl.DeviceIdType.MESH

_FWD = ['x', 'c', 'ctx', 'c_ctx', 'l0_ada_w', 'l0_ada_b', 'l0_norm1', 'l0_w_in', 'l0_conv_a', 'l0_q_norm', 'l0_w_qb',
        'l0_kv_norm', 'l0_w_kvb', 'l0_w_out', 'l0_norm2', 'l0_ffn_up', 'l0_ffn_conv_w', 'l0_ffn_conv_b', 'l0_ffn_down',
        'l1_ada_w', 'l1_ada_b', 'l1_norm1', 'l1_w_in', 'l1_gate_fw_w', 'l1_gate_fw_b', 'l1_gate_bw_w', 'l1_gate_bw_b',
        'l1_o_norm', 'l1_w_out', 'l1_norm2', 'l1_ffn_up', 'l1_ffn_conv_w', 'l1_ffn_conv_b', 'l1_ffn_down', 'final_norm']
_WEIGHTS = _FWD[3:]
_ARGS = _FWD + ['loss_target'] + ['m_' + n for n in _WEIGHTS] + ['v_' + n for n in _WEIGHTS]


def _dims():
    d = types.SimpleNamespace()
    d.D, d.T, d.TC = D_MODEL, SEQ, CTX_LEN
    d.NT = d.T + d.TC
    d.rt = 256 if d.TC % 256 == 0 else 128
    d.nlt = d.T // d.rt
    d.H = MLA_HEADS
    d.QL, d.KVL = Q_LORA, KV_LORA
    d.CC = D_MODEL // 2
    d.z0_kv = d.QL
    d.z0_kr = d.QL + d.KVL
    d.z0_pad = (-(d.QL + d.KVL + LANE)) % d.CC
    d.z0_ax = d.QL + d.KVL + LANE + d.z0_pad
    d.ZW0 = d.z0_ax + 3 * d.CC
    d.AB_COLS = d.KVL + QK_ROPE + d.QL + 3 * d.CC
    d.HQ = d.H * LANE
    d.GH = GLA_HEADS
    d.KEY = D_MODEL // 2
    d.VAL = D_MODEL
    d.dk = d.KEY // d.GH
    d.dv = d.VAL // d.GH
    d.z1_og = d.VAL
    d.z1_k = 2 * d.VAL
    d.z1_q = 2 * d.VAL + d.KEY
    d.z1_lr = 2 * d.VAL + 2 * d.KEY
    d.ZW1 = d.z1_lr + LANE
    d.GLA_COLS = 2 * d.KEY + 2 * d.VAL + 2 * GATE_RANK
    d.FF = D_FF
    d.tq = min(256, d.TC)
    return d


def _tile(n, pref, align=LANE):
    if n <= pref:
        return n
    t = (pref // align) * align
    while t >= align:
        if n % t == 0:
            return t
        t -= align
    raise ValueError(f"no tile for {n}")


def _cparams(sem):
    return pltpu.CompilerParams(dimension_semantics=sem, vmem_limit_bytes=VMEM_LIMIT)


def _dot(a, b, mode):
    dims = {'nn': (((1,), (0,)), ((), ())), 'nt': (((1,), (1,)), ((), ())), 'tn': (((0,), (0,)), ((), ()))}[mode]
    return lax.dot_general(a.astype(BF16), b.astype(BF16), dims, preferred_element_type=F32)


class _Rides:
    def __init__(self):
        self.pending, self.done = {}, {}

    def add(self, host, key, x, stacked):
        self.pending.setdefault(host, []).append((key, x, stacked))

    def take(self, host):
        return self.pending.pop(host, [])


def _ride_plumbing(riders):
    n = len(riders)
    if not n:
        return [], [], [], []
    spec = pl.BlockSpec(memory_space=pl.ANY)
    shapes = [jax.ShapeDtypeStruct((N_DEV,) + tuple(x.shape[1:] if st else x.shape), x.dtype) for _, x, st in riders]
    sems = [pltpu.SemaphoreType.DMA((n * (N_DEV - 1),)), pltpu.SemaphoreType.DMA((n * (N_DEV - 1),)),
            pltpu.SemaphoreType.DMA((n,))]
    return [spec] * n, shapes, [spec] * n, sems


def _ride_copies(stacked_flags, x_refs, o_refs, send_sems, recv_sems, local_sems):
    ix, iy, ic = lax.axis_index("x"), lax.axis_index("y"), lax.axis_index("c")
    me = 4 * ix + 2 * iy + ic
    local, sends, recvs = [], [], []
    for r, (stacked, x_ref, o_ref) in enumerate(zip(stacked_flags, x_refs, o_refs)):
        def src(p, x_ref=x_ref, stacked=stacked):
            return x_ref.at[p] if stacked else x_ref

        local.append(pltpu.make_async_copy(src(me), o_ref.at[me], local_sems.at[r]))
        for k in range(1, N_DEV):
            px, py, pc = (ix + ((k >> 2) & 1)) % 2, (iy + ((k >> 1) & 1)) % 2, (ic + (k & 1)) % 2
            peer = 4 * px + 2 * py + pc
            s = r * (N_DEV - 1) + k - 1
            sends.append(pltpu.make_async_remote_copy(
                src_ref=src(peer), dst_ref=o_ref.at[me], send_sem=send_sems.at[s], recv_sem=recv_sems.at[s],
                device_id=(px, py, pc), device_id_type=MESH_ID))
            recvs.append(pltpu.make_async_remote_copy(
                src_ref=src(peer), dst_ref=o_ref.at[peer], send_sem=send_sems.at[s], recv_sem=recv_sems.at[s],
                device_id=(px, py, pc), device_id_type=MESH_ID))

    def start():
        for cp in local + sends:
            cp.start()

    def wait():
        for cp in recvs:
            cp.wait_recv()
        for cp in sends:
            cp.wait_send()
        for cp in local:
            cp.wait()

    return start, wait


def _mm(pairs, mode, out_dtype, name, tm=768, tn=1024, tk=2816, rides=None):
    riders = rides.take(name) if rides is not None else []
    nr = len(riders)
    r_in, r_shapes, r_out, r_sems = _ride_plumbing(riders)
    a0, b0 = pairs[0]
    if mode == 'nn':
        (m, k), n = a0.shape, b0.shape[1]
    elif mode == 'nt':
        (m, k), n = a0.shape, b0.shape[0]
    else:
        (k, m), n = a0.shape, b0.shape[1]
    tm, tn, tk = _tile(m, tm), _tile(n, tn), _tile(k, tk)
    nk = k // tk
    if mode == 'nn':
        a_spec = pl.BlockSpec((tm, tk), lambda i, j, kk: (i, kk))
        b_spec = pl.BlockSpec((tk, tn), lambda i, j, kk: (kk, j))
    elif mode == 'nt':
        a_spec = pl.BlockSpec((tm, tk), lambda i, j, kk: (i, kk))
        b_spec = pl.BlockSpec((tn, tk), lambda i, j, kk: (j, kk))
    else:
        a_spec = pl.BlockSpec((tk, tm), lambda i, j, kk: (kk, i))
        b_spec = pl.BlockSpec((tk, tn), lambda i, j, kk: (kk, j))
    npairs = len(pairs)

    nin = 2 * npairs
    gi, gj = m // tm, n // tn

    def body(*refs):
        o_ref, acc_ref = refs[nin + nr], refs[nin + 2 * nr + 1]
        i, j, kk = pl.program_id(0), pl.program_id(1), pl.program_id(2)
        if nr:
            start, wait = _ride_copies([st for _, _, st in riders], refs[nin:nin + nr], refs[nin + nr + 1:nin + 2 * nr + 1],
                                       *refs[nin + 2 * nr + 2:])
            pl.when(jnp.logical_and(jnp.logical_and(i == 0, j == 0), kk == 0))(start)

        def dots():
            s = None
            for p in range(npairs):
                t = _dot(refs[2 * p][...], refs[2 * p + 1][...], mode)
                s = t if s is None else s + t
            return s

        if nk == 1:
            o_ref[...] = dots().astype(o_ref.dtype)
        else:
            @pl.when(kk == 0)
            def _():
                acc_ref[...] = dots()

            @pl.when(jnp.logical_and(kk > 0, kk < nk - 1))
            def _():
                acc_ref[...] += dots()

            @pl.when(kk == nk - 1)
            def _():
                o_ref[...] = (acc_ref[...] + dots()).astype(o_ref.dtype)

        if nr:
            pl.when(jnp.logical_and(jnp.logical_and(i == gi - 1, j == gj - 1), kk == nk - 1))(wait)

    flat = [t for ab in pairs for t in ab]
    res = pl.pallas_call(
        body, name=name,
        out_shape=[jax.ShapeDtypeStruct((m, n), out_dtype)] + r_shapes,
        grid=(gi, gj, nk),
        in_specs=[a_spec, b_spec] * npairs + r_in,
        out_specs=[pl.BlockSpec((tm, tn), lambda i, j, kk: (i, j))] + r_out,
        scratch_shapes=[pltpu.VMEM((tm, tn), F32)] + r_sems,
        compiler_params=_cparams(("arbitrary",) * 3 if nr else ("parallel", "parallel", "arbitrary")),
    )(*flat, *[x for _, x, _ in riders])
    for (key, _, _), arr in zip(riders, res[1:]):
        rides.done[key] = arr
    return res[0]


def _rowwise(body, nrows, rt, nlt, ins, outs, name, ncol=1):
    ntiles = nrows // rt
    in_specs, args = [], []
    for spec in ins:
        kind, arr = spec[0], spec[1]
        if kind == 'row':
            in_specs.append(pl.BlockSpec((rt, spec[3]), functools.partial(lambda i, j, cb: (i, cb), cb=spec[2])))
        elif kind == 'rowc':
            in_specs.append(pl.BlockSpec((rt, spec[3]), functools.partial(lambda i, j, cb: (i, cb + j), cb=spec[2])))
        elif kind == 'rowm':
            in_specs.append(pl.BlockSpec(
                (rt, spec[3]), functools.partial(lambda i, j, cb, md: (i, cb + j % md), cb=spec[2], md=spec[4])))
        elif kind == 'rowclamp':
            in_specs.append(pl.BlockSpec(
                (rt, spec[3]), functools.partial(lambda i, j, mb: (jnp.minimum(i, mb), 0), mb=spec[2])))
        elif kind == 'row3':
            in_specs.append(pl.BlockSpec((arr.shape[0], rt, arr.shape[2]), lambda i, j: (0, i, 0)))
        elif kind == 'seg':
            in_specs.append(pl.BlockSpec((None, 1, arr.shape[2]), lambda i, j: (i // nlt, 0, 0)))
        else:
            in_specs.append(pl.BlockSpec(arr.shape, functools.partial(lambda i, j, nd: (0,) * nd, nd=arr.ndim)))
        args.append(arr)
    out_shapes, out_specs = [], []
    for spec in outs:
        kind = spec[0]
        if kind == 'row':
            out_shapes.append(jax.ShapeDtypeStruct((nrows, spec[1]), spec[2]))
            out_specs.append(pl.BlockSpec((rt, spec[1]), lambda i, j: (i, 0)))
        elif kind == 'rowc':
            out_shapes.append(jax.ShapeDtypeStruct((nrows, spec[1]), spec[3]))
            out_specs.append(pl.BlockSpec((rt, spec[2]), lambda i, j: (i, j)))
        elif kind == 'acc':
            out_shapes.append(jax.ShapeDtypeStruct(spec[1], F32))
            out_specs.append(pl.BlockSpec(spec[1], functools.partial(lambda i, j, nd: (0,) * nd, nd=len(spec[1]))))
        else:
            out_shapes.append(jax.ShapeDtypeStruct((2, 1, spec[1]), F32))
            out_specs.append(pl.BlockSpec((None, 1, spec[1]), lambda i, j: (i // nlt, 0, 0)))
    n_in = len(ins)
    has_acc = any(s[0] in ('acc', 'segacc') for s in outs)

    def kern(*refs):
        i = pl.program_id(0)
        j = pl.program_id(1)
        vals = [r[...] for r in refs[:n_in]]
        res = body(i, j, *vals)
        for spec, ref, val in zip(outs, refs[n_in:], res):
            if spec[0] in ('row', 'rowc'):
                ref[...] = val.astype(ref.dtype)
            else:
                first = (i == 0) if spec[0] == 'acc' else jnp.logical_or(i == 0, i == nlt)

                @pl.when(first)
                def _(ref=ref, val=val):
                    ref[...] = val

                @pl.when(jnp.logical_not(first))
                def _(ref=ref, val=val):
                    ref[...] += val

    return pl.pallas_call(
        kern, name=name, out_shape=tuple(out_shapes), grid=(ntiles, ncol),
        in_specs=in_specs, out_specs=tuple(out_specs),
        compiler_params=_cparams(("arbitrary", "arbitrary") if has_acc else ("parallel", "parallel")),
    )(*args)


def _small(body, args, out_shapes, name):
    n_in = len(args)

    def kern(*refs):
        res = body(*[r[...] for r in refs[:n_in]])
        for ref, val in zip(refs[n_in:], res):
            ref[...] = val.astype(ref.dtype)

    return pl.pallas_call(
        kern, name=name, out_shape=tuple(out_shapes),
        in_specs=[pl.BlockSpec(memory_space=pltpu.VMEM)] * n_in,
        out_specs=tuple(pl.BlockSpec(memory_space=pltpu.VMEM) for _ in out_shapes),
        compiler_params=pltpu.CompilerParams(vmem_limit_bytes=VMEM_LIMIT),
    )(*args)


def _exchange(x, stacked, name):
    r_in, r_shapes, r_out, r_sems = _ride_plumbing([(name, x, stacked)])

    def body(x_ref, o_ref, send_sems, recv_sems, local_sems):
        start, wait = _ride_copies([stacked], [x_ref], [o_ref], send_sems, recv_sems, local_sems)
        start()
        wait()

    return pl.pallas_call(body, name=name, out_shape=r_shapes[0], in_specs=r_in, out_specs=r_out[0], scratch_shapes=r_sems)(x)


def _cast_bf16(x, name):
    r, c = x.shape
    tr = _tile(r, 256, 8)
    return pl.pallas_call(
        lambda x_ref, o_ref: o_ref.__setitem__(Ellipsis, x_ref[...].astype(BF16)), name=name,
        out_shape=jax.ShapeDtypeStruct((r, c), BF16), grid=(r // tr,),
        in_specs=[pl.BlockSpec((tr, c), lambda i: (i, 0))], out_specs=pl.BlockSpec((tr, c), lambda i: (i, 0)),
        compiler_params=_cparams(("parallel",)),
    )(x)


def _adam(parts, w, m, v, name):
    p, r, c = parts.shape
    tr = _tile(r, 64, 8)

    def body(p_ref, w_ref, m_ref, v_ref, g_ref, d_ref, nm_ref, nv_ref):
        g = p_ref[0].astype(F32)
        for q in range(1, p):
            g = g + p_ref[q].astype(F32)
        nm = ADAM_B1 * m_ref[...] + (1.0 - ADAM_B1) * g
        nv = ADAM_B2 * v_ref[...] + (1.0 - ADAM_B2) * (g * g)
        m_hat = nm / (1.0 - ADAM_B1 ** ADAM_STEP)
        v_hat = nv / (1.0 - ADAM_B2 ** ADAM_STEP)
        g_ref[...] = g
        d_ref[...] = -ADAM_LR * (m_hat / (jnp.sqrt(v_hat) + ADAM_EPS) + ADAM_WD * w_ref[...])
        nm_ref[...] = nm
        nv_ref[...] = nv

    spec = pl.BlockSpec((tr, c), lambda i: (i, 0))
    return pl.pallas_call(
        body, name=name, out_shape=tuple(jax.ShapeDtypeStruct((r, c), F32) for _ in range(4)), grid=(r // tr,),
        in_specs=[pl.BlockSpec((p, tr, c), lambda i: (0, i, 0)), spec, spec, spec], out_specs=(spec,) * 4,
        compiler_params=_cparams(("parallel",)),
    )(parts, w, m, v)


def _sum_parts(parts, name):
    p, r, c = parts.shape
    tr = _tile(r, 256, 8)

    def body(p_ref, o_ref):
        g = p_ref[0]
        for q in range(1, p):
            g = g + p_ref[q]
        o_ref[...] = g

    return pl.pallas_call(
        body, name=name, out_shape=jax.ShapeDtypeStruct((r, c), F32), grid=(r // tr,),
        in_specs=[pl.BlockSpec((p, tr, c), lambda i: (0, i, 0))], out_specs=pl.BlockSpec((tr, c), lambda i: (i, 0)),
        compiler_params=_cparams(("parallel",)),
    )(parts)


def _shifted(cur, prev8, next8, i, rt, nlt, ntiles):
    first = jnp.logical_or(i == 0, i == nlt)
    last = jnp.logical_or(i == nlt - 1, i == ntiles - 1)
    prev_row = jnp.where(first, 0.0, prev8[7:8, :])
    next_row = jnp.where(last, 0.0, next8[0:1, :])
    rows = lax.broadcasted_iota(jnp.int32, (rt, 1), 0)
    x_m1 = jnp.where(rows == 0, prev_row, pltpu.roll(cur, 1, 0))
    x_p1 = jnp.where(rows == rt - 1, next_row, pltpu.roll(cur, rt - 1, 0))
    return x_m1, x_p1


def _halo_specs(rt, tc, nrows, col_axis_first):
    r8 = rt // 8
    last8 = nrows // 8 - 1
    if col_axis_first:
        return [pl.BlockSpec((8, tc), lambda j, i: (jnp.maximum(i * r8 - 1, 0), j)),
                pl.BlockSpec((rt, tc), lambda j, i: (i, j)),
                pl.BlockSpec((8, tc), lambda j, i: (jnp.minimum((i + 1) * r8, last8), j))]
    return [pl.BlockSpec((8, tc), lambda i, j: (jnp.maximum(i * r8 - 1, 0), j)),
            pl.BlockSpec((rt, tc), lambda i, j: (i, j)),
            pl.BlockSpec((8, tc), lambda i, j: (jnp.minimum((i + 1) * r8, last8), j))]


def _dwconv(x, w8, d, out_dtype, name):
    nrows, c = x.shape
    rt, nlt = d.rt, d.nlt
    tc = _tile(c, CONV_COLS)
    ntiles = nrows // rt

    def body(p_ref, c_ref, n_ref, w_ref, o_ref):
        i = pl.program_id(0)
        cur = c_ref[...]
        x_m1, x_p1 = _shifted(cur, p_ref[...], n_ref[...], i, rt, nlt, ntiles)
        w = w_ref[...]
        o_ref[...] = (x_m1 * w[0:1] + cur * w[1:2] + x_p1 * w[2:3] + w[3:4]).astype(o_ref.dtype)

    return pl.pallas_call(
        body, name=name, out_shape=jax.ShapeDtypeStruct((nrows, c), out_dtype), grid=(ntiles, c // tc),
        in_specs=_halo_specs(rt, tc, nrows, False) + [pl.BlockSpec((8, tc), lambda i, j: (0, j))],
        out_specs=pl.BlockSpec((rt, tc), lambda i, j: (i, j)),
        compiler_params=_cparams(("parallel", "parallel")),
    )(x, x, x, w8)


def _dwconv_wgrad(x, dy, d, name):
    nrows, c = x.shape
    rt, nlt = d.rt, d.nlt
    tc = _tile(c, CONV_COLS)
    ntiles = nrows // rt

    def body(p_ref, c_ref, n_ref, dy_ref, o_ref):
        i = pl.program_id(1)
        cur = c_ref[...]
        dy = dy_ref[...]
        x_m1, x_p1 = _shifted(cur, p_ref[...], n_ref[...], i, rt, nlt, ntiles)
        sums = [jnp.sum(t * dy, axis=0, keepdims=True) for t in (x_m1, cur, x_p1)] + [jnp.sum(dy, axis=0, keepdims=True)]
        row = lax.broadcasted_iota(jnp.int32, (8, 1), 0)
        part = jnp.zeros((8, tc), F32)
        for k, s in enumerate(sums):
            part = jnp.where(row == k, s, part)

        @pl.when(i == 0)
        def _():
            o_ref[...] = part

        @pl.when(i != 0)
        def _():
            o_ref[...] += part

    return pl.pallas_call(
        body, name=name, out_shape=jax.ShapeDtypeStruct((8, c), F32), grid=(c // tc, ntiles),
        in_specs=_halo_specs(rt, tc, nrows, True) + [pl.BlockSpec((rt, tc), lambda j, i: (i, j))],
        out_specs=pl.BlockSpec((8, tc), lambda j, i: (0, j)),
        compiler_params=_cparams(("parallel", "arbitrary")),
    )(x, x, x, dy)


def _ffn_gate_fwd(p, w8, d, name):
    nrows, ff = p.shape[0], p.shape[1] // 2
    rt, nlt = d.rt, d.nlt
    tc = _tile(ff, GLU_COLS)
    nb = ff // tc
    ntiles = nrows // rt

    def body(gp_ref, gc_ref, gn_ref, vp_ref, vc_ref, vn_ref, wg_ref, wv_ref, o_ref):
        i = pl.program_id(0)
        us = []
        for p_ref, c_ref, n_ref, w_ref in ((gp_ref, gc_ref, gn_ref, wg_ref), (vp_ref, vc_ref, vn_ref, wv_ref)):
            cur, w = c_ref[...], w_ref[...]
            x_m1, x_p1 = _shifted(cur, p_ref[...], n_ref[...], i, rt, nlt, ntiles)
            us.append(x_m1 * w[0:1] + cur * w[1:2] + x_p1 * w[2:3] + w[3:4])
        o_ref[...] = (_silu(us[0]) * us[1]).astype(o_ref.dtype)

    r8, last8 = rt // 8, nrows // 8 - 1

    def halo(off):
        return [pl.BlockSpec((8, tc), lambda i, j: (jnp.maximum(i * r8 - 1, 0), j + off)),
                pl.BlockSpec((rt, tc), lambda i, j: (i, j + off)),
                pl.BlockSpec((8, tc), lambda i, j: (jnp.minimum((i + 1) * r8, last8), j + off))]
    return pl.pallas_call(
        body, name=name, out_shape=jax.ShapeDtypeStruct((nrows, ff), BF16), grid=(ntiles, nb),
        in_specs=halo(0) + halo(nb) + [pl.BlockSpec((8, tc), lambda i, j: (0, j)), pl.BlockSpec((8, tc), lambda i, j: (0, j + nb))],
        out_specs=pl.BlockSpec((rt, tc), lambda i, j: (i, j)),
        compiler_params=_cparams(("parallel", "parallel")),
    )(p, p, p, p, p, p, w8, w8)


def _ffn_gate_bwd(p, da, w8, d, name):
    nrows, ff = da.shape
    rt, nlt = d.rt, d.nlt
    tc = _tile(ff, GLU_COLS)
    nb = ff // tc
    ntiles = nrows // rt
    ext = rt + 16

    def body(gp_ref, gc_ref, gn_ref, vp_ref, vc_ref, vn_ref, ap_ref, ac_ref, an_ref, wg_ref, wv_ref,
             dpg_ref, dpv_ref, cg_ref, cv_ref):
        i = pl.program_id(1)
        first = jnp.logical_or(i == 0, i == nlt)
        last = jnp.logical_or(i == nlt - 1, i == ntiles - 1)

        def extended(p_ref, c_ref, n_ref):
            return jnp.concatenate([jnp.where(first, 0.0, p_ref[...]), c_ref[...], jnp.where(last, 0.0, n_ref[...])], axis=0)

        def conv(x, w):
            return pltpu.roll(x, 1, 0) * w[0:1] + x * w[1:2] + pltpu.roll(x, ext - 1, 0) * w[2:3]

        wg, wv = wg_ref[...], wv_ref[...]
        pg, pv, da_e = extended(gp_ref, gc_ref, gn_ref), extended(vp_ref, vc_ref, vn_ref), extended(ap_ref, ac_ref, an_ref)
        ug = conv(pg, wg) + wg[3:4]
        uv = conv(pv, wv) + wv[3:4]
        sg = _sigmoid(ug)
        dug = da_e * uv * (sg * (1.0 + ug * (1.0 - sg)))
        duv = da_e * (ug * sg)
        row = lax.broadcasted_iota(jnp.int32, (8, 1), 0)
        for p_e, du, w, dp_ref, c_ref in ((pg, dug, wg, dpg_ref, cg_ref), (pv, duv, wv, dpv_ref, cv_ref)):
            dp = pltpu.roll(du, 1, 0) * w[2:3] + du * w[1:2] + pltpu.roll(du, ext - 1, 0) * w[0:1]
            dp_ref[...] = dp[8:rt + 8].astype(dp_ref.dtype)
            du_c = du[8:rt + 8]
            sums = [jnp.sum(t[8:rt + 8] * du_c, axis=0, keepdims=True)
                    for t in (pltpu.roll(p_e, 1, 0), p_e, pltpu.roll(p_e, ext - 1, 0))] + [jnp.sum(du_c, axis=0, keepdims=True)]
            part = jnp.zeros((8, tc), F32)
            for k, s in enumerate(sums):
                part = jnp.where(row == k, s, part)

            @pl.when(i == 0)
            def _(c_ref=c_ref, part=part):
                c_ref[...] = part

            @pl.when(i != 0)
            def _(c_ref=c_ref, part=part):
                c_ref[...] += part

    r8, last8 = rt // 8, nrows // 8 - 1

    def halo(off):
        return [pl.BlockSpec((8, tc), lambda j, i: (jnp.maximum(i * r8 - 1, 0), j + off)),
                pl.BlockSpec((rt, tc), lambda j, i: (i, j + off)),
                pl.BlockSpec((8, tc), lambda j, i: (jnp.minimum((i + 1) * r8, last8), j + off))]
    tile = pl.BlockSpec((rt, tc), lambda j, i: (i, j))
    acc = pl.BlockSpec((8, tc), lambda j, i: (0, j))
    dpg, dpv, cg, cv = pl.pallas_call(
        body, name=name,
        out_shape=(jax.ShapeDtypeStruct((nrows, ff), BF16), jax.ShapeDtypeStruct((nrows, ff), BF16),
                   jax.ShapeDtypeStruct((8, ff), F32), jax.ShapeDtypeStruct((8, ff), F32)),
        grid=(nb, ntiles),
        in_specs=halo(0) + halo(nb) + halo(0) + [acc, pl.BlockSpec((8, tc), lambda j, i: (0, j + nb))],
        out_specs=(tile, tile, acc, acc),
        compiler_params=_cparams(("parallel", "arbitrary")),
    )(p, p, p, p, p, p, da, da, da, w8, w8)
    return dpg, dpv, jnp.concatenate([cg, cv], axis=1)


def _w8(w3, b=None):
    c = w3.shape[1]
    brow = jnp.zeros((1, c), F32) if b is None else b.reshape(1, c)
    return jnp.concatenate([w3, brow, jnp.zeros((4, c), F32)], axis=0)


def _rms(x, w):
    r = lax.rsqrt(jnp.mean(x * x, axis=-1, keepdims=True) + EPS)
    xh = x * r
    return xh * w, xh, r


def _rms_bwd(dy, xh, r, w):
    dxh = dy * w
    dx = r * (dxh - xh * jnp.mean(dxh * xh, axis=-1, keepdims=True))
    return dx, jnp.sum(dy * xh, axis=0, keepdims=True)


def _mod_bwd(dh, x, w, shift, scale):
    n, xh, r = _rms(x, w)
    dshift = jnp.sum(dh, axis=0, keepdims=True)
    dscale = jnp.sum(dh * n, axis=0, keepdims=True)
    dx, dw = _rms_bwd(dh * (1.0 + scale), xh, r, w)
    return dx, dw, dshift, dscale


def _sigmoid(x):
    return 1.0 / (1.0 + jnp.exp(-x))


def _silu(x):
    return x * _sigmoid(x)


def _dsilu(x):
    s = _sigmoid(x)
    return s * (1.0 + x * (1.0 - s))


def _rope(x, cos, sin_s):
    return x * cos + pltpu.roll(x, LANE // 2, 1) * sin_s


def _rope_t(dy, cos, sin_s):
    return dy * cos + pltpu.roll(dy * sin_s, LANE // 2, 1)


def _flash_fwd(q, kcat, kv, d, rides=None):
    riders = rides.take("flash_fwd") if rides is not None else []
    nr = len(riders)
    r_in, r_shapes, r_out, r_sems = _ride_plumbing(riders)
    nt, h, tq = d.NT, d.H, d.tq
    tkb = _tile(d.T, 4096)
    n_big = d.T // tkb
    nq_lat = d.T // tq

    def body(*refs):
        q_ref, k_ref, v_ref = refs[:3]
        o_ref, ob_ref, lse_ref = refs[3 + nr:6 + nr]
        qi = pl.program_id(1)
        if nr:
            start, wait = _ride_copies([st for _, _, st in riders], refs[3:3 + nr], refs[6 + nr:6 + 2 * nr], *refs[6 + 2 * nr:])
            pl.when(jnp.logical_and(pl.program_id(0) == 0, qi == 0))(start)
        q_t = q_ref[...]

        def chunk(k0, tk, carry):
            m, l, acc = carry
            ks = pl.ds(k0, tk)
            s = _dot(q_t, k_ref[ks, :], 'nt')
            m_new = jnp.maximum(m, jnp.max(s, axis=1, keepdims=True))
            p = jnp.exp2(s - m_new)
            alpha = jnp.exp2(m - m_new)
            return m_new, alpha * l + jnp.sum(p, axis=1, keepdims=True), alpha * acc + _dot(p, v_ref[ks, :], 'nn')

        def step(k0, tk, carry):
            sub = min(tk, FLASH_CHUNK)
            for c in range(tk // sub):
                off = k0 + c * sub
                carry = chunk(off if isinstance(off, int) else pl.multiple_of(off, sub), sub, carry)
            return carry

        init = (jnp.full((tq, 1), -1e30, F32), jnp.zeros((tq, 1), F32), jnp.zeros((tq, LANE), F32))
        trips = jnp.where(qi < nq_lat, n_big, 0)
        carry = lax.fori_loop(0, trips, lambda t, c: step(pl.multiple_of(t * tkb, tkb), tkb, c), init)
        m, l, acc = step(d.T, d.TC, carry)
        o = acc / l
        o_ref[...] = o
        ob_ref[...] = o.astype(BF16)
        lse_ref[...] = jnp.broadcast_to(m + jnp.log2(l), (tq, LANE))
        if nr:
            pl.when(jnp.logical_and(pl.program_id(0) == h - 1, qi == nt // tq - 1))(wait)

    out = pl.BlockSpec((tq, LANE), lambda hh, i: (i, hh))
    res = pl.pallas_call(
        body, name="flash_fwd",
        out_shape=[jax.ShapeDtypeStruct((nt, d.HQ), F32), jax.ShapeDtypeStruct((nt, d.HQ), BF16),
                   jax.ShapeDtypeStruct((nt, d.HQ), F32)] + r_shapes,
        grid=(h, nt // tq),
        in_specs=[pl.BlockSpec((tq, 2 * LANE), lambda hh, i: (i, hh)), pl.BlockSpec((nt, 2 * LANE), lambda hh, i: (0, hh)),
                  pl.BlockSpec((nt, LANE), lambda hh, i: (0, h + hh))] + r_in,
        out_specs=[out, out, out] + r_out,
        scratch_shapes=r_sems,
        compiler_params=_cparams(("arbitrary", "arbitrary") if nr else ("parallel", "parallel")),
    )(q, kcat, kv, *[x for _, x, _ in riders])
    for (key, _, _), arr in zip(riders, res[3:]):
        rides.done[key] = arr
    return res[0], res[1], res[2]


def _flash_bwd(q, kcat, kv, do, ld, d, rides=None):
    riders = rides.take("flash_bwd") if rides is not None else []
    nr = len(riders)
    r_in, r_shapes, r_out, r_sems = _ride_plumbing(riders)
    nt, h, tk = d.NT, d.H, d.tq
    tqb =_tile(d.T, 2048)
    n_big = d.T // tqb
    nk_lat = d.T // tk

    def body(*refs):
        q_ref, do_ref, ld_ref, k_ref, v_ref = refs[:5]
        dq_ref, dk_ref, dv_ref = refs[5 + nr:8 + nr]
        kt = pl.program_id(1)
        if nr:
            start, wait = _ride_copies([st for _, _, st in riders], refs[5:5 + nr], refs[8 + nr:8 + 2 * nr], *refs[8 + 2 * nr:])
            pl.when(jnp.logical_and(pl.program_id(0) == 0, kt == 0))(start)
        k_t, v_t = k_ref[...], v_ref[...]

        @pl.when(kt == 0)
        def _():
            dq_ref[...] = jnp.zeros_like(dq_ref)

        def step(q0, tq, carry):
            sub = min(tq, FLASH_CHUNK)
            for c in range(tq // sub):
                off = q0 + c * sub
                carry = chunk(off if isinstance(off, int) else pl.multiple_of(off, sub), sub, carry)
            return carry

        def chunk(q0, tq, carry):
            dk, dv = carry
            qs = pl.ds(q0, tq)
            q_t, do_t, ld_t = q_ref[qs, :], do_ref[qs, :], ld_ref[qs, :]
            p = jnp.exp2(_dot(q_t, k_t, 'nt') - ld_t[:, 0:1])
            ds = p * (_dot(do_t, v_t, 'nt') - ld_t[:, LANE // 2:LANE // 2 + 1])
            dq_ref[qs, :] += _dot(ds, k_t, 'nn')
            return dk + _dot(ds, q_t, 'tn'), dv + _dot(p, do_t, 'tn')

        init = (jnp.zeros((tk, 2 * LANE), F32), jnp.zeros((tk, LANE), F32))
        carry = lax.fori_loop(0, n_big, lambda t, c: step(pl.multiple_of(t * tqb, tqb), tqb, c), init)
        dk_ref[...] = carry[0] * LN2
        dv_ref[...] = carry[1].astype(BF16)

        @pl.when(kt >= nk_lat)
        def _():
            dk, dv = step(d.T, d.TC, carry)
            dk_ref[...] = dk * LN2
            dv_ref[...] = dv.astype(BF16)

        if nr:
            pl.when(jnp.logical_and(pl.program_id(0) == h - 1, kt == nt // tk - 1))(wait)

    res = lambda w: pl.BlockSpec((nt, w), lambda hh, i: (0, hh))
    outs = pl.pallas_call(
        body, name="flash_bwd",
        out_shape=[jax.ShapeDtypeStruct((nt, 2 * d.HQ), F32), jax.ShapeDtypeStruct((nt, 2 * d.HQ), F32),
                   jax.ShapeDtypeStruct((nt, d.HQ), BF16)] + r_shapes,
        grid=(h, nt // tk),
        in_specs=[res(2 * LANE), res(LANE), res(LANE), pl.BlockSpec((tk, 2 * LANE), lambda hh, i: (i, hh)),
                  pl.BlockSpec((tk, LANE), lambda hh, i: (i, h + hh))] + r_in,
        out_specs=[res(2 * LANE), pl.BlockSpec((tk, 2 * LANE), lambda hh, i: (i, hh)),
                   pl.BlockSpec((tk, LANE), lambda hh, i: (i, hh))] + r_out,
        scratch_shapes=r_sems,
        compiler_params=_cparams(("arbitrary", "arbitrary") if nr else ("parallel", "arbitrary")),
    )(q, do, ld, kcat, kv, *[x for _, x, _ in riders])
    for (key, _, _), arr in zip(riders, outs[3:]):
        rides.done[key] = arr
    return outs[0], outs[1], outs[2]


def _tri(dirn):
    r = lax.broadcasted_iota(jnp.int32, (CHUNK, CHUNK), 0)
    c = lax.broadcasted_iota(jnp.int32, (CHUNK, CHUNK), 1)
    return (c <= r) if dirn == 0 else (c >= r)


def _exact_mask_dot(mask_bf16, x):
    hi = x.astype(BF16)
    r1 = x - hi.astype(F32)
    mid = r1.astype(BF16)
    lo = (r1 - mid.astype(F32)).astype(BF16)
    dot = lambda t: lax.dot_general(mask_bf16, t, (((1,), (0,)), ((), ())), preferred_element_type=F32)
    return dot(hi) + dot(mid) + dot(lo)


def _gla_terms(q, k, g, dirn, dk):
    mb = _tri(dirn)
    b = _exact_mask_dot(mb.astype(BF16), g)
    tot = jnp.sum(g, axis=0, keepdims=True)
    qe = q * (float(dk) ** -0.5) * jnp.exp(b)
    ke = k * jnp.exp(-b)
    kd = k * jnp.exp(tot - b)
    att = jnp.where(mb, _dot(qe, ke, 'nt'), 0.0)
    return mb, b, tot, qe, ke, kd, att


def _gla_block_index(d, dirn):
    nb = d.NT // d.rt
    if dirn == 0:
        return lambda s: (s + d.nlt) % nb
    return lambda s: nb - 1 - s


def _gla_rows(j, nsub, dirn):
    r0 = (j if dirn == 0 else nsub - 1 - j) * CHUNK
    return slice(r0, r0 + CHUNK)


def _gla_fwd(z, g, d):
    nt, gh, dk, dv, rb = d.NT, d.GH, d.dk, d.dv, d.rt
    nb, nsub = nt // rb, rb // CHUNK
    qb, kb = d.z1_q // dk, d.z1_k // dk

    def body(*refs):
        @pl.when(pl.program_id(1) == 0)
        def _():
            for dirn in (0, 1):
                refs[12 + dirn][...] = jnp.zeros_like(refs[12 + dirn])

        for dirn in (0, 1):
            q_ref, k_ref, v_ref, g_ref = refs[4 * dirn:4 * dirn + 4]
            o_ref, st_ref, state = refs[8 + 2 * dirn], refs[9 + 2 * dirn], refs[12 + dirn]
            pre = []
            for j in range(nsub):
                rs = _gla_rows(j, nsub, dirn)
                v = v_ref[rs, :]
                _, _, tot, qe, _, kd, att = _gla_terms(q_ref[rs, :], k_ref[rs, :], g_ref[rs, :], dirn, dk)
                o_ref[rs, :] = _dot(att, v, 'nn')
                pre.append((rs, qe, jnp.exp(tot), _dot(v, kd, 'tn')))
            st = state[...]
            for j, (rs, qe, decay, update) in enumerate(pre):
                st_ref[j] = st
                o_ref[rs, :] += _dot(qe, st, 'nt')
                st = st * decay + update
            state[...] = st

    def col(dirn, w, off):
        bidx = _gla_block_index(d, dirn)
        return pl.BlockSpec((rb, w), lambda hh, s: (bidx(s), off + hh))

    in_specs, out_shapes, out_specs = [], [], []
    for dirn in (0, 1):
        in_specs += [col(dirn, dk, qb), col(dirn, dk, kb), col(dirn, dv, 0), col(dirn, dk, dirn * gh)]
        out_shapes += [jax.ShapeDtypeStruct((nt, d.VAL), F32), jax.ShapeDtypeStruct((gh, nb * nsub, dv, dk), F32)]
        out_specs += [col(dirn, dv, 0), pl.BlockSpec((None, nsub, dv, dk), lambda hh, s: (hh, s, 0, 0))]
    return pl.pallas_call(
        body, name="gla_fwd", out_shape=out_shapes, grid=(gh, nb), in_specs=in_specs, out_specs=out_specs,
        scratch_shapes=[pltpu.VMEM((dv, dk), F32), pltpu.VMEM((dv, dk), F32)],
        compiler_params=_cparams(("parallel", "arbitrary")),
    )(z, z, z, g, z, z, z, g)


def _gla_bwd(z, g, do, states_fw, states_bw, d, rides=None):
    riders = rides.take("gla_bwd") if rides is not None else []
    nr = len(riders)
    r_in, r_shapes, r_out, r_sems = _ride_plumbing(riders)
    nt, gh, dk, dv, rb = d.NT, d.GH, d.dk, d.dv, d.rt
    nb, nsub = nt // rb, rb // CHUNK
    qb, kb = d.z1_q // dk, d.z1_k // dk
    qscale = float(dk) ** -0.5
    n_in, n_out = 12, 8

    def body(*refs):
        outs = refs[n_in + nr:n_in + nr + n_out]
        dstates = refs[n_in + n_out + 2 * nr:n_in + n_out + 2 * nr + 2]
        if nr:
            start, wait = _ride_copies([st for _, _, st in riders], refs[n_in:n_in + nr],
                                       refs[n_in + nr + n_out:n_in + n_out + 2 * nr], *refs[n_in + n_out + 2 * nr + 2:])
            pl.when(jnp.logical_and(pl.program_id(0) == 0, pl.program_id(1) == 0))(start)

        @pl.when(pl.program_id(1) == 0)
        def _():
            for dstate in dstates:
                dstate[...] = jnp.zeros_like(dstate)

        for dirn in (0, 1):
            q_ref, k_ref, v_ref, g_ref, do_ref, st_ref = refs[6 * dirn:6 * dirn + 6]
            dq_ref, dk_ref, dv_ref, dg_ref = outs[4 * dirn:4 * dirn + 4]
            dst = dstates[dirn][...]
            for j in reversed(range(nsub)):
                rs = _gla_rows(j, nsub, dirn)
                q, k, v, g_, dout, st = q_ref[rs, :], k_ref[rs, :], v_ref[rs, :], g_ref[rs, :], do_ref[rs, :], st_ref[j]
                mb, b, tot, qe, ke, kd, att = _gla_terms(q, k, g_, dirn, dk)
                etot = jnp.exp(tot)
                datt = jnp.where(mb, _dot(dout, v, 'nt'), 0.0)
                dv_ref[rs, :] = _dot(att, dout, 'tn') + _dot(kd, dst, 'nt')
                dqe = _dot(datt, ke, 'nn') + _dot(dout, st, 'nn')
                dke = _dot(datt, qe, 'tn')
                dkd = _dot(v, dst, 'nn')
                dq_ref[rs, :] = dqe * (qscale * jnp.exp(b))
                dk_ref[rs, :] = dke * jnp.exp(-b) + dkd * jnp.exp(tot - b)
                dkd_kd = dkd * kd
                db = dqe * qe - dke * ke - dkd_kd
                dtot = jnp.sum(dkd_kd, axis=0, keepdims=True) + jnp.sum(dst * st, axis=0, keepdims=True) * etot
                dg_ref[rs, :] = _exact_mask_dot(_tri(1 - dirn).astype(BF16), db) + dtot
                dst = dst * etot + _dot(dout, qe, 'tn')
            dstates[dirn][...] = dst
        if nr:
            pl.when(jnp.logical_and(pl.program_id(0) == gh - 1, pl.program_id(1) == nb - 1))(wait)

    def col(dirn, w, off):
        bfwd = _gla_block_index(d, dirn)
        return pl.BlockSpec((rb, w), lambda hh, s: (bfwd(nb - 1 - s), off + hh))

    in_specs, out_shapes, out_specs = [], [], []
    for dirn in (0, 1):
        in_specs += [col(dirn, dk, qb), col(dirn, dk, kb), col(dirn, dv, 0), col(dirn, dk, dirn * gh), col(dirn, dv, 0),
                     pl.BlockSpec((None, nsub, dv, dk), lambda hh, s: (hh, nb - 1 - s, 0, 0))]
        out_shapes += [jax.ShapeDtypeStruct((nt, d.KEY), F32), jax.ShapeDtypeStruct((nt, d.KEY), F32),
                       jax.ShapeDtypeStruct((nt, d.VAL), F32), jax.ShapeDtypeStruct((nt, d.KEY), F32)]
        out_specs += [col(dirn, dk, 0), col(dirn, dk, 0), col(dirn, dv, 0), col(dirn, dk, 0)]
    res = pl.pallas_call(
        body, name="gla_bwd", out_shape=out_shapes + r_shapes, grid=(gh, nb), in_specs=in_specs + r_in,
        out_specs=out_specs + r_out,
        scratch_shapes=[pltpu.VMEM((dv, dk), F32), pltpu.VMEM((dv, dk), F32)] + r_sems,
        compiler_params=_cparams(("arbitrary", "arbitrary") if nr else ("parallel", "arbitrary")),
    )(z, z, z, g, do, states_fw, z, z, z, g, do, states_bw, *[x for _, x, _ in riders])
    for (key, _, _), arr in zip(riders, res[n_out:]):
        rides.done[key] = arr
    return res[0:4], res[4:8]


def _rope_pad(w):
    q = QK_ROPE // 4
    a1, a2, b1, b2 = (w[..., k * q:(k + 1) * q] for k in range(4))
    z = jnp.zeros(w.shape[:-1] + (LANE // 2 - 2 * q,), w.dtype)
    return jnp.concatenate([a1, b1, z, a2, b2, z], axis=-1)


def _rope_unpad(g):
    q = QK_ROPE // 4
    h = LANE // 2
    return jnp.concatenate([g[..., 0:q], g[..., h:h + q], g[..., q:2 * q], g[..., h + q:h + 2 * q]], axis=-1)


def _win0_to_kernel(w, d):
    kv_lat = w[:, :d.KVL]
    k_rope = w[:, d.KVL:d.KVL + QK_ROPE]
    q_lat = w[:, d.KVL + QK_ROPE:d.KVL + QK_ROPE + d.QL]
    rest = w[:, d.KVL + QK_ROPE + d.QL:]
    parts = [q_lat, kv_lat, _rope_pad(k_rope)]
    if d.z0_pad:
        parts.append(jnp.zeros((w.shape[0], d.z0_pad), w.dtype))
    return jnp.concatenate(parts + [rest], axis=1)


def _win0_from_kernel(g, d):
    return jnp.concatenate([g[:, d.z0_kv:d.z0_kv + d.KVL], _rope_unpad(g[:, d.z0_kr:d.z0_kr + LANE]), g[:, :d.QL],
                            g[:, d.z0_ax:]], axis=1)


def _wqb_to_kernel(w, d):
    wr = w.reshape(d.QL, d.H, QK_NOPE + QK_ROPE)
    return jnp.concatenate([wr[:, :, :QK_NOPE], _rope_pad(wr[:, :, QK_NOPE:])], axis=2).reshape(d.QL, 2 * d.HQ)


def _wqb_from_kernel(g, d):
    gr = g.reshape(d.QL, d.H, QK_NOPE + LANE)
    return jnp.concatenate([gr[:, :, :QK_NOPE], _rope_unpad(gr[:, :, QK_NOPE:])], axis=2).reshape(d.QL, d.H * (QK_NOPE + QK_ROPE))


def _wkvb_to_kernel(w, d):
    return w.reshape(d.KVL, d.H, 2, LANE).transpose(0, 2, 1, 3).reshape(d.KVL, 2 * d.HQ)


def _wkvb_from_kernel(g, d):
    return g.reshape(d.KVL, 2, d.H, LANE).transpose(0, 2, 1, 3).reshape(d.KVL, 2 * d.HQ)


def _win1_to_kernel(w, d):
    k = w[:, :d.KEY]
    v = w[:, d.KEY:d.KEY + d.VAL]
    lr = w[:, d.KEY + d.VAL:d.KEY + d.VAL + 2 * GATE_RANK]
    q = w[:, d.KEY + d.VAL + 2 * GATE_RANK:2 * d.KEY + d.VAL + 2 * GATE_RANK]
    og = w[:, 2 * d.KEY + d.VAL + 2 * GATE_RANK:]
    return jnp.concatenate([v, og, k, q, lr, jnp.zeros((w.shape[0], LANE - 2 * GATE_RANK), w.dtype)], axis=1)


def _win1_from_kernel(g, d):
    return jnp.concatenate([g[:, d.z1_k:d.z1_k + d.KEY], g[:, :d.VAL], g[:, d.z1_lr:d.z1_lr + 2 * GATE_RANK],
                            g[:, d.z1_q:d.z1_q + d.KEY], g[:, d.z1_og:d.z1_og + d.VAL]], axis=1)


def _gate_weight(fw_w, bw_w, d):
    z = jnp.zeros((GATE_RANK, d.KEY), F32)
    return jnp.concatenate([jnp.concatenate([fw_w, z], axis=1), jnp.concatenate([z, bw_w], axis=1),
                            jnp.zeros((LANE - 2 * GATE_RANK, 2 * d.KEY), F32)], axis=0)


def _rope_tables(d):
    t = jnp.arange(d.T)
    inv = ROPE_THETA ** (-jnp.arange(0, QK_ROPE // 2, 2, dtype=F32) / (QK_ROPE // 2))
    ar = (t // GRID_W).astype(F32)[:, None] * inv
    ac = (t % GRID_W).astype(F32)[:, None] * inv
    z = jnp.zeros((d.T, LANE // 2 - 2 * inv.shape[0]), F32)
    ang = jnp.concatenate([ar, ac, z, ar, ac, z], axis=1)
    cos = jnp.concatenate([jnp.cos(ang), jnp.ones((d.TC, LANE), F32)], axis=0)
    sin = jnp.concatenate([jnp.sin(ang), jnp.zeros((d.TC, LANE), F32)], axis=0)
    sgn = jnp.where(jnp.arange(LANE) < LANE // 2, -1.0, 1.0).astype(F32)
    return cos, sin * sgn


def _row(arr, cb=0, width=None):
    return ('row', arr, cb, arr.shape[1] if width is None else width)


def _mod_fwd(x, nw, shift, scale, d, name):
    def body(i, j, x, w, sh, sc):
        return (_rms(x, w)[0] * (1.0 + sc) + sh,)
    return _rowwise(body, d.NT, d.rt, d.nlt, [_row(x), ('full', nw), ('seg', shift), ('seg', scale)],
                    [('row', d.D, BF16)], name)[0]


def _mod_bwd_call(dres, dh, x, nw, shift, scale, d, name):
    def body(i, j, dres, dh, x, w, sh, sc):
        dx, dw, dsh, dsc = _mod_bwd(dh, x, w, sh, sc)
        return dres + dx, dw, dsh, dsc
    return _rowwise(body, d.NT, d.rt, d.nlt, [_row(dres), _row(dh), _row(x), ('full', nw), ('seg', shift), ('seg', scale)],
                    [('row', d.D, F32), ('acc', (1, d.D)), ('segacc', d.D), ('segacc', d.D)], name)


def _res_mod_fwd(x, y, gate, nw, shift, scale, d, name):
    def body(i, j, x, y, g, w, sh, sc):
        x1 = x + g * y
        return x1, _rms(x1, w)[0] * (1.0 + sc) + sh
    return _rowwise(body, d.NT, d.rt, d.nlt, [_row(x), _row(y), ('seg', gate), ('full', nw), ('seg', shift), ('seg', scale)],
                    [('row', d.D, F32), ('row', d.D, BF16)], name)


def _res_mod_bwd(dx2, dh2, x1, y, gate, nw, shift, scale, d, name):
    def body(i, j, dx2, dh2, x1, y, g, w, sh, sc):
        dx, dw, dsh, dsc = _mod_bwd(dh2, x1, w, sh, sc)
        dx1 = dx2 + dx
        return dx1, g * dx1, jnp.sum(dx1 * y, axis=0, keepdims=True), dw, dsh, dsc
    return _rowwise(body, d.NT, d.rt, d.nlt,
                    [_row(dx2), _row(dh2), _row(x1), _row(y), ('seg', gate), ('full', nw), ('seg', shift), ('seg', scale)],
                    [('row', d.D, F32), ('row', d.D, BF16), ('segacc', d.D), ('acc', (1, d.D)), ('segacc', d.D),
                     ('segacc', d.D)], name)


def _res_fwd(x1, f, gate, d, name):
    return _rowwise(lambda i, j, x1, f, g: (x1 + g * f,), d.NT, d.rt, d.nlt, [_row(x1), _row(f), ('seg', gate)],
                    [('row', d.D, F32)], name)[0]


def _res_bwd(dx2, f, gate, d, name):
    def body(i, j, dx2, f, g):
        return g * dx2, jnp.sum(dx2 * f, axis=0, keepdims=True)
    return _rowwise(body, d.NT, d.rt, d.nlt, [_row(dx2), _row(f), ('seg', gate)], [('row', d.D, BF16), ('segacc', d.D)], name)


def _w(w, key):
    if callable(w[key]):
        w[key] = w[key]()
    return w[key]


def _ffn_fwd(h2, w, d, tag, rides):
    p = _mm([(h2, _w(w, 'ffn_up'))], 'nn', F32, tag + '_up', rides=rides)
    a = _ffn_gate_fwd(p, _w8(w['ffn_conv_w'], w['ffn_conv_b']), d, tag + '_gate')
    f = _mm([(a, _w(w, 'ffn_down'))], 'nn', F32, tag + '_down', rides=rides)
    return p, a, f


def _ffn_bwd(df, h2, p, a, w, d, tag, rides, emit):
    w_up = w['ffn_up']
    da = _mm([(df, w['ffn_down'])], 'nt', F32, tag + '_down_dx', tn=GLU_COLS, rides=rides)
    emit('ffn_down', _mm([(a, df)], 'tn', GRAD_WIRE, tag + '_down_dw'))
    dpg, dpv, conv_g = _ffn_gate_bwd(p, da, _w8(w['ffn_conv_w'], w['ffn_conv_b']), d, tag + '_gate_bwd')
    dh2 = _mm([(dpg, w_up[:, :d.FF]), (dpv, w_up[:, d.FF:])], 'nt', F32, tag + '_up_dx', rides=rides)
    emit('ffn_up', jnp.concatenate([_mm([(h2, dpg)], 'tn', GRAD_WIRE, tag + '_up_dw_gate', tn=GLU_COLS),
                                    _mm([(h2, dpv)], 'tn', GRAD_WIRE, tag + '_up_dw_val', tn=GLU_COLS)], axis=1))
    return dh2, conv_g[0:3], conv_g[3]


def _ab_fwd(z, w, cos, sin_s, d, rides):
    qnw, kvnw = w['q_norm'], w['kv_norm']

    def prep(i, j, zq, zkv, zkr, qw, kw, cos, sin_s):
        return _rms(zq, qw)[0], _rms(zkv, kw)[0], _rope(zkr, cos, sin_s)
    qn, kvn, kr = _rowwise(prep, d.NT, d.rt, d.nlt,
                           [_row(z, 0, d.QL), _row(z, d.z0_kv // d.KVL, d.KVL), _row(z, d.z0_kr // LANE, LANE),
                            ('full', qnw), ('full', kvnw), _row(cos), _row(sin_s)],
                           [('row', d.QL, BF16), ('row', d.KVL, BF16), ('row', LANE, BF16)], 'ab_prep')
    qraw = _mm([(qn, w['w_qb'])], 'nn', F32, 'ab_qb')
    kv = _mm([(kvn, w['w_kvb'])], 'nn', BF16, 'ab_kvb')

    def qrope(i, j, qraw, cos, sin_s):
        parts = []
        for h in range(d.H):
            parts += [qraw[:, 2 * h * LANE:(2 * h + 1) * LANE], _rope(qraw[:, (2 * h + 1) * LANE:(2 * h + 2) * LANE], cos, sin_s)]
        return (jnp.concatenate(parts, axis=1) * (ATTN_SCALE / LN2),)
    q = _rowwise(qrope, d.NT, d.rt, d.nlt, [_row(qraw), _row(cos), _row(sin_s)], [('row', 2 * d.HQ, BF16)], 'ab_qrope')[0]

    def kcat_body(i, j, kn, kr):
        parts = []
        for h in range(d.H):
            parts += [kn[:, h * LANE:(h + 1) * LANE], kr]
        return (jnp.concatenate(parts, axis=1),)
    kcat = _rowwise(kcat_body, d.NT, d.rt, d.nlt, [_row(kv, 0, d.HQ), _row(kr)], [('row', 2 * d.HQ, BF16)], 'ab_kcat')[0]
    o, ob, lse = _flash_fwd(q, kcat, kv, d, rides)
    ab = d.z0_ax // d.CC
    s = _rowwise(lambda i, j, ax, ac: (ax * ac,), d.NT, d.rt, d.nlt, [_row(z, ab, d.CC), _row(z, ab + 2, d.CC)],
                 [('row', d.CC, F32)], 'ab_conv_in')[0]
    cv = _dwconv(s, _w8(w['conv_a']), d, F32, 'ab_conv')
    ymix = _rowwise(lambda i, j, a_b, cv, ob: (jnp.concatenate([(a_b * cv).astype(BF16), ob], axis=1),), d.NT, d.rt, d.nlt,
                    [_row(z, ab + 1, d.CC), _row(cv), _row(ob)], [('row', d.CC + d.HQ, BF16)], 'ab_mix')[0]
    return ymix, dict(qn=qn, kvn=kvn, q=q, kcat=kcat, kv=kv, o=o, lse=lse, s=s, cv=cv)


def _ab_bwd(dymix, z, sv, w, cos, sin_s, d, rides, emit):
    qnw, kvnw = w['q_norm'], w['kv_norm']
    assert d.CC % d.HQ == 0

    def dprep(i, j, dmo, o, lse):
        lane = lax.broadcasted_iota(jnp.int32, (1, LANE), 1)
        cols = []
        for h in range(d.H):
            hs = slice(h * LANE, (h + 1) * LANE)
            delta = jnp.sum(dmo[:, hs] * o[:, hs], axis=1, keepdims=True)
            cols.append(jnp.where(lane < LANE // 2, lse[:, hs], delta))
        return dmo, jnp.concatenate(cols, axis=1)
    do, ld = _rowwise(dprep, d.NT, d.rt, d.nlt, [_row(dymix, d.CC // d.HQ, d.HQ), _row(sv['o']), _row(sv['lse'])],
                      [('row', d.HQ, BF16), ('row', d.HQ, F32)], 'ab_do')
    dq, dkc, dvv = _flash_bwd(sv['q'], sv['kcat'], sv['kv'], do, ld, d, rides)

    def qrope_t(i, j, dq, cos, sin_s):
        dq = dq * ATTN_SCALE
        parts = []
        for h in range(d.H):
            parts += [dq[:, 2 * h * LANE:(2 * h + 1) * LANE], _rope_t(dq[:, (2 * h + 1) * LANE:(2 * h + 2) * LANE], cos, sin_s)]
        return (jnp.concatenate(parts, axis=1),)
    dqraw = _rowwise(qrope_t, d.NT, d.rt, d.nlt, [_row(dq), _row(cos), _row(sin_s)],
                     [('row', 2 * d.HQ, BF16)], 'ab_qrope_bwd')[0]

    def dkv_body(i, j, dkc, dv):
        dkr = dkc[:, LANE:2 * LANE]
        for h in range(1, d.H):
            dkr = dkr + dkc[:, (2 * h + 1) * LANE:(2 * h + 2) * LANE]
        parts = [dkc[:, 2 * h * LANE:(2 * h + 1) * LANE] for h in range(d.H)] + [dv.astype(F32)]
        return jnp.concatenate(parts, axis=1), dkr
    dkv, dkr = _rowwise(dkv_body, d.NT, d.rt, d.nlt, [_row(dkc), _row(dvv)], [('row', 2 * d.HQ, BF16), ('row', LANE, F32)],
                        'ab_dkv')
    dqn = _mm([(dqraw, w['w_qb'])], 'nt', F32, 'ab_qb_dx')
    emit('w_qb', _mm([(sv['qn'], dqraw)], 'tn', GRAD_WIRE, 'ab_qb_dw'))
    dkvn = _mm([(dkv, w['w_kvb'])], 'nt', F32, 'ab_kvb_dx')
    emit('w_kvb', _mm([(sv['kvn'], dkv)], 'tn', GRAD_WIRE, 'ab_kvb_dw'))
    ab = d.z0_ax // d.CC
    dab, dcv = _rowwise(lambda i, j, dya, cv, a_b: (dya * cv, dya * a_b), d.NT, d.rt, d.nlt,
                        [_row(dymix, 0, d.CC), _row(sv['cv']), _row(z, ab + 1, d.CC)],
                        [('row', d.CC, BF16), ('row', d.CC, F32)], 'ab_mix_bwd')
    ds = _dwconv(dcv, _w8(w['conv_a'][::-1]), d, F32, 'ab_conv_dx')
    conv_g = _dwconv_wgrad(sv['s'], dcv, d, 'ab_conv_dw')

    def assemble(i, j, dqn, dkvn, dkr, zq, zkv, qw, kw, cos, sin_s, ds, ax, ac, dab):
        _, xq, rq = _rms(zq, qw)
        dzq, dqw = _rms_bwd(dqn, xq, rq, qw)
        _, xk, rk = _rms(zkv, kw)
        dzkv, dkw = _rms_bwd(dkvn, xk, rk, kw)
        parts = [dzq, dzkv, _rope_t(dkr, cos, sin_s)]
        if d.z0_pad:
            parts.append(jnp.zeros((dzq.shape[0], d.z0_pad), F32))
        parts += [ds * ac, dab.astype(F32), ds * ax]
        return jnp.concatenate([t.astype(BF16) for t in parts], axis=1), dqw, dkw
    dz, dqw, dkw = _rowwise(assemble, d.NT, d.rt, d.nlt,
                            [_row(dqn), _row(dkvn), _row(dkr), _row(z, 0, d.QL), _row(z, d.z0_kv // d.KVL, d.KVL),
                             ('full', qnw), ('full', kvnw), _row(cos), _row(sin_s), _row(ds), _row(z, ab, d.CC),
                             _row(z, ab + 2, d.CC), _row(dab)],
                            [('row', d.ZW0, BF16), ('acc', (1, d.QL)), ('acc', (1, d.KVL))], 'ab_dz')
    return dz, dict(conv_a=conv_g[0:3], q_norm=dqw, kv_norm=dkw)


def _log_sigmoid(x):
    return jnp.minimum(x, 0.0) - jnp.log(1.0 + jnp.exp(-jnp.abs(x)))


def _gla_fwd_block(z, w, d):
    wg, bg, onw = w['gate_w'], w['gate_b'], w['o_norm']

    def gates(i, j, lr, wg, bg):
        return (_log_sigmoid(_dot(lr, wg, 'nn') + bg) / GATE_NORMALIZER,)
    g = _rowwise(gates, d.NT, d.rt, d.nlt, [_row(z, d.z1_lr // LANE, LANE), ('full', wg), ('full', bg)],
                 [('row', 2 * d.KEY, F32)], 'gla_gates')[0]
    of, stf, ob, stb = _gla_fwd(z, g, d)

    def outp(i, j, of, ob, og, ow):
        o = of + ob
        parts = [_rms(o[:, h * d.dv:(h + 1) * d.dv], ow)[0] for h in range(d.GH)]
        return (jnp.concatenate(parts, axis=1) * _silu(og),)
    ymix = _rowwise(outp, d.NT, d.rt, d.nlt, [_row(of), _row(ob), _row(z, d.z1_og // d.VAL, d.VAL), ('full', onw)],
                    [('row', d.VAL, BF16)], 'gla_out')[0]
    return ymix, dict(g=g, of=of, ob=ob, stf=stf, stb=stb)


def _gla_bwd_block(dymix, z, sv, w, d, rides):
    wg, bg, onw = w['gate_w'], w['gate_b'], w['o_norm']

    def outp_bwd(i, j, dy, of, ob, og, ow):
        o = of + ob
        dn = dy * _silu(og)
        dos, ns = [], []
        dow = jnp.zeros((1, d.dv), F32)
        for h in range(d.GH):
            hs = slice(h * d.dv, (h + 1) * d.dv)
            n, xh, r = _rms(o[:, hs], ow)
            do_h, dw_h = _rms_bwd(dn[:, hs], xh, r, ow)
            dos.append(do_h)
            ns.append(n)
            dow = dow + dw_h
        return jnp.concatenate(dos, axis=1), dy * jnp.concatenate(ns, axis=1) * _dsilu(og), dow
    do, dog, dow = _rowwise(outp_bwd, d.NT, d.rt, d.nlt,
                            [_row(dymix), _row(sv['of']), _row(sv['ob']), _row(z, d.z1_og // d.VAL, d.VAL), ('full', onw)],
                            [('row', d.VAL, F32), ('row', d.VAL, BF16), ('acc', (1, d.dv))], 'gla_out_bwd')
    (dq0, dk0, dv0, dg0), (dq1, dk1, dv1, dg1) = _gla_bwd(z, sv['g'], do, sv['stf'], sv['stb'], d, rides)

    def assemble(i, j, dg0, dg1, lr, wg, bg, dq0, dq1, dk0, dk1, dv0, dv1, dog):
        pre = _dot(lr, wg, 'nn') + bg
        e = jnp.exp(-jnp.abs(pre))
        dpre = jnp.concatenate([dg0, dg1], axis=1) * jnp.where(pre >= 0, e, 1.0) / (1.0 + e) / GATE_NORMALIZER
        dlr = _dot(dpre, wg, 'nt')
        parts = [dv0 + dv1, dog.astype(F32), dk0 + dk1, dq0 + dq1, dlr]
        return (jnp.concatenate([t.astype(BF16) for t in parts], axis=1), _dot(lr, dpre, 'tn'),
                jnp.sum(dpre, axis=0, keepdims=True))
    dz, dwg, dbg = _rowwise(assemble, d.NT, d.rt, d.nlt,
                            [_row(dg0), _row(dg1), _row(z, d.z1_lr // LANE, LANE), ('full', wg), ('full', bg), _row(dq0),
                             _row(dq1), _row(dk0), _row(dk1), _row(dv0), _row(dv1), _row(dog)],
                            [('row', d.ZW1, BF16), ('acc', (LANE, 2 * d.KEY)), ('acc', (1, 2 * d.KEY))], 'gla_dz')
    return dz, dict(gate_fw_w=dwg[:GATE_RANK, :d.KEY], gate_bw_w=dwg[GATE_RANK:2 * GATE_RANK, d.KEY:],
                    gate_fw_b=dbg[:, :d.KEY], gate_bw_b=dbg[:, d.KEY:], o_norm=dow)


def _loss_bwd(x, fnw, target, d):
    def body(i, j, x, w, tgt):
        y, xh, r = _rms(x, w)
        e = y - tgt
        dx, dw = _rms_bwd(e * (1.0 / d.D), xh, r, w)
        lat = i < d.nlt
        part = jnp.sum(jnp.sum(e * e, axis=1, keepdims=True), axis=0, keepdims=True) * (0.5 / d.D)
        return (jnp.where(lat, dx, 0.0), jnp.where(lat, jnp.broadcast_to(part, (8, LANE)), 0.0), jnp.where(lat, dw, 0.0))
    return _rowwise(body, d.NT, d.rt, d.nlt, [_row(x), ('full', fnw), ('rowclamp', target, d.nlt - 1, d.D)],
                    [('row', d.D, F32), ('acc', (8, LANE)), ('acc', (1, d.D))], 'loss')


def _layer_fwd(x, mods, w, mixer_fwd, d, tag, rides):
    sh1, sc1, g1, sh2, sc2, g2 = mods
    h = _mod_fwd(x, w['norm1'], sh1, sc1, d, tag + '_mod1')
    z = _mm([(h, _w(w, 'w_in'))], 'nn', F32, tag + '_in', rides=rides)
    ymix, msv = mixer_fwd(z)
    y = _mm([(ymix, _w(w, 'w_out'))], 'nn', F32, tag + '_out')
    x1, h2 = _res_mod_fwd(x, y, g1, w['norm2'], sh2, sc2, d, tag + '_mod2')
    p, a, f = _ffn_fwd(h2, w, d, tag + '_ffn', rides)
    x2 = _res_fwd(x1, f, g2, d, tag + '_res')
    return x2, dict(x=x, h=h, z=z, ymix=ymix, msv=msv, y=y, x1=x1, h2=h2, p=p, a=a, f=f)


def _layer_bwd(dx2, sv, mods, w, mixer_bwd, d, tag, rides, emit):
    sh1, sc1, g1, sh2, sc2, g2 = mods
    df, dg2 = _res_bwd(dx2, sv['f'], g2, d, tag + '_res_bwd')
    dh2, dconv_w, dconv_b = _ffn_bwd(df, sv['h2'], sv['p'], sv['a'], w, d, tag + '_ffn', rides, emit)
    dx1, dy, dg1, dn2, dsh2, dsc2 = _res_mod_bwd(dx2, dh2, sv['x1'], sv['y'], g1, w['norm2'], sh2, sc2, d, tag + '_mod2_bwd')
    dymix = _mm([(dy, w['w_out'])], 'nt', F32, tag + '_out_dx')
    emit('w_out', _mm([(sv['ymix'], dy)], 'tn', GRAD_WIRE, tag + '_out_dw'))
    dz, mg = mixer_bwd(dymix, sv['z'], sv['msv'])
    emit('w_in', _mm([(sv['h'], dz)], 'tn', GRAD_WIRE, tag + '_in_dw', rides=rides))
    dh = _mm([(dz, w['w_in'])], 'nt', F32, tag + '_in_dx', rides=rides)
    dx, dn1, dsh1, dsc1 = _mod_bwd_call(dx1, dh, sv['x'], w['norm1'], sh1, sc1, d, tag + '_mod1_bwd')
    grads = dict(mg, ffn_conv_w=dconv_w, ffn_conv_b=dconv_b, norm1=dn1, norm2=dn2)
    return dx, grads, [dsh1, dsc1, dg1, dsh2, dsc2, dg2]


def _pad_flat(v, mult=LANE):
    v = v.reshape(-1)
    return jnp.pad(v, (0, (-v.shape[0]) % mult))


def _pack(entries, row_mult):
    flat, offs, pos = [], [], 0
    for v in entries:
        f = _pad_flat(v)
        flat.append(f)
        offs.append(pos)
        pos += f.shape[0]
    tot = jnp.concatenate(flat)
    tot = jnp.pad(tot, (0, (-pos) % (row_mult * LANE)))
    return tot.reshape(-1, LANE), offs


def _unpack(packed, offs, shapes):
    flat = packed.reshape(-1)
    out = []
    for off, shp in zip(offs, shapes):
        n = 1
        for s in shp:
            n *= s
        out.append(flat[off:off + n].reshape(shp))
    return out


def _step(a):
    d = _dims()
    dm = d.D
    me = 4 * lax.axis_index("x") + 2 * lax.axis_index("y") + lax.axis_index("c")
    sds = jax.ShapeDtypeStruct

    rides = _Rides()
    fwd_hosts = {'l0_w_out': 'l0_in', 'l0_ffn_up': 'flash_fwd', 'l0_ffn_down': 'flash_fwd', 'l1_ffn_up': 'flash_fwd',
                 'l1_w_out': 'l0_ffn_up', 'l1_w_in': 'l0_ffn_up', 'l1_ffn_down': 'l0_ffn_down'}

    def gathered(name):
        shard = _cast_bf16(a[name], 'cast_' + name)
        if name in fwd_hosts:
            rides.add(fwd_hosts[name], name, shard, False)
            return lambda: rides.done[name]
        g = _exchange(shard, False, 'ag_' + name)
        return lambda: g

    def cols(name, relayout=lambda t: t):
        g = gathered(name)
        k, n = a[name].shape
        return lambda: relayout(g().transpose(1, 0, 2).reshape(k, N_DEV * n))

    def rows(name):
        g = gathered(name)
        return lambda: g().reshape(N_DEV * a[name].shape[0], a[name].shape[1])

    small_names = ['l0_conv_a', 'l0_ffn_conv_w', 'l1_ffn_conv_w', 'l1_gate_fw_w', 'l1_gate_bw_w']
    spack, soffs = _pack([a[n] for n in small_names], 8)
    sg = _exchange(spack, False, 'ag_small')
    small_w = {}
    for n, off in zip(small_names, soffs):
        r, c = a[n].shape
        shards = sg.reshape(N_DEV, -1)[:, off:off + r * c].reshape(N_DEV, r, c)
        small_w[n] = shards.transpose(1, 0, 2).reshape(r, N_DEV * c)

    w0 = dict(norm1=a['l0_norm1'].reshape(1, dm), norm2=a['l0_norm2'].reshape(1, dm),
              w_in=cols('l0_w_in', lambda t: _win0_to_kernel(t, d)), w_qb=cols('l0_w_qb', lambda t: _wqb_to_kernel(t, d))(),
              w_kvb=cols('l0_w_kvb', lambda t: _wkvb_to_kernel(t, d))(), w_out=rows('l0_w_out'),
              q_norm=a['l0_q_norm'].reshape(1, -1), kv_norm=a['l0_kv_norm'].reshape(1, -1), conv_a=small_w['l0_conv_a'],
              ffn_up=cols('l0_ffn_up'), ffn_conv_w=small_w['l0_ffn_conv_w'], ffn_conv_b=a['l0_ffn_conv_b'],
              ffn_down=rows('l0_ffn_down'))
    w1 = dict(norm1=a['l1_norm1'].reshape(1, dm), norm2=a['l1_norm2'].reshape(1, dm),
              w_in=cols('l1_w_in', lambda t: _win1_to_kernel(t, d)), w_out=rows('l1_w_out'),
              gate_w=_gate_weight(small_w['l1_gate_fw_w'], small_w['l1_gate_bw_w'], d),
              gate_b=jnp.concatenate([a['l1_gate_fw_b'], a['l1_gate_bw_b']]).reshape(1, -1),
              o_norm=a['l1_o_norm'].reshape(1, -1),
              ffn_up=cols('l1_ffn_up'), ffn_conv_w=small_w['l1_ffn_conv_w'], ffn_conv_b=a['l1_ffn_conv_b'],
              ffn_down=rows('l1_ffn_down'))

    c8 = _exchange(a['c'], False, 'ag_c').reshape(N_DEV, dm)
    c16 = jnp.concatenate([c8, a['c_ctx'].reshape(1, dm), jnp.zeros((7, dm), F32)], axis=0)
    act16, dact16 = _small(lambda v: (_silu(v), _dsilu(v)), [c16], [sds((16, dm), BF16), sds((16, dm), F32)], 'cond_silu')
    n6 = N_MOD * dm // N_DEV
    mod_sh = [_mm([(act16, a[f'l{l}_ada_w'])], 'nn', F32, f'ada{l}') for l in (0, 1)]
    mod_all = _exchange(jnp.concatenate(mod_sh, axis=1), False, 'ag_mod')
    mods = []
    for l in (0, 1):
        full = mod_all[:, :, l * n6:(l + 1) * n6].transpose(1, 0, 2).reshape(16, N_MOD * dm)
        mine = jnp.concatenate([lax.dynamic_slice_in_dim(full, me, 1, 0), full[8:9]], axis=0)
        m2 = _small(lambda r, b: (r + b,), [mine, a[f'l{l}_ada_b'].reshape(1, -1)], [sds((2, N_MOD * dm), F32)], f'ada{l}_bias')[0]
        mods.append([m2[:, k * dm:(k + 1) * dm].reshape(2, 1, dm) for k in range(N_MOD)])

    cos, sin_s = _rope_tables(d)
    x0 = jnp.concatenate([a['x'][0], a['ctx'][0]], axis=0)
    x2, sv0 = _layer_fwd(x0, mods[0], w0, lambda z: _ab_fwd(z, w0, cos, sin_s, d, rides), d, 'l0', rides)
    x4, sv1 = _layer_fwd(x2, mods[1], w1, lambda z: _gla_fwd_block(z, w1, d), d, 'l1', rides)
    dx4, loss_acc, dfn = _loss_bwd(x4, a['final_norm'].reshape(1, dm), a['loss_target'][0], d)

    bwd_hosts = {'l1_ffn_down': 'l1_ffn_up_dx', 'l1_ffn_up': 'gla_bwd', 'l1_w_out': 'gla_bwd', 'l1_w_in': 'l1_in_dx',
                 'l0_ffn_down': 'l0_ffn_up_dx', 'l0_ffn_up': 'flash_bwd', 'l0_w_out': 'flash_bwd', 'l0_w_qb': 'l0_in_dw',
                 'l0_w_kvb': 'l0_in_dw', 'l0_w_in': 'l0_in_dx'}
    from_kernel = {'l0_w_in': _win0_from_kernel, 'l0_w_qb': _wqb_from_kernel, 'l0_w_kvb': _wkvb_from_kernel,
                   'l1_w_in': _win1_from_kernel}
    slabs = {}

    def emitter(layer):
        def emit(wkey, dw):
            name = f'l{layer}_{wkey}'
            if name in from_kernel:
                dw = from_kernel[name](dw, d)
            if a[name].shape[0] == dw.shape[0]:
                stacked = dw.reshape(dw.shape[0], N_DEV, dw.shape[1] // N_DEV).transpose(1, 0, 2)
            else:
                stacked = dw.reshape(N_DEV, dw.shape[0] // N_DEV, dw.shape[1])
            if name in bwd_hosts:
                rides.add(bwd_hosts[name], 'rs_' + name, stacked, True)
                slabs[name] = lambda: rides.done['rs_' + name]
            else:
                got = _exchange(stacked, True, 'rs_' + name)
                slabs[name] = lambda: got
        return emit

    dx2, g1, dmod1 = _layer_bwd(dx4, sv1, mods[1], w1, lambda dy, z, msv: _gla_bwd_block(dy, z, msv, w1, d, rides), d, 'l1',
                                rides, emitter(1))
    dx0, g0, dmod0 = _layer_bwd(dx2, sv0, mods[0], w0,
                                lambda dy, z, msv: _ab_bwd(dy, z, msv, w0, cos, sin_s, d, rides, emitter(0)), d, 'l0',
                                rides, emitter(0))

    dm_rows = jnp.concatenate([jnp.concatenate([t.reshape(2, dm) for t in dmod], axis=1) for dmod in (dmod0, dmod1)], axis=0)
    dm_all = _exchange(dm_rows, False, 'ag_dmod')
    lat = dm_all[:, 0::2].transpose(1, 0, 2)
    ctxs = dm_all[:, 1::2].transpose(1, 0, 2)

    def ada_prep(lat, ctxs):
        csum = jnp.sum(ctxs, axis=1, keepdims=True)
        row = lax.broadcasted_iota(jnp.int32, (1, 8, 1), 1)
        g16 = jnp.concatenate([lat, jnp.where(row == 0, csum, 0.0)], axis=1)
        return g16, jnp.sum(lat, axis=1, keepdims=True) + csum
    g16, gb = _small(ada_prep, [lat, ctxs], [sds((2, 16, N_MOD * dm), F32), sds((2, 1, N_MOD * dm), F32)], 'ada_bwd_prep')
    g16_sh = [lax.dynamic_slice_in_dim(g16[l], me * n6, n6, 1) for l in (0, 1)]
    grad_ada_w = [_mm([(act16, g16_sh[l])], 'tn', F32, f'ada{l}_dw') for l in (0, 1)]
    dact = _mm([(g16_sh[0], a['l0_ada_w']), (g16_sh[1], a['l1_ada_w'])], 'nt', F32, 'ada_dact')
    dcc = _small(lambda t, s: (t * s,), [dact[8:9], dact16[8:9]], [sds((1, dm), F32)], 'cctx_grad')[0]

    res = {}
    for name in slabs:
        res[name] = _adam(slabs[name](), a[name], a['m_' + name], a['v_' + name], 'adam_' + name)
    for l in (0, 1):
        name = f'l{l}_ada_w'
        res[name] = _adam(grad_ada_w[l][None], a[name], a['m_' + name], a['v_' + name], 'adam_' + name)

    part = {'loss': loss_acc[0:1, 0:1], 'c_ctx': dcc, 'final_norm': dfn,
            'l0_norm1': g0['norm1'], 'l0_norm2': g0['norm2'], 'l0_q_norm': g0['q_norm'], 'l0_kv_norm': g0['kv_norm'],
            'l0_conv_a': g0['conv_a'], 'l0_ffn_conv_w': g0['ffn_conv_w'], 'l0_ffn_conv_b': g0['ffn_conv_b'],
            'l1_norm1': g1['norm1'], 'l1_norm2': g1['norm2'], 'l1_o_norm': g1['o_norm'],
            'l1_gate_fw_w': g1['gate_fw_w'], 'l1_gate_bw_w': g1['gate_bw_w'], 'l1_gate_fw_b': g1['gate_fw_b'],
            'l1_gate_bw_b': g1['gate_bw_b'], 'l1_ffn_conv_w': g1['ffn_conv_w'], 'l1_ffn_conv_b': g1['ffn_conv_b']}
    pkeys = list(part)
    ppack, poffs = _pack([part[k] for k in pkeys], 8)
    psum = _sum_parts(_exchange(ppack, False, 'ag_small_grads'), 'sum_small_grads')
    tot = dict(zip(pkeys, _unpack(psum, poffs, [part[k].shape for k in pkeys])))
    loss = tot['loss'].reshape(())
    sgrad = {}
    for n in _WEIGHTS:
        if n in res:
            continue
        if n.endswith('ada_b'):
            sgrad[n] = gb[int(n[1])].reshape(a[n].shape)
        elif n in small_names:
            c = a[n].shape[1]
            sgrad[n] = lax.dynamic_slice_in_dim(tot[n], me * c, c, 1)
        else:
            sgrad[n] = tot[n].reshape(a[n].shape)
    snames = list(sgrad)
    packs = [_pack([src[n] for n in snames], 8)[0] for src in
             (sgrad, {n: a[n] for n in snames}, {n: a['m_' + n] for n in snames}, {n: a['v_' + n] for n in snames})]
    offs = _pack([sgrad[n] for n in snames], 8)[1]
    outs = _adam(packs[0][None], packs[1], packs[2], packs[3], 'adam_small')
    for k in range(4):
        for n, val in zip(snames, _unpack(outs[k], offs, [a[n].shape for n in snames])):
            res.setdefault(n, [None] * 4)[k] = val

    grad_x = dx0[:d.T].reshape(1, d.T, dm)
    return (loss, grad_x, *[res[n][0] for n in _WEIGHTS], *[res[n][1] for n in _WEIGHTS], *[res[n][2] for n in _WEIGHTS],
            *[res[n][3] for n in _WEIGHTS])


def kernel(*args):
    return _step(dict(zip(_ARGS, args, strict=True)))
```
